```python
import math
import jax, jax.numpy as jnp
from jax import lax
import numpy as np

D_MODEL = 1024
BATCH = 32
SEQ = 256
DEPTH = 1
DEC_BATCH = 8
DEC_SEQ = 2048
PAST_LEN = 256

GRID_W = 64
D_MIX = D_MODEL
SSM_WIDTH = D_MIX // 2
ATTN_WIDTH = D_MIX - SSM_WIDTH
SSM_GROUP = 16
SSM_GROUPS = SSM_WIDTH // SSM_GROUP
SSM_STATE = 64
HEAD_DIM = 64
N_HEADS = ATTN_WIDTH // HEAD_DIM
N_KV_HEADS = 2
KV_REP = N_HEADS // N_KV_HEADS
KV_WIDTH = N_KV_HEADS * HEAD_DIM
WINDOW = 128
BLOCK = 128
ROPE_AXIS_DIM = HEAD_DIM // 2
ROPE_BASE = 10000.0
D_IN = 2 * SSM_WIDTH + ATTN_WIDTH + 2 * KV_WIDTH + ATTN_WIDTH
EPS = 1e-6
LAMBDA_RE_MAX = -1e-4
NEG_INF = -1e30

kernel_name = "hymba_s5_swa_flow_backbone_step"


def _rmsnorm(x, w):
    xf = x.astype(jnp.float32)
    ms = jnp.mean(xf * xf, axis=-1, keepdims=True)
    return (xf * lax.rsqrt(ms + EPS)).astype(x.dtype) * w


def _modulation(cond, w_mod, b_mod):
    mod = jax.nn.silu(cond) @ w_mod + b_mod
    return jnp.split(mod, 3, axis=-1)


def _split_proj(p):
    cuts = [SSM_WIDTH, 2 * SSM_WIDTH, 2 * SSM_WIDTH + ATTN_WIDTH,
            2 * SSM_WIDTH + ATTN_WIDTH + KV_WIDTH, 2 * SSM_WIDTH + ATTN_WIDTH + 2 * KV_WIDTH]
    return jnp.split(p, cuts, axis=-1)


def _s5_discretize(lam_re, lam_im, log_dt, b_re, b_im):
    lam_re = jnp.minimum(lam_re, LAMBDA_RE_MAX)
    dt = jnp.exp(log_dt)[:, None]
    mag = jnp.exp(lam_re * dt)
    ang = lam_im * dt
    lbar_re = mag * jnp.cos(ang)
    lbar_im = mag * jnp.sin(ang)
    nr = lbar_re - 1.0
    ni = lbar_im
    den = lam_re * lam_re + lam_im * lam_im
    f_re = (nr * lam_re + ni * lam_im) / den
    f_im = (ni * lam_re - nr * lam_im) / den
    bb_re = f_re[..., None] * b_re - f_im[..., None] * b_im
    bb_im = f_re[..., None] * b_im + f_im[..., None] * b_re
    return lbar_re, lbar_im, bb_re, bb_im


def _complex_linear_op(e1, e2):
    a1r, a1i, b1r, b1i = e1
    a2r, a2i, b2r, b2i = e2
    return (a2r * a1r - a2i * a1i,
            a2r * a1i + a2i * a1r,
            a2r * b1r - a2i * b1i + b2r,
            a2r * b1i + a2i * b1r + b2i)


def _s5_scan(bu_re, bu_im, lbar_re, lbar_im, h0_re, h0_im, reverse):
    idx = -1 if reverse else 0
    inj_re = lbar_re * h0_re - lbar_im * h0_im
    inj_im = lbar_re * h0_im + lbar_im * h0_re
    bu_re = bu_re.at[:, idx].add(inj_re)
    bu_im = bu_im.at[:, idx].add(inj_im)
    a_re = jnp.broadcast_to(lbar_re, bu_re.shape)
    a_im = jnp.broadcast_to(lbar_im, bu_im.shape)
    _, _, s_re, s_im = lax.associative_scan(_complex_linear_op, (a_re, a_im, bu_re, bu_im),
                                            reverse=reverse, axis=1)
    return s_re, s_im, s_re[:, idx], s_im[:, idx]


def _s5_branch(u, lam_re, lam_im, log_dt, b_re, b_im, c_re, c_im, d, w_glu, b_glu, h0_re, h0_im):
    f32 = jnp.float32
    bsz, L, _ = u.shape
    uf = u.astype(f32).reshape(bsz, L, SSM_GROUPS, SSM_GROUP)
    y = uf * d.astype(f32).reshape(SSM_GROUPS, SSM_GROUP)
    fin_re, fin_im = [], []
    for direction in range(2):
        lbar_re, lbar_im, bb_re, bb_im = _s5_discretize(
            lam_re[direction].astype(f32), lam_im[direction].astype(f32),
            log_dt[direction].astype(f32), b_re[direction].astype(f32), b_im[direction].astype(f32))
        bu_re = jnp.einsum('blgc,gpc->blgp', uf, bb_re)
        bu_im = jnp.einsum('blgc,gpc->blgp', uf, bb_im)
        s_re, s_im, f_re, f_im = _s5_scan(bu_re, bu_im, lbar_re, lbar_im,
                                          h0_re[:, direction].astype(f32),
                                          h0_im[:, direction].astype(f32),
                                          reverse=(direction == 1))
        y = (y + jnp.einsum('blgp,gcp->blgc', s_re, c_re[direction].astype(f32))
               - jnp.einsum('blgp,gcp->blgc', s_im, c_im[direction].astype(f32)))
        fin_re.append(f_re)
        fin_im.append(f_im)
    y = jax.nn.gelu(y.reshape(bsz, L, SSM_WIDTH))
    y = y * jax.nn.sigmoid(y @ w_glu.astype(f32) + b_glu.astype(f32))
    return y.astype(u.dtype), jnp.stack(fin_re, axis=1), jnp.stack(fin_im, axis=1)


def _axial_rope_tables(L):
    rows = L // GRID_W
    row = jnp.repeat(jnp.arange(rows, dtype=jnp.float32), GRID_W)
    col = jnp.tile(jnp.arange(GRID_W, dtype=jnp.float32), rows)
    freqs = ROPE_BASE ** (-jnp.arange(0, ROPE_AXIS_DIM, 2, dtype=jnp.float32) / ROPE_AXIS_DIM)
    ang_r = row[:, None] * freqs[None, :]
    ang_c = col[:, None] * freqs[None, :]
    return jnp.cos(ang_r), jnp.sin(ang_r), jnp.cos(ang_c), jnp.sin(ang_c)


def _rotate(x, cos, sin):
    half = x.shape[-1] // 2
    x1, x2 = x[..., :half], x[..., half:]
    cos = cos[None, :, None, :].astype(x.dtype)
    sin = sin[None, :, None, :].astype(x.dtype)
    return jnp.concatenate([x1 * cos - x2 * sin, x2 * cos + x1 * sin], axis=-1)


def _apply_axial_rope(x, cos_r, sin_r, cos_c, sin_c):
    return jnp.concatenate([_rotate(x[..., :ROPE_AXIS_DIM], cos_r, sin_r),
                            _rotate(x[..., ROPE_AXIS_DIM:], cos_c, sin_c)], axis=-1)


def _ctx_attention(q, k, v, sink):
    bsz, Lc, _ = q.shape
    qg = q.reshape(bsz, Lc, N_KV_HEADS, KV_REP, HEAD_DIM)
    kg = k.reshape(bsz, Lc, N_KV_HEADS, HEAD_DIM)
    vg = v.reshape(bsz, Lc, N_KV_HEADS, HEAD_DIM)
    s = jnp.einsum('bqgrd,bkgd->bgrqk', qg, kg).astype(jnp.float32) * (HEAD_DIM ** -0.5)
    sink_col = jnp.broadcast_to(sink.astype(jnp.float32).reshape(N_KV_HEADS, KV_REP)[None, :, :, None, None],
                                (bsz, N_KV_HEADS, KV_REP, Lc, 1))
    p = jax.nn.softmax(jnp.concatenate([s, sink_col], axis=-1), axis=-1)[..., :-1]
    out = jnp.einsum('bgrqk,bkgd->bqgrd', p.astype(v.dtype), vg)
    return out.reshape(bsz, Lc, ATTN_WIDTH)


def _band(xb):
    zeros = jnp.zeros_like(xb[:, :1])
    prev = jnp.concatenate([zeros, xb[:, :-1]], axis=1)
    nxt = jnp.concatenate([xb[:, 1:], zeros], axis=1)
    return jnp.concatenate([prev, xb, nxt], axis=2)


def _latent_attention(q, k, v, k_ctx, v_ctx, sink):
    bsz, L = q.shape[0], q.shape[1]
    nb = L // BLOCK
    scale = HEAD_DIM ** -0.5
    qb = q.reshape(bsz, nb, BLOCK, N_KV_HEADS, KV_REP, HEAD_DIM)
    k_band = _band(k.reshape(bsz, nb, BLOCK, N_KV_HEADS, HEAD_DIM))
    v_band = _band(v.reshape(bsz, nb, BLOCK, N_KV_HEADS, HEAD_DIM))
    s_loc = jnp.einsum('bnqgrd,bnkgd->bnqgrk', qb, k_band).astype(jnp.float32) * scale
    qpos = jnp.arange(nb)[:, None] * BLOCK + jnp.arange(BLOCK)[None, :]
    kpos = (jnp.arange(nb)[:, None] - 1) * BLOCK + jnp.arange(3 * BLOCK)[None, :]
    valid = ((jnp.abs(qpos[:, :, None] - kpos[:, None, :]) <= WINDOW)
             & (kpos[:, None, :] >= 0) & (kpos[:, None, :] < L))
    s_loc = jnp.where(valid[None, :, :, None, None, :], s_loc, NEG_INF)
    s_ctx = jnp.einsum('bnqgrd,bkgd->bnqgrk', qb, k_ctx).astype(jnp.float32) * scale
    sink_col = jnp.broadcast_to(sink.astype(jnp.float32).reshape(N_KV_HEADS, KV_REP)[:, :, None],
                                (bsz, nb, BLOCK, N_KV_HEADS, KV_REP, 1))
    p = jax.nn.softmax(jnp.concatenate([s_loc, s_ctx, sink_col], axis=-1), axis=-1).astype(v.dtype)
    nk = 3 * BLOCK
    out = (jnp.einsum('bnqgrk,bnkgd->bnqgrd', p[..., :nk], v_band)
           + jnp.einsum('bnqgrk,bkgd->bnqgrd', p[..., nk:-1], v_ctx))
    return out.reshape(bsz, L, ATTN_WIDTH)


def _context_layer(x, c_ctx, norm_w, w_mod, b_mod, w_in, lam_re, lam_im, log_dt, b_re, b_im,
                   c_re, c_im, d, w_glu, b_glu, sink, w_out):
    bsz, Lc, _ = x.shape
    shift, scale, gate = _modulation(c_ctx, w_mod, b_mod)
    h = _rmsnorm(x, norm_w) * (1.0 + scale) + shift
    u, z_ssm, q, k, v, z_attn = _split_proj(h @ w_in)
    h0 = jnp.zeros((bsz, 2, SSM_GROUPS, SSM_STATE), jnp.float32)
    y_ssm, fin_re, fin_im = _s5_branch(u, lam_re, lam_im, log_dt, b_re, b_im, c_re, c_im,
                                       d, w_glu, b_glu, h0, h0)
    y_attn = _ctx_attention(q, k, v, sink)
    mix = jnp.concatenate([y_ssm * jax.nn.silu(z_ssm), y_attn * jax.nn.silu(z_attn)], axis=-1)
    x = x + gate * (mix @ w_out)
    k_c = k.reshape(bsz, Lc, N_KV_HEADS, HEAD_DIM)
    v_c = v.reshape(bsz, Lc, N_KV_HEADS, HEAD_DIM)
    return x, k_c, v_c, fin_re.astype(x.dtype), fin_im.astype(x.dtype)


def _latent_layer(x, c, k_ctx, v_ctx, st_re, st_im, norm_w, w_mod, b_mod, w_in, lam_re, lam_im,
                  log_dt, b_re, b_im, c_re, c_im, d, w_glu, b_glu, sink, w_out):
    bsz, L, _ = x.shape
    shift, scale, gate = _modulation(c, w_mod, b_mod)
    shift, scale, gate = shift[:, None, :], scale[:, None, :], gate[:, None, :]
    h = _rmsnorm(x, norm_w) * (1.0 + scale) + shift
    u, z_ssm, q, k, v, z_attn = _split_proj(h @ w_in)
    y_ssm, _, _ = _s5_branch(u, lam_re, lam_im, log_dt, b_re, b_im, c_re, c_im,
                             d, w_glu, b_glu, st_re, st_im)
    tables = _axial_rope_tables(L)
    q = _apply_axial_rope(q.reshape(bsz, L, N_HEADS, HEAD_DIM), *tables)
    k = _apply_axial_rope(k.reshape(bsz, L, N_KV_HEADS, HEAD_DIM), *tables)
    v = v.reshape(bsz, L, N_KV_HEADS, HEAD_DIM)
    y_attn = _latent_attention(q, k, v, k_ctx, v_ctx, sink)
    mix = jnp.concatenate([y_ssm * jax.nn.silu(z_ssm), y_attn * jax.nn.silu(z_attn)], axis=-1)
    return x + gate * (mix @ w_out)


def setup_inputs(seed: int = 0) -> dict:
    key = jax.random.key(seed)
    ks = jax.random.split(key, 32)
    f32 = jnp.float32
    n = lambda k, s, sc: jax.random.normal(k, s, f32) * sc
    lam_re = -0.5 + n(ks[0], (DEPTH, 2, SSM_GROUPS, SSM_STATE), 0.01)
    lam_im = (math.pi * jnp.arange(SSM_STATE, dtype=f32))[None, None, None, :] + \
        n(ks[1], (DEPTH, 2, SSM_GROUPS, SSM_STATE), 0.01)
    log_dt = jax.random.uniform(ks[2], (DEPTH, 2, SSM_GROUPS), f32,
                                minval=math.log(1e-3), maxval=math.log(1e-1))
    b_sc = (2.0 * SSM_GROUP) ** -0.5
    c_sc = SSM_STATE ** -0.5
    return {
        "x_prompt": n(ks[3], (BATCH, SEQ, D_MODEL), 1.0),
        "x_sample": n(ks[4], (DEC_BATCH, DEC_SEQ, D_MODEL), 1.0),
        "c": n(ks[5], (DEC_BATCH, D_MODEL), 1.0),
        "cache_k": n(ks[6], (DEC_BATCH, DEPTH, PAST_LEN, N_KV_HEADS, HEAD_DIM), 1.0),
        "cache_v": n(ks[7], (DEC_BATCH, DEPTH, PAST_LEN, N_KV_HEADS, HEAD_DIM), 1.0),
        "state_ssm_re": n(ks[8], (DEC_BATCH, DEPTH, 2, SSM_GROUPS, SSM_STATE), 0.3),
        "state_ssm_im": n(ks[9], (DEC_BATCH, DEPTH, 2, SSM_GROUPS, SSM_STATE), 0.3),
        "c_ctx": n(ks[10], (D_MODEL,), 1.0),
        "norm_w": 1.0 + n(ks[11], (DEPTH, D_MODEL), 0.02),
        "w_mod": n(ks[12], (DEPTH, D_MODEL, 3 * D_MODEL), D_MODEL ** -0.5),
        "b_mod": n(ks[13], (DEPTH, 3 * D_MODEL), 0.01),
        "w_in": n(ks[14], (DEPTH, D_MODEL, D_IN), D_MODEL ** -0.5),
        "ssm_lambda_re": lam_re,
        "ssm_lambda_im": lam_im,
        "ssm_log_dt": log_dt,
        "ssm_b_re": n(ks[15], (DEPTH, 2, SSM_GROUPS, SSM_STATE, SSM_GROUP), b_sc),
        "ssm_b_im": n(ks[16], (DEPTH, 2, SSM_GROUPS, SSM_STATE, SSM_GROUP), b_sc),
        "ssm_c_re": n(ks[17], (DEPTH, 2, SSM_GROUPS, SSM_GROUP, SSM_STATE), c_sc),
        "ssm_c_im": n(ks[18], (DEPTH, 2, SSM_GROUPS, SSM_GROUP, SSM_STATE), c_sc),
        "ssm_d": n(ks[19], (DEPTH, SSM_WIDTH), 1.0),
        "w_glu": n(ks[20], (DEPTH, SSM_WIDTH, SSM_WIDTH), SSM_WIDTH ** -0.5),
        "b_glu": n(ks[21], (DEPTH, SSM_WIDTH), 0.01),
        "attn_sink": n(ks[22], (DEPTH, N_HEADS), 0.5),
        "w_out": n(ks[23], (DEPTH, D_MIX, D_MODEL), D_MIX ** -0.5),
        "final_norm_w": 1.0 + n(ks[24], (D_MODEL,), 0.02),
    }


def reference(x_prompt, x_sample, c, cache_k, cache_v, state_ssm_re, state_ssm_im, c_ctx,
              norm_w, w_mod, b_mod, w_in, ssm_lambda_re, ssm_lambda_im, ssm_log_dt,
              ssm_b_re, ssm_b_im, ssm_c_re, ssm_c_im, ssm_d, w_glu, b_glu, attn_sink,
              w_out, final_norm_w):
    xp = x_prompt
    ks_new, vs_new, sre_new, sim_new = [], [], [], []
    for layer in range(DEPTH):
        xp, k_c, v_c, f_re, f_im = _context_layer(
            xp, c_ctx, norm_w[layer], w_mod[layer], b_mod[layer], w_in[layer],
            ssm_lambda_re[layer], ssm_lambda_im[layer], ssm_log_dt[layer],
            ssm_b_re[layer], ssm_b_im[layer], ssm_c_re[layer], ssm_c_im[layer],
            ssm_d[layer], w_glu[layer], b_glu[layer], attn_sink[layer], w_out[layer])
        ks_new.append(k_c)
        vs_new.append(v_c)
        sre_new.append(f_re)
        sim_new.append(f_im)
    y_prompt = _rmsnorm(xp, final_norm_w)
    new_cache_k = jnp.stack(ks_new, axis=1)
    new_cache_v = jnp.stack(vs_new, axis=1)
    new_state_ssm_re = jnp.stack(sre_new, axis=1)
    new_state_ssm_im = jnp.stack(sim_new, axis=1)

    xs = x_sample
    for layer in range(DEPTH):
        xs = _latent_layer(
            xs, c, cache_k[:, layer], cache_v[:, layer], state_ssm_re[:, layer], state_ssm_im[:, layer],
            norm_w[layer], w_mod[layer], b_mod[layer], w_in[layer],
            ssm_lambda_re[layer], ssm_lambda_im[layer], ssm_log_dt[layer],
            ssm_b_re[layer], ssm_b_im[layer], ssm_c_re[layer], ssm_c_im[layer],
            ssm_d[layer], w_glu[layer], b_glu[layer], attn_sink[layer], w_out[layer])
    y_sample = _rmsnorm(xs, final_norm_w)
    return (y_prompt, y_sample, new_cache_k, new_cache_v, new_state_ssm_re, new_state_ssm_im)
```

```python
import functools
import math

import jax
import jax.numpy as jnp
from jax import lax
from jax.experimental import pallas as pl
from jax.experimental.pallas import tpu as pltpu

F32 = jnp.float32
BF16 = jnp.bfloat16

D_MODEL = 1024
SSM_WIDTH = 512
ATTN_WIDTH = 512
SSM_GROUP = 16
SSM_GROUPS = 32
SSM_STATE = 64
HEAD_DIM = 64
N_HEADS = 8
N_KV_HEADS = 2
KV_REP = 4
KV_WIDTH = 128
GRID_W = 64
ROPE_AXIS_DIM = 32
ROPE_BASE = 10000.0
EPS = 1e-6
LAMBDA_RE_MAX = -1e-4
NEG_INF = -1e30
PAST_LEN = 256

SEQS = 8
TL = 128
ROWS = SEQS * TL
HALF_CH = SSM_WIDTH // 2
HALF_ST = (SSM_GROUPS // 2) * SSM_STATE
ALL_ST = 2 * HALF_ST
TC = 64
SCAN_LANES = 512
VMEM_LIMIT = 58 * 1024 * 1024


def _const_spec(shape):
    zeros = (0,) * len(shape)
    return pl.BlockSpec(shape, lambda g, t: zeros, pipeline_mode=pl.Buffered(1))


def _modulation_kernel(cond_ref, w_ref, b_ref, out_ref):
    a = jax.nn.silu(cond_ref[...]).astype(BF16)
    out_ref[...] = jnp.dot(a, w_ref[...], preferred_element_type=F32) + b_ref[...]


def _modulation(cond, w_mod_bf16, b_mod):
    rows = cond.shape[0]
    n = w_mod_bf16.shape[1]
    tn = 1024
    return pl.pallas_call(
        _modulation_kernel,
        grid=(n // tn,),
        in_specs=[pl.BlockSpec((rows, D_MODEL), lambda j: (0, 0)),
                  pl.BlockSpec((D_MODEL, tn), lambda j: (0, j)),
                  pl.BlockSpec((1, tn), lambda j: (0, j))],
        out_specs=pl.BlockSpec((rows, tn), lambda j: (0, j)),
        out_shape=jax.ShapeDtypeStruct((rows, n), F32),
        name="modulation",
    )(cond, w_mod_bf16, b_mod)


def _discretize_kernel(lre_ref, lim_ref, ldt_ref, bre_ref, bim_ref,
                       lbre_ref, lbim_ref, bbre_ref, bbim_ref):
    lam_re = jnp.minimum(lre_ref[...], LAMBDA_RE_MAX)
    lam_im = lim_ref[...]
    dt = jnp.exp(ldt_ref[...])
    mag = jnp.exp(lam_re * dt)
    ang = lam_im * dt
    lbar_re = mag * jnp.cos(ang)
    lbar_im = mag * jnp.sin(ang)
    nr = lbar_re - 1.0
    ni = lbar_im
    den = lam_re * lam_re + lam_im * lam_im
    f_re = (nr * lam_re + ni * lam_im) / den
    f_im = (ni * lam_re - nr * lam_im) / den
    b_re = bre_ref[...]
    b_im = bim_ref[...]
    lbre_ref[...] = lbar_re
    lbim_ref[...] = lbar_im
    bbre_ref[...] = f_re * b_re - f_im * b_im
    bbim_ref[...] = f_re * b_im + f_im * b_re


def _discretize(lam_re, lam_im, log_dt, b_re, b_im):
    n = 2 * SSM_GROUPS * SSM_GROUP
    expand = lambda a: jnp.broadcast_to(a[:, :, None, :], (2, SSM_GROUPS, SSM_GROUP, SSM_STATE)).reshape(n, SSM_STATE)
    ldt = jnp.broadcast_to(log_dt[:, :, None, None], (2, SSM_GROUPS, SSM_GROUP, SSM_STATE)).reshape(n, SSM_STATE)
    bt = lambda a: jnp.transpose(a, (0, 1, 3, 2)).reshape(n, SSM_STATE)
    shp = jax.ShapeDtypeStruct((n, SSM_STATE), F32)
    lbre, lbim, bbre, bbim = pl.pallas_call(
        _discretize_kernel, out_shape=(shp, shp, shp, shp), name="discretize",
    )(expand(lam_re), expand(lam_im), ldt, bt(b_re), bt(b_im))
    r4 = lambda a: a.reshape(2, SSM_GROUPS, SSM_GROUP, SSM_STATE)
    return r4(lbre)[:, :, 0, :], r4(lbim)[:, :, 0, :], r4(bbre), r4(bbim)


def _block_diag_weights(bb_re, bb_im, c_re, c_im):
    gh = SSM_GROUPS // 2
    eye = jnp.eye(gh, dtype=F32)

    def in_blocks(a):
        a = a.reshape(2, 2, gh, SSM_GROUP, SSM_STATE)
        return jnp.einsum('ij,dhicp->dhicjp', eye, a).reshape(2, 2, gh * SSM_GROUP, gh * SSM_STATE)

    def out_blocks(a):
        a = a.reshape(2, 2, gh, SSM_GROUP, SSM_STATE)
        return jnp.einsum('ij,dhicp->dhipjc', eye, a).reshape(2, 2, gh * SSM_STATE, gh * SSM_GROUP)

    wb = jnp.concatenate([in_blocks(bb_re), in_blocks(bb_im)], axis=-1).astype(BF16)
    wc = jnp.concatenate([out_blocks(c_re), out_blocks(-c_im)], axis=-2).astype(BF16)
    return wb, wc


def _rope_tables(length):
    pos = jnp.arange(length)
    row = (pos // GRID_W).astype(F32)
    col = (pos % GRID_W).astype(F32)
    freqs = ROPE_BASE ** (-jnp.arange(0, ROPE_AXIS_DIM, 2, dtype=F32) / ROPE_AXIS_DIM)
    ang_r = row[:, None] * freqs[None, :]
    ang_c = col[:, None] * freqs[None, :]
    cos = jnp.concatenate([jnp.cos(ang_r), jnp.cos(ang_r), jnp.cos(ang_c), jnp.cos(ang_c)], axis=-1)
    sin = jnp.concatenate([-jnp.sin(ang_r), jnp.sin(ang_r), -jnp.sin(ang_c), jnp.sin(ang_c)], axis=-1)
    return jnp.tile(cos, (1, 2)), jnp.tile(sin, (1, 2))


def _norm_mod(h_ref, x_ref, mod_ref, nw_ref):
    nw = nw_ref[...]

    def body(b, carry):
        xb = x_ref[b]
        ms = jnp.mean(xb * xb, axis=-1, keepdims=True)
        hn = (xb * lax.rsqrt(ms + EPS)) * nw
        shift = mod_ref[pl.ds(b, 1), 0:D_MODEL]
        scale = mod_ref[pl.ds(b, 1), D_MODEL:2 * D_MODEL]
        r0 = pl.multiple_of(b * TL, TL)
        h_ref[pl.ds(r0, TL), :] = (hn * (1.0 + scale) + shift).astype(BF16)
        return carry

    lax.fori_loop(0, SEQS, body, 0)


def _rope(x, cos, sin):
    lane = lax.broadcasted_iota(jnp.int32, x.shape, 1)
    first = (lane & 31) < 16
    partner = jnp.where(first, pltpu.roll(x, 128 - 16, 1), pltpu.roll(x, 16, 1))
    x3 = x.reshape(SEQS, TL, 128)
    p3 = partner.reshape(SEQS, TL, 128)
    return (x3 * cos[None] + p3 * sin[None]).reshape(ROWS, 128)


def _to_token_major(dst_ref, src_ref):
    def body(l, carry):
        r0 = pl.multiple_of(l * SEQS, SEQS)
        for cb in range(SSM_WIDTH // 128):
            dst_ref[pl.ds(r0, SEQS), cb * 128:(cb + 1) * 128] = src_ref[cb, pl.ds(l, SEQS, stride=TL), :]
        return carry
    lax.fori_loop(0, TL, body, 0, unroll=4)


def _to_sequence_major(dst_ref, src_ref):
    def body(l, carry):
        r0 = pl.multiple_of(l * SEQS, SEQS)
        for cb in range(SSM_WIDTH // 128):
            dst_ref[cb, pl.ds(l, SEQS, stride=TL), :] = src_ref[pl.ds(r0, SEQS), cb * 128:(cb + 1) * 128]
        return carry
    lax.fori_loop(0, TL, body, 0, unroll=4)


def _store_column_blocks(dst_ref, val):
    for cb in range(SSM_WIDTH // 128):
        dst_ref[cb] = val[:, cb * 128:(cb + 1) * 128]


def _scan_chunk(s_ref, lb_ref, carry_ref, half, reverse):
    for cb in range(HALF_ST // SCAN_LANES):
        c0 = cb * SCAN_LANES
        g0 = half * HALF_ST + c0
        lr = lb_ref[0, :, g0:g0 + SCAN_LANES]
        li = lb_ref[1, :, g0:g0 + SCAN_LANES]
        sr = carry_ref[0, :, g0:g0 + SCAN_LANES]
        si = carry_ref[1, :, g0:g0 + SCAN_LANES]

        def step(i, state, c0=c0, lr=lr, li=li):
            sr, si = state
            tt = (TC - 1 - i) if reverse else i
            rows = pl.ds(pl.multiple_of(tt * SEQS, SEQS), SEQS)
            br = s_ref[rows, c0:c0 + SCAN_LANES]
            bi = s_ref[rows, HALF_ST + c0:HALF_ST + c0 + SCAN_LANES]
            nr = lr * sr - li * si + br
            ni = lr * si + li * sr + bi
            s_ref[rows, c0:c0 + SCAN_LANES] = nr
            s_ref[rows, HALF_ST + c0:HALF_ST + c0 + SCAN_LANES] = ni
            return nr, ni

        sr, si = lax.fori_loop(0, TC, step, (sr, si), unroll=2)
        carry_ref[0, :, g0:g0 + SCAN_LANES] = sr
        carry_ref[1, :, g0:g0 + SCAN_LANES] = si


def _s5_direction(u_tm_ref, s_ref, carry_ref, wb_ref, wc_ref, lb_ref, first_ref, emit, reverse):
    n_chunks = TL // TC
    chunk_rows = TC * SEQS
    first_rows = slice((TC - 1) * SEQS, TC * SEQS) if reverse else slice(0, SEQS)
    for half in range(2):
        for ci in range(n_chunks):
            c = (n_chunks - 1 - ci) if reverse else ci
            r0 = c * chunk_rows
            uc = u_tm_ref[r0:r0 + chunk_rows, half * HALF_CH:(half + 1) * HALF_CH].astype(BF16)
            s_ref[...] = jnp.dot(uc, wb_ref[half], preferred_element_type=F32)
            _scan_chunk(s_ref, lb_ref, carry_ref, half, reverse)
            if ci == 0:
                @pl.when(pl.program_id(1) == 0)
                def _(half=half):
                    cols = slice(half * HALF_ST, (half + 1) * HALF_ST)
                    first_ref[0, :, cols] = s_ref[first_rows, 0:HALF_ST]
                    first_ref[1, :, cols] = s_ref[first_rows, HALF_ST:2 * HALF_ST]
            y = jnp.dot(s_ref[...].astype(BF16), wc_ref[half], preferred_element_type=F32)
            emit(r0, chunk_rows, half, y)


def _pass_a_kernel(*refs, rotary):
    if rotary:
        (x_ref, mod_ref, nw_ref, w_ref, wb_ref, wc_ref, lb_ref, h0_ref, cos_ref, sin_ref,
         k_ref, v_ref, yb_ref, fin_ref, h_s, u_sm, u_tm, s_s, carry) = refs
    else:
        (x_ref, mod_ref, nw_ref, w_ref, wb_ref, wc_ref, lb_ref, h0_ref,
         k_ref, v_ref, yb_ref, fin_ref, h_s, u_sm, u_tm, s_s, carry) = refs

    @pl.when(pl.program_id(1) == 0)
    def _():
        carry[...] = h0_ref[...]

    _norm_mod(h_s, x_ref, mod_ref, nw_ref)
    h = h_s[...]
    _store_column_blocks(u_sm, jnp.dot(h, w_ref[:, 0:SSM_WIDTH], preferred_element_type=F32))
    k = jnp.dot(h, w_ref[:, SSM_WIDTH:SSM_WIDTH + KV_WIDTH], preferred_element_type=F32)
    v = jnp.dot(h, w_ref[:, SSM_WIDTH + KV_WIDTH:SSM_WIDTH + 2 * KV_WIDTH], preferred_element_type=F32)
    if rotary:
        k = _rope(k, cos_ref[...], sin_ref[...])
    k_ref[...] = k.reshape(SEQS, TL, KV_WIDTH).astype(k_ref.dtype)
    v_ref[...] = v.reshape(SEQS, TL, KV_WIDTH).astype(v_ref.dtype)

    _to_token_major(u_tm, u_sm)

    def emit(r0, nrows, half, y):
        yb_ref[r0:r0 + nrows, half * HALF_CH:(half + 1) * HALF_CH] = y

    _s5_direction(u_tm, s_s, carry, wb_ref, wc_ref, lb_ref, fin_ref, emit, reverse=True)


def _pass_a(x, mod, norm_w, w_a, wb, wc, lb, h0, rope, kv_dtype):
    batch, length, _ = x.shape
    ng, nt = batch // SEQS, length // TL
    rotary = rope is not None
    rev = lambda t: nt - 1 - t
    in_specs = [
        pl.BlockSpec((SEQS, TL, D_MODEL), lambda g, t: (g, rev(t), 0)),
        pl.BlockSpec((None, SEQS, 3 * D_MODEL), lambda g, t: (g, 0, 0)),
        _const_spec((1, D_MODEL)),
        _const_spec((D_MODEL, SSM_WIDTH + 2 * KV_WIDTH)),
        _const_spec((2, HALF_CH, 2 * HALF_ST)),
        _const_spec((2, 2 * HALF_ST, HALF_CH)),
        _const_spec((2, SEQS, ALL_ST)),
        pl.BlockSpec((None, 2, SEQS, ALL_ST), lambda g, t: (g, 0, 0, 0)),
    ]
    args = [x, mod, norm_w, w_a, wb, wc, lb, h0]
    if rotary:
        in_specs += [pl.BlockSpec((TL, 128), lambda g, t: (rev(t), 0))] * 2
        args += list(rope)
    out_specs = [
        pl.BlockSpec((SEQS, TL, KV_WIDTH), lambda g, t: (g, rev(t), 0)),
        pl.BlockSpec((SEQS, TL, KV_WIDTH), lambda g, t: (g, rev(t), 0)),
        pl.BlockSpec((None, None, ROWS, SSM_WIDTH), lambda g, t: (g, rev(t), 0, 0)),
        pl.BlockSpec((None, 2, SEQS, ALL_ST), lambda g, t: (g, 0, 0, 0)),
    ]
    out_shape = [
        jax.ShapeDtypeStruct((batch, length, KV_WIDTH), kv_dtype),
        jax.ShapeDtypeStruct((batch, length, KV_WIDTH), kv_dtype),
        jax.ShapeDtypeStruct((ng, nt, ROWS, SSM_WIDTH), F32),
        jax.ShapeDtypeStruct((ng, 2, SEQS, ALL_ST), F32),
    ]
    scratch = [
        pltpu.VMEM((ROWS, D_MODEL), BF16),
        pltpu.VMEM((SSM_WIDTH // 128, ROWS, 128), F32),
        pltpu.VMEM((ROWS, SSM_WIDTH), F32),
        pltpu.VMEM((TC * SEQS, 2 * HALF_ST), F32),
        pltpu.VMEM((2, SEQS, ALL_ST), F32),
    ]
    return pl.pallas_call(
        functools.partial(_pass_a_kernel, rotary=rotary),
        grid=(ng, nt),
        in_specs=in_specs, out_specs=out_specs, out_shape=out_shape, scratch_shapes=scratch,
        compiler_params=pltpu.CompilerParams(
            dimension_semantics=("arbitrary", "arbitrary"), vmem_limit_bytes=VMEM_LIMIT),
        name="pass_a_latent" if rotary else "pass_a_context",
    )(*args)


def _attention(b, q_s, z_s, mix_s, key_refs, val_refs, sink_ref, valid):
    r0 = pl.multiple_of(b * TL, TL)
    rows = pl.ds(r0, TL)
    for g in range(N_KV_HEADS):
        heads = [g * KV_REP + j for j in range(KV_REP)]
        lanes = slice(g * HEAD_DIM, (g + 1) * HEAD_DIM)
        qs = jnp.concatenate([q_s[rows, hd * HEAD_DIM:(hd + 1) * HEAD_DIM] for hd in heads], axis=0)
        keys = jnp.concatenate([r[b][:, lanes].astype(BF16) for r in key_refs], axis=0)
        vals = jnp.concatenate([r[b][:, lanes].astype(BF16) for r in val_refs], axis=0)
        s = lax.dot_general(qs, keys, (((1,), (1,)), ((), ())), preferred_element_type=F32)
        if valid is not None:
            s = jnp.where(valid, s, NEG_INF)
        sink = jnp.concatenate([jnp.full((TL, 1), sink_ref[hd], F32) for hd in heads], axis=0)
        m = jnp.maximum(jnp.max(s, axis=-1, keepdims=True), sink)
        p = jnp.exp(s - m)
        den = jnp.sum(p, axis=-1, keepdims=True) + jnp.exp(sink - m)
        o = jnp.dot(p.astype(BF16), vals, preferred_element_type=F32) / den
        for j, hd in enumerate(heads):
            c0 = ATTN_WIDTH + hd * HEAD_DIM
            gate = z_s[rows, c0:c0 + HEAD_DIM].astype(F32)
            mix_s[rows, c0:c0 + HEAD_DIM] = (o[j * TL:(j + 1) * TL] * gate).astype(BF16)


def _pass_b_kernel(*refs, windowed, n_tiles):
    if windowed:
        (sink_ref, x_ref, mod_ref, nw_ref, w_ref, wb_ref, wc_ref, lb_ref, h0_ref, d_ref, wg_ref, bg_ref,
         yb_ref, kx_ref, vx_ref, wo_ref, fw_ref, cos_ref, sin_ref,
         kp_ref, kc_ref, kn_ref, vp_ref, vc_ref, vn_ref,
         y_ref, fin_ref, h_s, u_sm, u_tm, s_s, carry, z_s, q_s, y_tm, mix_s) = refs
    else:
        (sink_ref, x_ref, mod_ref, nw_ref, w_ref, wb_ref, wc_ref, lb_ref, h0_ref, d_ref, wg_ref, bg_ref,
         yb_ref, kx_ref, vx_ref, wo_ref, fw_ref,
         y_ref, fin_ref, h_s, u_sm, u_tm, s_s, carry, z_s, q_s, y_tm, mix_s) = refs
    t = pl.program_id(1)

    @pl.when(t == 0)
    def _():
        carry[...] = h0_ref[...]

    _norm_mod(h_s, x_ref, mod_ref, nw_ref)
    h = h_s[...]
    _store_column_blocks(u_sm, jnp.dot(h, w_ref[:, 0:SSM_WIDTH], preferred_element_type=F32))
    z = jnp.dot(h, w_ref[:, SSM_WIDTH:2 * SSM_WIDTH], preferred_element_type=F32)
    z_s[:, 0:SSM_WIDTH] = jax.nn.silu(z).astype(BF16)
    z = jnp.dot(h, w_ref[:, 3 * SSM_WIDTH:4 * SSM_WIDTH], preferred_element_type=F32)
    z_s[:, SSM_WIDTH:2 * SSM_WIDTH] = jax.nn.silu(z).astype(BF16)
    for cg in range(ATTN_WIDTH // 128):
        c0 = 2 * SSM_WIDTH + cg * 128
        q = jnp.dot(h, w_ref[:, c0:c0 + 128], preferred_element_type=F32)
        if windowed:
            q = _rope(q, cos_ref[...], sin_ref[...])
        q_s[:, cg * 128:(cg + 1) * 128] = (q * (HEAD_DIM ** -0.5)).astype(BF16)

    _to_token_major(u_tm, u_sm)

    def emit(r0, nrows, half, y):
        cols = slice(half * HALF_CH, (half + 1) * HALF_CH)
        y_tm[r0:r0 + nrows, cols] = y + yb_ref[r0:r0 + nrows, cols] + u_tm[r0:r0 + nrows, cols] * d_ref[:, cols]

    _s5_direction(u_tm, s_s, carry, wb_ref, wc_ref, lb_ref, fin_ref, emit, reverse=False)

    glu_rows = 256
    for rc in range(ROWS // glu_rows):
        rr = slice(rc * glu_rows, (rc + 1) * glu_rows)
        y = jax.nn.gelu(y_tm[rr, :])
        gl = jnp.dot(y.astype(BF16), wg_ref[...], preferred_element_type=F32) + bg_ref[...]
        y_tm[rr, :] = y * jax.nn.sigmoid(gl)
    _to_sequence_major(u_sm, y_tm)
    for cb in range(SSM_WIDTH // 128):
        cols = slice(cb * 128, (cb + 1) * 128)
        mix_s[:, cols] = (u_sm[cb] * z_s[:, cols].astype(F32)).astype(BF16)

    if windowed:
        ri = lax.broadcasted_iota(jnp.int32, (KV_REP * TL, 3 * TL + PAST_LEN), 0) & (TL - 1)
        cj = lax.broadcasted_iota(jnp.int32, (KV_REP * TL, 3 * TL + PAST_LEN), 1)
        prev_ok = (cj < TL) & (cj >= ri) & (t > 0)
        next_ok = (cj >= 2 * TL) & (cj < 3 * TL) & ((cj - 2 * TL) <= ri) & (t < n_tiles - 1)
        valid = prev_ok | ((cj >= TL) & (cj < 2 * TL)) | next_ok | (cj >= 3 * TL)
        key_refs = (kp_ref, kc_ref, kn_ref, kx_ref)
        val_refs = (vp_ref, vc_ref, vn_ref, vx_ref)
    else:
        valid = None
        key_refs = (kx_ref,)
        val_refs = (vx_ref,)

    def attn_body(b, carry_):
        _attention(b, q_s, z_s, mix_s, key_refs, val_refs, sink_ref, valid)
        return carry_

    lax.fori_loop(0, SEQS, attn_body, 0)

    y_ref[...] = jnp.dot(mix_s[...], wo_ref[...], preferred_element_type=F32).reshape(SEQS, TL, D_MODEL)
    fw = fw_ref[...]

    def fin_body(b, carry_):
        gate = mod_ref[pl.ds(b, 1), 2 * D_MODEL:3 * D_MODEL]
        r = x_ref[b] + gate * y_ref[b]
        ms = jnp.mean(r * r, axis=-1, keepdims=True)
        y_ref[b] = (r * lax.rsqrt(ms + EPS)) * fw
        return carry_

    lax.fori_loop(0, SEQS, fin_body, 0)


def _pass_b(x, mod, norm_w, w_b, wb, wc, lb, h0, d, w_glu, b_glu, yb, kx, vx, w_out, fnorm_w, sink,
            rope, k_loc, v_loc):
    batch, length, _ = x.shape
    ng, nt = batch // SEQS, length // TL
    windowed = rope is not None
    in_specs = [
        pl.BlockSpec(memory_space=pltpu.SMEM),
        pl.BlockSpec((SEQS, TL, D_MODEL), lambda g, t: (g, t, 0)),
        pl.BlockSpec((None, SEQS, 3 * D_MODEL), lambda g, t: (g, 0, 0)),
        _const_spec((1, D_MODEL)),
        _const_spec((D_MODEL, 4 * SSM_WIDTH)),
        _const_spec((2, HALF_CH, 2 * HALF_ST)),
        _const_spec((2, 2 * HALF_ST, HALF_CH)),
        _const_spec((2, SEQS, ALL_ST)),
        pl.BlockSpec((None, 2, SEQS, ALL_ST), lambda g, t: (g, 0, 0, 0)),
        _const_spec((1, SSM_WIDTH)),
        _const_spec((SSM_WIDTH, SSM_WIDTH)),
        _const_spec((1, SSM_WIDTH)),
        pl.BlockSpec((None, None, ROWS, SSM_WIDTH), lambda g, t: (g, t, 0, 0)),
        pl.BlockSpec((SEQS, PAST_LEN, KV_WIDTH), lambda g, t: (g, 0, 0)),
        pl.BlockSpec((SEQS, PAST_LEN, KV_WIDTH), lambda g, t: (g, 0, 0)),
        _const_spec((D_MODEL, D_MODEL)),
        _const_spec((1, D_MODEL)),
    ]
    args = [sink, x, mod, norm_w, w_b, wb, wc, lb, h0, d, w_glu, b_glu, yb, kx, vx, w_out, fnorm_w]
    if windowed:
        in_specs += [pl.BlockSpec((TL, 128), lambda g, t: (t, 0))] * 2
        args += list(rope)
        band = [
            pl.BlockSpec((SEQS, TL, KV_WIDTH), lambda g, t: (g, jnp.maximum(t - 1, 0), 0)),
            pl.BlockSpec((SEQS, TL, KV_WIDTH), lambda g, t: (g, t, 0)),
            pl.BlockSpec((SEQS, TL, KV_WIDTH), lambda g, t: (g, jnp.minimum(t + 1, nt - 1), 0)),
        ]
        in_specs += band + band
        args += [k_loc] * 3 + [v_loc] * 3
    out_specs = [
        pl.BlockSpec((SEQS, TL, D_MODEL), lambda g, t: (g, t, 0)),
        pl.BlockSpec((None, 2, SEQS, ALL_ST), lambda g, t: (g, 0, 0, 0)),
    ]
    out_shape = [
        jax.ShapeDtypeStruct((batch, length, D_MODEL), F32),
        jax.ShapeDtypeStruct((ng, 2, SEQS, ALL_ST), F32),
    ]
    scratch = [
        pltpu.VMEM((ROWS, D_MODEL), BF16),
        pltpu.VMEM((SSM_WIDTH // 128, ROWS, 128), F32),
        pltpu.VMEM((ROWS, SSM_WIDTH), F32),
        pltpu.VMEM((TC * SEQS, 2 * HALF_ST), F32),
        pltpu.VMEM((2, SEQS, ALL_ST), F32),
        pltpu.VMEM((ROWS, 2 * SSM_WIDTH), BF16),
        pltpu.VMEM((ROWS, ATTN_WIDTH), BF16),
        pltpu.VMEM((ROWS, SSM_WIDTH), F32),
        pltpu.VMEM((ROWS, D_MODEL), BF16),
    ]
    return pl.pallas_call(
        functools.partial(_pass_b_kernel, windowed=windowed, n_tiles=nt),
        grid=(ng, nt),
        in_specs=in_specs, out_specs=out_specs, out_shape=out_shape, scratch_shapes=scratch,
        compiler_params=pltpu.CompilerParams(
            dimension_semantics=("arbitrary", "arbitrary"), vmem_limit_bytes=VMEM_LIMIT),
        name="pass_b_latent" if windowed else "pass_b_context",
    )(*args)


def kernel(x_prompt, x_sample, c, cache_k, cache_v, state_ssm_re, state_ssm_im, c_ctx, norm_w, w_mod, b_mod, w_in, ssm_lambda_re, ssm_lambda_im, ssm_log_dt, ssm_b_re, ssm_b_im, ssm_c_re, ssm_c_im, ssm_d, w_glu, b_glu, attn_sink, w_out, final_norm_w):
    assert norm_w.shape[0] == 1, "single trunk layer"
    batch, seq, _ = x_prompt.shape
    dec_batch, dec_seq, _ = x_sample.shape
    assert dec_batch == SEQS and batch % SEQS == 0 and seq % TL == 0 and dec_seq % TL == 0
    assert seq == PAST_LEN and cache_k.shape[2] == PAST_LEN
    ng = batch // SEQS

    w_in0 = w_in[0].astype(BF16)
    u0, z0, q0, k0, v0, za0 = 0, 512, 1024, 1536, 1664, 1792
    w_a = jnp.concatenate([w_in0[:, u0:z0], w_in0[:, k0:za0]], axis=1)
    w_b = jnp.concatenate([w_in0[:, u0:k0], w_in0[:, za0:]], axis=1)
    nw = norm_w[0][None, :]
    fw = final_norm_w[None, :]
    d = ssm_d[0][None, :]
    wg = w_glu[0].astype(BF16)
    bg = b_glu[0][None, :]
    wo = w_out[0].astype(BF16)
    sink = attn_sink[0]

    cond = jnp.concatenate([c, c_ctx[None, :], jnp.zeros((16 - SEQS - 1, D_MODEL), F32)], axis=0)
    mod = _modulation(cond, w_mod[0].astype(BF16), b_mod[0][None, :])
    mod_lat = mod[None, 0:SEQS]
    mod_ctx = jnp.broadcast_to(mod[None, SEQS:SEQS + 1], (ng, SEQS, 3 * D_MODEL))

    lbar_re, lbar_im, bb_re, bb_im = _discretize(
        ssm_lambda_re[0], ssm_lambda_im[0], ssm_log_dt[0], ssm_b_re[0], ssm_b_im[0])
    wb, wc = _block_diag_weights(bb_re, bb_im, ssm_c_re[0], ssm_c_im[0])
    lb = jnp.stack([lbar_re.reshape(2, ALL_ST), lbar_im.reshape(2, ALL_ST)], axis=1)
    lb = jnp.broadcast_to(lb[:, :, None, :], (2, 2, SEQS, ALL_ST))

    h0_lat = jnp.stack([state_ssm_re[:, 0].reshape(SEQS, 2, ALL_ST),
                        state_ssm_im[:, 0].reshape(SEQS, 2, ALL_ST)], axis=0)
    h0_lat = jnp.transpose(h0_lat, (2, 0, 1, 3))[:, None]
    h0_ctx = jnp.zeros((2, ng, 2, SEQS, ALL_ST), F32)
    rope = _rope_tables(dec_seq)
    FWD, BWD = 0, 1

    k_ctx, v_ctx, yb_ctx, fin_b = _pass_a(x_prompt, mod_ctx, nw, w_a, wb[BWD], wc[BWD], lb[BWD],
                                          h0_ctx[BWD], None, F32)
    y_prompt, fin_f = _pass_b(x_prompt, mod_ctx, nw, w_b, wb[FWD], wc[FWD], lb[FWD], h0_ctx[FWD], d, wg, bg,
                              yb_ctx, k_ctx, v_ctx, wo, fw, sink, None, None, None)

    k_lat, v_lat, yb_lat, _ = _pass_a(x_sample, mod_lat, nw, w_a, wb[BWD], wc[BWD], lb[BWD],
                                      h0_lat[BWD], rope, BF16)
    kx = cache_k[:, 0].reshape(dec_batch, PAST_LEN, KV_WIDTH)
    vx = cache_v[:, 0].reshape(dec_batch, PAST_LEN, KV_WIDTH)
    y_sample, _ = _pass_b(x_sample, mod_lat, nw, w_b, wb[FWD], wc[FWD], lb[FWD], h0_lat[FWD], d, wg, bg,
                          yb_lat, kx, vx, wo, fw, sink, rope, k_lat, v_lat)

    new_cache_k = k_ctx.reshape(batch, 1, seq, N_KV_HEADS, HEAD_DIM)
    new_cache_v = v_ctx.reshape(batch, 1, seq, N_KV_HEADS, HEAD_DIM)

    def states(fin, part):
        return fin[:, part].reshape(batch, SSM_GROUPS, SSM_STATE)

    new_re = jnp.stack([states(fin_f, 0), states(fin_b, 0)], axis=1)[:, None]
    new_im = jnp.stack([states(fin_f, 1), states(fin_b, 1)], axis=1)[:, None]
    return (y_prompt, y_sample, new_cache_k, new_cache_v, new_re, new_im)
```

```python
import functools
import math

import jax
import jax.numpy as jnp
from jax import lax
from jax.experimental import pallas as pl
from jax.experimental.pallas import tpu as pltpu

F32 = jnp.float32
BF16 = jnp.bfloat16

D_MODEL = 1024
SSM_WIDTH = 512
ATTN_WIDTH = 512
SSM_GROUP = 16
SSM_GROUPS = 32
SSM_STATE = 64
HEAD_DIM = 64
N_HEADS = 8
N_KV_HEADS = 2
KV_REP = 4
KV_WIDTH = 128
GRID_W = 64
ROPE_AXIS_DIM = 32
ROPE_BASE = 10000.0
EPS = 1e-6
LAMBDA_RE_MAX = -1e-4
NEG_INF = -1e30
PAST_LEN = 256

SEQS = 8
TL = 128
ROWS = SEQS * TL
HALF_CH = SSM_WIDTH // 2
HALF_ST = (SSM_GROUPS // 2) * SSM_STATE
ALL_ST = 2 * HALF_ST
TC = 64
SCAN_LANES = 512
VMEM_LIMIT = 58 * 1024 * 1024


def _const_spec(shape):
    zeros = (0,) * len(shape)
    return pl.BlockSpec(shape, lambda g, t: zeros, pipeline_mode=pl.Buffered(1))


def _modulation_kernel(cond_ref, w_ref, b_ref, out_ref):
    a = jax.nn.silu(cond_ref[...]).astype(BF16)
    out_ref[...] = jnp.dot(a, w_ref[...], preferred_element_type=F32) + b_ref[...]


def _modulation(cond, w_mod_bf16, b_mod):
    rows = cond.shape[0]
    n = w_mod_bf16.shape[1]
    tn = 1024
    return pl.pallas_call(
        _modulation_kernel,
        grid=(n // tn,),
        in_specs=[pl.BlockSpec((rows, D_MODEL), lambda j: (0, 0)),
                  pl.BlockSpec((D_MODEL, tn), lambda j: (0, j)),
                  pl.BlockSpec((1, tn), lambda j: (0, j))],
        out_specs=pl.BlockSpec((rows, tn), lambda j: (0, j)),
        out_shape=jax.ShapeDtypeStruct((rows, n), F32),
        name="modulation",
    )(cond, w_mod_bf16, b_mod)


def _discretize_kernel(lre_ref, lim_ref, ldt_ref, bre_ref, bim_ref,
                       lbre_ref, lbim_ref, bbre_ref, bbim_ref):
    lam_re = jnp.minimum(lre_ref[...], LAMBDA_RE_MAX)
    lam_im = lim_ref[...]
    dt = jnp.exp(ldt_ref[...])
    mag = jnp.exp(lam_re * dt)
    ang = lam_im * dt
    lbar_re = mag * jnp.cos(ang)
    lbar_im = mag * jnp.sin(ang)
    nr = lbar_re - 1.0
    ni = lbar_im
    den = lam_re * lam_re + lam_im * lam_im
    f_re = (nr * lam_re + ni * lam_im) / den
    f_im = (ni * lam_re - nr * lam_im) / den
    b_re = bre_ref[...]
    b_im = bim_ref[...]
    lbre_ref[...] = lbar_re
    lbim_ref[...] = lbar_im
    bbre_ref[...] = f_re * b_re - f_im * b_im
    bbim_ref[...] = f_re * b_im + f_im * b_re


def _discretize(lam_re, lam_im, log_dt, b_re, b_im):
    n = 2 * SSM_GROUPS * SSM_GROUP
    expand = lambda a: jnp.broadcast_to(a[:, :, None, :], (2, SSM_GROUPS, SSM_GROUP, SSM_STATE)).reshape(n, SSM_STATE)
    ldt = jnp.broadcast_to(log_dt[:, :, None, None], (2, SSM_GROUPS, SSM_GROUP, SSM_STATE)).reshape(n, SSM_STATE)
    bt = lambda a: jnp.transpose(a, (0, 1, 3, 2)).reshape(n, SSM_STATE)
    shp = jax.ShapeDtypeStruct((n, SSM_STATE), F32)
    lbre, lbim, bbre, bbim = pl.pallas_call(
        _discretize_kernel, out_shape=(shp, shp, shp, shp), name="discretize",
    )(expand(lam_re), expand(lam_im), ldt, bt(b_re), bt(b_im))
    r4 = lambda a: a.reshape(2, SSM_GROUPS, SSM_GROUP, SSM_STATE)
    return r4(lbre)[:, :, 0, :], r4(lbim)[:, :, 0, :], r4(bbre), r4(bbim)


def _block_diag_weights(bb_re, bb_im, c_re, c_im):
    gh = SSM_GROUPS // 2
    eye = jnp.eye(gh, dtype=F32)

    def in_blocks(a):
        a = a.reshape(2, 2, gh, SSM_GROUP, SSM_STATE)
        return jnp.einsum('ij,dhicp->dhicjp', eye, a).reshape(2, 2, gh * SSM_GROUP, gh * SSM_STATE)

    def out_blocks(a):
        a = a.reshape(2, 2, gh, SSM_GROUP, SSM_STATE)
        return jnp.einsum('ij,dhicp->dhipjc', eye, a).reshape(2, 2, gh * SSM_STATE, gh * SSM_GROUP)

    wb = jnp.concatenate([in_blocks(bb_re), in_blocks(bb_im)], axis=-1).astype(BF16)
    wc = jnp.concatenate([out_blocks(c_re), out_blocks(-c_im)], axis=-2).astype(BF16)
    return wb, wc


def _rope_tables(length):
    pos = jnp.arange(length)
    row = (pos // GRID_W).astype(F32)
    col = (pos % GRID_W).astype(F32)
    freqs = ROPE_BASE ** (-jnp.arange(0, ROPE_AXIS_DIM, 2, dtype=F32) / ROPE_AXIS_DIM)
    ang_r = row[:, None] * freqs[None, :]
    ang_c = col[:, None] * freqs[None, :]
    cos = jnp.concatenate([jnp.cos(ang_r), jnp.cos(ang_r), jnp.cos(ang_c), jnp.cos(ang_c)], axis=-1)
    sin = jnp.concatenate([-jnp.sin(ang_r), jnp.sin(ang_r), -jnp.sin(ang_c), jnp.sin(ang_c)], axis=-1)
    return jnp.tile(cos, (1, 2)), jnp.tile(sin, (1, 2))


def _norm_mod(h_ref, x_ref, mod_ref, nw_ref):
    nw = nw_ref[...]

    def body(b, carry):
        xb = x_ref[b]
        ms = jnp.mean(xb * xb, axis=-1, keepdims=True)
        hn = (xb * lax.rsqrt(ms + EPS)) * nw
        shift = mod_ref[pl.ds(b, 1), 0:D_MODEL]
        scale = mod_ref[pl.ds(b, 1), D_MODEL:2 * D_MODEL]
        r0 = pl.multiple_of(b * TL, TL)
        h_ref[pl.ds(r0, TL), :] = (hn * (1.0 + scale) + shift).astype(BF16)
        return carry

    lax.fori_loop(0, SEQS, body, 0)


def _rope(x, cos, sin):
    lane = lax.broadcasted_iota(jnp.int32, x.shape, 1)
    first = (lane & 31) < 16
    partner = jnp.where(first, pltpu.roll(x, 128 - 16, 1), pltpu.roll(x, 16, 1))
    x3 = x.reshape(SEQS, TL, 128)
    p3 = partner.reshape(SEQS, TL, 128)
    return (x3 * cos[None] + p3 * sin[None]).reshape(ROWS, 128)


def _to_token_major(dst_ref, src_ref):
    def body(l, carry):
        r0 = pl.multiple_of(l * SEQS, SEQS)
        for cb in range(SSM_WIDTH // 128):
            dst_ref[pl.ds(r0, SEQS), cb * 128:(cb + 1) * 128] = src_ref[cb, pl.ds(l, SEQS, stride=TL), :]
        return carry
    lax.fori_loop(0, TL, body, 0, unroll=4)


def _to_sequence_major(dst_ref, src_ref):
    def body(l, carry):
        r0 = pl.multiple_of(l * SEQS, SEQS)
        for cb in range(SSM_WIDTH // 128):
            dst_ref[cb, pl.ds(l, SEQS, stride=TL), :] = src_ref[pl.ds(r0, SEQS), cb * 128:(cb + 1) * 128]
        return carry
    lax.fori_loop(0, TL, body, 0, unroll=4)


def _store_column_blocks(dst_ref, val):
    for cb in range(SSM_WIDTH // 128):
        dst_ref[cb] = val[:, cb * 128:(cb + 1) * 128]


def _scan_chunk(s_ref, lb_ref, carry_ref, half, reverse):
    for cb in range(HALF_ST // SCAN_LANES):
        c0 = cb * SCAN_LANES
        g0 = half * HALF_ST + c0
        lr = lb_ref[0, :, g0:g0 + SCAN_LANES]
        li = lb_ref[1, :, g0:g0 + SCAN_LANES]
        sr = carry_ref[0, :, g0:g0 + SCAN_LANES]
        si = carry_ref[1, :, g0:g0 + SCAN_LANES]

        def step(i, state, c0=c0, lr=lr, li=li):
            sr, si = state
            tt = (TC - 1 - i) if reverse else i
            rows = pl.ds(pl.multiple_of(tt * SEQS, SEQS), SEQS)
            br = s_ref[rows, c0:c0 + SCAN_LANES]
            bi = s_ref[rows, HALF_ST + c0:HALF_ST + c0 + SCAN_LANES]
            nr = lr * sr - li * si + br
            ni = lr * si + li * sr + bi
            s_ref[rows, c0:c0 + SCAN_LANES] = nr
            s_ref[rows, HALF_ST + c0:HALF_ST + c0 + SCAN_LANES] = ni
            return nr, ni

        sr, si = lax.fori_loop(0, TC, step, (sr, si), unroll=2)
        carry_ref[0, :, g0:g0 + SCAN_LANES] = sr
        carry_ref[1, :, g0:g0 + SCAN_LANES] = si


def _s5_direction(u_tm_ref, s_ref, carry_ref, wb_ref, wc_ref, lb_ref, first_ref, emit, reverse):
    n_chunks = TL // TC
    chunk_rows = TC * SEQS
    first_rows = slice((TC - 1) * SEQS, TC * SEQS) if reverse else slice(0, SEQS)
    for half in range(2):
        for ci in range(n_chunks):
            c = (n_chunks - 1 - ci) if reverse else ci
            r0 = c * chunk_rows
            uc = u_tm_ref[r0:r0 + chunk_rows, half * HALF_CH:(half + 1) * HALF_CH].astype(BF16)
            s_ref[...] = jnp.dot(uc, wb_ref[half], preferred_element_type=F32)
            _scan_chunk(s_ref, lb_ref, carry_ref, half, reverse)
            if ci == 0:
                @pl.when(pl.program_id(1) == 0)
                def _(half=half):
                    cols = slice(half * HALF_ST, (half + 1) * HALF_ST)
                    first_ref[0, :, cols] = s_ref[first_rows, 0:HALF_ST]
                    first_ref[1, :, cols] = s_ref[first_rows, HALF_ST:2 * HALF_ST]
            y = jnp.dot(s_ref[...].astype(BF16), wc_ref[half], preferred_element_type=F32)
            emit(r0, chunk_rows, half, y)


def _pass_a_kernel(*refs, rotary):
    if rotary:
        (x_ref, mod_ref, nw_ref, w_ref, wb_ref, wc_ref, lb_ref, h0_ref, cos_ref, sin_ref,
         kcat_ref, vt_ref, yb_ref, fin_ref, h_s, u_sm, u_tm, s_s, carry) = refs
    else:
        (x_ref, mod_ref, nw_ref, w_ref, wb_ref, wc_ref, lb_ref, h0_ref,
         k_ref, v_ref, kcat_ref, vt_ref, yb_ref, fin_ref, h_s, u_sm, u_tm, s_s, carry) = refs

    @pl.when(pl.program_id(1) == 0)
    def _():
        carry[...] = h0_ref[...]

    _norm_mod(h_s, x_ref, mod_ref, nw_ref)
    ukv = jnp.dot(h_s[...], w_ref[...], preferred_element_type=F32)
    _store_column_blocks(u_sm, ukv[:, 0:SSM_WIDTH])
    k = ukv[:, SSM_WIDTH:SSM_WIDTH + KV_WIDTH]
    v = ukv[:, SSM_WIDTH + KV_WIDTH:SSM_WIDTH + 2 * KV_WIDTH]
    if rotary:
        k = _rope(k, cos_ref[...], sin_ref[...])
    else:
        k_ref[...] = k.reshape(SEQS, TL, KV_WIDTH)
        v_ref[...] = v.reshape(SEQS, TL, KV_WIDTH)
    kcat = jnp.concatenate([k, pltpu.roll(k, HEAD_DIM, 1)], axis=1).astype(BF16)
    kcat_ref[...] = kcat.reshape(SEQS, TL, 2 * KV_WIDTH)
    for b in range(SEQS):
        vt_ref[b] = v[b * TL:(b + 1) * TL, :].T.astype(BF16)

    _to_token_major(u_tm, u_sm)

    def emit(r0, nrows, half, y):
        yb_ref[r0:r0 + nrows, half * HALF_CH:(half + 1) * HALF_CH] = y

    _s5_direction(u_tm, s_s, carry, wb_ref, wc_ref, lb_ref, fin_ref, emit, reverse=True)


def _pass_a(x, mod, norm_w, w_a, wb, wc, lb, h0, rope):
    batch, length, _ = x.shape
    ng, nt = batch // SEQS, length // TL
    rotary = rope is not None
    rev = lambda t: nt - 1 - t
    in_specs = [
        pl.BlockSpec((SEQS, TL, D_MODEL), lambda g, t: (g, rev(t), 0)),
        pl.BlockSpec((None, SEQS, 3 * D_MODEL), lambda g, t: (g, 0, 0)),
        _const_spec((1, D_MODEL)),
        _const_spec((D_MODEL, SSM_WIDTH + 2 * KV_WIDTH)),
        _const_spec((2, HALF_CH, 2 * HALF_ST)),
        _const_spec((2, 2 * HALF_ST, HALF_CH)),
        _const_spec((2, SEQS, ALL_ST)),
        pl.BlockSpec((None, 2, SEQS, ALL_ST), lambda g, t: (g, 0, 0, 0)),
    ]
    args = [x, mod, norm_w, w_a, wb, wc, lb, h0]
    if rotary:
        in_specs += [pl.BlockSpec((TL, 128), lambda g, t: (rev(t), 0))] * 2
        args += list(rope)
    out_specs = [
        pl.BlockSpec((SEQS, TL, 2 * KV_WIDTH), lambda g, t: (g, rev(t), 0)),
        pl.BlockSpec((SEQS, KV_WIDTH, TL), lambda g, t: (g, 0, rev(t))),
        pl.BlockSpec((None, None, ROWS, SSM_WIDTH), lambda g, t: (g, rev(t), 0, 0)),
        pl.BlockSpec((None, 2, SEQS, ALL_ST), lambda g, t: (g, 0, 0, 0)),
    ]
    out_shape = [
        jax.ShapeDtypeStruct((batch, length, 2 * KV_WIDTH), BF16),
        jax.ShapeDtypeStruct((batch, KV_WIDTH, length), BF16),
        jax.ShapeDtypeStruct((ng, nt, ROWS, SSM_WIDTH), F32),
        jax.ShapeDtypeStruct((ng, 2, SEQS, ALL_ST), F32),
    ]
    if not rotary:
        cache_spec = pl.BlockSpec((SEQS, TL, KV_WIDTH), lambda g, t: (g, rev(t), 0))
        cache_shape = jax.ShapeDtypeStruct((batch, length, KV_WIDTH), F32)
        out_specs = [cache_spec, cache_spec] + out_specs
        out_shape = [cache_shape, cache_shape] + out_shape
    scratch = [
        pltpu.VMEM((ROWS, D_MODEL), BF16),
        pltpu.VMEM((SSM_WIDTH // 128, ROWS, 128), F32),
        pltpu.VMEM((ROWS, SSM_WIDTH), F32),
        pltpu.VMEM((TC * SEQS, 2 * HALF_ST), F32),
        pltpu.VMEM((2, SEQS, ALL_ST), F32),
    ]
    return pl.pallas_call(
        functools.partial(_pass_a_kernel, rotary=rotary),
        grid=(ng, nt),
        in_specs=in_specs, out_specs=out_specs, out_shape=out_shape, scratch_shapes=scratch,
        compiler_params=pltpu.CompilerParams(
            dimension_semantics=("arbitrary", "arbitrary"), vmem_limit_bytes=VMEM_LIMIT),
        name="pass_a_latent" if rotary else "pass_a_context",
    )(*args)


def _attention(b, q_refs, z_s, mix_s, kcat_refs, vt_refs, sink_ref, valid):
    rows = pl.ds(pl.multiple_of(b * TL, TL), TL)
    kcat = jnp.concatenate([r[b] for r in kcat_refs], axis=0)
    vt = jnp.concatenate([r[b] for r in vt_refs], axis=1)
    lane = lax.broadcasted_iota(jnp.int32, (1, 2 * TL), 1)
    units = [(g, side) for g in range(N_KV_HEADS) for side in range(2)]
    scores = []
    for g, side in units:
        qg = jnp.concatenate([q_refs[side][rows, blk * 128:(blk + 1) * 128] for blk in (2 * g, 2 * g + 1)], axis=0)
        kg = kcat[:, 0:KV_WIDTH] if g == side else kcat[:, KV_WIDTH:2 * KV_WIDTH]
        scores.append(lax.dot_general(kg, qg, (((1,), (1,)), ((), ())), preferred_element_type=F32))
    probs, dens = [], []
    for (g, side), s in zip(units, scores):
        if valid is not None:
            s = jnp.where(valid, s, NEG_INF)
        sink = jnp.where(lane < TL, sink_ref[4 * g + side], sink_ref[4 * g + 2 + side])
        m = jnp.maximum(jnp.max(s, axis=0, keepdims=True), sink)
        p = jnp.exp(s - m)
        dens.append(jnp.sum(p, axis=0, keepdims=True) + jnp.exp(sink - m))
        probs.append(p.astype(BF16))
    out_t = {}
    for (g, side), p, den in zip(units, probs, dens):
        ot = jnp.dot(vt[g * HEAD_DIM:(g + 1) * HEAD_DIM, :], p, preferred_element_type=F32)
        out_t[g, side] = ot / den
    for g in range(N_KV_HEADS):
        for i, blk in enumerate((2 * g, 2 * g + 1)):
            lanes = slice(i * TL, (i + 1) * TL)
            pair_t = jnp.concatenate([out_t[g, 0][:, lanes], out_t[g, 1][:, lanes]], axis=0)
            c0 = ATTN_WIDTH + blk * 128
            mix_s[rows, c0:c0 + 128] = (pair_t.T * z_s[rows, c0:c0 + 128].astype(F32)).astype(BF16)


def _pass_b_kernel(*refs, windowed, n_tiles):
    if windowed:
        (sink_ref, x_ref, mod_ref, nw_ref, w_ref, wb_ref, wc_ref, lb_ref, h0_ref, d_ref, wg_ref, bg_ref,
         yb_ref, kx_ref, vx_ref, wo_ref, fw_ref, cos_ref, sin_ref,
         kp_ref, kc_ref, kn_ref, vp_ref, vc_ref, vn_ref,
         y_ref, fin_ref, h_s, u_sm, u_tm, s_s, carry, z_s, ql_s, qr_s) = refs
    else:
        (sink_ref, x_ref, mod_ref, nw_ref, w_ref, wb_ref, wc_ref, lb_ref, h0_ref, d_ref, wg_ref, bg_ref,
         yb_ref, kx_ref, vx_ref, wo_ref, fw_ref,
         y_ref, fin_ref, h_s, u_sm, u_tm, s_s, carry, z_s, ql_s, qr_s) = refs
    t = pl.program_id(1)
    y_tm = u_tm
    mix_s = h_s

    @pl.when(t == 0)
    def _():
        carry[...] = h0_ref[...]

    _norm_mod(h_s, x_ref, mod_ref, nw_ref)
    h = h_s[...]
    _store_column_blocks(u_sm, jnp.dot(h, w_ref[:, 0:SSM_WIDTH], preferred_element_type=F32))
    z = jnp.dot(h, w_ref[:, SSM_WIDTH:2 * SSM_WIDTH], preferred_element_type=F32)
    z_s[:, 0:SSM_WIDTH] = jax.nn.silu(z).astype(BF16)
    z = jnp.dot(h, w_ref[:, 3 * SSM_WIDTH:4 * SSM_WIDTH], preferred_element_type=F32)
    z_s[:, SSM_WIDTH:2 * SSM_WIDTH] = jax.nn.silu(z).astype(BF16)
    qall = jnp.dot(h, w_ref[:, 2 * SSM_WIDTH:3 * SSM_WIDTH], preferred_element_type=F32)
    even_head = lax.broadcasted_iota(jnp.int32, (ROWS, 128), 1) < HEAD_DIM
    for cg in range(ATTN_WIDTH // 128):
        cols = slice(cg * 128, (cg + 1) * 128)
        q = qall[:, cols]
        if windowed:
            q = _rope(q, cos_ref[...], sin_ref[...])
        q = q * (HEAD_DIM ** -0.5)
        ql_s[:, cols] = jnp.where(even_head, q, 0.0).astype(BF16)
        qr_s[:, cols] = jnp.where(even_head, 0.0, q).astype(BF16)

    _to_token_major(u_tm, u_sm)

    def emit(r0, nrows, half, y):
        cols = slice(half * HALF_CH, (half + 1) * HALF_CH)
        y_tm[r0:r0 + nrows, cols] = y + yb_ref[r0:r0 + nrows, cols] + u_tm[r0:r0 + nrows, cols] * d_ref[:, cols]

    _s5_direction(u_tm, s_s, carry, wb_ref, wc_ref, lb_ref, fin_ref, emit, reverse=False)

    glu_rows = 256
    for rc in range(ROWS // glu_rows):
        rr = slice(rc * glu_rows, (rc + 1) * glu_rows)
        y = jax.nn.gelu(y_tm[rr, :])
        gl = jnp.dot(y.astype(BF16), wg_ref[...], preferred_element_type=F32) + bg_ref[...]
        y_tm[rr, :] = y * jax.nn.sigmoid(gl)
    _to_sequence_major(u_sm, y_tm)
    for cb in range(SSM_WIDTH // 128):
        cols = slice(cb * 128, (cb + 1) * 128)
        mix_s[:, cols] = (u_sm[cb] * z_s[:, cols].astype(F32)).astype(BF16)

    if windowed:
        kj = lax.broadcasted_iota(jnp.int32, (3 * TL + PAST_LEN, 2 * TL), 0)
        qi = lax.broadcasted_iota(jnp.int32, (3 * TL + PAST_LEN, 2 * TL), 1) & (TL - 1)
        prev_ok = (kj < TL) & (kj >= qi) & (t > 0)
        next_ok = (kj >= 2 * TL) & (kj < 3 * TL) & ((kj - 2 * TL) <= qi) & (t < n_tiles - 1)
        valid = prev_ok | ((kj >= TL) & (kj < 2 * TL)) | next_ok | (kj >= 3 * TL)
        key_refs = (kp_ref, kc_ref, kn_ref, kx_ref)
        val_refs = (vp_ref, vc_ref, vn_ref, vx_ref)
    else:
        valid = None
        key_refs = (kx_ref,)
        val_refs = (vx_ref,)

    def attn_body(b, carry_):
        _attention(b, (ql_s, qr_s), z_s, mix_s, key_refs, val_refs, sink_ref, valid)
        return carry_

    lax.fori_loop(0, SEQS, attn_body, 0)

    y_ref[...] = jnp.dot(mix_s[...], wo_ref[...], preferred_element_type=F32).reshape(SEQS, TL, D_MODEL)
    fw = fw_ref[...]

    def fin_body(b, carry_):
        gate = mod_ref[pl.ds(b, 1), 2 * D_MODEL:3 * D_MODEL]
        r = x_ref[b] + gate * y_ref[b]
        ms = jnp.mean(r * r, axis=-1, keepdims=True)
        y_ref[b] = (r * lax.rsqrt(ms + EPS)) * fw
        return carry_

    lax.fori_loop(0, SEQS, fin_body, 0)


def _pass_b(x, mod, norm_w, w_b, wb, wc, lb, h0, d, w_glu, b_glu, yb, kx, vx, w_out, fnorm_w, sink,
            rope, k_loc, v_loc):
    batch, length, _ = x.shape
    ng, nt = batch // SEQS, length // TL
    windowed = rope is not None
    in_specs = [
        pl.BlockSpec(memory_space=pltpu.SMEM),
        pl.BlockSpec((SEQS, TL, D_MODEL), lambda g, t: (g, t, 0)),
        pl.BlockSpec((None, SEQS, 3 * D_MODEL), lambda g, t: (g, 0, 0)),
        _const_spec((1, D_MODEL)),
        _const_spec((D_MODEL, 4 * SSM_WIDTH)),
        _const_spec((2, HALF_CH, 2 * HALF_ST)),
        _const_spec((2, 2 * HALF_ST, HALF_CH)),
        _const_spec((2, SEQS, ALL_ST)),
        pl.BlockSpec((None, 2, SEQS, ALL_ST), lambda g, t: (g, 0, 0, 0)),
        _const_spec((1, SSM_WIDTH)),
        _const_spec((SSM_WIDTH, SSM_WIDTH)),
        _const_spec((1, SSM_WIDTH)),
        pl.BlockSpec((None, None, ROWS, SSM_WIDTH), lambda g, t: (g, t, 0, 0)),
        pl.BlockSpec((SEQS, PAST_LEN, 2 * KV_WIDTH), lambda g, t: (g, 0, 0)),
        pl.BlockSpec((SEQS, KV_WIDTH, PAST_LEN), lambda g, t: (g, 0, 0)),
        _const_spec((D_MODEL, D_MODEL)),
        _const_spec((1, D_MODEL)),
    ]
    args = [sink, x, mod, norm_w, w_b, wb, wc, lb, h0, d, w_glu, b_glu, yb, kx, vx, w_out, fnorm_w]
    if windowed:
        in_specs += [pl.BlockSpec((TL, 128), lambda g, t: (t, 0))] * 2
        args += list(rope)
        band_t = (lambda t: jnp.maximum(t - 1, 0), lambda t: t, lambda t: jnp.minimum(t + 1, nt - 1))
        in_specs += [pl.BlockSpec((SEQS, TL, 2 * KV_WIDTH), lambda g, t, f=f: (g, f(t), 0)) for f in band_t]
        in_specs += [pl.BlockSpec((SEQS, KV_WIDTH, TL), lambda g, t, f=f: (g, 0, f(t))) for f in band_t]
        args += [k_loc] * 3 + [v_loc] * 3
    out_specs = [
        pl.BlockSpec((SEQS, TL, D_MODEL), lambda g, t: (g, t, 0)),
        pl.BlockSpec((None, 2, SEQS, ALL_ST), lambda g, t: (g, 0, 0, 0)),
    ]
    out_shape = [
        jax.ShapeDtypeStruct((batch, length, D_MODEL), F32),
        jax.ShapeDtypeStruct((ng, 2, SEQS, ALL_ST), F32),
    ]
    scratch = [
        pltpu.VMEM((ROWS, D_MODEL), BF16),
        pltpu.VMEM((SSM_WIDTH // 128, ROWS, 128), F32),
        pltpu.VMEM((ROWS, SSM_WIDTH), F32),
        pltpu.VMEM((TC * SEQS, 2 * HALF_ST), F32),
        pltpu.VMEM((2, SEQS, ALL_ST), F32),
        pltpu.VMEM((ROWS, 2 * SSM_WIDTH), BF16),
        pltpu.VMEM((ROWS, ATTN_WIDTH), BF16),
        pltpu.VMEM((ROWS, ATTN_WIDTH), BF16),
    ]
    return pl.pallas_call(
        functools.partial(_pass_b_kernel, windowed=windowed, n_tiles=nt),
        grid=(ng, nt),
        in_specs=in_specs, out_specs=out_specs, out_shape=out_shape, scratch_shapes=scratch,
        compiler_params=pltpu.CompilerParams(
            dimension_semantics=("arbitrary", "arbitrary"), vmem_limit_bytes=VMEM_LIMIT),
        name="pass_b_latent" if windowed else "pass_b_context",
    )(*args)


def kernel(x_prompt, x_sample, c, cache_k, cache_v, state_ssm_re, state_ssm_im, c_ctx, norm_w, w_mod, b_mod, w_in, ssm_lambda_re, ssm_lambda_im, ssm_log_dt, ssm_b_re, ssm_b_im, ssm_c_re, ssm_c_im, ssm_d, w_glu, b_glu, attn_sink, w_out, final_norm_w):
    assert norm_w.shape[0] == 1, "single trunk layer"
    batch, seq, _ = x_prompt.shape
    dec_batch, dec_seq, _ = x_sample.shape
    assert dec_batch == SEQS and batch % SEQS == 0 and seq % TL == 0 and dec_seq % TL == 0
    assert seq == PAST_LEN and cache_k.shape[2] == PAST_LEN
    ng = batch // SEQS

    w_in0 = w_in[0].astype(BF16)
    u0, z0, q0, k0, v0, za0 = 0, 512, 1024, 1536, 1664, 1792
    w_a = jnp.concatenate([w_in0[:, u0:z0], w_in0[:, k0:za0]], axis=1)
    w_b = jnp.concatenate([w_in0[:, u0:k0], w_in0[:, za0:]], axis=1)
    nw = norm_w[0][None, :]
    fw = final_norm_w[None, :]
    d = ssm_d[0][None, :]
    wg = w_glu[0].astype(BF16)
    bg = b_glu[0][None, :]
    wo = w_out[0].astype(BF16)
    sink = attn_sink[0]

    cond = jnp.concatenate([c, c_ctx[None, :], jnp.zeros((16 - SEQS - 1, D_MODEL), F32)], axis=0)
    mod = _modulation(cond, w_mod[0].astype(BF16), b_mod[0][None, :])
    mod_lat = mod[None, 0:SEQS]
    mod_ctx = jnp.broadcast_to(mod[None, SEQS:SEQS + 1], (ng, SEQS, 3 * D_MODEL))

    lbar_re, lbar_im, bb_re, bb_im = _discretize(
        ssm_lambda_re[0], ssm_lambda_im[0], ssm_log_dt[0], ssm_b_re[0], ssm_b_im[0])
    wb, wc = _block_diag_weights(bb_re, bb_im, ssm_c_re[0], ssm_c_im[0])
    lb = jnp.stack([lbar_re.reshape(2, ALL_ST), lbar_im.reshape(2, ALL_ST)], axis=1)
    lb = jnp.broadcast_to(lb[:, :, None, :], (2, 2, SEQS, ALL_ST))

    h0_lat = jnp.stack([state_ssm_re[:, 0].reshape(SEQS, 2, ALL_ST),
                        state_ssm_im[:, 0].reshape(SEQS, 2, ALL_ST)], axis=0)
    h0_lat = jnp.transpose(h0_lat, (2, 0, 1, 3))[:, None]
    h0_ctx = jnp.zeros((2, ng, 2, SEQS, ALL_ST), F32)
    rope = _rope_tables(dec_seq)
    FWD, BWD = 0, 1

    k_ctx, v_ctx, kcat_ctx, vt_ctx, yb_ctx, fin_b = _pass_a(
        x_prompt, mod_ctx, nw, w_a, wb[BWD], wc[BWD], lb[BWD], h0_ctx[BWD], None)
    y_prompt, fin_f = _pass_b(x_prompt, mod_ctx, nw, w_b, wb[FWD], wc[FWD], lb[FWD], h0_ctx[FWD], d, wg, bg,
                              yb_ctx, kcat_ctx, vt_ctx, wo, fw, sink, None, None, None)

    kcat_lat, vt_lat, yb_lat, _ = _pass_a(
        x_sample, mod_lat, nw, w_a, wb[BWD], wc[BWD], lb[BWD], h0_lat[BWD], rope)
    kx = cache_k[:, 0].reshape(dec_batch, PAST_LEN, KV_WIDTH)
    kx = jnp.concatenate([kx, jnp.roll(kx, HEAD_DIM, axis=-1)], axis=-1).astype(BF16)
    vx = jnp.swapaxes(cache_v[:, 0].reshape(dec_batch, PAST_LEN, KV_WIDTH), 1, 2).astype(BF16)
    y_sample, _ = _pass_b(x_sample, mod_lat, nw, w_b, wb[FWD], wc[FWD], lb[FWD], h0_lat[FWD], d, wg, bg,
                          yb_lat, kx, vx, wo, fw, sink, rope, kcat_lat, vt_lat)

    new_cache_k = k_ctx.reshape(batch, 1, seq, N_KV_HEADS, HEAD_DIM)
    new_cache_v = v_ctx.reshape(batch, 1, seq, N_KV_HEADS, HEAD_DIM)

    def states(fin, part):
        return fin[:, part].reshape(batch, SSM_GROUPS, SSM_STATE)

    new_re = jnp.stack([states(fin_f, 0), states(fin_b, 0)], axis=1)[:, None]
    new_im = jnp.stack([states(fin_f, 1), states(fin_b, 1)], axis=1)[:, None]
    return (y_prompt, y_sample, new_cache_k, new_cache_v, new_re, new_im)
```

```python
import functools
import math

import jax
import jax.numpy as jnp
from jax import lax
from jax.experimental import pallas as pl
from jax.experimental.pallas import tpu as pltpu

F32 = jnp.float32
BF16 = jnp.bfloat16

D_MODEL = 1024
SSM_WIDTH = 512
ATTN_WIDTH = 512
SSM_GROUP = 16
SSM_GROUPS = 32
SSM_STATE = 64
HEAD_DIM = 64
N_HEADS = 8
N_KV_HEADS = 2
KV_REP = 4
KV_WIDTH = 128
GRID_W = 64
ROPE_AXIS_DIM = 32
ROPE_BASE = 10000.0
EPS = 1e-6
LAMBDA_RE_MAX = -1e-4
NEG_INF = -1e30
LOG2E = math.log2(math.e)
PAST_LEN = 256

SEQS = 8
TL = 128
ROWS = SEQS * TL
HALF_CH = SSM_WIDTH // 2
HALF_ST = (SSM_GROUPS // 2) * SSM_STATE
ALL_ST = 2 * HALF_ST
TC = 64
SCAN_LANES = 1024
VMEM_LIMIT = 58 * 1024 * 1024


def _const_spec(shape):
    zeros = (0,) * len(shape)
    return pl.BlockSpec(shape, lambda g, t: zeros, pipeline_mode=pl.Buffered(1))


def _modulation_kernel(cond_ref, w_ref, b_ref, out_ref):
    a = jax.nn.silu(cond_ref[...]).astype(BF16)
    out_ref[...] = jnp.dot(a, w_ref[...], preferred_element_type=F32) + b_ref[...]


def _modulation(cond, w_mod_bf16, b_mod):
    rows = cond.shape[0]
    n = w_mod_bf16.shape[1]
    tn = 1024
    return pl.pallas_call(
        _modulation_kernel,
        grid=(n // tn,),
        in_specs=[pl.BlockSpec((rows, D_MODEL), lambda j: (0, 0)),
                  pl.BlockSpec((D_MODEL, tn), lambda j: (0, j)),
                  pl.BlockSpec((1, tn), lambda j: (0, j))],
        out_specs=pl.BlockSpec((rows, tn), lambda j: (0, j)),
        out_shape=jax.ShapeDtypeStruct((rows, n), F32),
        name="modulation",
    )(cond, w_mod_bf16, b_mod)


def _discretize_kernel(lre_ref, lim_ref, ldt_ref, bre_ref, bim_ref,
                       lbre_ref, lbim_ref, bbre_ref, bbim_ref):
    lam_re = jnp.minimum(lre_ref[...], LAMBDA_RE_MAX)
    lam_im = lim_ref[...]
    dt = jnp.exp(ldt_ref[...])
    mag = jnp.exp(lam_re * dt)
    ang = lam_im * dt
    lbar_re = mag * jnp.cos(ang)
    lbar_im = mag * jnp.sin(ang)
    nr = lbar_re - 1.0
    ni = lbar_im
    den = lam_re * lam_re + lam_im * lam_im
    f_re = (nr * lam_re + ni * lam_im) / den
    f_im = (ni * lam_re - nr * lam_im) / den
    b_re = bre_ref[...]
    b_im = bim_ref[...]
    lbre_ref[...] = lbar_re
    lbim_ref[...] = lbar_im
    bbre_ref[...] = f_re * b_re - f_im * b_im
    bbim_ref[...] = f_re * b_im + f_im * b_re


def _discretize(lam_re, lam_im, log_dt, b_re, b_im):
    n = 2 * SSM_GROUPS * SSM_GROUP
    expand = lambda a: jnp.broadcast_to(a[:, :, None, :], (2, SSM_GROUPS, SSM_GROUP, SSM_STATE)).reshape(n, SSM_STATE)
    ldt = jnp.broadcast_to(log_dt[:, :, None, None], (2, SSM_GROUPS, SSM_GROUP, SSM_STATE)).reshape(n, SSM_STATE)
    bt = lambda a: jnp.transpose(a, (0, 1, 3, 2)).reshape(n, SSM_STATE)
    shp = jax.ShapeDtypeStruct((n, SSM_STATE), F32)
    lbre, lbim, bbre, bbim = pl.pallas_call(
        _discretize_kernel, out_shape=(shp, shp, shp, shp), name="discretize",
    )(expand(lam_re), expand(lam_im), ldt, bt(b_re), bt(b_im))
    r4 = lambda a: a.reshape(2, SSM_GROUPS, SSM_GROUP, SSM_STATE)
    return r4(lbre)[:, :, 0, :], r4(lbim)[:, :, 0, :], r4(bbre), r4(bbim)


def _block_diag_weights(bb_re, bb_im, c_re, c_im):
    gh = SSM_GROUPS // 2
    eye = jnp.eye(gh, dtype=F32)

    def in_blocks(a):
        a = a.reshape(2, 2, gh, SSM_GROUP, SSM_STATE)
        return jnp.einsum('ij,dhicp->dhicjp', eye, a).reshape(2, 2, gh * SSM_GROUP, gh * SSM_STATE)

    def out_blocks(a):
        a = a.reshape(2, 2, gh, SSM_GROUP, SSM_STATE)
        return jnp.einsum('ij,dhicp->dhipjc', eye, a).reshape(2, 2, gh * SSM_STATE, gh * SSM_GROUP)

    wb = jnp.concatenate([in_blocks(bb_re), in_blocks(bb_im)], axis=-1).astype(BF16)
    wc = jnp.concatenate([out_blocks(c_re), out_blocks(-c_im)], axis=-2).astype(BF16)
    return wb, wc


def _rope_tables(length):
    pos = jnp.arange(length)
    row = (pos // GRID_W).astype(F32)
    col = (pos % GRID_W).astype(F32)
    freqs = ROPE_BASE ** (-jnp.arange(0, ROPE_AXIS_DIM, 2, dtype=F32) / ROPE_AXIS_DIM)
    ang_r = row[:, None] * freqs[None, :]
    ang_c = col[:, None] * freqs[None, :]
    cos = jnp.concatenate([jnp.cos(ang_r), jnp.cos(ang_r), jnp.cos(ang_c), jnp.cos(ang_c)], axis=-1)
    sin = jnp.concatenate([-jnp.sin(ang_r), jnp.sin(ang_r), -jnp.sin(ang_c), jnp.sin(ang_c)], axis=-1)
    return jnp.tile(cos, (1, 2)), jnp.tile(sin, (1, 2))


def _norm_mod(h_ref, x_ref, mod_ref, nw_ref):
    nw = nw_ref[...]

    def body(b, carry):
        xb = x_ref[b]
        ms = jnp.mean(xb * xb, axis=-1, keepdims=True)
        shift = mod_ref[pl.ds(b, 1), 0:D_MODEL]
        gain = nw * (1.0 + mod_ref[pl.ds(b, 1), D_MODEL:2 * D_MODEL])
        r0 = pl.multiple_of(b * TL, TL)
        h_ref[pl.ds(r0, TL), :] = ((xb * lax.rsqrt(ms + EPS)) * gain + shift).astype(BF16)
        return carry

    lax.fori_loop(0, SEQS, body, 0)


def _rope(x, cos, sin):
    lane = lax.broadcasted_iota(jnp.int32, x.shape, 1)
    first = (lane & 31) < 16
    partner = jnp.where(first, pltpu.roll(x, 128 - 16, 1), pltpu.roll(x, 16, 1))
    x3 = x.reshape(SEQS, TL, 128)
    p3 = partner.reshape(SEQS, TL, 128)
    return (x3 * cos[None] + p3 * sin[None]).reshape(ROWS, 128)


def _to_token_major(dst_ref, src_ref):
    def body(l, carry):
        r0 = pl.multiple_of(l * SEQS, SEQS)
        for cb in range(SSM_WIDTH // 128):
            dst_ref[pl.ds(r0, SEQS), cb * 128:(cb + 1) * 128] = src_ref[cb, pl.ds(l, SEQS, stride=TL), :]
        return carry
    lax.fori_loop(0, TL, body, 0, unroll=4)


def _to_sequence_major(dst_ref, src_ref):
    def body(l, carry):
        r0 = pl.multiple_of(l * SEQS, SEQS)
        for cb in range(SSM_WIDTH // 128):
            dst_ref[cb, pl.ds(l, SEQS, stride=TL), :] = src_ref[pl.ds(r0, SEQS), cb * 128:(cb + 1) * 128]
        return carry
    lax.fori_loop(0, TL, body, 0, unroll=4)


def _store_column_blocks(dst_ref, val):
    for cb in range(SSM_WIDTH // 128):
        dst_ref[cb] = val[:, cb * 128:(cb + 1) * 128]


def _scan_chunk(s_ref, lb_ref, carry_ref, half, reverse):
    for cb in range(HALF_ST // SCAN_LANES):
        c0 = cb * SCAN_LANES
        g0 = half * HALF_ST + c0
        lr = lb_ref[0, :, g0:g0 + SCAN_LANES]
        li = lb_ref[1, :, g0:g0 + SCAN_LANES]
        sr = carry_ref[0, :, g0:g0 + SCAN_LANES]
        si = carry_ref[1, :, g0:g0 + SCAN_LANES]

        def step(i, state, c0=c0, lr=lr, li=li):
            sr, si = state
            tt = (TC - 1 - i) if reverse else i
            rows = pl.ds(pl.multiple_of(tt * SEQS, SEQS), SEQS)
            br = s_ref[rows, c0:c0 + SCAN_LANES]
            bi = s_ref[rows, HALF_ST + c0:HALF_ST + c0 + SCAN_LANES]
            nr = lr * sr - li * si + br
            ni = lr * si + li * sr + bi
            s_ref[rows, c0:c0 + SCAN_LANES] = nr
            s_ref[rows, HALF_ST + c0:HALF_ST + c0 + SCAN_LANES] = ni
            return nr, ni

        sr, si = lax.fori_loop(0, TC, step, (sr, si), unroll=4)
        carry_ref[0, :, g0:g0 + SCAN_LANES] = sr
        carry_ref[1, :, g0:g0 + SCAN_LANES] = si


def _s5_direction(u_tm_ref, s_ref, carry_ref, wb_ref, wc_ref, lb_ref, first_ref, emit, reverse):
    n_chunks = TL // TC
    chunk_rows = TC * SEQS
    first_rows = slice((TC - 1) * SEQS, TC * SEQS) if reverse else slice(0, SEQS)
    for half in range(2):
        for ci in range(n_chunks):
            c = (n_chunks - 1 - ci) if reverse else ci
            r0 = c * chunk_rows
            uc = u_tm_ref[r0:r0 + chunk_rows, half * HALF_CH:(half + 1) * HALF_CH].astype(BF16)
            s_ref[...] = jnp.dot(uc, wb_ref[half], preferred_element_type=F32)
            _scan_chunk(s_ref, lb_ref, carry_ref, half, reverse)
            if ci == 0:
                @pl.when(pl.program_id(1) == 0)
                def _(half=half):
                    cols = slice(half * HALF_ST, (half + 1) * HALF_ST)
                    first_ref[0, :, cols] = s_ref[first_rows, 0:HALF_ST]
                    first_ref[1, :, cols] = s_ref[first_rows, HALF_ST:2 * HALF_ST]
            y = jnp.dot(s_ref[...].astype(BF16), wc_ref[half], preferred_element_type=F32)
            emit(r0, chunk_rows, half, y)


def _pass_a_kernel(*refs, rotary):
    if rotary:
        (x_ref, mod_ref, nw_ref, w_ref, wb_ref, wc_ref, lb_ref, h0_ref, cos_ref, sin_ref,
         kcat_ref, vt_ref, yb_ref, fin_ref, h_s, u_sm, u_tm, s_s, carry) = refs
    else:
        (x_ref, mod_ref, nw_ref, w_ref, wb_ref, wc_ref, lb_ref, h0_ref,
         k_ref, v_ref, kcat_ref, vt_ref, yb_ref, fin_ref, h_s, u_sm, u_tm, s_s, carry) = refs

    @pl.when(pl.program_id(1) == 0)
    def _():
        carry[...] = h0_ref[...]

    _norm_mod(h_s, x_ref, mod_ref, nw_ref)
    ukv = jnp.dot(h_s[...], w_ref[...], preferred_element_type=F32)
    _store_column_blocks(u_sm, ukv[:, 0:SSM_WIDTH])
    k = ukv[:, SSM_WIDTH:SSM_WIDTH + KV_WIDTH]
    v = ukv[:, SSM_WIDTH + KV_WIDTH:SSM_WIDTH + 2 * KV_WIDTH]
    if rotary:
        k = _rope(k, cos_ref[...], sin_ref[...])
    else:
        k_ref[...] = k.reshape(SEQS, TL, KV_WIDTH)
        v_ref[...] = v.reshape(SEQS, TL, KV_WIDTH)
    kcat = jnp.concatenate([k, pltpu.roll(k, HEAD_DIM, 1)], axis=1).astype(BF16)
    kcat_ref[...] = kcat.reshape(SEQS, TL, 2 * KV_WIDTH)
    for b in range(SEQS):
        vt_ref[b] = v[b * TL:(b + 1) * TL, :].T.astype(BF16)

    _to_token_major(u_tm, u_sm)

    def emit(r0, nrows, half, y):
        yb_ref[r0:r0 + nrows, half * HALF_CH:(half + 1) * HALF_CH] = y

    _s5_direction(u_tm, s_s, carry, wb_ref, wc_ref, lb_ref, fin_ref, emit, reverse=True)


def _pass_a(x, mod, norm_w, w_a, wb, wc, lb, h0, rope):
    batch, length, _ = x.shape
    ng, nt = batch // SEQS, length // TL
    rotary = rope is not None
    rev = lambda t: nt - 1 - t
    in_specs = [
        pl.BlockSpec((SEQS, TL, D_MODEL), lambda g, t: (g, rev(t), 0)),
        pl.BlockSpec((None, SEQS, 3 * D_MODEL), lambda g, t: (g, 0, 0)),
        _const_spec((1, D_MODEL)),
        _const_spec((D_MODEL, SSM_WIDTH + 2 * KV_WIDTH)),
        _const_spec((2, HALF_CH, 2 * HALF_ST)),
        _const_spec((2, 2 * HALF_ST, HALF_CH)),
        _const_spec((2, SEQS, ALL_ST)),
        pl.BlockSpec((None, 2, SEQS, ALL_ST), lambda g, t: (g, 0, 0, 0)),
    ]
    args = [x, mod, norm_w, w_a, wb, wc, lb, h0]
    if rotary:
        in_specs += [pl.BlockSpec((TL, 128), lambda g, t: (rev(t), 0))] * 2
        args += list(rope)
    out_specs = [
        pl.BlockSpec((SEQS, TL, 2 * KV_WIDTH), lambda g, t: (g, rev(t), 0)),
        pl.BlockSpec((SEQS, KV_WIDTH, TL), lambda g, t: (g, 0, rev(t))),
        pl.BlockSpec((None, None, ROWS, SSM_WIDTH), lambda g, t: (g, rev(t), 0, 0)),
        pl.BlockSpec((None, 2, SEQS, ALL_ST), lambda g, t: (g, 0, 0, 0)),
    ]
    out_shape = [
        jax.ShapeDtypeStruct((batch, length, 2 * KV_WIDTH), BF16),
        jax.ShapeDtypeStruct((batch, KV_WIDTH, length), BF16),
        jax.ShapeDtypeStruct((ng, nt, ROWS, SSM_WIDTH), F32),
        jax.ShapeDtypeStruct((ng, 2, SEQS, ALL_ST), F32),
    ]
    if not rotary:
        cache_spec = pl.BlockSpec((SEQS, TL, KV_WIDTH), lambda g, t: (g, rev(t), 0))
        cache_shape = jax.ShapeDtypeStruct((batch, length, KV_WIDTH), F32)
        out_specs = [cache_spec, cache_spec] + out_specs
        out_shape = [cache_shape, cache_shape] + out_shape
    scratch = [
        pltpu.VMEM((ROWS, D_MODEL), BF16),
        pltpu.VMEM((SSM_WIDTH // 128, ROWS, 128), F32),
        pltpu.VMEM((ROWS, SSM_WIDTH), F32),
        pltpu.VMEM((TC * SEQS, 2 * HALF_ST), F32),
        pltpu.VMEM((2, SEQS, ALL_ST), F32),
    ]
    return pl.pallas_call(
        functools.partial(_pass_a_kernel, rotary=rotary),
        grid=(ng, nt),
        in_specs=in_specs, out_specs=out_specs, out_shape=out_shape, scratch_shapes=scratch,
        compiler_params=pltpu.CompilerParams(
            dimension_semantics=("arbitrary", "arbitrary"), vmem_limit_bytes=VMEM_LIMIT),
        name="pass_a_latent" if rotary else "pass_a_context",
    )(*args)


def _attention(b, q_refs, z_s, mix_s, kcat_refs, vt_refs, sink_ref, masks):
    rows = pl.ds(pl.multiple_of(b * TL, TL), TL)
    kcat = jnp.concatenate([r[b] for r in kcat_refs], axis=0)
    vt = jnp.concatenate([r[b] for r in vt_refs], axis=1)
    lane = lax.broadcasted_iota(jnp.int32, (1, 2 * TL), 1)
    units = [(g, side) for g in range(N_KV_HEADS) for side in range(2)]
    scores = []
    for g, side in units:
        qg = jnp.concatenate([q_refs[side][rows, blk * 128:(blk + 1) * 128] for blk in (2 * g, 2 * g + 1)], axis=0)
        kg = kcat[:, 0:KV_WIDTH] if g == side else kcat[:, KV_WIDTH:2 * KV_WIDTH]
        scores.append(lax.dot_general(kg, qg, (((1,), (1,)), ((), ())), preferred_element_type=F32))
    probs, dens = [], []
    for (g, side), s in zip(units, scores):
        if masks is None:
            pieces = [s]
        else:
            pieces = [jnp.where(masks[0], s[0:TL], NEG_INF), s[TL:2 * TL],
                      jnp.where(masks[1], s[2 * TL:3 * TL], NEG_INF), s[3 * TL:]]
        sink = jnp.where(lane < TL, sink_ref[4 * g + side] * LOG2E, sink_ref[4 * g + 2 + side] * LOG2E)
        m = sink
        for piece in pieces:
            m = jnp.maximum(m, jnp.max(piece, axis=0, keepdims=True))
        den = jnp.exp2(sink - m)
        ps = []
        for piece in pieces:
            p = jnp.exp2(piece - m)
            den = den + jnp.sum(p, axis=0, keepdims=True)
            ps.append(p.astype(BF16))
        dens.append(den)
        probs.append(ps[0] if len(ps) == 1 else jnp.concatenate(ps, axis=0))
    out_t = {}
    for (g, side), p, den in zip(units, probs, dens):
        ot = jnp.dot(vt[g * HEAD_DIM:(g + 1) * HEAD_DIM, :], p, preferred_element_type=F32)
        out_t[g, side] = ot / den
    for g in range(N_KV_HEADS):
        for i, blk in enumerate((2 * g, 2 * g + 1)):
            lanes = slice(i * TL, (i + 1) * TL)
            pair_t = jnp.concatenate([out_t[g, 0][:, lanes], out_t[g, 1][:, lanes]], axis=0)
            c0 = ATTN_WIDTH + blk * 128
            mix_s[rows, c0:c0 + 128] = (pair_t.T * z_s[rows, c0:c0 + 128].astype(F32)).astype(BF16)


def _pass_b_kernel(*refs, windowed, n_tiles):
    if windowed:
        (sink_ref, x_ref, mod_ref, nw_ref, w_ref, wb_ref, wc_ref, lb_ref, h0_ref, d_ref, wg_ref, bg_ref,
         yb_ref, kx_ref, vx_ref, wo_ref, fw_ref, cos_ref, sin_ref,
         kp_ref, kc_ref, kn_ref, vp_ref, vc_ref, vn_ref,
         y_ref, fin_ref, h_s, u_sm, u_tm, s_s, carry, z_s, ql_s, qr_s) = refs
    else:
        (sink_ref, x_ref, mod_ref, nw_ref, w_ref, wb_ref, wc_ref, lb_ref, h0_ref, d_ref, wg_ref, bg_ref,
         yb_ref, kx_ref, vx_ref, wo_ref, fw_ref,
         y_ref, fin_ref, h_s, u_sm, u_tm, s_s, carry, z_s, ql_s, qr_s) = refs
    t = pl.program_id(1)
    y_tm = u_tm
    mix_s = h_s

    @pl.when(t == 0)
    def _():
        carry[...] = h0_ref[...]

    _norm_mod(h_s, x_ref, mod_ref, nw_ref)
    h = h_s[...]
    _store_column_blocks(u_sm, jnp.dot(h, w_ref[:, 0:SSM_WIDTH], preferred_element_type=F32))
    z = jnp.dot(h, w_ref[:, SSM_WIDTH:2 * SSM_WIDTH], preferred_element_type=F32)
    z_s[:, 0:SSM_WIDTH] = jax.nn.silu(z).astype(BF16)
    z = jnp.dot(h, w_ref[:, 3 * SSM_WIDTH:4 * SSM_WIDTH], preferred_element_type=F32)
    z_s[:, SSM_WIDTH:2 * SSM_WIDTH] = jax.nn.silu(z).astype(BF16)
    qall = jnp.dot(h, w_ref[:, 2 * SSM_WIDTH:3 * SSM_WIDTH], preferred_element_type=F32)
    even_head = lax.broadcasted_iota(jnp.int32, (ROWS, 128), 1) < HEAD_DIM
    for cg in range(ATTN_WIDTH // 128):
        cols = slice(cg * 128, (cg + 1) * 128)
        q = qall[:, cols]
        if windowed:
            q = _rope(q, cos_ref[...], sin_ref[...])
        q = q * (LOG2E * HEAD_DIM ** -0.5)
        ql_s[:, cols] = jnp.where(even_head, q, 0.0).astype(BF16)
        qr_s[:, cols] = jnp.where(even_head, 0.0, q).astype(BF16)

    _to_token_major(u_tm, u_sm)

    def emit(r0, nrows, half, y):
        cols = slice(half * HALF_CH, (half + 1) * HALF_CH)
        y_tm[r0:r0 + nrows, cols] = y + yb_ref[r0:r0 + nrows, cols] + u_tm[r0:r0 + nrows, cols] * d_ref[:, cols]

    _s5_direction(u_tm, s_s, carry, wb_ref, wc_ref, lb_ref, fin_ref, emit, reverse=False)

    glu_rows = 256
    for rc in range(ROWS // glu_rows):
        rr = slice(rc * glu_rows, (rc + 1) * glu_rows)
        y = jax.nn.gelu(y_tm[rr, :])
        gl = jnp.dot(y.astype(BF16), wg_ref[...], preferred_element_type=F32) + bg_ref[...]
        y_tm[rr, :] = y * jax.nn.sigmoid(gl)
    _to_sequence_major(u_sm, y_tm)
    for cb in range(SSM_WIDTH // 128):
        cols = slice(cb * 128, (cb + 1) * 128)
        mix_s[:, cols] = (u_sm[cb] * z_s[:, cols].astype(F32)).astype(BF16)

    if windowed:
        kj = lax.broadcasted_iota(jnp.int32, (TL, 2 * TL), 0)
        qi = lax.broadcasted_iota(jnp.int32, (TL, 2 * TL), 1) & (TL - 1)
        masks = ((kj >= qi) & (t > 0), (kj <= qi) & (t < n_tiles - 1))
        key_refs = (kp_ref, kc_ref, kn_ref, kx_ref)
        val_refs = (vp_ref, vc_ref, vn_ref, vx_ref)
    else:
        masks = None
        key_refs = (kx_ref,)
        val_refs = (vx_ref,)

    def attn_body(b, carry_):
        _attention(b, (ql_s, qr_s), z_s, mix_s, key_refs, val_refs, sink_ref, masks)
        return carry_

    lax.fori_loop(0, SEQS, attn_body, 0)

    y_ref[...] = jnp.dot(mix_s[...], wo_ref[...], preferred_element_type=F32).reshape(SEQS, TL, D_MODEL)
    fw = fw_ref[...]

    def fin_body(b, carry_):
        gate = mod_ref[pl.ds(b, 1), 2 * D_MODEL:3 * D_MODEL]
        r = x_ref[b] + gate * y_ref[b]
        ms = jnp.mean(r * r, axis=-1, keepdims=True)
        y_ref[b] = (r * lax.rsqrt(ms + EPS)) * fw
        return carry_

    lax.fori_loop(0, SEQS, fin_body, 0)


def _pass_b(x, mod, norm_w, w_b, wb, wc, lb, h0, d, w_glu, b_glu, yb, kx, vx, w_out, fnorm_w, sink,
            rope, k_loc, v_loc):
    batch, length, _ = x.shape
    ng, nt = batch // SEQS, length // TL
    windowed = rope is not None
    in_specs = [
        pl.BlockSpec(memory_space=pltpu.SMEM),
        pl.BlockSpec((SEQS, TL, D_MODEL), lambda g, t: (g, t, 0)),
        pl.BlockSpec((None, SEQS, 3 * D_MODEL), lambda g, t: (g, 0, 0)),
        _const_spec((1, D_MODEL)),
        _const_spec((D_MODEL, 4 * SSM_WIDTH)),
        _const_spec((2, HALF_CH, 2 * HALF_ST)),
        _const_spec((2, 2 * HALF_ST, HALF_CH)),
        _const_spec((2, SEQS, ALL_ST)),
        pl.BlockSpec((None, 2, SEQS, ALL_ST), lambda g, t: (g, 0, 0, 0)),
        _const_spec((1, SSM_WIDTH)),
        _const_spec((SSM_WIDTH, SSM_WIDTH)),
        _const_spec((1, SSM_WIDTH)),
        pl.BlockSpec((None, None, ROWS, SSM_WIDTH), lambda g, t: (g, t, 0, 0)),
        pl.BlockSpec((SEQS, PAST_LEN, 2 * KV_WIDTH), lambda g, t: (g, 0, 0)),
        pl.BlockSpec((SEQS, KV_WIDTH, PAST_LEN), lambda g, t: (g, 0, 0)),
        _const_spec((D_MODEL, D_MODEL)),
        _const_spec((1, D_MODEL)),
    ]
    args = [sink, x, mod, norm_w, w_b, wb, wc, lb, h0, d, w_glu, b_glu, yb, kx, vx, w_out, fnorm_w]
    if windowed:
        in_specs += [pl.BlockSpec((TL, 128), lambda g, t: (t, 0))] * 2
        args += list(rope)
        band_t = (lambda t: jnp.maximum(t - 1, 0), lambda t: t, lambda t: jnp.minimum(t + 1, nt - 1))
        in_specs += [pl.BlockSpec((SEQS, TL, 2 * KV_WIDTH), lambda g, t, f=f: (g, f(t), 0)) for f in band_t]
        in_specs += [pl.BlockSpec((SEQS, KV_WIDTH, TL), lambda g, t, f=f: (g, 0, f(t))) for f in band_t]
        args += [k_loc] * 3 + [v_loc] * 3
    out_specs = [
        pl.BlockSpec((SEQS, TL, D_MODEL), lambda g, t: (g, t, 0)),
        pl.BlockSpec((None, 2, SEQS, ALL_ST), lambda g, t: (g, 0, 0, 0)),
    ]
    out_shape = [
        jax.ShapeDtypeStruct((batch, length, D_MODEL), F32),
        jax.ShapeDtypeStruct((ng, 2, SEQS, ALL_ST), F32),
    ]
    scratch = [
        pltpu.VMEM((ROWS, D_MODEL), BF16),
        pltpu.VMEM((SSM_WIDTH // 128, ROWS, 128), F32),
        pltpu.VMEM((ROWS, SSM_WIDTH), F32),
        pltpu.VMEM((TC * SEQS, 2 * HALF_ST), F32),
        pltpu.VMEM((2, SEQS, ALL_ST), F32),
        pltpu.VMEM((ROWS, 2 * SSM_WIDTH), BF16),
        pltpu.VMEM((ROWS, ATTN_WIDTH), BF16),
        pltpu.VMEM((ROWS, ATTN_WIDTH), BF16),
    ]
    return pl.pallas_call(
        functools.partial(_pass_b_kernel, windowed=windowed, n_tiles=nt),
        grid=(ng, nt),
        in_specs=in_specs, out_specs=out_specs, out_shape=out_shape, scratch_shapes=scratch,
        compiler_params=pltpu.CompilerParams(
            dimension_semantics=("arbitrary", "arbitrary"), vmem_limit_bytes=VMEM_LIMIT),
        name="pass_b_latent" if windowed else "pass_b_context",
    )(*args)


def kernel(x_prompt, x_sample, c, cache_k, cache_v, state_ssm_re, state_ssm_im, c_ctx, norm_w, w_mod, b_mod, w_in, ssm_lambda_re, ssm_lambda_im, ssm_log_dt, ssm_b_re, ssm_b_im, ssm_c_re, ssm_c_im, ssm_d, w_glu, b_glu, attn_sink, w_out, final_norm_w):
    assert norm_w.shape[0] == 1, "single trunk layer"
    batch, seq, _ = x_prompt.shape
    dec_batch, dec_seq, _ = x_sample.shape
    assert dec_batch == SEQS and batch % SEQS == 0 and seq % TL == 0 and dec_seq % TL == 0
    assert seq == PAST_LEN and cache_k.shape[2] == PAST_LEN
    ng = batch // SEQS

    w_in0 = w_in[0].astype(BF16)
    u0, z0, q0, k0, v0, za0 = 0, 512, 1024, 1536, 1664, 1792
    w_a = jnp.concatenate([w_in0[:, u0:z0], w_in0[:, k0:za0]], axis=1)
    w_b = jnp.concatenate([w_in0[:, u0:k0], w_in0[:, za0:]], axis=1)
    nw = norm_w[0][None, :]
    fw = final_norm_w[None, :]
    d = ssm_d[0][None, :]
    wg = w_glu[0].astype(BF16)
    bg = b_glu[0][None, :]
    wo = w_out[0].astype(BF16)
    sink = attn_sink[0]

    cond = jnp.concatenate([c, c_ctx[None, :], jnp.zeros((16 - SEQS - 1, D_MODEL), F32)], axis=0)
    mod = _modulation(cond, w_mod[0].astype(BF16), b_mod[0][None, :])
    mod_lat = mod[None, 0:SEQS]
    mod_ctx = jnp.broadcast_to(mod[None, SEQS:SEQS + 1], (ng, SEQS, 3 * D_MODEL))

    lbar_re, lbar_im, bb_re, bb_im = _discretize(
        ssm_lambda_re[0], ssm_lambda_im[0], ssm_log_dt[0], ssm_b_re[0], ssm_b_im[0])
    wb, wc = _block_diag_weights(bb_re, bb_im, ssm_c_re[0], ssm_c_im[0])
    lb = jnp.stack([lbar_re.reshape(2, ALL_ST), lbar_im.reshape(2, ALL_ST)], axis=1)
    lb = jnp.broadcast_to(lb[:, :, None, :], (2, 2, SEQS, ALL_ST))

    h0_lat = jnp.stack([state_ssm_re[:, 0].reshape(SEQS, 2, ALL_ST),
                        state_ssm_im[:, 0].reshape(SEQS, 2, ALL_ST)], axis=0)
    h0_lat = jnp.transpose(h0_lat, (2, 0, 1, 3))[:, None]
    h0_ctx = jnp.zeros((2, ng, 2, SEQS, ALL_ST), F32)
    rope = _rope_tables(dec_seq)
    FWD, BWD = 0, 1

    k_ctx, v_ctx, kcat_ctx, vt_ctx, yb_ctx, fin_b = _pass_a(
        x_prompt, mod_ctx, nw, w_a, wb[BWD], wc[BWD], lb[BWD], h0_ctx[BWD], None)
    y_prompt, fin_f = _pass_b(x_prompt, mod_ctx, nw, w_b, wb[FWD], wc[FWD], lb[FWD], h0_ctx[FWD], d, wg, bg,
                              yb_ctx, kcat_ctx, vt_ctx, wo, fw, sink, None, None, None)

    kcat_lat, vt_lat, yb_lat, _ = _pass_a(
        x_sample, mod_lat, nw, w_a, wb[BWD], wc[BWD], lb[BWD], h0_lat[BWD], rope)
    kx = cache_k[:, 0].reshape(dec_batch, PAST_LEN, KV_WIDTH)
    kx = jnp.concatenate([kx, jnp.roll(kx, HEAD_DIM, axis=-1)], axis=-1).astype(BF16)
    vx = jnp.swapaxes(cache_v[:, 0].reshape(dec_batch, PAST_LEN, KV_WIDTH), 1, 2).astype(BF16)
    y_sample, _ = _pass_b(x_sample, mod_lat, nw, w_b, wb[FWD], wc[FWD], lb[FWD], h0_lat[FWD], d, wg, bg,
                          yb_lat, kx, vx, wo, fw, sink, rope, kcat_lat, vt_lat)

    new_cache_k = k_ctx.reshape(batch, 1, seq, N_KV_HEADS, HEAD_DIM)
    new_cache_v = v_ctx.reshape(batch, 1, seq, N_KV_HEADS, HEAD_DIM)

    def states(fin, part):
        return fin[:, part].reshape(batch, SSM_GROUPS, SSM_STATE)

    new_re = jnp.stack([states(fin_f, 0), states(fin_b, 0)], axis=1)[:, None]
    new_im = jnp.stack([states(fin_f, 1), states(fin_b, 1)], axis=1)[:, None]
    return (y_prompt, y_sample, new_cache_k, new_cache_v, new_re, new_im)
```

```python
import functools
import math

import jax
import jax.numpy as jnp
from jax import lax
from jax.experimental import pallas as pl
from jax.experimental.pallas import tpu as pltpu

F32 = jnp.float32
BF16 = jnp.bfloat16

D_MODEL = 1024
SSM_WIDTH = 512
ATTN_WIDTH = 512
SSM_GROUP = 16
SSM_GROUPS = 32
SSM_STATE = 64
HEAD_DIM = 64
N_HEADS = 8
N_KV_HEADS = 2
KV_REP = 4
KV_WIDTH = 128
GRID_W = 64
ROPE_AXIS_DIM = 32
ROPE_BASE = 10000.0
EPS = 1e-6
LAMBDA_RE_MAX = -1e-4
NEG_INF = -1e30
LOG2E = math.log2(math.e)
PAST_LEN = 256

SEQS = 8
TL = 128
ROWS = SEQS * TL
HALF_CH = SSM_WIDTH // 2
HALF_ST = (SSM_GROUPS // 2) * SSM_STATE
ALL_ST = 2 * HALF_ST
TC = 32
VMEM_LIMIT = 58 * 1024 * 1024


def _const_spec(shape):
    zeros = (0,) * len(shape)
    return pl.BlockSpec(shape, lambda g, t: zeros, pipeline_mode=pl.Buffered(1))


def _modulation_kernel(cond_ref, w_ref, b_ref, out_ref):
    a = jax.nn.silu(cond_ref[...]).astype(BF16)
    out_ref[...] = jnp.dot(a, w_ref[...], preferred_element_type=F32) + b_ref[...]


def _modulation(cond, w_mod_bf16, b_mod):
    rows = cond.shape[0]
    n = w_mod_bf16.shape[1]
    tn = 1024
    return pl.pallas_call(
        _modulation_kernel,
        grid=(n // tn,),
        in_specs=[pl.BlockSpec((rows, D_MODEL), lambda j: (0, 0)),
                  pl.BlockSpec((D_MODEL, tn), lambda j: (0, j)),
                  pl.BlockSpec((1, tn), lambda j: (0, j))],
        out_specs=pl.BlockSpec((rows, tn), lambda j: (0, j)),
        out_shape=jax.ShapeDtypeStruct((rows, n), F32),
        name="modulation",
    )(cond, w_mod_bf16, b_mod)


def _discretize_kernel(lre_ref, lim_ref, ldt_ref, bre_ref, bim_ref,
                       lbre_ref, lbim_ref, bbre_ref, bbim_ref):
    lam_re = jnp.minimum(lre_ref[...], LAMBDA_RE_MAX)
    lam_im = lim_ref[...]
    dt = jnp.exp(ldt_ref[...])
    mag = jnp.exp(lam_re * dt)
    ang = lam_im * dt
    lbar_re = mag * jnp.cos(ang)
    lbar_im = mag * jnp.sin(ang)
    nr = lbar_re - 1.0
    ni = lbar_im
    den = lam_re * lam_re + lam_im * lam_im
    f_re = (nr * lam_re + ni * lam_im) / den
    f_im = (ni * lam_re - nr * lam_im) / den
    b_re = bre_ref[...]
    b_im = bim_ref[...]
    lbre_ref[...] = lbar_re
    lbim_ref[...] = lbar_im
    bbre_ref[...] = f_re * b_re - f_im * b_im
    bbim_ref[...] = f_re * b_im + f_im * b_re


def _discretize(lam_re, lam_im, log_dt, b_re, b_im):
    n = 2 * SSM_GROUPS * SSM_GROUP
    expand = lambda a: jnp.broadcast_to(a[:, :, None, :], (2, SSM_GROUPS, SSM_GROUP, SSM_STATE)).reshape(n, SSM_STATE)
    ldt = jnp.broadcast_to(log_dt[:, :, None, None], (2, SSM_GROUPS, SSM_GROUP, SSM_STATE)).reshape(n, SSM_STATE)
    bt = lambda a: jnp.transpose(a, (0, 1, 3, 2)).reshape(n, SSM_STATE)
    shp = jax.ShapeDtypeStruct((n, SSM_STATE), F32)
    lbre, lbim, bbre, bbim = pl.pallas_call(
        _discretize_kernel, out_shape=(shp, shp, shp, shp), name="discretize",
    )(expand(lam_re), expand(lam_im), ldt, bt(b_re), bt(b_im))
    r4 = lambda a: a.reshape(2, SSM_GROUPS, SSM_GROUP, SSM_STATE)
    return r4(lbre)[:, :, 0, :], r4(lbim)[:, :, 0, :], r4(bbre), r4(bbim)


def _block_diag_weights(bb_re, bb_im, c_re, c_im):
    gh = SSM_GROUPS // 2
    eye = jnp.eye(gh, dtype=F32)

    def in_blocks(a):
        a = a.reshape(2, 2, gh, SSM_GROUP, SSM_STATE)
        return jnp.einsum('ij,dhicp->dhicjp', eye, a).reshape(2, 2, gh * SSM_GROUP, gh * SSM_STATE)

    def out_blocks(a):
        a = a.reshape(2, 2, gh, SSM_GROUP, SSM_STATE)
        return jnp.einsum('ij,dhicp->dhipjc', eye, a).reshape(2, 2, gh * SSM_STATE, gh * SSM_GROUP)

    wb = jnp.concatenate([in_blocks(bb_re), in_blocks(bb_im)], axis=-1).astype(BF16)
    wc = jnp.concatenate([out_blocks(c_re), out_blocks(-c_im)], axis=-2).astype(BF16)
    return wb, wc


def _rope_tables(length):
    pos = jnp.arange(length)
    row = (pos // GRID_W).astype(F32)
    col = (pos % GRID_W).astype(F32)
    freqs = ROPE_BASE ** (-jnp.arange(0, ROPE_AXIS_DIM, 2, dtype=F32) / ROPE_AXIS_DIM)
    ang_r = row[:, None] * freqs[None, :]
    ang_c = col[:, None] * freqs[None, :]
    cos = jnp.concatenate([jnp.cos(ang_r), jnp.cos(ang_r), jnp.cos(ang_c), jnp.cos(ang_c)], axis=-1)
    sin = jnp.concatenate([-jnp.sin(ang_r), jnp.sin(ang_r), -jnp.sin(ang_c), jnp.sin(ang_c)], axis=-1)
    return jnp.tile(cos, (1, 2)), jnp.tile(sin, (1, 2))


def _norm_mod(h_ref, x_ref, mod_ref, nw_ref):
    nw = nw_ref[...]

    def body(b, carry):
        xb = x_ref[b]
        ms = jnp.mean(xb * xb, axis=-1, keepdims=True)
        shift = mod_ref[pl.ds(b, 1), 0:D_MODEL]
        gain = nw * (1.0 + mod_ref[pl.ds(b, 1), D_MODEL:2 * D_MODEL])
        r0 = pl.multiple_of(b * TL, TL)
        h_ref[pl.ds(r0, TL), :] = ((xb * lax.rsqrt(ms + EPS)) * gain + shift).astype(BF16)
        return carry

    lax.fori_loop(0, SEQS, body, 0)


def _rope(x, cos, sin):
    lane = lax.broadcasted_iota(jnp.int32, x.shape, 1)
    first = (lane & 31) < 16
    partner = jnp.where(first, pltpu.roll(x, 128 - 16, 1), pltpu.roll(x, 16, 1))
    x3 = x.reshape(SEQS, TL, 128)
    p3 = partner.reshape(SEQS, TL, 128)
    return (x3 * cos[None] + p3 * sin[None]).reshape(ROWS, 128)


def _to_token_major(dst_ref, src_ref):
    def body(l, carry):
        r0 = pl.multiple_of(l * SEQS, SEQS)
        for cb in range(SSM_WIDTH // 128):
            dst_ref[pl.ds(r0, SEQS), cb * 128:(cb + 1) * 128] = src_ref[cb, pl.ds(l, SEQS, stride=TL), :]
        return carry
    lax.fori_loop(0, TL, body, 0, unroll=4)


def _to_sequence_major(dst_ref, src_ref):
    def body(l, carry):
        r0 = pl.multiple_of(l * SEQS, SEQS)
        for cb in range(SSM_WIDTH // 128):
            dst_ref[cb, pl.ds(l, SEQS, stride=TL), :] = src_ref[pl.ds(r0, SEQS), cb * 128:(cb + 1) * 128]
        return carry
    lax.fori_loop(0, TL, body, 0, unroll=4)


def _store_column_blocks(dst_ref, val):
    for cb in range(SSM_WIDTH // 128):
        dst_ref[cb] = val[:, cb * 128:(cb + 1) * 128]


def _scan_chunk(s_ref, lb_ref, carry_ref, half, reverse):
    cols = slice(half * HALF_ST, (half + 1) * HALF_ST)
    lr = lb_ref[0, :, cols]
    li = lb_ref[1, :, cols]
    sr = carry_ref[0, :, cols]
    si = carry_ref[1, :, cols]
    for i in range(TC):
        tt = (TC - 1 - i) if reverse else i
        rows = slice(tt * SEQS, (tt + 1) * SEQS)
        br = s_ref[rows, 0:HALF_ST]
        bi = s_ref[rows, HALF_ST:2 * HALF_ST]
        sr, si = lr * sr - li * si + br, lr * si + li * sr + bi
        s_ref[rows, 0:HALF_ST] = sr
        s_ref[rows, HALF_ST:2 * HALF_ST] = si
    carry_ref[0, :, cols] = sr
    carry_ref[1, :, cols] = si


def _s5_direction(u_tm_ref, s_ref, carry_ref, wb_ref, wc_ref, lb_ref, first_ref, first_s, emit, reverse):
    n_chunks = TL // TC
    chunk_rows = TC * SEQS
    first_rows = slice((TC - 1) * SEQS, TC * SEQS) if reverse else slice(0, SEQS)
    order = [(n_chunks - 1 - ci) if reverse else ci for ci in range(n_chunks)]
    units = [(half, c) for c in order for half in range(2)]

    def project_in(i):
        half, c = units[i]
        uc = u_tm_ref[c * chunk_rows:(c + 1) * chunk_rows, half * HALF_CH:(half + 1) * HALF_CH].astype(BF16)
        s_ref[i % 2] = jnp.dot(uc, wb_ref[half], preferred_element_type=F32)

    project_in(0)
    for i, (half, c) in enumerate(units):
        if i + 1 < len(units):
            project_in(i + 1)
        buf = s_ref.at[i % 2]
        _scan_chunk(buf, lb_ref, carry_ref, half, reverse)
        if c == order[0]:
            cols = slice(half * HALF_ST, (half + 1) * HALF_ST)
            first_s[0, :, cols] = buf[first_rows, 0:HALF_ST]
            first_s[1, :, cols] = buf[first_rows, HALF_ST:2 * HALF_ST]
        y = jnp.dot(buf[...].astype(BF16), wc_ref[half], preferred_element_type=F32)
        emit(c * chunk_rows, chunk_rows, half, y)

    @pl.when(pl.program_id(1) == 0)
    def _():
        first_ref[...] = first_s[...]


def _pass_a_kernel(*refs, rotary):
    if rotary:
        (x_ref, mod_ref, nw_ref, w_ref, wb_ref, wc_ref, lb_ref, h0_ref, cos_ref, sin_ref,
         kcat_ref, vt_ref, yb_ref, fin_ref, h_s, u_sm, u_tm, s_s, carry, first_s) = refs
    else:
        (x_ref, mod_ref, nw_ref, w_ref, wb_ref, wc_ref, lb_ref, h0_ref,
         k_ref, v_ref, kcat_ref, vt_ref, yb_ref, fin_ref, h_s, u_sm, u_tm, s_s, carry, first_s) = refs

    @pl.when(pl.program_id(1) == 0)
    def _():
        carry[...] = h0_ref[...]

    _norm_mod(h_s, x_ref, mod_ref, nw_ref)
    ukv = jnp.dot(h_s[...], w_ref[...], preferred_element_type=F32)
    _store_column_blocks(u_sm, ukv[:, 0:SSM_WIDTH])
    k = ukv[:, SSM_WIDTH:SSM_WIDTH + KV_WIDTH]
    v = ukv[:, SSM_WIDTH + KV_WIDTH:SSM_WIDTH + 2 * KV_WIDTH]
    if rotary:
        k = _rope(k, cos_ref[...], sin_ref[...])
    else:
        k_ref[...] = k.reshape(SEQS, TL, KV_WIDTH)
        v_ref[...] = v.reshape(SEQS, TL, KV_WIDTH)
    kcat = jnp.concatenate([k, pltpu.roll(k, HEAD_DIM, 1)], axis=1).astype(BF16)
    kcat_ref[...] = kcat.reshape(SEQS, TL, 2 * KV_WIDTH)
    for b in range(SEQS):
        vt_ref[b] = v[b * TL:(b + 1) * TL, :].T.astype(BF16)

    _to_token_major(u_tm, u_sm)

    def emit(r0, nrows, half, y):
        yb_ref[r0:r0 + nrows, half * HALF_CH:(half + 1) * HALF_CH] = y

    _s5_direction(u_tm, s_s, carry, wb_ref, wc_ref, lb_ref, fin_ref, first_s, emit, reverse=True)


def _pass_a(x, mod, norm_w, w_a, wb, wc, lb, h0, rope):
    batch, length, _ = x.shape
    ng, nt = batch // SEQS, length // TL
    rotary = rope is not None
    rev = lambda t: nt - 1 - t
    in_specs = [
        pl.BlockSpec((SEQS, TL, D_MODEL), lambda g, t: (g, rev(t), 0)),
        pl.BlockSpec((None, SEQS, 3 * D_MODEL), lambda g, t: (g, 0, 0)),
        _const_spec((1, D_MODEL)),
        _const_spec((D_MODEL, SSM_WIDTH + 2 * KV_WIDTH)),
        _const_spec((2, HALF_CH, 2 * HALF_ST)),
        _const_spec((2, 2 * HALF_ST, HALF_CH)),
        _const_spec((2, SEQS, ALL_ST)),
        pl.BlockSpec((None, 2, SEQS, ALL_ST), lambda g, t: (g, 0, 0, 0)),
    ]
    args = [x, mod, norm_w, w_a, wb, wc, lb, h0]
    if rotary:
        in_specs += [pl.BlockSpec((TL, 128), lambda g, t: (rev(t), 0))] * 2
        args += list(rope)
    out_specs = [
        pl.BlockSpec((SEQS, TL, 2 * KV_WIDTH), lambda g, t: (g, rev(t), 0)),
        pl.BlockSpec((SEQS, KV_WIDTH, TL), lambda g, t: (g, 0, rev(t))),
        pl.BlockSpec((None, None, ROWS, SSM_WIDTH), lambda g, t: (g, rev(t), 0, 0)),
        pl.BlockSpec((None, 2, SEQS, ALL_ST), lambda g, t: (g, 0, 0, 0)),
    ]
    out_shape = [
        jax.ShapeDtypeStruct((batch, length, 2 * KV_WIDTH), BF16),
        jax.ShapeDtypeStruct((batch, KV_WIDTH, length), BF16),
        jax.ShapeDtypeStruct((ng, nt, ROWS, SSM_WIDTH), F32),
        jax.ShapeDtypeStruct((ng, 2, SEQS, ALL_ST), F32),
    ]
    if not rotary:
        cache_spec = pl.BlockSpec((SEQS, TL, KV_WIDTH), lambda g, t: (g, rev(t), 0))
        cache_shape = jax.ShapeDtypeStruct((batch, length, KV_WIDTH), F32)
        out_specs = [cache_spec, cache_spec] + out_specs
        out_shape = [cache_shape, cache_shape] + out_shape
    scratch = [
        pltpu.VMEM((ROWS, D_MODEL), BF16),
        pltpu.VMEM((SSM_WIDTH // 128, ROWS, 128), F32),
        pltpu.VMEM((ROWS, SSM_WIDTH), F32),
        pltpu.VMEM((2, TC * SEQS, 2 * HALF_ST), F32),
        pltpu.VMEM((2, SEQS, ALL_ST), F32),
        pltpu.VMEM((2, SEQS, ALL_ST), F32),
    ]
    return pl.pallas_call(
        functools.partial(_pass_a_kernel, rotary=rotary),
        grid=(ng, nt),
        in_specs=in_specs, out_specs=out_specs, out_shape=out_shape, scratch_shapes=scratch,
        compiler_params=pltpu.CompilerParams(
            dimension_semantics=("arbitrary", "arbitrary"), vmem_limit_bytes=VMEM_LIMIT),
        name="pass_a_latent" if rotary else "pass_a_context",
    )(*args)


def _attention(b, q_refs, z_s, mix_s, kcat_refs, vt_refs, sink_ref, masks):
    rows = pl.ds(pl.multiple_of(b * TL, TL), TL)
    kcat = jnp.concatenate([r[b] for r in kcat_refs], axis=0)
    vt = jnp.concatenate([r[b] for r in vt_refs], axis=1)
    lane = lax.broadcasted_iota(jnp.int32, (1, 2 * TL), 1)
    units = [(g, side) for g in range(N_KV_HEADS) for side in range(2)]

    def scores(g, side):
        qg = jnp.concatenate([q_refs[side][rows, blk * 128:(blk + 1) * 128] for blk in (2 * g, 2 * g + 1)], axis=0)
        kg = kcat[:, 0:KV_WIDTH] if g == side else kcat[:, KV_WIDTH:2 * KV_WIDTH]
        return lax.dot_general(kg, qg, (((1,), (1,)), ((), ())), preferred_element_type=F32)

    def softmax(g, side, s):
        if masks is None:
            pieces = [s]
        else:
            pieces = [jnp.where(masks[0], s[0:TL], NEG_INF), s[TL:2 * TL],
                      jnp.where(masks[1], s[2 * TL:3 * TL], NEG_INF), s[3 * TL:]]
        sink = jnp.where(lane < TL, sink_ref[4 * g + side] * LOG2E, sink_ref[4 * g + 2 + side] * LOG2E)
        m = sink
        for piece in pieces:
            m = jnp.maximum(m, jnp.max(piece, axis=0, keepdims=True))
        den = jnp.exp2(sink - m)
        ps = []
        for piece in pieces:
            p = jnp.exp2(piece - m)
            den = den + jnp.sum(p, axis=0, keepdims=True)
            ps.append(p.astype(BF16))
        return (ps[0] if len(ps) == 1 else jnp.concatenate(ps, axis=0)), den

    out_t = {}
    all_scores = [scores(g, side) for g, side in units]
    for (g, side), s in zip(units, all_scores):
        p, den = softmax(g, side, s)
        ot = jnp.dot(vt[g * HEAD_DIM:(g + 1) * HEAD_DIM, :], p, preferred_element_type=F32)
        out_t[g, side] = ot / den
    for g in range(N_KV_HEADS):
        for i, blk in enumerate((2 * g, 2 * g + 1)):
            lanes = slice(i * TL, (i + 1) * TL)
            pair_t = jnp.concatenate([out_t[g, 0][:, lanes], out_t[g, 1][:, lanes]], axis=0)
            c0 = ATTN_WIDTH + blk * 128
            mix_s[rows, c0:c0 + 128] = (pair_t.T * z_s[rows, c0:c0 + 128].astype(F32)).astype(BF16)


def _pass_b_kernel(*refs, windowed, n_tiles):
    if windowed:
        (sink_ref, x_ref, mod_ref, nw_ref, w_ref, wb_ref, wc_ref, lb_ref, h0_ref, d_ref, wg_ref, bg_ref,
         yb_ref, kx_ref, vx_ref, wo_ref, fw_ref, cos_ref, sin_ref,
         kp_ref, kc_ref, kn_ref, vp_ref, vc_ref, vn_ref,
         y_ref, fin_ref, h_s, u_sm, u_tm, s_s, carry, first_s, z_s, ql_s, qr_s) = refs
    else:
        (sink_ref, x_ref, mod_ref, nw_ref, w_ref, wb_ref, wc_ref, lb_ref, h0_ref, d_ref, wg_ref, bg_ref,
         yb_ref, kx_ref, vx_ref, wo_ref, fw_ref,
         y_ref, fin_ref, h_s, u_sm, u_tm, s_s, carry, first_s, z_s, ql_s, qr_s) = refs
    t = pl.program_id(1)
    y_tm = u_tm
    mix_s = h_s

    @pl.when(t == 0)
    def _():
        carry[...] = h0_ref[...]

    _norm_mod(h_s, x_ref, mod_ref, nw_ref)
    h = h_s[...]
    _store_column_blocks(u_sm, jnp.dot(h, w_ref[:, 0:SSM_WIDTH], preferred_element_type=F32))
    z = jnp.dot(h, w_ref[:, SSM_WIDTH:2 * SSM_WIDTH], preferred_element_type=F32)
    z_s[:, 0:SSM_WIDTH] = jax.nn.silu(z).astype(BF16)
    z = jnp.dot(h, w_ref[:, 3 * SSM_WIDTH:4 * SSM_WIDTH], preferred_element_type=F32)
    z_s[:, SSM_WIDTH:2 * SSM_WIDTH] = jax.nn.silu(z).astype(BF16)
    qall = jnp.dot(h, w_ref[:, 2 * SSM_WIDTH:3 * SSM_WIDTH], preferred_element_type=F32)
    even_head = lax.broadcasted_iota(jnp.int32, (ROWS, 128), 1) < HEAD_DIM
    for cg in range(ATTN_WIDTH // 128):
        cols = slice(cg * 128, (cg + 1) * 128)
        q = qall[:, cols]
        if windowed:
            q = _rope(q, cos_ref[...], sin_ref[...])
        q = q * (LOG2E * HEAD_DIM ** -0.5)
        ql_s[:, cols] = jnp.where(even_head, q, 0.0).astype(BF16)
        qr_s[:, cols] = jnp.where(even_head, 0.0, q).astype(BF16)

    _to_token_major(u_tm, u_sm)

    def emit(r0, nrows, half, y):
        cols = slice(half * HALF_CH, (half + 1) * HALF_CH)
        y_tm[r0:r0 + nrows, cols] = y + yb_ref[r0:r0 + nrows, cols] + u_tm[r0:r0 + nrows, cols] * d_ref[:, cols]

    _s5_direction(u_tm, s_s, carry, wb_ref, wc_ref, lb_ref, fin_ref, first_s, emit, reverse=False)

    glu_rows = 256
    for rc in range(ROWS // glu_rows):
        rr = slice(rc * glu_rows, (rc + 1) * glu_rows)
        y = jax.nn.gelu(y_tm[rr, :])
        gl = jnp.dot(y.astype(BF16), wg_ref[...], preferred_element_type=F32) + bg_ref[...]
        y_tm[rr, :] = y * jax.nn.sigmoid(gl)
    _to_sequence_major(u_sm, y_tm)
    for cb in range(SSM_WIDTH // 128):
        cols = slice(cb * 128, (cb + 1) * 128)
        mix_s[:, cols] = (u_sm[cb] * z_s[:, cols].astype(F32)).astype(BF16)

    if windowed:
        kj = lax.broadcasted_iota(jnp.int32, (TL, 2 * TL), 0)
        qi = lax.broadcasted_iota(jnp.int32, (TL, 2 * TL), 1) & (TL - 1)
        masks = ((kj >= qi) & (t > 0), (kj <= qi) & (t < n_tiles - 1))
        key_refs = (kp_ref, kc_ref, kn_ref, kx_ref)
        val_refs = (vp_ref, vc_ref, vn_ref, vx_ref)
    else:
        masks = None
        key_refs = (kx_ref,)
        val_refs = (vx_ref,)

    def attn_body(b, carry_):
        _attention(b, (ql_s, qr_s), z_s, mix_s, key_refs, val_refs, sink_ref, masks)
        return carry_

    lax.fori_loop(0, SEQS, attn_body, 0)

    y_ref[...] = jnp.dot(mix_s[...], wo_ref[...], preferred_element_type=F32).reshape(SEQS, TL, D_MODEL)
    fw = fw_ref[...]

    def fin_body(b, carry_):
        gate = mod_ref[pl.ds(b, 1), 2 * D_MODEL:3 * D_MODEL]
        r = x_ref[b] + gate * y_ref[b]
        ms = jnp.mean(r * r, axis=-1, keepdims=True)
        y_ref[b] = (r * lax.rsqrt(ms + EPS)) * fw
        return carry_

    lax.fori_loop(0, SEQS, fin_body, 0)


def _pass_b(x, mod, norm_w, w_b, wb, wc, lb, h0, d, w_glu, b_glu, yb, kx, vx, w_out, fnorm_w, sink,
            rope, k_loc, v_loc):
    batch, length, _ = x.shape
    ng, nt = batch // SEQS, length // TL
    windowed = rope is not None
    in_specs = [
        pl.BlockSpec(memory_space=pltpu.SMEM),
        pl.BlockSpec((SEQS, TL, D_MODEL), lambda g, t: (g, t, 0)),
        pl.BlockSpec((None, SEQS, 3 * D_MODEL), lambda g, t: (g, 0, 0)),
        _const_spec((1, D_MODEL)),
        _const_spec((D_MODEL, 4 * SSM_WIDTH)),
        _const_spec((2, HALF_CH, 2 * HALF_ST)),
        _const_spec((2, 2 * HALF_ST, HALF_CH)),
        _const_spec((2, SEQS, ALL_ST)),
        pl.BlockSpec((None, 2, SEQS, ALL_ST), lambda g, t: (g, 0, 0, 0)),
        _const_spec((1, SSM_WIDTH)),
        _const_spec((SSM_WIDTH, SSM_WIDTH)),
        _const_spec((1, SSM_WIDTH)),
        pl.BlockSpec((None, None, ROWS, SSM_WIDTH), lambda g, t: (g, t, 0, 0)),
        pl.BlockSpec((SEQS, PAST_LEN, 2 * KV_WIDTH), lambda g, t: (g, 0, 0)),
        pl.BlockSpec((SEQS, KV_WIDTH, PAST_LEN), lambda g, t: (g, 0, 0)),
        _const_spec((D_MODEL, D_MODEL)),
        _const_spec((1, D_MODEL)),
    ]
    args = [sink, x, mod, norm_w, w_b, wb, wc, lb, h0, d, w_glu, b_glu, yb, kx, vx, w_out, fnorm_w]
    if windowed:
        in_specs += [pl.BlockSpec((TL, 128), lambda g, t: (t, 0))] * 2
        args += list(rope)
        band_t = (lambda t: jnp.maximum(t - 1, 0), lambda t: t, lambda t: jnp.minimum(t + 1, nt - 1))
        in_specs += [pl.BlockSpec((SEQS, TL, 2 * KV_WIDTH), lambda g, t, f=f: (g, f(t), 0)) for f in band_t]
        in_specs += [pl.BlockSpec((SEQS, KV_WIDTH, TL), lambda g, t, f=f: (g, 0, f(t))) for f in band_t]
        args += [k_loc] * 3 + [v_loc] * 3
    out_specs = [
        pl.BlockSpec((SEQS, TL, D_MODEL), lambda g, t: (g, t, 0)),
        pl.BlockSpec((None, 2, SEQS, ALL_ST), lambda g, t: (g, 0, 0, 0)),
    ]
    out_shape = [
        jax.ShapeDtypeStruct((batch, length, D_MODEL), F32),
        jax.ShapeDtypeStruct((ng, 2, SEQS, ALL_ST), F32),
    ]
    scratch = [
        pltpu.VMEM((ROWS, D_MODEL), BF16),
        pltpu.VMEM((SSM_WIDTH // 128, ROWS, 128), F32),
        pltpu.VMEM((ROWS, SSM_WIDTH), F32),
        pltpu.VMEM((2, TC * SEQS, 2 * HALF_ST), F32),
        pltpu.VMEM((2, SEQS, ALL_ST), F32),
        pltpu.VMEM((2, SEQS, ALL_ST), F32),
        pltpu.VMEM((ROWS, 2 * SSM_WIDTH), BF16),
        pltpu.VMEM((ROWS, ATTN_WIDTH), BF16),
        pltpu.VMEM((ROWS, ATTN_WIDTH), BF16),
    ]
    return pl.pallas_call(
        functools.partial(_pass_b_kernel, windowed=windowed, n_tiles=nt),
        grid=(ng, nt),
        in_specs=in_specs, out_specs=out_specs, out_shape=out_shape, scratch_shapes=scratch,
        compiler_params=pltpu.CompilerParams(
            dimension_semantics=("arbitrary", "arbitrary"), vmem_limit_bytes=VMEM_LIMIT),
        name="pass_b_latent" if windowed else "pass_b_context",
    )(*args)


def kernel(x_prompt, x_sample, c, cache_k, cache_v, state_ssm_re, state_ssm_im, c_ctx, norm_w, w_mod, b_mod, w_in, ssm_lambda_re, ssm_lambda_im, ssm_log_dt, ssm_b_re, ssm_b_im, ssm_c_re, ssm_c_im, ssm_d, w_glu, b_glu, attn_sink, w_out, final_norm_w):
    assert norm_w.shape[0] == 1, "single trunk layer"
    batch, seq, _ = x_prompt.shape
    dec_batch, dec_seq, _ = x_sample.shape
    assert dec_batch == SEQS and batch % SEQS == 0 and seq % TL == 0 and dec_seq % TL == 0
    assert seq == PAST_LEN and cache_k.shape[2] == PAST_LEN
    ng = batch // SEQS

    w_in0 = w_in[0].astype(BF16)
    u0, z0, q0, k0, v0, za0 = 0, 512, 1024, 1536, 1664, 1792
    w_a = jnp.concatenate([w_in0[:, u0:z0], w_in0[:, k0:za0]], axis=1)
    w_b = jnp.concatenate([w_in0[:, u0:k0], w_in0[:, za0:]], axis=1)
    nw = norm_w[0][None, :]
    fw = final_norm_w[None, :]
    d = ssm_d[0][None, :]
    wg = w_glu[0].astype(BF16)
    bg = b_glu[0][None, :]
    wo = w_out[0].astype(BF16)
    sink = attn_sink[0]

    cond = jnp.concatenate([c, c_ctx[None, :], jnp.zeros((16 - SEQS - 1, D_MODEL), F32)], axis=0)
    mod = _modulation(cond, w_mod[0].astype(BF16), b_mod[0][None, :])
    mod_lat = mod[None, 0:SEQS]
    mod_ctx = jnp.broadcast_to(mod[None, SEQS:SEQS + 1], (ng, SEQS, 3 * D_MODEL))

    lbar_re, lbar_im, bb_re, bb_im = _discretize(
        ssm_lambda_re[0], ssm_lambda_im[0], ssm_log_dt[0], ssm_b_re[0], ssm_b_im[0])
    wb, wc = _block_diag_weights(bb_re, bb_im, ssm_c_re[0], ssm_c_im[0])
    lb = jnp.stack([lbar_re.reshape(2, ALL_ST), lbar_im.reshape(2, ALL_ST)], axis=1)
    lb = jnp.broadcast_to(lb[:, :, None, :], (2, 2, SEQS, ALL_ST))

    h0_lat = jnp.stack([state_ssm_re[:, 0].reshape(SEQS, 2, ALL_ST),
                        state_ssm_im[:, 0].reshape(SEQS, 2, ALL_ST)], axis=0)
    h0_lat = jnp.transpose(h0_lat, (2, 0, 1, 3))[:, None]
    h0_ctx = jnp.zeros((2, ng, 2, SEQS, ALL_ST), F32)
    rope = _rope_tables(dec_seq)
    FWD, BWD = 0, 1

    k_ctx, v_ctx, kcat_ctx, vt_ctx, yb_ctx, fin_b = _pass_a(
        x_prompt, mod_ctx, nw, w_a, wb[BWD], wc[BWD], lb[BWD], h0_ctx[BWD], None)
    y_prompt, fin_f = _pass_b(x_prompt, mod_ctx, nw, w_b, wb[FWD], wc[FWD], lb[FWD], h0_ctx[FWD], d, wg, bg,
                              yb_ctx, kcat_ctx, vt_ctx, wo, fw, sink, None, None, None)

    kcat_lat, vt_lat, yb_lat, _ = _pass_a(
        x_sample, mod_lat, nw, w_a, wb[BWD], wc[BWD], lb[BWD], h0_lat[BWD], rope)
    kx = cache_k[:, 0].reshape(dec_batch, PAST_LEN, KV_WIDTH)
    kx = jnp.concatenate([kx, jnp.roll(kx, HEAD_DIM, axis=-1)], axis=-1).astype(BF16)
    vx = jnp.swapaxes(cache_v[:, 0].reshape(dec_batch, PAST_LEN, KV_WIDTH), 1, 2).astype(BF16)
    y_sample, _ = _pass_b(x_sample, mod_lat, nw, w_b, wb[FWD], wc[FWD], lb[FWD], h0_lat[FWD], d, wg, bg,
                          yb_lat, kx, vx, wo, fw, sink, rope, kcat_lat, vt_lat)

    new_cache_k = k_ctx.reshape(batch, 1, seq, N_KV_HEADS, HEAD_DIM)
    new_cache_v = v_ctx.reshape(batch, 1, seq, N_KV_HEADS, HEAD_DIM)

    def states(fin, part):
        return fin[:, part].reshape(batch, SSM_GROUPS, SSM_STATE)

    new_re = jnp.stack([states(fin_f, 0), states(fin_b, 0)], axis=1)[:, None]
    new_im = jnp.stack([states(fin_f, 1), states(fin_b, 1)], axis=1)[:, None]
    return (y_prompt, y_sample, new_cache_k, new_cache_v, new_re, new_im)
```

```python
import functools
import math

import jax
import jax.numpy as jnp
from jax import lax
from jax.experimental import pallas as pl
from jax.experimental.pallas import tpu as pltpu

F32 = jnp.float32
BF16 = jnp.bfloat16

D_MODEL = 1024
SSM_WIDTH = 512
ATTN_WIDTH = 512
SSM_GROUP = 16
SSM_GROUPS = 32
SSM_STATE = 64
HEAD_DIM = 64
N_HEADS = 8
N_KV_HEADS = 2
KV_REP = 4
KV_WIDTH = 128
GRID_W = 64
ROPE_AXIS_DIM = 32
ROPE_BASE = 10000.0
EPS = 1e-6
LAMBDA_RE_MAX = -1e-4
NEG_INF = -1e30
LOG2E = math.log2(math.e)
PAST_LEN = 256

LANES = 128
SEQS = 8
TL = 128
ROWS = SEQS * TL
PITCH = TL + 8
SSM_BLOCKS = SSM_WIDTH // LANES
HALF_CH = SSM_WIDTH // 2
HALF_ST = (SSM_GROUPS // 2) * SSM_STATE
ALL_ST = 2 * HALF_ST
TC = 32
VMEM_LIMIT = 58 * 1024 * 1024


def _const_spec(shape):
    zeros = (0,) * len(shape)
    return pl.BlockSpec(shape, lambda g, t: zeros, pipeline_mode=pl.Buffered(1))


def _modulation_kernel(cond_ref, w_ref, b_ref, out_ref):
    a = jax.nn.silu(cond_ref[...]).astype(BF16)
    out_ref[...] = jnp.dot(a, w_ref[...], preferred_element_type=F32) + b_ref[...]


def _modulation(cond, w_mod_bf16, b_mod):
    rows = cond.shape[0]
    n = w_mod_bf16.shape[1]
    tn = 1024
    return pl.pallas_call(
        _modulation_kernel,
        grid=(n // tn,),
        in_specs=[pl.BlockSpec((rows, D_MODEL), lambda j: (0, 0)),
                  pl.BlockSpec((D_MODEL, tn), lambda j: (0, j)),
                  pl.BlockSpec((1, tn), lambda j: (0, j))],
        out_specs=pl.BlockSpec((rows, tn), lambda j: (0, j)),
        out_shape=jax.ShapeDtypeStruct((rows, n), F32),
        name="modulation",
    )(cond, w_mod_bf16, b_mod)


def _discretize_kernel(lre_ref, lim_ref, ldt_ref, bre_ref, bim_ref,
                       lbre_ref, lbim_ref, bbre_ref, bbim_ref):
    lam_re = jnp.minimum(lre_ref[...], LAMBDA_RE_MAX)
    lam_im = lim_ref[...]
    dt = jnp.exp(ldt_ref[...])
    mag = jnp.exp(lam_re * dt)
    ang = lam_im * dt
    lbar_re = mag * jnp.cos(ang)
    lbar_im = mag * jnp.sin(ang)
    nr = lbar_re - 1.0
    ni = lbar_im
    den = lam_re * lam_re + lam_im * lam_im
    f_re = (nr * lam_re + ni * lam_im) / den
    f_im = (ni * lam_re - nr * lam_im) / den
    b_re = bre_ref[...]
    b_im = bim_ref[...]
    lbre_ref[...] = lbar_re
    lbim_ref[...] = lbar_im
    bbre_ref[...] = f_re * b_re - f_im * b_im
    bbim_ref[...] = f_re * b_im + f_im * b_re


def _discretize(lam_re, lam_im, log_dt, b_re, b_im):
    n = 2 * SSM_GROUPS * SSM_GROUP
    expand = lambda a: jnp.broadcast_to(a[:, :, None, :], (2, SSM_GROUPS, SSM_GROUP, SSM_STATE)).reshape(n, SSM_STATE)
    ldt = jnp.broadcast_to(log_dt[:, :, None, None], (2, SSM_GROUPS, SSM_GROUP, SSM_STATE)).reshape(n, SSM_STATE)
    bt = lambda a: jnp.transpose(a, (0, 1, 3, 2)).reshape(n, SSM_STATE)
    shp = jax.ShapeDtypeStruct((n, SSM_STATE), F32)
    lbre, lbim, bbre, bbim = pl.pallas_call(
        _discretize_kernel, out_shape=(shp, shp, shp, shp), name="discretize",
    )(expand(lam_re), expand(lam_im), ldt, bt(b_re), bt(b_im))
    r4 = lambda a: a.reshape(2, SSM_GROUPS, SSM_GROUP, SSM_STATE)
    return r4(lbre)[:, :, 0, :], r4(lbim)[:, :, 0, :], r4(bbre), r4(bbim)


def _block_diag_weights(bb_re, bb_im, c_re, c_im):
    gh = SSM_GROUPS // 2
    eye = jnp.eye(gh, dtype=F32)

    def in_blocks(a):
        a = a.reshape(2, 2, gh, SSM_GROUP, SSM_STATE)
        return jnp.einsum('ij,dhicp->dhicjp', eye, a).reshape(2, 2, gh * SSM_GROUP, gh * SSM_STATE)

    def out_blocks(a):
        a = a.reshape(2, 2, gh, SSM_GROUP, SSM_STATE)
        return jnp.einsum('ij,dhicp->dhipjc', eye, a).reshape(2, 2, gh * SSM_STATE, gh * SSM_GROUP)

    wb = jnp.concatenate([in_blocks(bb_re), in_blocks(bb_im)], axis=-1).astype(BF16)
    wc = jnp.concatenate([out_blocks(c_re), out_blocks(-c_im)], axis=-2).astype(BF16)
    return wb, wc


def _rope_tables(length):
    pos = jnp.arange(length)
    row = (pos // GRID_W).astype(F32)
    col = (pos % GRID_W).astype(F32)
    freqs = ROPE_BASE ** (-jnp.arange(0, ROPE_AXIS_DIM, 2, dtype=F32) / ROPE_AXIS_DIM)
    ang_r = row[:, None] * freqs[None, :]
    ang_c = col[:, None] * freqs[None, :]
    cos = jnp.concatenate([jnp.cos(ang_r), jnp.cos(ang_r), jnp.cos(ang_c), jnp.cos(ang_c)], axis=-1)
    sin = jnp.concatenate([-jnp.sin(ang_r), jnp.sin(ang_r), -jnp.sin(ang_c), jnp.sin(ang_c)], axis=-1)
    return jnp.tile(cos, (1, 2)), jnp.tile(sin, (1, 2))


def _seq_rows(b):
    return slice(b * TL, (b + 1) * TL)


def _norm_mod(h_ref, x_ref, mod_ref, nw_ref, b):
    xb = x_ref[b]
    ms = jnp.mean(xb * xb, axis=-1, keepdims=True)
    shift = mod_ref[b:b + 1, 0:D_MODEL]
    gain = nw_ref[...] * (1.0 + mod_ref[b:b + 1, D_MODEL:2 * D_MODEL])
    h_ref[_seq_rows(b), :] = ((xb * lax.rsqrt(ms + EPS)) * gain + shift).astype(BF16)


def _rope(x, cos, sin):
    lane = lax.broadcasted_iota(jnp.int32, x.shape, 1)
    first = (lane & 31) < 16
    partner = jnp.where(first, pltpu.roll(x, LANES - 16, 1), pltpu.roll(x, 16, 1))
    x3 = x.reshape(SEQS, TL, LANES)
    p3 = partner.reshape(SEQS, TL, LANES)
    return (x3 * cos[None] + p3 * sin[None]).reshape(ROWS, LANES)


def _store_column_blocks(dst_ref, val):
    for b in range(SEQS):
        for cb in range(SSM_BLOCKS):
            dst_ref[cb, b * PITCH:b * PITCH + TL, :] = val[_seq_rows(b), cb * LANES:(cb + 1) * LANES]


def _to_token_major(dst_ref, src_ref):
    for l in range(TL):
        for cb in range(SSM_BLOCKS):
            dst_ref[l * SEQS:(l + 1) * SEQS, cb * LANES:(cb + 1) * LANES] = src_ref[cb, pl.ds(l, SEQS, stride=PITCH), :]


def _to_sequence_major(dst_ref, src_ref):
    for l in range(TL):
        for cb in range(SSM_BLOCKS):
            dst_ref[cb, pl.ds(l, SEQS, stride=PITCH), :] = src_ref[l * SEQS:(l + 1) * SEQS, cb * LANES:(cb + 1) * LANES]


def _scan_chunk(s_ref, lb_ref, carry_ref, half, reverse):
    cols = slice(half * HALF_ST, (half + 1) * HALF_ST)
    lr = lb_ref[0, :, cols]
    li = lb_ref[1, :, cols]
    sr = carry_ref[0, :, cols]
    si = carry_ref[1, :, cols]
    for i in range(TC):
        tt = (TC - 1 - i) if reverse else i
        rows = slice(tt * SEQS, (tt + 1) * SEQS)
        br = s_ref[rows, 0:HALF_ST]
        bi = s_ref[rows, HALF_ST:2 * HALF_ST]
        sr, si = lr * sr - li * si + br, lr * si + li * sr + bi
        s_ref[rows, 0:HALF_ST] = sr
        s_ref[rows, HALF_ST:2 * HALF_ST] = si
    carry_ref[0, :, cols] = sr
    carry_ref[1, :, cols] = si


def _s5_direction(u_tm_ref, s_ref, carry_ref, wb_ref, wc_ref, lb_ref, first_ref, first_s, emit, reverse,
                  between=None):
    n_chunks = TL // TC
    chunk_rows = TC * SEQS
    first_rows = slice((TC - 1) * SEQS, TC * SEQS) if reverse else slice(0, SEQS)
    order = [(n_chunks - 1 - ci) if reverse else ci for ci in range(n_chunks)]
    units = [(half, c) for c in order for half in range(2)]

    def project_in(i):
        half, c = units[i]
        uc = u_tm_ref[c * chunk_rows:(c + 1) * chunk_rows, half * HALF_CH:(half + 1) * HALF_CH].astype(BF16)
        s_ref[i % 2] = jnp.dot(uc, wb_ref[half], preferred_element_type=F32)

    project_in(0)
    for i, (half, c) in enumerate(units):
        if i + 1 < len(units):
            project_in(i + 1)
        buf = s_ref.at[i % 2]
        _scan_chunk(buf, lb_ref, carry_ref, half, reverse)
        if c == order[0]:
            cols = slice(half * HALF_ST, (half + 1) * HALF_ST)
            first_s[0, :, cols] = buf[first_rows, 0:HALF_ST]
            first_s[1, :, cols] = buf[first_rows, HALF_ST:2 * HALF_ST]
        y = jnp.dot(buf[...].astype(BF16), wc_ref[half], preferred_element_type=F32)
        emit(c * chunk_rows, chunk_rows, half, y)
        if between is not None:
            between(i)

    @pl.when(pl.program_id(1) == 0)
    def _():
        first_ref[...] = first_s[...]


N_S5_UNITS = 2 * (TL // TC)


def _pass_a_kernel(*refs, rotary):
    if rotary:
        (x_ref, mod_ref, nw_ref, w_ref, wb_ref, wc_ref, lb_ref, h0_ref, d_ref, cos_ref, sin_ref,
         kcat_ref, vt_ref, utm_ref, yb_ref, fin_ref, h_s, u_sm, u_tm, s_s, carry, first_s) = refs
    else:
        (x_ref, mod_ref, nw_ref, w_ref, wb_ref, wc_ref, lb_ref, h0_ref, d_ref,
         k_ref, v_ref, kcat_ref, vt_ref, utm_ref, yb_ref, fin_ref, h_s, u_sm, u_tm, s_s, carry, first_s) = refs

    @pl.when(pl.program_id(1) == 0)
    def _():
        carry[...] = h0_ref[...]

    for b in range(SEQS):
        _norm_mod(h_s, x_ref, mod_ref, nw_ref, b)
    ukv = jnp.dot(h_s[...], w_ref[...], preferred_element_type=F32)
    _store_column_blocks(u_sm, ukv[:, 0:SSM_WIDTH])
    k = ukv[:, SSM_WIDTH:SSM_WIDTH + KV_WIDTH]
    v = ukv[:, SSM_WIDTH + KV_WIDTH:SSM_WIDTH + 2 * KV_WIDTH]
    if rotary:
        k = _rope(k, cos_ref[...], sin_ref[...])
    else:
        k_ref[...] = k.reshape(SEQS, TL, KV_WIDTH)
        v_ref[...] = v.reshape(SEQS, TL, KV_WIDTH)
    kcat = jnp.concatenate([k, pltpu.roll(k, HEAD_DIM, 1)], axis=1).astype(BF16)
    kcat_ref[...] = kcat.reshape(SEQS, TL, 2 * KV_WIDTH)
    for b in range(SEQS):
        vt_ref[b] = v[_seq_rows(b), :].T.astype(BF16)

    _to_token_major(u_tm, u_sm)
    utm_ref[...] = u_tm[...].astype(BF16)

    def emit(r0, nrows, half, y):
        rows, cols = slice(r0, r0 + nrows), slice(half * HALF_CH, (half + 1) * HALF_CH)
        yb_ref[rows, cols] = y + u_tm[rows, cols] * d_ref[:, cols]

    _s5_direction(u_tm, s_s, carry, wb_ref, wc_ref, lb_ref, fin_ref, first_s, emit, reverse=True)


def _pass_a(x, mod, norm_w, w_a, wb, wc, lb, h0, d, rope):
    batch, length, _ = x.shape
    ng, nt = batch // SEQS, length // TL
    rotary = rope is not None
    rev = lambda t: nt - 1 - t
    in_specs = [
        pl.BlockSpec((SEQS, TL, D_MODEL), lambda g, t: (g, rev(t), 0)),
        pl.BlockSpec((None, SEQS, 3 * D_MODEL), lambda g, t: (g, 0, 0)),
        _const_spec((1, D_MODEL)),
        _const_spec((D_MODEL, SSM_WIDTH + 2 * KV_WIDTH)),
        _const_spec((2, HALF_CH, 2 * HALF_ST)),
        _const_spec((2, 2 * HALF_ST, HALF_CH)),
        _const_spec((2, SEQS, ALL_ST)),
        pl.BlockSpec((None, 2, SEQS, ALL_ST), lambda g, t: (g, 0, 0, 0)),
        _const_spec((1, SSM_WIDTH)),
    ]
    args = [x, mod, norm_w, w_a, wb, wc, lb, h0, d]
    if rotary:
        in_specs += [pl.BlockSpec((TL, LANES), lambda g, t: (rev(t), 0))] * 2
        args += list(rope)
    tile_spec = pl.BlockSpec((None, None, ROWS, SSM_WIDTH), lambda g, t: (g, rev(t), 0, 0))
    out_specs = [
        pl.BlockSpec((SEQS, TL, 2 * KV_WIDTH), lambda g, t: (g, rev(t), 0)),
        pl.BlockSpec((SEQS, KV_WIDTH, TL), lambda g, t: (g, 0, rev(t))),
        tile_spec,
        tile_spec,
        pl.BlockSpec((None, 2, SEQS, ALL_ST), lambda g, t: (g, 0, 0, 0)),
    ]
    out_shape = [
        jax.ShapeDtypeStruct((batch, length, 2 * KV_WIDTH), BF16),
        jax.ShapeDtypeStruct((batch, KV_WIDTH, length), BF16),
        jax.ShapeDtypeStruct((ng, nt, ROWS, SSM_WIDTH), BF16),
        jax.ShapeDtypeStruct((ng, nt, ROWS, SSM_WIDTH), F32),
        jax.ShapeDtypeStruct((ng, 2, SEQS, ALL_ST), F32),
    ]
    if not rotary:
        cache_spec = pl.BlockSpec((SEQS, TL, KV_WIDTH), lambda g, t: (g, rev(t), 0))
        cache_shape = jax.ShapeDtypeStruct((batch, length, KV_WIDTH), F32)
        out_specs = [cache_spec, cache_spec] + out_specs
        out_shape = [cache_shape, cache_shape] + out_shape
    scratch = [
        pltpu.VMEM((ROWS, D_MODEL), BF16),
        pltpu.VMEM((SSM_BLOCKS, SEQS * PITCH, LANES), F32),
        pltpu.VMEM((ROWS, SSM_WIDTH), F32),
        pltpu.VMEM((2, TC * SEQS, 2 * HALF_ST), F32),
        pltpu.VMEM((2, SEQS, ALL_ST), F32),
        pltpu.VMEM((2, SEQS, ALL_ST), F32),
    ]
    return pl.pallas_call(
        functools.partial(_pass_a_kernel, rotary=rotary),
        grid=(ng, nt),
        in_specs=in_specs, out_specs=out_specs, out_shape=out_shape, scratch_shapes=scratch,
        compiler_params=pltpu.CompilerParams(
            dimension_semantics=("arbitrary", "arbitrary"), vmem_limit_bytes=VMEM_LIMIT),
        name="pass_a_latent" if rotary else "pass_a_context",
    )(*args)


def _attention(b, q_refs, z_s, mix_s, kcat_refs, vt_refs, sink_ref, masks):
    rows = _seq_rows(b)
    kcat = jnp.concatenate([r[b] for r in kcat_refs], axis=0)
    vt = jnp.concatenate([r[b] for r in vt_refs], axis=1)
    lane = lax.broadcasted_iota(jnp.int32, (1, 2 * TL), 1)
    units = [(g, side) for g in range(N_KV_HEADS) for side in range(2)]

    def scores(g, side):
        qg = jnp.concatenate([q_refs[side][rows, blk * LANES:(blk + 1) * LANES] for blk in (2 * g, 2 * g + 1)], axis=0)
        kg = kcat[:, 0:KV_WIDTH] if g == side else kcat[:, KV_WIDTH:2 * KV_WIDTH]
        return lax.dot_general(kg, qg, (((1,), (1,)), ((), ())), preferred_element_type=F32)

    def softmax(g, side, s):
        if masks is None:
            pieces = [s]
        else:
            pieces = [jnp.where(masks[0], s[0:TL], NEG_INF), s[TL:2 * TL],
                      jnp.where(masks[1], s[2 * TL:3 * TL], NEG_INF), s[3 * TL:]]
        sink = jnp.where(lane < TL, sink_ref[4 * g + side] * LOG2E, sink_ref[4 * g + 2 + side] * LOG2E)
        m = sink
        for piece in pieces:
            m = jnp.maximum(m, jnp.max(piece, axis=0, keepdims=True))
        den = jnp.exp2(sink - m)
        ps = []
        for piece in pieces:
            p = jnp.exp2(piece - m)
            den = den + jnp.sum(p, axis=0, keepdims=True)
            ps.append(p.astype(BF16))
        return (ps[0] if len(ps) == 1 else jnp.concatenate(ps, axis=0)), den

    out_t = {}
    all_scores = [scores(g, side) for g, side in units]
    for (g, side), s in zip(units, all_scores):
        p, den = softmax(g, side, s)
        ot = jnp.dot(vt[g * HEAD_DIM:(g + 1) * HEAD_DIM, :], p, preferred_element_type=F32)
        out_t[g, side] = ot / den
    for g in range(N_KV_HEADS):
        for i, blk in enumerate((2 * g, 2 * g + 1)):
            lanes = slice(i * TL, (i + 1) * TL)
            pair_t = jnp.concatenate([out_t[g, 0][:, lanes], out_t[g, 1][:, lanes]], axis=0)
            c0 = ATTN_WIDTH + blk * LANES
            mix_s[rows, c0:c0 + LANES] = (pair_t.T * z_s[rows, c0:c0 + LANES].astype(F32)).astype(BF16)


def _pass_b_kernel(*refs, windowed, n_tiles):
    if windowed:
        (sink_ref, x_ref, mod_ref, nw_ref, w_ref, wb_ref, wc_ref, lb_ref, h0_ref, wg_ref, bg_ref,
         utm_ref, yb_ref, kx_ref, vx_ref, wo_ref, fw_ref, cos_ref, sin_ref,
         kp_ref, kc_ref, kn_ref, vp_ref, vc_ref, vn_ref,
         y_ref, fin_ref, h_s, y_sm, y_tm, s_s, carry, first_s, z_s, ql_s, qr_s) = refs
    else:
        (sink_ref, x_ref, mod_ref, nw_ref, w_ref, wb_ref, wc_ref, lb_ref, h0_ref, wg_ref, bg_ref,
         utm_ref, yb_ref, kx_ref, vx_ref, wo_ref, fw_ref,
         y_ref, fin_ref, h_s, y_sm, y_tm, s_s, carry, first_s, z_s, ql_s, qr_s) = refs
    t = pl.program_id(1)
    mix_s = h_s

    @pl.when(t == 0)
    def _():
        carry[...] = h0_ref[...]

    if windowed:
        kj = lax.broadcasted_iota(jnp.int32, (TL, 2 * TL), 0)
        qi = lax.broadcasted_iota(jnp.int32, (TL, 2 * TL), 1) & (TL - 1)
        masks = ((kj >= qi) & (t > 0), (kj <= qi) & (t < n_tiles - 1))
        key_refs = (kp_ref, kc_ref, kn_ref, kx_ref)
        val_refs = (vp_ref, vc_ref, vn_ref, vx_ref)
    else:
        masks = None
        key_refs = (kx_ref,)
        val_refs = (vx_ref,)

    def project():
        h = h_s[...]
        z = jnp.dot(h, w_ref[:, 0:SSM_WIDTH], preferred_element_type=F32)
        z_s[:, 0:SSM_WIDTH] = jax.nn.silu(z).astype(BF16)
        qall = jnp.dot(h, w_ref[:, SSM_WIDTH:2 * SSM_WIDTH], preferred_element_type=F32)
        even_head = lax.broadcasted_iota(jnp.int32, (ROWS, LANES), 1) < HEAD_DIM
        for cg in range(ATTN_WIDTH // LANES):
            cols = slice(cg * LANES, (cg + 1) * LANES)
            q = qall[:, cols]
            if windowed:
                q = _rope(q, cos_ref[...], sin_ref[...])
            q = q * (LOG2E * HEAD_DIM ** -0.5)
            ql_s[:, cols] = jnp.where(even_head, q, 0.0).astype(BF16)
            qr_s[:, cols] = jnp.where(even_head, 0.0, q).astype(BF16)
        z = jnp.dot(h, w_ref[:, 2 * SSM_WIDTH:3 * SSM_WIDTH], preferred_element_type=F32)
        z_s[:, SSM_WIDTH:2 * SSM_WIDTH] = jax.nn.silu(z).astype(BF16)

    norm_units = N_S5_UNITS // 2
    seqs_per_unit = SEQS // (N_S5_UNITS - norm_units)

    def between(i):
        if i < norm_units:
            for b in range(i * SEQS // norm_units, (i + 1) * SEQS // norm_units):
                _norm_mod(h_s, x_ref, mod_ref, nw_ref, b)
            if i == norm_units - 1:
                project()
        else:
            j = i - norm_units
            for b in range(j * seqs_per_unit, (j + 1) * seqs_per_unit):
                _attention(b, (ql_s, qr_s), z_s, mix_s, key_refs, val_refs, sink_ref, masks)

    def emit(r0, nrows, half, y):
        rows, cols = slice(r0, r0 + nrows), slice(half * HALF_CH, (half + 1) * HALF_CH)
        y_tm[rows, cols] = y + yb_ref[rows, cols]

    _s5_direction(utm_ref, s_s, carry, wb_ref, wc_ref, lb_ref, fin_ref, first_s, emit, reverse=False,
                  between=between)

    glu_rows = 256
    for rc in range(ROWS // glu_rows):
        rr = slice(rc * glu_rows, (rc + 1) * glu_rows)
        y = jax.nn.gelu(y_tm[rr, :])
        gl = jnp.dot(y.astype(BF16), wg_ref[...], preferred_element_type=F32) + bg_ref[...]
        y_tm[rr, :] = y * jax.nn.sigmoid(gl)
    _to_sequence_major(y_sm, y_tm)
    for b in range(SEQS):
        for cb in range(SSM_BLOCKS):
            cols = slice(cb * LANES, (cb + 1) * LANES)
            gated = y_sm[cb, b * PITCH:b * PITCH + TL, :] * z_s[_seq_rows(b), cols].astype(F32)
            mix_s[_seq_rows(b), cols] = gated.astype(BF16)

    y_ref[...] = jnp.dot(mix_s[...], wo_ref[...], preferred_element_type=F32).reshape(SEQS, TL, D_MODEL)
    for b in range(SEQS):
        gate = mod_ref[b:b + 1, 2 * D_MODEL:3 * D_MODEL]
        r = x_ref[b] + gate * y_ref[b]
        ms = jnp.mean(r * r, axis=-1, keepdims=True)
        y_ref[b] = (r * lax.rsqrt(ms + EPS)) * fw_ref[...]


def _pass_b(x, mod, norm_w, w_b, wb, wc, lb, h0, w_glu, b_glu, utm, yb, kx, vx, w_out, fnorm_w, sink,
            rope, k_loc, v_loc):
    batch, length, _ = x.shape
    ng, nt = batch // SEQS, length // TL
    windowed = rope is not None
    tile_spec = pl.BlockSpec((None, None, ROWS, SSM_WIDTH), lambda g, t: (g, t, 0, 0))
    in_specs = [
        pl.BlockSpec(memory_space=pltpu.SMEM),
        pl.BlockSpec((SEQS, TL, D_MODEL), lambda g, t: (g, t, 0)),
        pl.BlockSpec((None, SEQS, 3 * D_MODEL), lambda g, t: (g, 0, 0)),
        _const_spec((1, D_MODEL)),
        _const_spec((D_MODEL, 3 * SSM_WIDTH)),
        _const_spec((2, HALF_CH, 2 * HALF_ST)),
        _const_spec((2, 2 * HALF_ST, HALF_CH)),
        _const_spec((2, SEQS, ALL_ST)),
        pl.BlockSpec((None, 2, SEQS, ALL_ST), lambda g, t: (g, 0, 0, 0)),
        _const_spec((SSM_WIDTH, SSM_WIDTH)),
        _const_spec((1, SSM_WIDTH)),
        tile_spec,
        tile_spec,
        pl.BlockSpec((SEQS, PAST_LEN, 2 * KV_WIDTH), lambda g, t: (g, 0, 0)),
        pl.BlockSpec((SEQS, KV_WIDTH, PAST_LEN), lambda g, t: (g, 0, 0)),
        _const_spec((D_MODEL, D_MODEL)),
        _const_spec((1, D_MODEL)),
    ]
    args = [sink, x, mod, norm_w, w_b, wb, wc, lb, h0, w_glu, b_glu, utm, yb, kx, vx, w_out, fnorm_w]
    if windowed:
        in_specs += [pl.BlockSpec((TL, LANES), lambda g, t: (t, 0))] * 2
        args += list(rope)
        band_t = (lambda t: jnp.maximum(t - 1, 0), lambda t: t, lambda t: jnp.minimum(t + 1, nt - 1))
        in_specs += [pl.BlockSpec((SEQS, TL, 2 * KV_WIDTH), lambda g, t, f=f: (g, f(t), 0)) for f in band_t]
        in_specs += [pl.BlockSpec((SEQS, KV_WIDTH, TL), lambda g, t, f=f: (g, 0, f(t))) for f in band_t]
        args += [k_loc] * 3 + [v_loc] * 3
    out_specs = [
        pl.BlockSpec((SEQS, TL, D_MODEL), lambda g, t: (g, t, 0)),
        pl.BlockSpec((None, 2, SEQS, ALL_ST), lambda g, t: (g, 0, 0, 0)),
    ]
    out_shape = [
        jax.ShapeDtypeStruct((batch, length, D_MODEL), F32),
        jax.ShapeDtypeStruct((ng, 2, SEQS, ALL_ST), F32),
    ]
    scratch = [
        pltpu.VMEM((ROWS, D_MODEL), BF16),
        pltpu.VMEM((SSM_BLOCKS, SEQS * PITCH, LANES), F32),
        pltpu.VMEM((ROWS, SSM_WIDTH), F32),
        pltpu.VMEM((2, TC * SEQS, 2 * HALF_ST), F32),
        pltpu.VMEM((2, SEQS, ALL_ST), F32),
        pltpu.VMEM((2, SEQS, ALL_ST), F32),
        pltpu.VMEM((ROWS, 2 * SSM_WIDTH), BF16),
        pltpu.VMEM((ROWS, ATTN_WIDTH), BF16),
        pltpu.VMEM((ROWS, ATTN_WIDTH), BF16),
    ]
    return pl.pallas_call(
        functools.partial(_pass_b_kernel, windowed=windowed, n_tiles=nt),
        grid=(ng, nt),
        in_specs=in_specs, out_specs=out_specs, out_shape=out_shape, scratch_shapes=scratch,
        compiler_params=pltpu.CompilerParams(
            dimension_semantics=("arbitrary", "arbitrary"), vmem_limit_bytes=VMEM_LIMIT),
        name="pass_b_latent" if windowed else "pass_b_context",
    )(*args)


def kernel(x_prompt, x_sample, c, cache_k, cache_v, state_ssm_re, state_ssm_im, c_ctx, norm_w, w_mod, b_mod, w_in, ssm_lambda_re, ssm_lambda_im, ssm_log_dt, ssm_b_re, ssm_b_im, ssm_c_re, ssm_c_im, ssm_d, w_glu, b_glu, attn_sink, w_out, final_norm_w):
    assert norm_w.shape[0] == 1, "single trunk layer"
    batch, seq, _ = x_prompt.shape
    dec_batch, dec_seq, _ = x_sample.shape
    assert dec_batch == SEQS and batch % SEQS == 0 and seq % TL == 0 and dec_seq % TL == 0
    assert seq == PAST_LEN and cache_k.shape[2] == PAST_LEN
    ng = batch // SEQS

    w_in0 = w_in[0].astype(BF16)
    u0, z0, q0, k0, v0, za0 = 0, 512, 1024, 1536, 1664, 1792
    w_a = jnp.concatenate([w_in0[:, u0:z0], w_in0[:, k0:za0]], axis=1)
    w_b = jnp.concatenate([w_in0[:, z0:k0], w_in0[:, za0:]], axis=1)
    nw = norm_w[0][None, :]
    fw = final_norm_w[None, :]
    d = ssm_d[0][None, :]
    wg = w_glu[0].astype(BF16)
    bg = b_glu[0][None, :]
    wo = w_out[0].astype(BF16)
    sink = attn_sink[0]

    cond = jnp.concatenate([c, c_ctx[None, :], jnp.zeros((16 - SEQS - 1, D_MODEL), F32)], axis=0)
    mod = _modulation(cond, w_mod[0].astype(BF16), b_mod[0][None, :])
    mod_lat = mod[None, 0:SEQS]
    mod_ctx = jnp.broadcast_to(mod[None, SEQS:SEQS + 1], (ng, SEQS, 3 * D_MODEL))

    lbar_re, lbar_im, bb_re, bb_im = _discretize(
        ssm_lambda_re[0], ssm_lambda_im[0], ssm_log_dt[0], ssm_b_re[0], ssm_b_im[0])
    wb, wc = _block_diag_weights(bb_re, bb_im, ssm_c_re[0], ssm_c_im[0])
    lb = jnp.stack([lbar_re.reshape(2, ALL_ST), lbar_im.reshape(2, ALL_ST)], axis=1)
    lb = jnp.broadcast_to(lb[:, :, None, :], (2, 2, SEQS, ALL_ST))

    h0_lat = jnp.stack([state_ssm_re[:, 0].reshape(SEQS, 2, ALL_ST),
                        state_ssm_im[:, 0].reshape(SEQS, 2, ALL_ST)], axis=0)
    h0_lat = jnp.transpose(h0_lat, (2, 0, 1, 3))[:, None]
    h0_ctx = jnp.zeros((2, ng, 2, SEQS, ALL_ST), F32)
    rope = _rope_tables(dec_seq)
    FWD, BWD = 0, 1

    k_ctx, v_ctx, kcat_ctx, vt_ctx, utm_ctx, yb_ctx, fin_b = _pass_a(
        x_prompt, mod_ctx, nw, w_a, wb[BWD], wc[BWD], lb[BWD], h0_ctx[BWD], d, None)
    y_prompt, fin_f = _pass_b(x_prompt, mod_ctx, nw, w_b, wb[FWD], wc[FWD], lb[FWD], h0_ctx[FWD], wg, bg,
                              utm_ctx, yb_ctx, kcat_ctx, vt_ctx, wo, fw, sink, None, None, None)

    kcat_lat, vt_lat, utm_lat, yb_lat, _ = _pass_a(
        x_sample, mod_lat, nw, w_a, wb[BWD], wc[BWD], lb[BWD], h0_lat[BWD], d, rope)
    kx = cache_k[:, 0].reshape(dec_batch, PAST_LEN, KV_WIDTH)
    kx = jnp.concatenate([kx, jnp.roll(kx, HEAD_DIM, axis=-1)], axis=-1).astype(BF16)
    vx = jnp.swapaxes(cache_v[:, 0].reshape(dec_batch, PAST_LEN, KV_WIDTH), 1, 2).astype(BF16)
    y_sample, _ = _pass_b(x_sample, mod_lat, nw, w_b, wb[FWD], wc[FWD], lb[FWD], h0_lat[FWD], wg, bg,
                          utm_lat, yb_lat, kx, vx, wo, fw, sink, rope, kcat_lat, vt_lat)

    new_cache_k = k_ctx.reshape(batch, 1, seq, N_KV_HEADS, HEAD_DIM)
    new_cache_v = v_ctx.reshape(batch, 1, seq, N_KV_HEADS, HEAD_DIM)

    def states(fin, part):
        return fin[:, part].reshape(batch, SSM_GROUPS, SSM_STATE)

    new_re = jnp.stack([states(fin_f, 0), states(fin_b, 0)], axis=1)[:, None]
    new_im = jnp.stack([states(fin_f, 1), states(fin_b, 1)], axis=1)[:, None]
    return (y_prompt, y_sample, new_cache_k, new_cache_v, new_re, new_im)
```

```python
import functools
import math

import jax
import jax.numpy as jnp
from jax import lax
from jax.experimental import pallas as pl
from jax.experimental.pallas import tpu as pltpu

F32 = jnp.float32
BF16 = jnp.bfloat16

D_MODEL = 1024
SSM_WIDTH = 512
ATTN_WIDTH = 512
SSM_GROUP = 16
SSM_GROUPS = 32
SSM_STATE = 64
HEAD_DIM = 64
N_HEADS = 8
N_KV_HEADS = 2
KV_REP = 4
KV_WIDTH = 128
GRID_W = 64
ROPE_AXIS_DIM = 32
ROPE_BASE = 10000.0
EPS = 1e-6
LAMBDA_RE_MAX = -1e-4
NEG_INF = -1e30
LOG2E = math.log2(math.e)
PAST_LEN = 256

LANES = 128
SEQS = 8
TL = 128
ROWS = SEQS * TL
PITCH = TL + 8
SSM_BLOCKS = SSM_WIDTH // LANES
HALF_CH = SSM_WIDTH // 2
HALF_ST = (SSM_GROUPS // 2) * SSM_STATE
ALL_ST = 2 * HALF_ST
TC_A = 32
TC_B = 32
VMEM_LIMIT = 58 * 1024 * 1024


def _const_spec(shape, index=None):
    index = (0,) * len(shape) if index is None else index
    return pl.BlockSpec(shape, lambda g, t: index, pipeline_mode=pl.Buffered(1))


def _modulation_kernel(cond_ref, w_ref, b_ref, out_ref):
    a = jax.nn.silu(cond_ref[...]).astype(BF16)
    out_ref[...] = jnp.dot(a, w_ref[...].astype(BF16), preferred_element_type=F32) + b_ref[...]


def _modulation(cond, w_mod, b_mod):
    rows = cond.shape[0]
    n = w_mod.shape[1]
    tn = 1024
    return pl.pallas_call(
        _modulation_kernel,
        grid=(n // tn,),
        in_specs=[pl.BlockSpec((rows, D_MODEL), lambda j: (0, 0)),
                  pl.BlockSpec((D_MODEL, tn), lambda j: (0, j)),
                  pl.BlockSpec((1, tn), lambda j: (0, j))],
        out_specs=pl.BlockSpec((rows, tn), lambda j: (0, j)),
        out_shape=jax.ShapeDtypeStruct((rows, n), F32),
        name="modulation",
    )(cond, w_mod, b_mod)


def _zoh(lam_re, lam_im, log_dt):
    lam_re = jnp.minimum(lam_re, LAMBDA_RE_MAX)
    dt = jnp.exp(log_dt)
    mag = jnp.exp(lam_re * dt)
    ang = lam_im * dt
    lbar_re = mag * jnp.cos(ang)
    lbar_im = mag * jnp.sin(ang)
    nr = lbar_re - 1.0
    ni = lbar_im
    den = lam_re * lam_re + lam_im * lam_im
    f_re = (nr * lam_re + ni * lam_im) / den
    f_im = (ni * lam_re - nr * lam_im) / den
    return lbar_re, lbar_im, f_re, f_im


def _s5_weights_kernel(lam_ref, bt_ref, cre_ref, cim_ref, lamflat_ref, wb_ref, wc_ref, lb_ref):
    _, _, f_re, f_im = _zoh(lam_ref[0], lam_ref[1], lam_ref[2])
    b_re, b_im = bt_ref[0], bt_ref[1]
    bb_re = f_re * b_re - f_im * b_im
    bb_im = f_re * b_im + f_im * b_re

    p_row = lax.broadcasted_iota(jnp.int32, (SSM_STATE, HALF_ST), 0)
    p_col = lax.broadcasted_iota(jnp.int32, (SSM_STATE, HALF_ST), 1) & (SSM_STATE - 1)
    spread = jnp.where(p_row == p_col, 1.0, 0.0).astype(BF16)
    g_row = lax.broadcasted_iota(jnp.int32, (HALF_CH, HALF_ST), 0) // SSM_GROUP
    g_col = lax.broadcasted_iota(jnp.int32, (HALF_CH, HALF_ST), 1) // SSM_STATE
    own_block = g_row == g_col

    def blocks(x):
        return jnp.where(own_block, jnp.dot(x.astype(BF16), spread, preferred_element_type=F32), 0.0)

    wb_ref[:, 0:HALF_ST] = blocks(bb_re).astype(BF16)
    wb_ref[:, HALF_ST:2 * HALF_ST] = blocks(bb_im).astype(BF16)
    wc_ref[0:HALF_ST, :] = blocks(cre_ref[...]).T.astype(BF16)
    wc_ref[HALF_ST:2 * HALF_ST, :] = blocks(-cim_ref[...]).T.astype(BF16)

    lbar_re, lbar_im, _, _ = _zoh(lamflat_ref[0], lamflat_ref[1], lamflat_ref[2])
    lb_ref[0] = lbar_re
    lb_ref[1] = lbar_im


def _s5_weights(lam_re, lam_im, log_dt, b_re, b_im, c_re, c_im):
    n = 2 * SSM_GROUPS * SSM_GROUP
    lam3 = jnp.stack([lam_re, lam_im, jnp.broadcast_to(log_dt[:, :, None], lam_re.shape)])
    lam_rows = jnp.broadcast_to(lam3[:, :, :, None, :], (3, 2, SSM_GROUPS, SSM_GROUP, SSM_STATE)).reshape(3, n, SSM_STATE)
    lam_flat = jnp.broadcast_to(lam3.reshape(3, 1, 2 * ALL_ST), (3, SEQS, 2 * ALL_ST))
    bt = jnp.transpose(jnp.stack([b_re, b_im]), (0, 1, 2, 4, 3)).reshape(2, n, SSM_STATE)
    quarter = lambda d, h: 2 * d + h
    return pl.pallas_call(
        _s5_weights_kernel,
        grid=(2, 2),
        in_specs=[pl.BlockSpec((3, HALF_CH, SSM_STATE), lambda d, h: (0, quarter(d, h), 0)),
                  pl.BlockSpec((2, HALF_CH, SSM_STATE), lambda d, h: (0, quarter(d, h), 0)),
                  pl.BlockSpec((HALF_CH, SSM_STATE), lambda d, h: (quarter(d, h), 0)),
                  pl.BlockSpec((HALF_CH, SSM_STATE), lambda d, h: (quarter(d, h), 0)),
                  pl.BlockSpec((3, SEQS, 2 * ALL_ST), lambda d, h: (0, 0, 0))],
        out_specs=[pl.BlockSpec((None, None, HALF_CH, 2 * HALF_ST), lambda d, h: (d, h, 0, 0)),
                   pl.BlockSpec((None, None, 2 * HALF_ST, HALF_CH), lambda d, h: (d, h, 0, 0)),
                   pl.BlockSpec((2, SEQS, 2 * ALL_ST), lambda d, h: (0, 0, 0))],
        out_shape=[jax.ShapeDtypeStruct((2, 2, HALF_CH, 2 * HALF_ST), BF16),
                   jax.ShapeDtypeStruct((2, 2, 2 * HALF_ST, HALF_CH), BF16),
                   jax.ShapeDtypeStruct((2, SEQS, 2 * ALL_ST), F32)],
        name="s5_weights",
    )(lam_rows, bt, c_re.reshape(n, SSM_STATE), c_im.reshape(n, SSM_STATE), lam_flat)


def _rope_tables(length):
    pos = jnp.arange(length)
    row = (pos // GRID_W).astype(F32)
    col = (pos % GRID_W).astype(F32)
    freqs = ROPE_BASE ** (-jnp.arange(0, ROPE_AXIS_DIM, 2, dtype=F32) / ROPE_AXIS_DIM)
    ang_r = row[:, None] * freqs[None, :]
    ang_c = col[:, None] * freqs[None, :]
    cos = jnp.concatenate([jnp.cos(ang_r), jnp.cos(ang_r), jnp.cos(ang_c), jnp.cos(ang_c)], axis=-1)
    sin = jnp.concatenate([-jnp.sin(ang_r), jnp.sin(ang_r), -jnp.sin(ang_c), jnp.sin(ang_c)], axis=-1)
    return jnp.tile(cos, (1, 2)), jnp.tile(sin, (1, 2))


def _seq_rows(b):
    return slice(b * TL, (b + 1) * TL)


def _mod_row(b, per_sequence):
    return b if per_sequence else SEQS


def _norm_mod(h_ref, x_ref, mod_ref, nw_ref, b, per_sequence):
    xb = x_ref[b]
    ms = jnp.mean(xb * xb, axis=-1, keepdims=True)
    m = _mod_row(b, per_sequence)
    shift = mod_ref[m:m + 1, 0:D_MODEL]
    gain = nw_ref[...] * (1.0 + mod_ref[m:m + 1, D_MODEL:2 * D_MODEL])
    h_ref[_seq_rows(b), :] = ((xb * lax.rsqrt(ms + EPS)) * gain + shift).astype(BF16)


def _rope(x, cos, sin):
    lane = lax.broadcasted_iota(jnp.int32, x.shape, 1)
    first = (lane & 31) < 16
    partner = jnp.where(first, pltpu.roll(x, LANES - 16, 1), pltpu.roll(x, 16, 1))
    x3 = x.reshape(SEQS, TL, LANES)
    p3 = partner.reshape(SEQS, TL, LANES)
    return (x3 * cos[None] + p3 * sin[None]).reshape(ROWS, LANES)


def _store_column_blocks(dst_ref, val):
    for b in range(SEQS):
        for cb in range(SSM_BLOCKS):
            dst_ref[cb, b * PITCH:b * PITCH + TL, :] = val[_seq_rows(b), cb * LANES:(cb + 1) * LANES]


def _to_token_major(dst_ref, src_ref):
    for l in range(TL):
        for cb in range(SSM_BLOCKS):
            dst_ref[l * SEQS:(l + 1) * SEQS, cb * LANES:(cb + 1) * LANES] = src_ref[cb, pl.ds(l, SEQS, stride=PITCH), :]


def _to_sequence_major(dst_ref, src_ref):
    for l in range(TL):
        for cb in range(SSM_BLOCKS):
            dst_ref[cb, pl.ds(l, SEQS, stride=PITCH), :] = src_ref[l * SEQS:(l + 1) * SEQS, cb * LANES:(cb + 1) * LANES]


def _scan_chunk(s_ref, lb_ref, carry_ref, half, reverse):
    steps = s_ref.shape[0] // SEQS
    cols = slice(half * HALF_ST, (half + 1) * HALF_ST)
    lr = lb_ref[0, :, cols]
    li = lb_ref[1, :, cols]
    sr = carry_ref[0, :, cols]
    si = carry_ref[1, :, cols]
    for i in range(steps):
        tt = (steps - 1 - i) if reverse else i
        rows = slice(tt * SEQS, (tt + 1) * SEQS)
        br = s_ref[rows, 0:HALF_ST]
        bi = s_ref[rows, HALF_ST:2 * HALF_ST]
        sr, si = lr * sr - li * si + br, lr * si + li * sr + bi
        s_ref[rows, 0:HALF_ST] = sr
        s_ref[rows, HALF_ST:2 * HALF_ST] = si
    carry_ref[0, :, cols] = sr
    carry_ref[1, :, cols] = si


def _s5_direction(u_tm_ref, s_ref, carry_ref, wb_ref, wc_ref, lb_ref, first_ref, first_s, emit, reverse,
                  between=None):
    chunk_rows = s_ref.shape[1]
    n_chunks = ROWS // chunk_rows
    first_rows = slice(chunk_rows - SEQS, chunk_rows) if reverse else slice(0, SEQS)
    order = [(n_chunks - 1 - ci) if reverse else ci for ci in range(n_chunks)]
    units = [(half, c) for c in order for half in range(2)]

    def project_in(i):
        half, c = units[i]
        uc = u_tm_ref[c * chunk_rows:(c + 1) * chunk_rows, half * HALF_CH:(half + 1) * HALF_CH].astype(BF16)
        s_ref[i % 2] = jnp.dot(uc, wb_ref[half], preferred_element_type=F32)

    project_in(0)
    for i, (half, c) in enumerate(units):
        if i + 1 < len(units):
            project_in(i + 1)
        buf = s_ref.at[i % 2]
        _scan_chunk(buf, lb_ref, carry_ref, half, reverse)
        if first_ref is not None and c == order[0]:
            cols = slice(half * HALF_ST, (half + 1) * HALF_ST)
            first_s[0, :, cols] = buf[first_rows, 0:HALF_ST]
            first_s[1, :, cols] = buf[first_rows, HALF_ST:2 * HALF_ST]
        y = jnp.dot(buf[...].astype(BF16), wc_ref[half], preferred_element_type=F32)
        emit(c * chunk_rows, chunk_rows, half, y)
        if between is not None:
            between(i)

    if first_ref is not None:
        @pl.when(pl.program_id(1) == 0)
        def _():
            first_ref[...] = first_s[...]


N_S5_UNITS_B = 2 * (TL // TC_B)


def _pass_a_kernel(*refs, rotary):
    if rotary:
        (x_ref, mod_ref, nw_ref, wu_ref, wkv_ref, wb_ref, wc_ref, lb_ref, d_ref, h0re_ref, h0im_ref, cos_ref, sin_ref,
         kcat_ref, vt_ref, utm_ref, yb_ref, h_s, u_sm, u_tm, s_s, carry) = refs
        fin_ref = first_s = None
    else:
        (x_ref, mod_ref, nw_ref, wu_ref, wkv_ref, wb_ref, wc_ref, lb_ref, d_ref,
         k_ref, v_ref, kcat_ref, vt_ref, utm_ref, yb_ref, fin_ref, h_s, u_sm, u_tm, s_s, carry, first_s) = refs

    @pl.when(pl.program_id(1) == 0)
    def _():
        if rotary:
            carry[0] = h0re_ref[...]
            carry[1] = h0im_ref[...]
        else:
            carry[...] = jnp.zeros_like(carry)

    for b in range(SEQS):
        _norm_mod(h_s, x_ref, mod_ref, nw_ref, b, per_sequence=rotary)
    h = h_s[...]
    _store_column_blocks(u_sm, jnp.dot(h, wu_ref[...], preferred_element_type=F32))
    kv = jnp.dot(h, wkv_ref[...], preferred_element_type=F32)
    k = kv[:, 0:KV_WIDTH]
    v = kv[:, KV_WIDTH:2 * KV_WIDTH]
    if rotary:
        k = _rope(k, cos_ref[...], sin_ref[...])
    else:
        k_ref[...] = k.reshape(SEQS, TL, KV_WIDTH)
        v_ref[...] = v.reshape(SEQS, TL, KV_WIDTH)
    kcat = jnp.concatenate([k, pltpu.roll(k, HEAD_DIM, 1)], axis=1).astype(BF16)
    kcat_ref[...] = kcat.reshape(SEQS, TL, 2 * KV_WIDTH)
    for b in range(SEQS):
        vt_ref[b] = v[_seq_rows(b), :].T.astype(BF16)

    _to_token_major(u_tm, u_sm)
    utm_ref[...] = u_tm[...].astype(BF16)

    def emit(r0, nrows, half, y):
        rows, cols = slice(r0, r0 + nrows), slice(half * HALF_CH, (half + 1) * HALF_CH)
        yb_ref[rows, cols] = y + u_tm[rows, cols] * d_ref[:, cols]

    _s5_direction(u_tm, s_s, carry, wb_ref, wc_ref, lb_ref, fin_ref, first_s, emit, reverse=True)


BWD, FWD = 1, 0
W_IN_KV_BLOCK = (2 * SSM_WIDTH + ATTN_WIDTH) // (2 * KV_WIDTH)


def _s5_param_specs(direction):
    return [_const_spec((None, 2, HALF_CH, 2 * HALF_ST), (direction, 0, 0, 0)),
            _const_spec((None, 2, 2 * HALF_ST, HALF_CH), (direction, 0, 0, 0)),
            _const_spec((2, SEQS, ALL_ST), (0, 0, direction))]


def _h0_specs(direction):
    return [_const_spec((SEQS, ALL_ST), (0, direction))] * 2


def _pass_a(x, mod, norm_w, w_in, wb, wc, lb, d, h0, rope):
    batch, length, _ = x.shape
    ng, nt = batch // SEQS, length // TL
    rotary = rope is not None
    rev = lambda t: nt - 1 - t
    in_specs = [
        pl.BlockSpec((SEQS, TL, D_MODEL), lambda g, t: (g, rev(t), 0)),
        _const_spec(mod.shape),
        _const_spec((1, D_MODEL)),
        _const_spec((D_MODEL, SSM_WIDTH), (0, 0)),
        _const_spec((D_MODEL, 2 * KV_WIDTH), (0, W_IN_KV_BLOCK)),
    ] + _s5_param_specs(BWD) + [
        _const_spec((1, SSM_WIDTH)),
    ]
    args = [x, mod, norm_w, w_in, w_in, wb, wc, lb, d]
    if rotary:
        in_specs += _h0_specs(BWD) + [pl.BlockSpec((TL, LANES), lambda g, t: (rev(t), 0))] * 2
        args += list(h0) + list(rope)
    tile_spec = pl.BlockSpec((None, None, ROWS, SSM_WIDTH), lambda g, t: (g, rev(t), 0, 0))
    out_specs = [
        pl.BlockSpec((SEQS, TL, 2 * KV_WIDTH), lambda g, t: (g, rev(t), 0)),
        pl.BlockSpec((SEQS, KV_WIDTH, TL), lambda g, t: (g, 0, rev(t))),
        tile_spec,
        tile_spec,
    ]
    out_shape = [
        jax.ShapeDtypeStruct((batch, length, 2 * KV_WIDTH), BF16),
        jax.ShapeDtypeStruct((batch, KV_WIDTH, length), BF16),
        jax.ShapeDtypeStruct((ng, nt, ROWS, SSM_WIDTH), BF16),
        jax.ShapeDtypeStruct((ng, nt, ROWS, SSM_WIDTH), F32),
    ]
    scratch = [
        pltpu.VMEM((ROWS, D_MODEL), BF16),
        pltpu.VMEM((SSM_BLOCKS, SEQS * PITCH, LANES), F32),
        pltpu.VMEM((ROWS, SSM_WIDTH), F32),
        pltpu.VMEM((2, TC_A * SEQS, 2 * HALF_ST), F32),
        pltpu.VMEM((2, SEQS, ALL_ST), F32),
    ]
    if not rotary:
        cache_spec = pl.BlockSpec((SEQS, TL, KV_WIDTH), lambda g, t: (g, rev(t), 0))
        cache_shape = jax.ShapeDtypeStruct((batch, length, KV_WIDTH), F32)
        out_specs = [cache_spec, cache_spec] + out_specs + [pl.BlockSpec((None, 2, SEQS, ALL_ST), lambda g, t: (g, 0, 0, 0))]
        out_shape = [cache_shape, cache_shape] + out_shape + [jax.ShapeDtypeStruct((ng, 2, SEQS, ALL_ST), F32)]
        scratch += [pltpu.VMEM((2, SEQS, ALL_ST), F32)]
    return pl.pallas_call(
        functools.partial(_pass_a_kernel, rotary=rotary),
        grid=(ng, nt),
        in_specs=in_specs, out_specs=out_specs, out_shape=out_shape, scratch_shapes=scratch,
        compiler_params=pltpu.CompilerParams(
            dimension_semantics=("arbitrary", "arbitrary"), vmem_limit_bytes=VMEM_LIMIT),
        name="pass_a_latent" if rotary else "pass_a_context",
    )(*args)


def _attention(b, q_refs, z_s, mix_s, kcat_refs, vt_refs, sink_ref, masks):
    rows = _seq_rows(b)
    kcat = jnp.concatenate([r[b] for r in kcat_refs], axis=0)
    vt = jnp.concatenate([r[b] for r in vt_refs], axis=1)
    lane = lax.broadcasted_iota(jnp.int32, (1, 2 * TL), 1)
    units = [(g, side) for g in range(N_KV_HEADS) for side in range(2)]

    def scores(g, side):
        qg = jnp.concatenate([q_refs[side][rows, blk * LANES:(blk + 1) * LANES] for blk in (2 * g, 2 * g + 1)], axis=0)
        kg = kcat[:, 0:KV_WIDTH] if g == side else kcat[:, KV_WIDTH:2 * KV_WIDTH]
        return lax.dot_general(kg, qg, (((1,), (1,)), ((), ())), preferred_element_type=F32)

    def softmax(g, side, s):
        if masks is None:
            pieces = [s]
        else:
            pieces = [jnp.where(masks[0], s[0:TL], NEG_INF), s[TL:2 * TL],
                      jnp.where(masks[1], s[2 * TL:3 * TL], NEG_INF), s[3 * TL:]]
        sink = jnp.where(lane < TL, sink_ref[4 * g + side] * LOG2E, sink_ref[4 * g + 2 + side] * LOG2E)
        m = sink
        for piece in pieces:
            m = jnp.maximum(m, jnp.max(piece, axis=0, keepdims=True))
        den = jnp.exp2(sink - m)
        ps = []
        for piece in pieces:
            p = jnp.exp2(piece - m)
            den = den + jnp.sum(p, axis=0, keepdims=True)
            ps.append(p.astype(BF16))
        return (ps[0] if len(ps) == 1 else jnp.concatenate(ps, axis=0)), den

    out_t = {}
    all_scores = [scores(g, side) for g, side in units]
    for (g, side), s in zip(units, all_scores):
        p, den = softmax(g, side, s)
        ot = jnp.dot(vt[g * HEAD_DIM:(g + 1) * HEAD_DIM, :], p, preferred_element_type=F32)
        out_t[g, side] = ot / den
    for g in range(N_KV_HEADS):
        for i, blk in enumerate((2 * g, 2 * g + 1)):
            lanes = slice(i * TL, (i + 1) * TL)
            pair_t = jnp.concatenate([out_t[g, 0][:, lanes], out_t[g, 1][:, lanes]], axis=0)
            c0 = ATTN_WIDTH + blk * LANES
            mix_s[rows, c0:c0 + LANES] = (pair_t.T * z_s[rows, c0:c0 + LANES].astype(F32)).astype(BF16)


def _pass_b_kernel(*refs, windowed, n_tiles):
    if windowed:
        (sink_ref, x_ref, mod_ref, nw_ref, wz_ref, wq_ref, wza0_ref, wza1_ref, wb_ref, wc_ref, lb_ref, wg_ref, bg_ref,
         utm_ref, yb_ref, kx_ref, vx_ref, wo_ref, fw_ref, h0re_ref, h0im_ref, cos_ref, sin_ref,
         kp_ref, kc_ref, kn_ref, vp_ref, vc_ref, vn_ref,
         y_ref, h_s, y_sm, y_tm, s_s, carry, z_s, ql_s, qr_s) = refs
        fin_ref = first_s = None
    else:
        (sink_ref, x_ref, mod_ref, nw_ref, wz_ref, wq_ref, wza0_ref, wza1_ref, wb_ref, wc_ref, lb_ref, wg_ref, bg_ref,
         utm_ref, yb_ref, kx_ref, vx_ref, wo_ref, fw_ref,
         y_ref, fin_ref, h_s, y_sm, y_tm, s_s, carry, z_s, ql_s, qr_s, first_s) = refs
    t = pl.program_id(1)
    mix_s = h_s

    @pl.when(t == 0)
    def _():
        if windowed:
            carry[0] = h0re_ref[...]
            carry[1] = h0im_ref[...]
        else:
            carry[...] = jnp.zeros_like(carry)

    if windowed:
        kj = lax.broadcasted_iota(jnp.int32, (TL, 2 * TL), 0)
        qi = lax.broadcasted_iota(jnp.int32, (TL, 2 * TL), 1) & (TL - 1)
        masks = ((kj >= qi) & (t > 0), (kj <= qi) & (t < n_tiles - 1))
        key_refs = (kp_ref, kc_ref, kn_ref, kx_ref)
        val_refs = (vp_ref, vc_ref, vn_ref, vx_ref)
    else:
        masks = None
        key_refs = (kx_ref,)
        val_refs = (vx_ref,)

    def project():
        h = h_s[...]
        z = jnp.dot(h, wz_ref[...], preferred_element_type=F32)
        z_s[:, 0:SSM_WIDTH] = jax.nn.silu(z).astype(BF16)
        qall = jnp.dot(h, wq_ref[...], preferred_element_type=F32)
        even_head = lax.broadcasted_iota(jnp.int32, (ROWS, LANES), 1) < HEAD_DIM
        for cg in range(ATTN_WIDTH // LANES):
            cols = slice(cg * LANES, (cg + 1) * LANES)
            q = qall[:, cols]
            if windowed:
                q = _rope(q, cos_ref[...], sin_ref[...])
            q = q * (LOG2E * HEAD_DIM ** -0.5)
            ql_s[:, cols] = jnp.where(even_head, q, 0.0).astype(BF16)
            qr_s[:, cols] = jnp.where(even_head, 0.0, q).astype(BF16)
        for i, wza_ref in enumerate((wza0_ref, wza1_ref)):
            z = jnp.dot(h, wza_ref[...], preferred_element_type=F32)
            c0 = SSM_WIDTH + i * W_IN_ZA_BLOCK
            z_s[:, c0:c0 + W_IN_ZA_BLOCK] = jax.nn.silu(z).astype(BF16)

    norm_units = N_S5_UNITS_B // 2
    seqs_per_unit = SEQS // (N_S5_UNITS_B - norm_units)

    def between(i):
        if i < norm_units:
            for b in range(i * SEQS // norm_units, (i + 1) * SEQS // norm_units):
                _norm_mod(h_s, x_ref, mod_ref, nw_ref, b, per_sequence=windowed)
            if i == norm_units - 1:
                project()
        else:
            j = i - norm_units
            for b in range(j * seqs_per_unit, (j + 1) * seqs_per_unit):
                _attention(b, (ql_s, qr_s), z_s, mix_s, key_refs, val_refs, sink_ref, masks)

    def emit(r0, nrows, half, y):
        rows, cols = slice(r0, r0 + nrows), slice(half * HALF_CH, (half + 1) * HALF_CH)
        y_tm[rows, cols] = y + yb_ref[rows, cols]

    _s5_direction(utm_ref, s_s, carry, wb_ref, wc_ref, lb_ref, fin_ref, first_s, emit, reverse=False,
                  between=between)

    glu_rows = 256
    for rc in range(ROWS // glu_rows):
        rr = slice(rc * glu_rows, (rc + 1) * glu_rows)
        y = jax.nn.gelu(y_tm[rr, :])
        gl = jnp.dot(y.astype(BF16), wg_ref[...], preferred_element_type=F32) + bg_ref[...]
        y_tm[rr, :] = y * jax.nn.sigmoid(gl)
    _to_sequence_major(y_sm, y_tm)
    for b in range(SEQS):
        for cb in range(SSM_BLOCKS):
            cols = slice(cb * LANES, (cb + 1) * LANES)
            gated = y_sm[cb, b * PITCH:b * PITCH + TL, :] * z_s[_seq_rows(b), cols].astype(F32)
            mix_s[_seq_rows(b), cols] = gated.astype(BF16)

    y_ref[...] = jnp.dot(mix_s[...], wo_ref[...], preferred_element_type=F32).reshape(SEQS, TL, D_MODEL)
    for b in range(SEQS):
        m = _mod_row(b, per_sequence=windowed)
        gate = mod_ref[m:m + 1, 2 * D_MODEL:3 * D_MODEL]
        r = x_ref[b] + gate * y_ref[b]
        ms = jnp.mean(r * r, axis=-1, keepdims=True)
        y_ref[b] = (r * lax.rsqrt(ms + EPS)) * fw_ref[...]


W_IN_ZA_BLOCK = ATTN_WIDTH // 2
W_IN_ZA_FIRST = (2 * SSM_WIDTH + ATTN_WIDTH + 2 * KV_WIDTH) // W_IN_ZA_BLOCK


def _pass_b(x, mod, norm_w, w_in, wb, wc, lb, w_glu, b_glu, utm, yb, kx, vx, w_out, fnorm_w, sink,
            h0, rope, k_loc, v_loc):
    batch, length, _ = x.shape
    ng, nt = batch // SEQS, length // TL
    windowed = rope is not None
    tile_spec = pl.BlockSpec((None, None, ROWS, SSM_WIDTH), lambda g, t: (g, t, 0, 0))
    in_specs = [
        pl.BlockSpec(memory_space=pltpu.SMEM),
        pl.BlockSpec((SEQS, TL, D_MODEL), lambda g, t: (g, t, 0)),
        _const_spec(mod.shape),
        _const_spec((1, D_MODEL)),
        _const_spec((D_MODEL, SSM_WIDTH), (0, 1)),
        _const_spec((D_MODEL, ATTN_WIDTH), (0, 2)),
        _const_spec((D_MODEL, W_IN_ZA_BLOCK), (0, W_IN_ZA_FIRST)),
        _const_spec((D_MODEL, W_IN_ZA_BLOCK), (0, W_IN_ZA_FIRST + 1)),
    ] + _s5_param_specs(FWD) + [
        _const_spec((SSM_WIDTH, SSM_WIDTH)),
        _const_spec((1, SSM_WIDTH)),
        tile_spec,
        tile_spec,
        pl.BlockSpec((SEQS, PAST_LEN, 2 * KV_WIDTH), lambda g, t: (g, 0, 0)),
        pl.BlockSpec((SEQS, KV_WIDTH, PAST_LEN), lambda g, t: (g, 0, 0)),
        _const_spec((D_MODEL, D_MODEL)),
        _const_spec((1, D_MODEL)),
    ]
    args = [sink, x, mod, norm_w, w_in, w_in, w_in, w_in, wb, wc, lb, w_glu, b_glu, utm, yb, kx, vx, w_out, fnorm_w]
    if windowed:
        in_specs += _h0_specs(FWD) + [pl.BlockSpec((TL, LANES), lambda g, t: (t, 0))] * 2
        args += list(h0) + list(rope)
        band_t = (lambda t: jnp.maximum(t - 1, 0), lambda t: t, lambda t: jnp.minimum(t + 1, nt - 1))
        in_specs += [pl.BlockSpec((SEQS, TL, 2 * KV_WIDTH), lambda g, t, f=f: (g, f(t), 0)) for f in band_t]
        in_specs += [pl.BlockSpec((SEQS, KV_WIDTH, TL), lambda g, t, f=f: (g, 0, f(t))) for f in band_t]
        args += [k_loc] * 3 + [v_loc] * 3
    out_specs = [pl.BlockSpec((SEQS, TL, D_MODEL), lambda g, t: (g, t, 0))]
    out_shape = [jax.ShapeDtypeStruct((batch, length, D_MODEL), F32)]
    scratch = [
        pltpu.VMEM((ROWS, D_MODEL), BF16),
        pltpu.VMEM((SSM_BLOCKS, SEQS * PITCH, LANES), F32),
        pltpu.VMEM((ROWS, SSM_WIDTH), F32),
        pltpu.VMEM((2, TC_B * SEQS, 2 * HALF_ST), F32),
        pltpu.VMEM((2, SEQS, ALL_ST), F32),
        pltpu.VMEM((ROWS, 2 * SSM_WIDTH), BF16),
        pltpu.VMEM((ROWS, ATTN_WIDTH), BF16),
        pltpu.VMEM((ROWS, ATTN_WIDTH), BF16),
    ]
    if not windowed:
        out_specs += [pl.BlockSpec((None, 2, SEQS, ALL_ST), lambda g, t: (g, 0, 0, 0))]
        out_shape += [jax.ShapeDtypeStruct((ng, 2, SEQS, ALL_ST), F32)]
        scratch += [pltpu.VMEM((2, SEQS, ALL_ST), F32)]
    return pl.pallas_call(
        functools.partial(_pass_b_kernel, windowed=windowed, n_tiles=nt),
        grid=(ng, nt),
        in_specs=in_specs, out_specs=out_specs, out_shape=out_shape, scratch_shapes=scratch,
        compiler_params=pltpu.CompilerParams(
            dimension_semantics=("arbitrary", "arbitrary"), vmem_limit_bytes=VMEM_LIMIT),
        name="pass_b_latent" if windowed else "pass_b_context",
    )(*args)


def kernel(x_prompt, x_sample, c, cache_k, cache_v, state_ssm_re, state_ssm_im, c_ctx, norm_w, w_mod, b_mod, w_in, ssm_lambda_re, ssm_lambda_im, ssm_log_dt, ssm_b_re, ssm_b_im, ssm_c_re, ssm_c_im, ssm_d, w_glu, b_glu, attn_sink, w_out, final_norm_w):
    assert norm_w.shape[0] == 1, "single trunk layer"
    batch, seq, _ = x_prompt.shape
    dec_batch, dec_seq, _ = x_sample.shape
    assert dec_batch == SEQS and batch % SEQS == 0 and seq % TL == 0 and dec_seq % TL == 0
    assert seq == PAST_LEN and cache_k.shape[2] == PAST_LEN

    w_in0 = w_in[0].astype(BF16)
    nw = norm_w[0][None, :]
    fw = final_norm_w[None, :]
    d = ssm_d[0][None, :]
    wg = w_glu[0].astype(BF16)
    bg = b_glu[0][None, :]
    wo = w_out[0].astype(BF16)
    sink = attn_sink[0]

    cond = jnp.concatenate([c, c_ctx[None, :], jnp.zeros((16 - SEQS - 1, D_MODEL), F32)], axis=0)
    mod = _modulation(cond, w_mod[0], b_mod[0][None, :])

    wb, wc, lb = _s5_weights(ssm_lambda_re[0], ssm_lambda_im[0], ssm_log_dt[0], ssm_b_re[0], ssm_b_im[0],
                             ssm_c_re[0], ssm_c_im[0])
    h0 = (state_ssm_re.reshape(SEQS, 2 * ALL_ST), state_ssm_im.reshape(SEQS, 2 * ALL_ST))
    rope = _rope_tables(dec_seq)

    k_ctx, v_ctx, kcat_ctx, vt_ctx, utm_ctx, yb_ctx, fin_b = _pass_a(
        x_prompt, mod, nw, w_in0, wb, wc, lb, d, None, None)
    y_prompt, fin_f = _pass_b(x_prompt, mod, nw, w_in0, wb, wc, lb, wg, bg,
                              utm_ctx, yb_ctx, kcat_ctx, vt_ctx, wo, fw, sink, None, None, None, None)

    kcat_lat, vt_lat, utm_lat, yb_lat = _pass_a(
        x_sample, mod, nw, w_in0, wb, wc, lb, d, h0, rope)
    kx = cache_k[:, 0].reshape(dec_batch, PAST_LEN, KV_WIDTH)
    kx = jnp.concatenate([kx, jnp.roll(kx, HEAD_DIM, axis=-1)], axis=-1).astype(BF16)
    vx = jnp.swapaxes(cache_v[:, 0].reshape(dec_batch, PAST_LEN, KV_WIDTH), 1, 2).astype(BF16)
    (y_sample,) = _pass_b(x_sample, mod, nw, w_in0, wb, wc, lb, wg, bg,
                          utm_lat, yb_lat, kx, vx, wo, fw, sink, h0, rope, kcat_lat, vt_lat)

    new_cache_k = k_ctx.reshape(batch, 1, seq, N_KV_HEADS, HEAD_DIM)
    new_cache_v = v_ctx.reshape(batch, 1, seq, N_KV_HEADS, HEAD_DIM)

    def states(fin, part):
        return fin[:, part].reshape(batch, SSM_GROUPS, SSM_STATE)

    new_re = jnp.stack([states(fin_f, 0), states(fin_b, 0)], axis=1)[:, None]
    new_im = jnp.stack([states(fin_f, 1), states(fin_b, 1)], axis=1)[:, None]
    return (y_prompt, y_sample, new_cache_k, new_cache_v, new_re, new_im)
```

```python
import functools
import math

import jax
import jax.numpy as jnp
from jax import lax
from jax.experimental import pallas as pl
from jax.experimental.pallas import tpu as pltpu

F32 = jnp.float32
BF16 = jnp.bfloat16

D_MODEL = 1024
SSM_WIDTH = 512
ATTN_WIDTH = 512
SSM_GROUP = 16
SSM_GROUPS = 32
SSM_STATE = 64
HEAD_DIM = 64
N_HEADS = 8
N_KV_HEADS = 2
KV_REP = 4
KV_WIDTH = 128
GRID_W = 64
ROPE_AXIS_DIM = 32
ROPE_BASE = 10000.0
EPS = 1e-6
LAMBDA_RE_MAX = -1e-4
NEG_INF = -1e30
LOG2E = math.log2(math.e)
PAST_LEN = 256

LANES = 128
BF16_SUBLANES = 16
SEQS = 8
TL = 128
ROWS = SEQS * TL
PITCH = TL + 8
SSM_BLOCKS = SSM_WIDTH // LANES
HALF_CH = SSM_WIDTH // 2
HALF_ST = (SSM_GROUPS // 2) * SSM_STATE
ALL_ST = 2 * HALF_ST
TC_A = 32
TC_B = 32
VMEM_LIMIT = 58 * 1024 * 1024


def _const_spec(shape, index=None):
    index = (0,) * len(shape) if index is None else index
    return pl.BlockSpec(shape, lambda g, t: index, pipeline_mode=pl.Buffered(1))


def _modulation_kernel(cond_ref, w_ref, b_ref, out_ref):
    a = jax.nn.silu(cond_ref[...]).astype(BF16)
    out_ref[...] = jnp.dot(a, w_ref[...].astype(BF16), preferred_element_type=F32) + b_ref[...]


def _modulation(cond, w_mod, b_mod):
    rows = cond.shape[0]
    n = w_mod.shape[1]
    tn = 1024
    return pl.pallas_call(
        _modulation_kernel,
        grid=(n // tn,),
        in_specs=[pl.BlockSpec((rows, D_MODEL), lambda j: (0, 0)),
                  pl.BlockSpec((D_MODEL, tn), lambda j: (0, j)),
                  pl.BlockSpec((1, tn), lambda j: (0, j))],
        out_specs=pl.BlockSpec((rows, tn), lambda j: (0, j)),
        out_shape=jax.ShapeDtypeStruct((rows, n), F32),
        name="modulation",
    )(cond, w_mod, b_mod)


def _zoh(lam_re, lam_im, log_dt):
    lam_re = jnp.minimum(lam_re, LAMBDA_RE_MAX)
    dt = jnp.exp(log_dt)
    mag = jnp.exp(lam_re * dt)
    ang = lam_im * dt
    lbar_re = mag * jnp.cos(ang)
    lbar_im = mag * jnp.sin(ang)
    nr = lbar_re - 1.0
    ni = lbar_im
    den = lam_re * lam_re + lam_im * lam_im
    f_re = (nr * lam_re + ni * lam_im) / den
    f_im = (ni * lam_re - nr * lam_im) / den
    return lbar_re, lbar_im, f_re, f_im


def _s5_weights_kernel(lam_ref, bt_ref, cre_ref, cim_ref, lamflat_ref, wb_ref, wc_ref, lb_ref):
    _, _, f_re, f_im = _zoh(lam_ref[0], lam_ref[1], lam_ref[2])
    b_re, b_im = bt_ref[0], bt_ref[1]
    bb_re = f_re * b_re - f_im * b_im
    bb_im = f_re * b_im + f_im * b_re

    p_row = lax.broadcasted_iota(jnp.int32, (SSM_STATE, HALF_ST), 0)
    p_col = lax.broadcasted_iota(jnp.int32, (SSM_STATE, HALF_ST), 1) & (SSM_STATE - 1)
    spread = jnp.where(p_row == p_col, 1.0, 0.0).astype(BF16)
    g_row = lax.broadcasted_iota(jnp.int32, (HALF_CH, HALF_ST), 0) // SSM_GROUP
    g_col = lax.broadcasted_iota(jnp.int32, (HALF_CH, HALF_ST), 1) // SSM_STATE
    own_block = g_row == g_col

    def blocks(x):
        return jnp.where(own_block, jnp.dot(x.astype(BF16), spread, preferred_element_type=F32), 0.0)

    wb_ref[:, 0:HALF_ST] = blocks(bb_re).astype(BF16)
    wb_ref[:, HALF_ST:2 * HALF_ST] = blocks(bb_im).astype(BF16)
    wc_ref[0:HALF_ST, :] = blocks(cre_ref[...]).T.astype(BF16)
    wc_ref[HALF_ST:2 * HALF_ST, :] = blocks(-cim_ref[...]).T.astype(BF16)

    @pl.when((pl.program_id(0) == 0) & (pl.program_id(1) == 0))
    def _():
        lbar_re, lbar_im, _, _ = _zoh(lamflat_ref[0], lamflat_ref[1], lamflat_ref[2])
        lb_ref[0] = lbar_re
        lb_ref[1] = lbar_im


def _s5_weights(lam_re, lam_im, log_dt, b_re, b_im, c_re, c_im):
    n = 2 * SSM_GROUPS * SSM_GROUP
    lam3 = jnp.stack([lam_re, lam_im, jnp.broadcast_to(log_dt[:, :, None], lam_re.shape)])
    lam_rows = jnp.broadcast_to(lam3[:, :, :, None, :], (3, 2, SSM_GROUPS, SSM_GROUP, SSM_STATE)).reshape(3, n, SSM_STATE)
    lam_flat = jnp.broadcast_to(lam3.reshape(3, 1, 2 * ALL_ST), (3, SEQS, 2 * ALL_ST))
    bt = jnp.transpose(jnp.stack([b_re, b_im]), (0, 1, 2, 4, 3)).reshape(2, n, SSM_STATE)
    quarter = lambda d, h: 2 * d + h
    return pl.pallas_call(
        _s5_weights_kernel,
        grid=(2, 2),
        in_specs=[pl.BlockSpec((3, HALF_CH, SSM_STATE), lambda d, h: (0, quarter(d, h), 0)),
                  pl.BlockSpec((2, HALF_CH, SSM_STATE), lambda d, h: (0, quarter(d, h), 0)),
                  pl.BlockSpec((HALF_CH, SSM_STATE), lambda d, h: (quarter(d, h), 0)),
                  pl.BlockSpec((HALF_CH, SSM_STATE), lambda d, h: (quarter(d, h), 0)),
                  pl.BlockSpec((3, SEQS, 2 * ALL_ST), lambda d, h: (0, 0, 0))],
        out_specs=[pl.BlockSpec((None, None, HALF_CH, 2 * HALF_ST), lambda d, h: (d, h, 0, 0)),
                   pl.BlockSpec((None, None, 2 * HALF_ST, HALF_CH), lambda d, h: (d, h, 0, 0)),
                   pl.BlockSpec((2, SEQS, 2 * ALL_ST), lambda d, h: (0, 0, 0))],
        out_shape=[jax.ShapeDtypeStruct((2, 2, HALF_CH, 2 * HALF_ST), BF16),
                   jax.ShapeDtypeStruct((2, 2, 2 * HALF_ST, HALF_CH), BF16),
                   jax.ShapeDtypeStruct((2, SEQS, 2 * ALL_ST), F32)],
        name="s5_weights",
    )(lam_rows, bt, c_re.reshape(n, SSM_STATE), c_im.reshape(n, SSM_STATE), lam_flat)


def _rope_tables(length):
    pos = jnp.arange(length)
    row = (pos // GRID_W).astype(F32)
    col = (pos % GRID_W).astype(F32)
    freqs = ROPE_BASE ** (-jnp.arange(0, ROPE_AXIS_DIM, 2, dtype=F32) / ROPE_AXIS_DIM)
    ang_r = row[:, None] * freqs[None, :]
    ang_c = col[:, None] * freqs[None, :]
    cos = jnp.concatenate([jnp.cos(ang_r), jnp.cos(ang_r), jnp.cos(ang_c), jnp.cos(ang_c)], axis=-1)
    sin = jnp.concatenate([-jnp.sin(ang_r), jnp.sin(ang_r), -jnp.sin(ang_c), jnp.sin(ang_c)], axis=-1)
    return jnp.tile(cos, (1, 2)), jnp.tile(sin, (1, 2))


def _seq_rows(b):
    return slice(b * TL, (b + 1) * TL)


def _mod_row(b, per_sequence):
    return b if per_sequence else SEQS


def _norm_mod(h_ref, x_ref, mod_ref, nw_ref, b, per_sequence):
    xb = x_ref[b]
    ms = jnp.mean(xb * xb, axis=-1, keepdims=True)
    m = _mod_row(b, per_sequence)
    shift = mod_ref[m:m + 1, 0:D_MODEL]
    gain = nw_ref[...] * (1.0 + mod_ref[m:m + 1, D_MODEL:2 * D_MODEL])
    h_ref[_seq_rows(b), :] = ((xb * lax.rsqrt(ms + EPS)) * gain + shift).astype(BF16)


def _rope(x, cos, sin):
    lane = lax.broadcasted_iota(jnp.int32, x.shape, 1)
    first = (lane & 31) < 16
    partner = jnp.where(first, pltpu.roll(x, LANES - 16, 1), pltpu.roll(x, 16, 1))
    x3 = x.reshape(SEQS, TL, LANES)
    p3 = partner.reshape(SEQS, TL, LANES)
    return (x3 * cos[None] + p3 * sin[None]).reshape(ROWS, LANES)


def _store_column_blocks(dst_ref, val):
    for b in range(SEQS):
        for cb in range(SSM_BLOCKS):
            dst_ref[cb, b * PITCH:b * PITCH + TL, :] = val[_seq_rows(b), cb * LANES:(cb + 1) * LANES]


def _to_token_major(dst_ref, src_ref):
    for l in range(TL):
        for cb in range(SSM_BLOCKS):
            dst_ref[l * SEQS:(l + 1) * SEQS, cb * LANES:(cb + 1) * LANES] = src_ref[cb, pl.ds(l, SEQS, stride=PITCH), :]


def _to_sequence_major(dst_ref, src_ref):
    for l in range(TL):
        for cb in range(SSM_BLOCKS):
            dst_ref[cb, pl.ds(l, SEQS, stride=PITCH), :] = src_ref[l * SEQS:(l + 1) * SEQS, cb * LANES:(cb + 1) * LANES]


def _scan_chunk(s_ref, lb_ref, carry_ref, half, reverse):
    steps = s_ref.shape[0] // SEQS
    cols = slice(half * HALF_ST, (half + 1) * HALF_ST)
    lr = lb_ref[0, :, cols]
    li = lb_ref[1, :, cols]
    sr = carry_ref[0, :, cols]
    si = carry_ref[1, :, cols]
    for i in range(steps):
        tt = (steps - 1 - i) if reverse else i
        rows = slice(tt * SEQS, (tt + 1) * SEQS)
        br = s_ref[rows, 0:HALF_ST]
        bi = s_ref[rows, HALF_ST:2 * HALF_ST]
        sr, si = lr * sr - li * si + br, lr * si + li * sr + bi
        s_ref[rows, 0:HALF_ST] = sr
        s_ref[rows, HALF_ST:2 * HALF_ST] = si
    carry_ref[0, :, cols] = sr
    carry_ref[1, :, cols] = si


def _s5_direction(u_tm_ref, s_ref, carry_ref, wb_ref, wc_ref, lb_ref, first_ref, first_s, emit, reverse,
                  between=None):
    chunk_rows = s_ref.shape[1]
    n_chunks = ROWS // chunk_rows
    first_rows = slice(chunk_rows - SEQS, chunk_rows) if reverse else slice(0, SEQS)
    order = [(n_chunks - 1 - ci) if reverse else ci for ci in range(n_chunks)]
    units = [(half, c) for c in order for half in range(2)]

    def project_in(i):
        half, c = units[i]
        uc = u_tm_ref[c * chunk_rows:(c + 1) * chunk_rows, half * HALF_CH:(half + 1) * HALF_CH].astype(BF16)
        s_ref[i % 2] = jnp.dot(uc, wb_ref[half], preferred_element_type=F32)

    project_in(0)
    for i, (half, c) in enumerate(units):
        if i + 1 < len(units):
            project_in(i + 1)
        buf = s_ref.at[i % 2]
        _scan_chunk(buf, lb_ref, carry_ref, half, reverse)
        if first_ref is not None and c == order[0]:
            cols = slice(half * HALF_ST, (half + 1) * HALF_ST)
            first_s[0, :, cols] = buf[first_rows, 0:HALF_ST]
            first_s[1, :, cols] = buf[first_rows, HALF_ST:2 * HALF_ST]
        y = jnp.dot(buf[...].astype(BF16), wc_ref[half], preferred_element_type=F32)
        emit(c * chunk_rows, chunk_rows, half, y)
        if between is not None:
            between(i)

    if first_ref is not None:
        @pl.when(pl.program_id(1) == 0)
        def _():
            first_ref[...] = first_s[...]


N_S5_UNITS_B = 2 * (TL // TC_B)


def _pass_a_kernel(*refs, rotary):
    if rotary:
        (x_ref, mod_ref, nw_ref, wu_ref, wkv_ref, wb_ref, wc_ref, lb_ref, d_ref, h0re_ref, h0im_ref, cos_ref, sin_ref,
         kcat_ref, vt_ref, utm_ref, yb_ref, h_s, u_sm, u_tm, s_s, carry) = refs
        fin_ref = first_s = None
    else:
        (x_ref, mod_ref, nw_ref, wu_ref, wkv_ref, wb_ref, wc_ref, lb_ref, d_ref,
         k_ref, v_ref, kcat_ref, vt_ref, utm_ref, yb_ref, fin_ref, h_s, u_sm, u_tm, s_s, carry, first_s) = refs

    @pl.when(pl.program_id(1) == 0)
    def _():
        if rotary:
            carry[0] = h0re_ref[...]
            carry[1] = h0im_ref[...]
        else:
            carry[...] = jnp.zeros_like(carry)

    for b in range(SEQS):
        _norm_mod(h_s, x_ref, mod_ref, nw_ref, b, per_sequence=rotary)
    h = h_s[...]
    _store_column_blocks(u_sm, jnp.dot(h, wu_ref[...], preferred_element_type=F32))
    kv = jnp.dot(h, wkv_ref[...], preferred_element_type=F32)
    k = kv[:, 0:KV_WIDTH]
    v = kv[:, KV_WIDTH:2 * KV_WIDTH]
    if rotary:
        k = _rope(k, cos_ref[...], sin_ref[...])
    else:
        k_ref[...] = k.reshape(SEQS, TL, KV_WIDTH)
        v_ref[...] = v.reshape(SEQS, TL, KV_WIDTH)
    kcat = jnp.concatenate([k, pltpu.roll(k, HEAD_DIM, 1)], axis=1).astype(BF16)
    kcat_ref[...] = kcat.reshape(SEQS, TL, 2 * KV_WIDTH)
    for b in range(SEQS):
        vt_ref[b] = v[_seq_rows(b), :].T.astype(BF16)

    _to_token_major(u_tm, u_sm)
    utm_ref[...] = u_tm[...].astype(BF16)

    def emit(r0, nrows, half, y):
        rows, cols = slice(r0, r0 + nrows), slice(half * HALF_CH, (half + 1) * HALF_CH)
        yb_ref[rows, cols] = y + u_tm[rows, cols] * d_ref[:, cols]

    _s5_direction(u_tm, s_s, carry, wb_ref, wc_ref, lb_ref, fin_ref, first_s, emit, reverse=True)


BWD, FWD = 1, 0
W_IN_KV_BLOCK = (2 * SSM_WIDTH + ATTN_WIDTH) // (2 * KV_WIDTH)


def _s5_param_specs(direction):
    return [_const_spec((None, 2, HALF_CH, 2 * HALF_ST), (direction, 0, 0, 0)),
            _const_spec((None, 2, 2 * HALF_ST, HALF_CH), (direction, 0, 0, 0)),
            _const_spec((2, SEQS, ALL_ST), (0, 0, direction))]


def _h0_specs(direction):
    return [_const_spec((SEQS, ALL_ST), (0, direction))] * 2


def _pass_a(x, mod, norm_w, w_in, wb, wc, lb, d, h0, rope):
    batch, length, _ = x.shape
    ng, nt = batch // SEQS, length // TL
    rotary = rope is not None
    rev = lambda t: nt - 1 - t
    in_specs = [
        pl.BlockSpec((SEQS, TL, D_MODEL), lambda g, t: (g, rev(t), 0)),
        _const_spec(mod.shape),
        _const_spec((1, D_MODEL)),
        _const_spec((D_MODEL, SSM_WIDTH), (0, 0)),
        _const_spec((D_MODEL, 2 * KV_WIDTH), (0, W_IN_KV_BLOCK)),
    ] + _s5_param_specs(BWD) + [
        _const_spec((1, SSM_WIDTH)),
    ]
    args = [x, mod, norm_w, w_in, w_in, wb, wc, lb, d]
    if rotary:
        in_specs += _h0_specs(BWD) + [pl.BlockSpec((TL, LANES), lambda g, t: (rev(t), 0))] * 2
        args += list(h0) + list(rope)
    tile_spec = pl.BlockSpec((None, None, ROWS, SSM_WIDTH), lambda g, t: (g, rev(t), 0, 0))
    out_specs = [
        pl.BlockSpec((SEQS, TL, 2 * KV_WIDTH), lambda g, t: (g, rev(t), 0)),
        pl.BlockSpec((SEQS, KV_WIDTH, TL), lambda g, t: (g, 0, rev(t))),
        tile_spec,
        tile_spec,
    ]
    out_shape = [
        jax.ShapeDtypeStruct((batch, length, 2 * KV_WIDTH), BF16),
        jax.ShapeDtypeStruct((batch, KV_WIDTH, length), BF16),
        jax.ShapeDtypeStruct((ng, nt, ROWS, SSM_WIDTH), BF16),
        jax.ShapeDtypeStruct((ng, nt, ROWS, SSM_WIDTH), F32),
    ]
    scratch = [
        pltpu.VMEM((ROWS, D_MODEL), BF16),
        pltpu.VMEM((SSM_BLOCKS, SEQS * PITCH, LANES), F32),
        pltpu.VMEM((ROWS, SSM_WIDTH), F32),
        pltpu.VMEM((2, TC_A * SEQS, 2 * HALF_ST), F32),
        pltpu.VMEM((2, SEQS, ALL_ST), F32),
    ]
    if not rotary:
        cache_spec = pl.BlockSpec((SEQS, TL, KV_WIDTH), lambda g, t: (g, rev(t), 0))
        cache_shape = jax.ShapeDtypeStruct((batch, length, KV_WIDTH), F32)
        out_specs = [cache_spec, cache_spec] + out_specs + [pl.BlockSpec((None, 2, SEQS, ALL_ST), lambda g, t: (g, 0, 0, 0))]
        out_shape = [cache_shape, cache_shape] + out_shape + [jax.ShapeDtypeStruct((ng, 2, SEQS, ALL_ST), F32)]
        scratch += [pltpu.VMEM((2, SEQS, ALL_ST), F32)]
    return pl.pallas_call(
        functools.partial(_pass_a_kernel, rotary=rotary),
        grid=(ng, nt),
        in_specs=in_specs, out_specs=out_specs, out_shape=out_shape, scratch_shapes=scratch,
        compiler_params=pltpu.CompilerParams(
            dimension_semantics=("arbitrary", "arbitrary"), vmem_limit_bytes=VMEM_LIMIT),
        name="pass_a_latent" if rotary else "pass_a_context",
    )(*args)


def _attention(b, q_refs, z_s, mix_s, kcat_refs, vt_refs, sink_ref, masks):
    rows = _seq_rows(b)
    kcat = jnp.concatenate([r[b] for r in kcat_refs], axis=0)
    vt = jnp.concatenate([r[b] for r in vt_refs], axis=1)
    lane = lax.broadcasted_iota(jnp.int32, (1, 2 * TL), 1)
    units = [(g, side) for g in range(N_KV_HEADS) for side in range(2)]

    def scores(g, side):
        qg = jnp.concatenate([q_refs[side][rows, blk * LANES:(blk + 1) * LANES] for blk in (2 * g, 2 * g + 1)], axis=0)
        kg = kcat[:, 0:KV_WIDTH] if g == side else kcat[:, KV_WIDTH:2 * KV_WIDTH]
        return lax.dot_general(kg, qg, (((1,), (1,)), ((), ())), preferred_element_type=F32)

    def softmax(g, side, s):
        if masks is None:
            pieces = [s]
        else:
            pieces = [jnp.where(masks[0], s[0:TL], NEG_INF), s[TL:2 * TL],
                      jnp.where(masks[1], s[2 * TL:3 * TL], NEG_INF), s[3 * TL:]]
        sink = jnp.where(lane < TL, sink_ref[4 * g + side] * LOG2E, sink_ref[4 * g + 2 + side] * LOG2E)
        m = sink
        for piece in pieces:
            m = jnp.maximum(m, jnp.max(piece, axis=0, keepdims=True))
        ps = [jnp.exp2(piece - m).astype(BF16) for piece in pieces]
        return (ps[0] if len(ps) == 1 else jnp.concatenate(ps, axis=0)), jnp.exp2(sink - m)

    ones_rows = jnp.ones((BF16_SUBLANES, vt.shape[1]), BF16)
    values = [jnp.concatenate([vt[g * HEAD_DIM:(g + 1) * HEAD_DIM, :], ones_rows], axis=0) for g in range(N_KV_HEADS)]

    out_t = {}
    all_scores = [scores(g, side) for g, side in units]
    for (g, side), s in zip(units, all_scores):
        p, sink_term = softmax(g, side, s)
        ot = jnp.dot(values[g], p, preferred_element_type=F32)
        out_t[g, side] = ot[0:HEAD_DIM] / (ot[HEAD_DIM:HEAD_DIM + 1] + sink_term)
    for g in range(N_KV_HEADS):
        for i, blk in enumerate((2 * g, 2 * g + 1)):
            lanes = slice(i * TL, (i + 1) * TL)
            pair_t = jnp.concatenate([out_t[g, 0][:, lanes], out_t[g, 1][:, lanes]], axis=0)
            c0 = ATTN_WIDTH + blk * LANES
            mix_s[rows, c0:c0 + LANES] = (pair_t.T * z_s[rows, c0:c0 + LANES].astype(F32)).astype(BF16)


def _pass_b_kernel(*refs, windowed, n_tiles):
    if windowed:
        (sink_ref, x_ref, mod_ref, nw_ref, wz_ref, wq_ref, wza0_ref, wza1_ref, wb_ref, wc_ref, lb_ref, wg_ref, bg_ref,
         utm_ref, yb_ref, kx_ref, vx_ref, wo_ref, fw_ref, h0re_ref, h0im_ref, cos_ref, sin_ref,
         kp_ref, kc_ref, kn_ref, vp_ref, vc_ref, vn_ref,
         y_ref, h_s, y_sm, y_tm, s_s, carry, z_s, ql_s, qr_s, mix_s) = refs
        fin_ref = first_s = None
    else:
        (sink_ref, x_ref, mod_ref, nw_ref, wz_ref, wq_ref, wza0_ref, wza1_ref, wb_ref, wc_ref, lb_ref, wg_ref, bg_ref,
         utm_ref, yb_ref, kx_ref, vx_ref, wo_ref, fw_ref,
         y_ref, fin_ref, h_s, y_sm, y_tm, s_s, carry, z_s, ql_s, qr_s, mix_s, first_s) = refs
    t = pl.program_id(1)

    @pl.when(t == 0)
    def _():
        if windowed:
            carry[0] = h0re_ref[...]
            carry[1] = h0im_ref[...]
        else:
            carry[...] = jnp.zeros_like(carry)

    if windowed:
        kj = lax.broadcasted_iota(jnp.int32, (TL, 2 * TL), 0)
        qi = lax.broadcasted_iota(jnp.int32, (TL, 2 * TL), 1) & (TL - 1)
        masks = ((kj >= qi) & (t > 0), (kj <= qi) & (t < n_tiles - 1))
        key_refs = (kp_ref, kc_ref, kn_ref, kx_ref)
        val_refs = (vp_ref, vc_ref, vn_ref, vx_ref)
    else:
        masks = None
        key_refs = (kx_ref,)
        val_refs = (vx_ref,)

    def project_gate_ssm():
        z = jnp.dot(h_s[...], wz_ref[...], preferred_element_type=F32)
        z_s[:, 0:SSM_WIDTH] = jax.nn.silu(z).astype(BF16)

    def project_attention():
        h = h_s[...]
        qall = jnp.dot(h, wq_ref[...], preferred_element_type=F32)
        even_head = lax.broadcasted_iota(jnp.int32, (ROWS, LANES), 1) < HEAD_DIM
        for cg in range(ATTN_WIDTH // LANES):
            cols = slice(cg * LANES, (cg + 1) * LANES)
            q = qall[:, cols]
            if windowed:
                q = _rope(q, cos_ref[...], sin_ref[...])
            q = q * (LOG2E * HEAD_DIM ** -0.5)
            ql_s[:, cols] = jnp.where(even_head, q, 0.0).astype(BF16)
            qr_s[:, cols] = jnp.where(even_head, 0.0, q).astype(BF16)
        for i, wza_ref in enumerate((wza0_ref, wza1_ref)):
            z = jnp.dot(h, wza_ref[...], preferred_element_type=F32)
            c0 = SSM_WIDTH + i * W_IN_ZA_BLOCK
            z_s[:, c0:c0 + W_IN_ZA_BLOCK] = jax.nn.silu(z).astype(BF16)

    norm_units = N_S5_UNITS_B // 2
    seqs_per_unit = SEQS // (N_S5_UNITS_B - norm_units)

    def between(i):
        if i < norm_units:
            for b in range(i * SEQS // norm_units, (i + 1) * SEQS // norm_units):
                _norm_mod(h_s, x_ref, mod_ref, nw_ref, b, per_sequence=windowed)
            if i == norm_units - 1:
                project_attention()
        else:
            j = i - norm_units
            for b in range(j * seqs_per_unit, (j + 1) * seqs_per_unit):
                _attention(b, (ql_s, qr_s), z_s, mix_s, key_refs, val_refs, sink_ref, masks)
            if j == 0:
                project_gate_ssm()

    def emit(r0, nrows, half, y):
        rows, cols = slice(r0, r0 + nrows), slice(half * HALF_CH, (half + 1) * HALF_CH)
        y_tm[rows, cols] = y + yb_ref[rows, cols]

    _s5_direction(utm_ref, s_s, carry, wb_ref, wc_ref, lb_ref, fin_ref, first_s, emit, reverse=False,
                  between=between)

    glu_rows = 256
    for rc in range(ROWS // glu_rows):
        rr = slice(rc * glu_rows, (rc + 1) * glu_rows)
        y = jax.nn.gelu(y_tm[rr, :])
        gl = jnp.dot(y.astype(BF16), wg_ref[...], preferred_element_type=F32) + bg_ref[...]
        y_tm[rr, :] = y * jax.nn.sigmoid(gl)
    _to_sequence_major(y_sm, y_tm)
    for b in range(SEQS):
        for cb in range(SSM_BLOCKS):
            cols = slice(cb * LANES, (cb + 1) * LANES)
            gated = y_sm[cb, b * PITCH:b * PITCH + TL, :] * z_s[_seq_rows(b), cols].astype(F32)
            mix_s[_seq_rows(b), cols] = gated.astype(BF16)

    half_seqs = SEQS // 2
    for hf in range(2):
        rows = slice(hf * half_seqs * TL, (hf + 1) * half_seqs * TL)
        proj = jnp.dot(mix_s[rows, :], wo_ref[...], preferred_element_type=F32)
        y_ref[hf * half_seqs:(hf + 1) * half_seqs] = proj.reshape(half_seqs, TL, D_MODEL)
    for b in range(SEQS):
        m = _mod_row(b, per_sequence=windowed)
        gate = mod_ref[m:m + 1, 2 * D_MODEL:3 * D_MODEL]
        r = x_ref[b] + gate * y_ref[b]
        ms = jnp.mean(r * r, axis=-1, keepdims=True)
        y_ref[b] = (r * lax.rsqrt(ms + EPS)) * fw_ref[...]


W_IN_ZA_BLOCK = ATTN_WIDTH // 2
W_IN_ZA_FIRST = (2 * SSM_WIDTH + ATTN_WIDTH + 2 * KV_WIDTH) // W_IN_ZA_BLOCK


def _pass_b(x, mod, norm_w, w_in, wb, wc, lb, w_glu, b_glu, utm, yb, kx, vx, w_out, fnorm_w, sink,
            h0, rope, k_loc, v_loc):
    batch, length, _ = x.shape
    ng, nt = batch // SEQS, length // TL
    windowed = rope is not None
    tile_spec = pl.BlockSpec((None, None, ROWS, SSM_WIDTH), lambda g, t: (g, t, 0, 0))
    in_specs = [
        pl.BlockSpec(memory_space=pltpu.SMEM),
        pl.BlockSpec((SEQS, TL, D_MODEL), lambda g, t: (g, t, 0)),
        _const_spec(mod.shape),
        _const_spec((1, D_MODEL)),
        _const_spec((D_MODEL, SSM_WIDTH), (0, 1)),
        _const_spec((D_MODEL, ATTN_WIDTH), (0, 2)),
        _const_spec((D_MODEL, W_IN_ZA_BLOCK), (0, W_IN_ZA_FIRST)),
        _const_spec((D_MODEL, W_IN_ZA_BLOCK), (0, W_IN_ZA_FIRST + 1)),
    ] + _s5_param_specs(FWD) + [
        _const_spec((SSM_WIDTH, SSM_WIDTH)),
        _const_spec((1, SSM_WIDTH)),
        tile_spec,
        tile_spec,
        pl.BlockSpec((SEQS, PAST_LEN, 2 * KV_WIDTH), lambda g, t: (g, 0, 0)),
        pl.BlockSpec((SEQS, KV_WIDTH, PAST_LEN), lambda g, t: (g, 0, 0)),
        _const_spec((D_MODEL, D_MODEL)),
        _const_spec((1, D_MODEL)),
    ]
    args = [sink, x, mod, norm_w, w_in, w_in, w_in, w_in, wb, wc, lb, w_glu, b_glu, utm, yb, kx, vx, w_out, fnorm_w]
    if windowed:
        in_specs += _h0_specs(FWD) + [pl.BlockSpec((TL, LANES), lambda g, t: (t, 0))] * 2
        args += list(h0) + list(rope)
        band_t = (lambda t: jnp.maximum(t - 1, 0), lambda t: t, lambda t: jnp.minimum(t + 1, nt - 1))
        in_specs += [pl.BlockSpec((SEQS, TL, 2 * KV_WIDTH), lambda g, t, f=f: (g, f(t), 0)) for f in band_t]
        in_specs += [pl.BlockSpec((SEQS, KV_WIDTH, TL), lambda g, t, f=f: (g, 0, f(t))) for f in band_t]
        args += [k_loc] * 3 + [v_loc] * 3
    out_specs = [pl.BlockSpec((SEQS, TL, D_MODEL), lambda g, t: (g, t, 0))]
    out_shape = [jax.ShapeDtypeStruct((batch, length, D_MODEL), F32)]
    scratch = [
        pltpu.VMEM((ROWS, D_MODEL), BF16),
        pltpu.VMEM((SSM_BLOCKS, SEQS * PITCH, LANES), F32),
        pltpu.VMEM((ROWS, SSM_WIDTH), F32),
        pltpu.VMEM((2, TC_B * SEQS, 2 * HALF_ST), F32),
        pltpu.VMEM((2, SEQS, ALL_ST), F32),
        pltpu.VMEM((ROWS, 2 * SSM_WIDTH), BF16),
        pltpu.VMEM((ROWS, ATTN_WIDTH), BF16),
        pltpu.VMEM((ROWS, ATTN_WIDTH), BF16),
        pltpu.VMEM((ROWS, D_MODEL), BF16),
    ]
    if not windowed:
        out_specs += [pl.BlockSpec((None, 2, SEQS, ALL_ST), lambda g, t: (g, 0, 0, 0))]
        out_shape += [jax.ShapeDtypeStruct((ng, 2, SEQS, ALL_ST), F32)]
        scratch += [pltpu.VMEM((2, SEQS, ALL_ST), F32)]
    return pl.pallas_call(
        functools.partial(_pass_b_kernel, windowed=windowed, n_tiles=nt),
        grid=(ng, nt),
        in_specs=in_specs, out_specs=out_specs, out_shape=out_shape, scratch_shapes=scratch,
        compiler_params=pltpu.CompilerParams(
            dimension_semantics=("arbitrary", "arbitrary"), vmem_limit_bytes=VMEM_LIMIT),
        name="pass_b_latent" if windowed else "pass_b_context",
    )(*args)


def kernel(x_prompt, x_sample, c, cache_k, cache_v, state_ssm_re, state_ssm_im, c_ctx, norm_w, w_mod, b_mod, w_in, ssm_lambda_re, ssm_lambda_im, ssm_log_dt, ssm_b_re, ssm_b_im, ssm_c_re, ssm_c_im, ssm_d, w_glu, b_glu, attn_sink, w_out, final_norm_w):
    assert norm_w.shape[0] == 1, "single trunk layer"
    batch, seq, _ = x_prompt.shape
    dec_batch, dec_seq, _ = x_sample.shape
    assert dec_batch == SEQS and batch % SEQS == 0 and seq % TL == 0 and dec_seq % TL == 0
    assert seq == PAST_LEN and cache_k.shape[2] == PAST_LEN

    w_in0 = w_in[0].astype(BF16)
    nw = norm_w[0][None, :]
    fw = final_norm_w[None, :]
    d = ssm_d[0][None, :]
    wg = w_glu[0].astype(BF16)
    bg = b_glu[0][None, :]
    wo = w_out[0].astype(BF16)
    sink = attn_sink[0]

    cond = jnp.concatenate([c, c_ctx[None, :], jnp.zeros((16 - SEQS - 1, D_MODEL), F32)], axis=0)
    mod = _modulation(cond, w_mod[0], b_mod[0][None, :])

    wb, wc, lb = _s5_weights(ssm_lambda_re[0], ssm_lambda_im[0], ssm_log_dt[0], ssm_b_re[0], ssm_b_im[0],
                             ssm_c_re[0], ssm_c_im[0])
    h0 = (state_ssm_re.reshape(SEQS, 2 * ALL_ST), state_ssm_im.reshape(SEQS, 2 * ALL_ST))
    rope = _rope_tables(dec_seq)

    k_ctx, v_ctx, kcat_ctx, vt_ctx, utm_ctx, yb_ctx, fin_b = _pass_a(
        x_prompt, mod, nw, w_in0, wb, wc, lb, d, None, None)
    y_prompt, fin_f = _pass_b(x_prompt, mod, nw, w_in0, wb, wc, lb, wg, bg,
                              utm_ctx, yb_ctx, kcat_ctx, vt_ctx, wo, fw, sink, None, None, None, None)

    kcat_lat, vt_lat, utm_lat, yb_lat = _pass_a(
        x_sample, mod, nw, w_in0, wb, wc, lb, d, h0, rope)
    kx = cache_k[:, 0].reshape(dec_batch, PAST_LEN, KV_WIDTH)
    kx = jnp.concatenate([kx, jnp.roll(kx, HEAD_DIM, axis=-1)], axis=-1).astype(BF16)
    vx = jnp.swapaxes(cache_v[:, 0].reshape(dec_batch, PAST_LEN, KV_WIDTH), 1, 2).astype(BF16)
    (y_sample,) = _pass_b(x_sample, mod, nw, w_in0, wb, wc, lb, wg, bg,
                          utm_lat, yb_lat, kx, vx, wo, fw, sink, h0, rope, kcat_lat, vt_lat)

    new_cache_k = k_ctx.reshape(batch, 1, seq, N_KV_HEADS, HEAD_DIM)
    new_cache_v = v_ctx.reshape(batch, 1, seq, N_KV_HEADS, HEAD_DIM)

    def states(fin, part):
        return fin[:, part].reshape(batch, SSM_GROUPS, SSM_STATE)

    new_re = jnp.stack([states(fin_f, 0), states(fin_b, 0)], axis=1)[:, None]
    new_im = jnp.stack([states(fin_f, 1), states(fin_b, 1)], axis=1)[:, None]
    return (y_prompt, y_sample, new_cache_k, new_cache_v, new_re, new_im)
```

```python
import functools
import math

import jax
import jax.numpy as jnp
from jax import lax
from jax.experimental import pallas as pl
from jax.experimental.pallas import tpu as pltpu

F32 = jnp.float32
BF16 = jnp.bfloat16

D_MODEL = 1024
SSM_WIDTH = 512
ATTN_WIDTH = 512
SSM_GROUP = 16
SSM_GROUPS = 32
SSM_STATE = 64
HEAD_DIM = 64
N_HEADS = 8
N_KV_HEADS = 2
KV_REP = 4
KV_WIDTH = 128
GRID_W = 64
ROPE_AXIS_DIM = 32
ROPE_BASE = 10000.0
EPS = 1e-6
LAMBDA_RE_MAX = -1e-4
NEG_INF = -1e30
LOG2E = math.log2(math.e)
PAST_LEN = 256

LANES = 128
BF16_SUBLANES = 16
SEQS = 8
TL = 128
ROWS = SEQS * TL
PITCH = TL + 8
SSM_BLOCKS = SSM_WIDTH // LANES
HALF_CH = SSM_WIDTH // 2
HALF_ST = (SSM_GROUPS // 2) * SSM_STATE
ALL_ST = 2 * HALF_ST
TC_A = 32
TC_B = 32
VMEM_LIMIT = 58 * 1024 * 1024


def _const_spec(shape, index=None):
    index = (0,) * len(shape) if index is None else index
    return pl.BlockSpec(shape, lambda g, t: index, pipeline_mode=pl.Buffered(1))


def _modulation_kernel(cond_ref, w_ref, b_ref, out_ref):
    a = jax.nn.silu(cond_ref[...]).astype(BF16)
    out_ref[...] = jnp.dot(a, w_ref[...].astype(BF16), preferred_element_type=F32) + b_ref[...]


def _modulation(cond, w_mod, b_mod):
    rows = cond.shape[0]
    n = w_mod.shape[1]
    tn = 1024
    return pl.pallas_call(
        _modulation_kernel,
        grid=(n // tn,),
        in_specs=[pl.BlockSpec((rows, D_MODEL), lambda j: (0, 0)),
                  pl.BlockSpec((D_MODEL, tn), lambda j: (0, j)),
                  pl.BlockSpec((1, tn), lambda j: (0, j))],
        out_specs=pl.BlockSpec((rows, tn), lambda j: (0, j)),
        out_shape=jax.ShapeDtypeStruct((rows, n), F32),
        name="modulation",
    )(cond, w_mod, b_mod)


def _zoh(lam_re, lam_im, log_dt):
    lam_re = jnp.minimum(lam_re, LAMBDA_RE_MAX)
    dt = jnp.exp(log_dt)
    mag = jnp.exp(lam_re * dt)
    ang = lam_im * dt
    lbar_re = mag * jnp.cos(ang)
    lbar_im = mag * jnp.sin(ang)
    nr = lbar_re - 1.0
    ni = lbar_im
    den = lam_re * lam_re + lam_im * lam_im
    f_re = (nr * lam_re + ni * lam_im) / den
    f_im = (ni * lam_re - nr * lam_im) / den
    return lbar_re, lbar_im, f_re, f_im


def _s5_weights_kernel(lam_ref, bt_ref, cre_ref, cim_ref, lamflat_ref, wb_ref, wc_ref, lb_ref):
    _, _, f_re, f_im = _zoh(lam_ref[0], lam_ref[1], lam_ref[2])
    b_re, b_im = bt_ref[0], bt_ref[1]
    bb_re = f_re * b_re - f_im * b_im
    bb_im = f_re * b_im + f_im * b_re

    p_row = lax.broadcasted_iota(jnp.int32, (SSM_STATE, HALF_ST), 0)
    p_col = lax.broadcasted_iota(jnp.int32, (SSM_STATE, HALF_ST), 1) & (SSM_STATE - 1)
    spread = jnp.where(p_row == p_col, 1.0, 0.0).astype(BF16)
    g_row = lax.broadcasted_iota(jnp.int32, (HALF_CH, HALF_ST), 0) // SSM_GROUP
    g_col = lax.broadcasted_iota(jnp.int32, (HALF_CH, HALF_ST), 1) // SSM_STATE
    own_block = g_row == g_col

    def blocks(x):
        return jnp.where(own_block, jnp.dot(x.astype(BF16), spread, preferred_element_type=F32), 0.0)

    wb_ref[:, 0:HALF_ST] = blocks(bb_re).astype(BF16)
    wb_ref[:, HALF_ST:2 * HALF_ST] = blocks(bb_im).astype(BF16)
    wc_ref[0:HALF_ST, :] = blocks(cre_ref[...]).T.astype(BF16)
    wc_ref[HALF_ST:2 * HALF_ST, :] = blocks(-cim_ref[...]).T.astype(BF16)

    @pl.when((pl.program_id(0) == 0) & (pl.program_id(1) == 0))
    def _():
        lbar_re, lbar_im, _, _ = _zoh(lamflat_ref[0], lamflat_ref[1], lamflat_ref[2])
        lb_ref[0] = lbar_re
        lb_ref[1] = lbar_im


def _s5_weights(lam_re, lam_im, log_dt, b_re, b_im, c_re, c_im):
    n = 2 * SSM_GROUPS * SSM_GROUP
    lam3 = jnp.stack([lam_re, lam_im, jnp.broadcast_to(log_dt[:, :, None], lam_re.shape)])
    lam_rows = jnp.broadcast_to(lam3[:, :, :, None, :], (3, 2, SSM_GROUPS, SSM_GROUP, SSM_STATE)).reshape(3, n, SSM_STATE)
    lam_flat = jnp.broadcast_to(lam3.reshape(3, 1, 2 * ALL_ST), (3, SEQS, 2 * ALL_ST))
    bt = jnp.transpose(jnp.stack([b_re, b_im]), (0, 1, 2, 4, 3)).reshape(2, n, SSM_STATE)
    quarter = lambda d, h: 2 * d + h
    return pl.pallas_call(
        _s5_weights_kernel,
        grid=(2, 2),
        in_specs=[pl.BlockSpec((3, HALF_CH, SSM_STATE), lambda d, h: (0, quarter(d, h), 0)),
                  pl.BlockSpec((2, HALF_CH, SSM_STATE), lambda d, h: (0, quarter(d, h), 0)),
                  pl.BlockSpec((HALF_CH, SSM_STATE), lambda d, h: (quarter(d, h), 0)),
                  pl.BlockSpec((HALF_CH, SSM_STATE), lambda d, h: (quarter(d, h), 0)),
                  pl.BlockSpec((3, SEQS, 2 * ALL_ST), lambda d, h: (0, 0, 0))],
        out_specs=[pl.BlockSpec((None, None, HALF_CH, 2 * HALF_ST), lambda d, h: (d, h, 0, 0)),
                   pl.BlockSpec((None, None, 2 * HALF_ST, HALF_CH), lambda d, h: (d, h, 0, 0)),
                   pl.BlockSpec((2, SEQS, 2 * ALL_ST), lambda d, h: (0, 0, 0))],
        out_shape=[jax.ShapeDtypeStruct((2, 2, HALF_CH, 2 * HALF_ST), BF16),
                   jax.ShapeDtypeStruct((2, 2, 2 * HALF_ST, HALF_CH), BF16),
                   jax.ShapeDtypeStruct((2, SEQS, 2 * ALL_ST), F32)],
        name="s5_weights",
    )(lam_rows, bt, c_re.reshape(n, SSM_STATE), c_im.reshape(n, SSM_STATE), lam_flat)


def _rope_tables(length):
    pos = jnp.arange(length)
    row = (pos // GRID_W).astype(F32)
    col = (pos % GRID_W).astype(F32)
    freqs = ROPE_BASE ** (-jnp.arange(0, ROPE_AXIS_DIM, 2, dtype=F32) / ROPE_AXIS_DIM)
    ang_r = row[:, None] * freqs[None, :]
    ang_c = col[:, None] * freqs[None, :]
    cos = jnp.concatenate([jnp.cos(ang_r), jnp.cos(ang_r), jnp.cos(ang_c), jnp.cos(ang_c)], axis=-1)
    sin = jnp.concatenate([-jnp.sin(ang_r), jnp.sin(ang_r), -jnp.sin(ang_c), jnp.sin(ang_c)], axis=-1)
    return jnp.tile(cos, (1, 2)), jnp.tile(sin, (1, 2))


def _seq_rows(b):
    return slice(b * TL, (b + 1) * TL)


def _mod_row(b, per_sequence):
    return b if per_sequence else SEQS


def _norm_mod(h_ref, x_ref, mod_ref, nw_ref, b, per_sequence):
    xb = x_ref[b]
    ms = jnp.mean(xb * xb, axis=-1, keepdims=True)
    m = _mod_row(b, per_sequence)
    shift = mod_ref[m:m + 1, 0:D_MODEL]
    gain = nw_ref[...] * (1.0 + mod_ref[m:m + 1, D_MODEL:2 * D_MODEL])
    h_ref[_seq_rows(b), :] = ((xb * lax.rsqrt(ms + EPS)) * gain + shift).astype(BF16)


def _rope(x, cos, sin):
    lane = lax.broadcasted_iota(jnp.int32, x.shape, 1)
    first = (lane & 31) < 16
    partner = jnp.where(first, pltpu.roll(x, LANES - 16, 1), pltpu.roll(x, 16, 1))
    x3 = x.reshape(SEQS, TL, LANES)
    p3 = partner.reshape(SEQS, TL, LANES)
    return (x3 * cos[None] + p3 * sin[None]).reshape(ROWS, LANES)


def _store_column_blocks(dst_ref, val, first_seq):
    for j in range(val.shape[0] // TL):
        b = first_seq + j
        for cb in range(SSM_BLOCKS):
            dst_ref[cb, b * PITCH:b * PITCH + TL, :] = val[_seq_rows(j), cb * LANES:(cb + 1) * LANES]


def _to_token_major(dst_ref, src_ref):
    for l in range(TL):
        for cb in range(SSM_BLOCKS):
            dst_ref[l * SEQS:(l + 1) * SEQS, cb * LANES:(cb + 1) * LANES] = src_ref[cb, pl.ds(l, SEQS, stride=PITCH), :]


def _to_sequence_major(dst_ref, src_ref):
    for l in range(TL):
        for cb in range(SSM_BLOCKS):
            dst_ref[cb, pl.ds(l, SEQS, stride=PITCH), :] = src_ref[l * SEQS:(l + 1) * SEQS, cb * LANES:(cb + 1) * LANES]


def _scan_chunk(s_ref, lb_ref, carry_ref, half, reverse):
    steps = s_ref.shape[0] // SEQS
    cols = slice(half * HALF_ST, (half + 1) * HALF_ST)
    lr = lb_ref[0, :, cols]
    li = lb_ref[1, :, cols]
    sr = carry_ref[0, :, cols]
    si = carry_ref[1, :, cols]
    for i in range(steps):
        tt = (steps - 1 - i) if reverse else i
        rows = slice(tt * SEQS, (tt + 1) * SEQS)
        br = s_ref[rows, 0:HALF_ST]
        bi = s_ref[rows, HALF_ST:2 * HALF_ST]
        sr, si = lr * sr - li * si + br, lr * si + li * sr + bi
        s_ref[rows, 0:HALF_ST] = sr
        s_ref[rows, HALF_ST:2 * HALF_ST] = si
    carry_ref[0, :, cols] = sr
    carry_ref[1, :, cols] = si


def _s5_direction(u_tm_ref, s_ref, carry_ref, wb_ref, wc_ref, lb_ref, first_ref, first_s, emit, reverse,
                  between=None):
    chunk_rows = s_ref.shape[1]
    n_chunks = ROWS // chunk_rows
    first_rows = slice(chunk_rows - SEQS, chunk_rows) if reverse else slice(0, SEQS)
    order = [(n_chunks - 1 - ci) if reverse else ci for ci in range(n_chunks)]
    units = [(half, c) for c in order for half in range(2)]

    def project_in(i):
        half, c = units[i]
        uc = u_tm_ref[c * chunk_rows:(c + 1) * chunk_rows, half * HALF_CH:(half + 1) * HALF_CH].astype(BF16)
        s_ref[i % 2] = jnp.dot(uc, wb_ref[half], preferred_element_type=F32)

    project_in(0)
    for i, (half, c) in enumerate(units):
        if i + 1 < len(units):
            project_in(i + 1)
        buf = s_ref.at[i % 2]
        _scan_chunk(buf, lb_ref, carry_ref, half, reverse)
        if first_ref is not None and c == order[0]:
            cols = slice(half * HALF_ST, (half + 1) * HALF_ST)
            first_s[0, :, cols] = buf[first_rows, 0:HALF_ST]
            first_s[1, :, cols] = buf[first_rows, HALF_ST:2 * HALF_ST]
        y = jnp.dot(buf[...].astype(BF16), wc_ref[half], preferred_element_type=F32)
        emit(c * chunk_rows, chunk_rows, half, y)
        if between is not None:
            between(i)

    if first_ref is not None:
        @pl.when(pl.program_id(1) == 0)
        def _():
            first_ref[...] = first_s[...]


N_S5_UNITS_B = 2 * (TL // TC_B)


def _pass_a_kernel(*refs, rotary):
    if rotary:
        (x_ref, mod_ref, nw_ref, w_ref, wb_ref, wc_ref, lb_ref, d_ref, h0re_ref, h0im_ref, cos_ref, sin_ref,
         kcat_ref, vt_ref, utm_ref, yb_ref, zs_ref, za_ref, ql_ref, qr_ref, h_s, u_sm, u_tm, s_s, carry) = refs
        fin_ref = first_s = None
    else:
        (x_ref, mod_ref, nw_ref, w_ref, wb_ref, wc_ref, lb_ref, d_ref,
         k_ref, v_ref, kcat_ref, vt_ref, utm_ref, yb_ref, zs_ref, za_ref, ql_ref, qr_ref, fin_ref,
         h_s, u_sm, u_tm, s_s, carry, first_s) = refs

    @pl.when(pl.program_id(1) == 0)
    def _():
        if rotary:
            carry[0] = h0re_ref[...]
            carry[1] = h0im_ref[...]
        else:
            carry[...] = jnp.zeros_like(carry)

    half_seqs = SEQS // 2
    for hf in range(2):
        for b in range(hf * half_seqs, (hf + 1) * half_seqs):
            _norm_mod(h_s, x_ref, mod_ref, nw_ref, b, per_sequence=rotary)
        rows = slice(hf * half_seqs * TL, (hf + 1) * half_seqs * TL)
        u_half = jnp.dot(h_s[rows, :], w_ref[:, W_IN_U:W_IN_U + SSM_WIDTH], preferred_element_type=F32)
        _store_column_blocks(u_sm, u_half, first_seq=hf * half_seqs)
    _to_token_major(u_tm, u_sm)

    def project(c0, width):
        return jnp.dot(h_s[...], w_ref[:, c0:c0 + width], preferred_element_type=F32)

    utm_ref[...] = u_tm[...].astype(BF16)

    def project_kv():
        kv = project(W_IN_K, 2 * KV_WIDTH)
        k = kv[:, 0:KV_WIDTH]
        v = kv[:, KV_WIDTH:2 * KV_WIDTH]
        if rotary:
            k = _rope(k, cos_ref[...], sin_ref[...])
        else:
            k_ref[...] = k.reshape(SEQS, TL, KV_WIDTH)
            v_ref[...] = v.reshape(SEQS, TL, KV_WIDTH)
        kcat = jnp.concatenate([k, pltpu.roll(k, HEAD_DIM, 1)], axis=1).astype(BF16)
        kcat_ref[...] = kcat.reshape(SEQS, TL, 2 * KV_WIDTH)
        for b in range(SEQS):
            vt_ref[b] = v[_seq_rows(b), :].T.astype(BF16)

    def project_q():
        qall = project(W_IN_Q, ATTN_WIDTH)
        even_head = lax.broadcasted_iota(jnp.int32, (ROWS, LANES), 1) < HEAD_DIM
        for cg in range(ATTN_WIDTH // LANES):
            cols = slice(cg * LANES, (cg + 1) * LANES)
            q = qall[:, cols]
            if rotary:
                q = _rope(q, cos_ref[...], sin_ref[...])
            q = q * (LOG2E * HEAD_DIM ** -0.5)
            ql_ref[:, cols] = jnp.where(even_head, q, 0.0).astype(BF16)
            qr_ref[:, cols] = jnp.where(even_head, 0.0, q).astype(BF16)

    def project_gate(c0, out_ref):
        out_ref[...] = jax.nn.silu(project(c0, out_ref.shape[1])).astype(BF16)

    side_work = [project_kv, project_q, lambda: project_gate(W_IN_ZS, zs_ref), lambda: project_gate(W_IN_ZA, za_ref)]

    def between(i):
        if i < len(side_work):
            side_work[i]()

    def emit(r0, nrows, half, y):
        rows, cols = slice(r0, r0 + nrows), slice(half * HALF_CH, (half + 1) * HALF_CH)
        yb_ref[rows, cols] = y + u_tm[rows, cols] * d_ref[:, cols]

    _s5_direction(u_tm, s_s, carry, wb_ref, wc_ref, lb_ref, fin_ref, first_s, emit, reverse=True,
                  between=between)


BWD, FWD = 1, 0
W_IN_U = 0
W_IN_ZS = SSM_WIDTH
W_IN_Q = 2 * SSM_WIDTH
W_IN_K = 2 * SSM_WIDTH + ATTN_WIDTH
W_IN_ZA = W_IN_K + 2 * KV_WIDTH
D_IN = W_IN_ZA + ATTN_WIDTH


def _s5_param_specs(direction):
    return [_const_spec((None, 2, HALF_CH, 2 * HALF_ST), (direction, 0, 0, 0)),
            _const_spec((None, 2, 2 * HALF_ST, HALF_CH), (direction, 0, 0, 0)),
            _const_spec((2, SEQS, ALL_ST), (0, 0, direction))]


def _h0_specs(direction):
    return [_const_spec((SEQS, ALL_ST), (0, direction))] * 2


def _pass_a(x, mod, norm_w, w_in, wb, wc, lb, d, h0, rope):
    batch, length, _ = x.shape
    ng, nt = batch // SEQS, length // TL
    rotary = rope is not None
    rev = lambda t: nt - 1 - t
    in_specs = [
        pl.BlockSpec((SEQS, TL, D_MODEL), lambda g, t: (g, rev(t), 0)),
        _const_spec(mod.shape),
        _const_spec((1, D_MODEL)),
        _const_spec((D_MODEL, D_IN)),
    ] + _s5_param_specs(BWD) + [
        _const_spec((1, SSM_WIDTH)),
    ]
    args = [x, mod, norm_w, w_in, wb, wc, lb, d]
    if rotary:
        in_specs += _h0_specs(BWD) + [pl.BlockSpec((TL, LANES), lambda g, t: (rev(t), 0))] * 2
        args += list(h0) + list(rope)
    tile_spec = pl.BlockSpec((None, None, ROWS, SSM_WIDTH), lambda g, t: (g, rev(t), 0, 0))
    tile_bf16 = jax.ShapeDtypeStruct((ng, nt, ROWS, SSM_WIDTH), BF16)
    assert ATTN_WIDTH == SSM_WIDTH
    out_specs = [
        pl.BlockSpec((SEQS, TL, 2 * KV_WIDTH), lambda g, t: (g, rev(t), 0)),
        pl.BlockSpec((SEQS, KV_WIDTH, TL), lambda g, t: (g, 0, rev(t))),
    ] + [tile_spec] * 6
    out_shape = [
        jax.ShapeDtypeStruct((batch, length, 2 * KV_WIDTH), BF16),
        jax.ShapeDtypeStruct((batch, KV_WIDTH, length), BF16),
        tile_bf16,
        jax.ShapeDtypeStruct((ng, nt, ROWS, SSM_WIDTH), F32),
        tile_bf16, tile_bf16,
        tile_bf16, tile_bf16,
    ]
    scratch = [
        pltpu.VMEM((ROWS, D_MODEL), BF16),
        pltpu.VMEM((SSM_BLOCKS, SEQS * PITCH, LANES), F32),
        pltpu.VMEM((ROWS, SSM_WIDTH), F32),
        pltpu.VMEM((2, TC_A * SEQS, 2 * HALF_ST), F32),
        pltpu.VMEM((2, SEQS, ALL_ST), F32),
    ]
    if not rotary:
        cache_spec = pl.BlockSpec((SEQS, TL, KV_WIDTH), lambda g, t: (g, rev(t), 0))
        cache_shape = jax.ShapeDtypeStruct((batch, length, KV_WIDTH), F32)
        out_specs = [cache_spec, cache_spec] + out_specs + [pl.BlockSpec((None, 2, SEQS, ALL_ST), lambda g, t: (g, 0, 0, 0))]
        out_shape = [cache_shape, cache_shape] + out_shape + [jax.ShapeDtypeStruct((ng, 2, SEQS, ALL_ST), F32)]
        scratch += [pltpu.VMEM((2, SEQS, ALL_ST), F32)]
    return pl.pallas_call(
        functools.partial(_pass_a_kernel, rotary=rotary),
        grid=(ng, nt),
        in_specs=in_specs, out_specs=out_specs, out_shape=out_shape, scratch_shapes=scratch,
        compiler_params=pltpu.CompilerParams(
            dimension_semantics=("arbitrary", "arbitrary"), vmem_limit_bytes=VMEM_LIMIT),
        name="pass_a_latent" if rotary else "pass_a_context",
    )(*args)


def _attention(b, q_refs, za_ref, mix_s, kcat_refs, vt_refs, sink_ref, masks):
    rows = _seq_rows(b)
    kcat = jnp.concatenate([r[b] for r in kcat_refs], axis=0)
    vt = jnp.concatenate([r[b] for r in vt_refs], axis=1)
    segment = lax.broadcasted_iota(jnp.int32, (1, 4 * TL), 1) // TL
    kv_heads = range(N_KV_HEADS)

    def side(g, grp):
        return g ^ grp

    def head(g, i, grp):
        return 4 * g + 2 * i + side(g, grp)

    def scores(grp):
        q = jnp.concatenate([q_refs[side(g, grp)][rows, blk * LANES:(blk + 1) * LANES]
                             for g in kv_heads for blk in (2 * g, 2 * g + 1)], axis=0)
        return lax.dot_general(kcat[:, grp * KV_WIDTH:(grp + 1) * KV_WIDTH], q, (((1,), (1,)), ((), ())),
                               preferred_element_type=F32)

    def softmax(grp, s):
        if masks is None:
            pieces = [s]
        else:
            pieces = [jnp.where(masks[0], s[0:TL], NEG_INF), s[TL:2 * TL],
                      jnp.where(masks[1], s[2 * TL:3 * TL], NEG_INF), s[3 * TL:]]
        sinks = [sink_ref[head(g, i, grp)] * LOG2E for g in kv_heads for i in range(2)]
        sink = jnp.where(segment == 0, sinks[0], jnp.where(segment == 1, sinks[1],
                                                           jnp.where(segment == 2, sinks[2], sinks[3])))
        m = sink
        for piece in pieces:
            m = jnp.maximum(m, jnp.max(piece, axis=0, keepdims=True))
        ps = [jnp.exp2(piece - m).astype(BF16) for piece in pieces]
        return (ps[0] if len(ps) == 1 else jnp.concatenate(ps, axis=0)), jnp.exp2(sink - m)

    ones_rows = jnp.ones((BF16_SUBLANES, vt.shape[1]), BF16)

    all_scores = [scores(grp) for grp in range(2)]
    probs, sink_terms = zip(*[softmax(grp, s) for grp, s in enumerate(all_scores)])
    for g in kv_heads:
        cols = slice(g * 2 * TL, (g + 1) * 2 * TL)
        values = jnp.concatenate([vt[g * HEAD_DIM:(g + 1) * HEAD_DIM, :], ones_rows], axis=0)
        p = jnp.concatenate([probs[0][:, cols], probs[1][:, cols]], axis=1)
        sink_term = jnp.concatenate([sink_terms[0][:, cols], sink_terms[1][:, cols]], axis=1)
        ot = jnp.dot(values, p, preferred_element_type=F32)
        out_t = ot[0:HEAD_DIM] / (ot[HEAD_DIM:HEAD_DIM + 1] + sink_term)
        for i in range(2):
            left = out_t[:, (2 * g + i) * TL:(2 * g + i + 1) * TL]
            right = out_t[:, (2 * (g ^ 1) + i) * TL:(2 * (g ^ 1) + i + 1) * TL]
            blk_cols = slice((2 * g + i) * LANES, (2 * g + i + 1) * LANES)
            pair_t = jnp.concatenate([left, right], axis=0)
            gated = pair_t.T * za_ref[rows, blk_cols].astype(F32)
            mix_s[rows, ATTN_WIDTH + blk_cols.start:ATTN_WIDTH + blk_cols.stop] = gated.astype(BF16)


def _pass_b_kernel(*refs, windowed, n_tiles):
    if windowed:
        (sink_ref, x_ref, mod_ref, wb_ref, wc_ref, lb_ref, wg_ref, bg_ref,
         utm_ref, yb_ref, zs_ref, za_ref, ql_ref, qr_ref, kx_ref, vx_ref, wo_ref, fw_ref, h0re_ref, h0im_ref,
         kp_ref, kc_ref, kn_ref, vp_ref, vc_ref, vn_ref,
         y_ref, y_sm, y_tm, s_s, carry, mix_s) = refs
        fin_ref = first_s = None
    else:
        (sink_ref, x_ref, mod_ref, wb_ref, wc_ref, lb_ref, wg_ref, bg_ref,
         utm_ref, yb_ref, zs_ref, za_ref, ql_ref, qr_ref, kx_ref, vx_ref, wo_ref, fw_ref,
         y_ref, fin_ref, y_sm, y_tm, s_s, carry, mix_s, first_s) = refs
    t = pl.program_id(1)

    @pl.when(t == 0)
    def _():
        if windowed:
            carry[0] = h0re_ref[...]
            carry[1] = h0im_ref[...]
        else:
            carry[...] = jnp.zeros_like(carry)

    if windowed:
        kj = lax.broadcasted_iota(jnp.int32, (TL, 4 * TL), 0)
        qi = lax.broadcasted_iota(jnp.int32, (TL, 4 * TL), 1) & (TL - 1)
        masks = ((kj >= qi) & (t > 0), (kj <= qi) & (t < n_tiles - 1))
        key_refs = (kp_ref, kc_ref, kn_ref, kx_ref)
        val_refs = (vp_ref, vc_ref, vn_ref, vx_ref)
    else:
        masks = None
        key_refs = (kx_ref,)
        val_refs = (vx_ref,)

    seqs_per_unit = SEQS // N_S5_UNITS_B

    def between(i):
        for b in range(i * seqs_per_unit, (i + 1) * seqs_per_unit):
            _attention(b, (ql_ref, qr_ref), za_ref, mix_s, key_refs, val_refs, sink_ref, masks)

    def emit(r0, nrows, half, y):
        rows, cols = slice(r0, r0 + nrows), slice(half * HALF_CH, (half + 1) * HALF_CH)
        y_tm[rows, cols] = y + yb_ref[rows, cols]

    _s5_direction(utm_ref, s_s, carry, wb_ref, wc_ref, lb_ref, fin_ref, first_s, emit, reverse=False,
                  between=between)

    glu_rows = 256
    for rc in range(ROWS // glu_rows):
        rr = slice(rc * glu_rows, (rc + 1) * glu_rows)
        y = jax.nn.gelu(y_tm[rr, :])
        gl = jnp.dot(y.astype(BF16), wg_ref[...], preferred_element_type=F32) + bg_ref[...]
        y_tm[rr, :] = y * jax.nn.sigmoid(gl)
    _to_sequence_major(y_sm, y_tm)
    for b in range(SEQS):
        for cb in range(SSM_BLOCKS):
            cols = slice(cb * LANES, (cb + 1) * LANES)
            gated = y_sm[cb, b * PITCH:b * PITCH + TL, :] * zs_ref[_seq_rows(b), cols].astype(F32)
            mix_s[_seq_rows(b), cols] = gated.astype(BF16)

    half_seqs = SEQS // 2
    for hf in range(2):
        rows = slice(hf * half_seqs * TL, (hf + 1) * half_seqs * TL)
        proj = jnp.dot(mix_s[rows, :], wo_ref[...], preferred_element_type=F32)
        y_ref[hf * half_seqs:(hf + 1) * half_seqs] = proj.reshape(half_seqs, TL, D_MODEL)
    for b in range(SEQS):
        m = _mod_row(b, per_sequence=windowed)
        gate = mod_ref[m:m + 1, 2 * D_MODEL:3 * D_MODEL]
        r = x_ref[b] + gate * y_ref[b]
        ms = jnp.mean(r * r, axis=-1, keepdims=True)
        y_ref[b] = (r * lax.rsqrt(ms + EPS)) * fw_ref[...]


def _pass_b(x, mod, wb, wc, lb, w_glu, b_glu, from_a, kx, vx, w_out, fnorm_w, sink, h0, k_loc, v_loc):
    batch, length, _ = x.shape
    ng, nt = batch // SEQS, length // TL
    windowed = k_loc is not None
    tile_spec = pl.BlockSpec((None, None, ROWS, SSM_WIDTH), lambda g, t: (g, t, 0, 0))
    in_specs = [
        pl.BlockSpec(memory_space=pltpu.SMEM),
        pl.BlockSpec((SEQS, TL, D_MODEL), lambda g, t: (g, t, 0)),
        _const_spec(mod.shape),
    ] + _s5_param_specs(FWD) + [
        _const_spec((SSM_WIDTH, SSM_WIDTH)),
        _const_spec((1, SSM_WIDTH)),
    ] + [tile_spec] * len(from_a) + [
        pl.BlockSpec((SEQS, PAST_LEN, 2 * KV_WIDTH), lambda g, t: (g, 0, 0)),
        pl.BlockSpec((SEQS, KV_WIDTH, PAST_LEN), lambda g, t: (g, 0, 0)),
        _const_spec((D_MODEL, D_MODEL)),
        _const_spec((1, D_MODEL)),
    ]
    args = [sink, x, mod, wb, wc, lb, w_glu, b_glu, *from_a, kx, vx, w_out, fnorm_w]
    if windowed:
        in_specs += _h0_specs(FWD)
        args += list(h0)
        band_t = (lambda t: jnp.maximum(t - 1, 0), lambda t: t, lambda t: jnp.minimum(t + 1, nt - 1))
        in_specs += [pl.BlockSpec((SEQS, TL, 2 * KV_WIDTH), lambda g, t, f=f: (g, f(t), 0)) for f in band_t]
        in_specs += [pl.BlockSpec((SEQS, KV_WIDTH, TL), lambda g, t, f=f: (g, 0, f(t))) for f in band_t]
        args += [k_loc] * 3 + [v_loc] * 3
    out_specs = [pl.BlockSpec((SEQS, TL, D_MODEL), lambda g, t: (g, t, 0))]
    out_shape = [jax.ShapeDtypeStruct((batch, length, D_MODEL), F32)]
    scratch = [
        pltpu.VMEM((SSM_BLOCKS, SEQS * PITCH, LANES), F32),
        pltpu.VMEM((ROWS, SSM_WIDTH), F32),
        pltpu.VMEM((2, TC_B * SEQS, 2 * HALF_ST), F32),
        pltpu.VMEM((2, SEQS, ALL_ST), F32),
        pltpu.VMEM((ROWS, D_MODEL), BF16),
    ]
    if not windowed:
        out_specs += [pl.BlockSpec((None, 2, SEQS, ALL_ST), lambda g, t: (g, 0, 0, 0))]
        out_shape += [jax.ShapeDtypeStruct((ng, 2, SEQS, ALL_ST), F32)]
        scratch += [pltpu.VMEM((2, SEQS, ALL_ST), F32)]
    return pl.pallas_call(
        functools.partial(_pass_b_kernel, windowed=windowed, n_tiles=nt),
        grid=(ng, nt),
        in_specs=in_specs, out_specs=out_specs, out_shape=out_shape, scratch_shapes=scratch,
        compiler_params=pltpu.CompilerParams(
            dimension_semantics=("arbitrary", "arbitrary"), vmem_limit_bytes=VMEM_LIMIT),
        name="pass_b_latent" if windowed else "pass_b_context",
    )(*args)


def kernel(x_prompt, x_sample, c, cache_k, cache_v, state_ssm_re, state_ssm_im, c_ctx, norm_w, w_mod, b_mod, w_in, ssm_lambda_re, ssm_lambda_im, ssm_log_dt, ssm_b_re, ssm_b_im, ssm_c_re, ssm_c_im, ssm_d, w_glu, b_glu, attn_sink, w_out, final_norm_w):
    assert norm_w.shape[0] == 1, "single trunk layer"
    batch, seq, _ = x_prompt.shape
    dec_batch, dec_seq, _ = x_sample.shape
    assert dec_batch == SEQS and batch % SEQS == 0 and seq % TL == 0 and dec_seq % TL == 0
    assert seq == PAST_LEN and cache_k.shape[2] == PAST_LEN

    w_in0 = w_in[0].astype(BF16)
    nw = norm_w[0][None, :]
    fw = final_norm_w[None, :]
    d = ssm_d[0][None, :]
    wg = w_glu[0].astype(BF16)
    bg = b_glu[0][None, :]
    wo = w_out[0].astype(BF16)
    sink = attn_sink[0]

    cond = jnp.concatenate([c, c_ctx[None, :], jnp.zeros((16 - SEQS - 1, D_MODEL), F32)], axis=0)
    mod = _modulation(cond, w_mod[0], b_mod[0][None, :])

    wb, wc, lb = _s5_weights(ssm_lambda_re[0], ssm_lambda_im[0], ssm_log_dt[0], ssm_b_re[0], ssm_b_im[0],
                             ssm_c_re[0], ssm_c_im[0])
    h0 = (state_ssm_re.reshape(SEQS, 2 * ALL_ST), state_ssm_im.reshape(SEQS, 2 * ALL_ST))
    rope = _rope_tables(dec_seq)

    k_ctx, v_ctx, kcat_ctx, vt_ctx, *tiles_ctx, fin_b = _pass_a(
        x_prompt, mod, nw, w_in0, wb, wc, lb, d, None, None)
    y_prompt, fin_f = _pass_b(x_prompt, mod, wb, wc, lb, wg, bg, tiles_ctx, kcat_ctx, vt_ctx, wo, fw, sink,
                              None, None, None)

    kcat_lat, vt_lat, *tiles_lat = _pass_a(x_sample, mod, nw, w_in0, wb, wc, lb, d, h0, rope)
    kx = cache_k[:, 0].reshape(dec_batch, PAST_LEN, KV_WIDTH)
    kx = jnp.concatenate([kx, jnp.roll(kx, HEAD_DIM, axis=-1)], axis=-1).astype(BF16)
    vx = jnp.swapaxes(cache_v[:, 0].reshape(dec_batch, PAST_LEN, KV_WIDTH), 1, 2).astype(BF16)
    (y_sample,) = _pass_b(x_sample, mod, wb, wc, lb, wg, bg, tiles_lat, kx, vx, wo, fw, sink,
                          h0, kcat_lat, vt_lat)

    new_cache_k = k_ctx.reshape(batch, 1, seq, N_KV_HEADS, HEAD_DIM)
    new_cache_v = v_ctx.reshape(batch, 1, seq, N_KV_HEADS, HEAD_DIM)

    def states(fin, part):
        return fin[:, part].reshape(batch, SSM_GROUPS, SSM_STATE)

    new_re = jnp.stack([states(fin_f, 0), states(fin_b, 0)], axis=1)[:, None]
    new_im = jnp.stack([states(fin_f, 1), states(fin_b, 1)], axis=1)[:, None]
    return (y_prompt, y_sample, new_cache_k, new_cache_v, new_re, new_im)
```

```python
import functools
import math

import jax
import jax.numpy as jnp
from jax import lax
from jax.experimental import pallas as pl
from jax.experimental.pallas import tpu as pltpu

F32 = jnp.float32
BF16 = jnp.bfloat16

D_MODEL = 1024
SSM_WIDTH = 512
ATTN_WIDTH = 512
SSM_GROUP = 16
SSM_GROUPS = 32
SSM_STATE = 64
HEAD_DIM = 64
N_HEADS = 8
N_KV_HEADS = 2
KV_REP = 4
KV_WIDTH = 128
GRID_W = 64
ROPE_AXIS_DIM = 32
ROPE_BASE = 10000.0
EPS = 1e-6
LAMBDA_RE_MAX = -1e-4
NEG_INF = -1e30
LOG2E = math.log2(math.e)
PAST_LEN = 256

LANES = 128
BF16_SUBLANES = 16
SEQS = 8
TL = 128
ROWS = SEQS * TL
PITCH = TL + 8
SSM_BLOCKS = SSM_WIDTH // LANES
BLOCK_GROUPS = LANES // SSM_GROUP
BLOCK_ST = BLOCK_GROUPS * SSM_STATE
ALL_ST = SSM_GROUPS * SSM_STATE
PAIRS = TL // 2
PAIR_ROWS = PAIRS * SEQS
CHUNK_PAIRS = 32
CHUNK_ROWS = CHUNK_PAIRS * SEQS
N_CHUNKS = PAIRS // CHUNK_PAIRS
N_S5_UNITS = SSM_BLOCKS * N_CHUNKS
VMEM_LIMIT = 58 * 1024 * 1024

BWD, FWD = 1, 0
W_IN_U = 0
W_IN_ZS = SSM_WIDTH
W_IN_Q = 2 * SSM_WIDTH
W_IN_K = 2 * SSM_WIDTH + ATTN_WIDTH
W_IN_ZA = W_IN_K + 2 * KV_WIDTH
D_IN = W_IN_ZA + ATTN_WIDTH


def _const_spec(shape, index=None):
    index = (0,) * len(shape) if index is None else index
    return pl.BlockSpec(shape, lambda g, t: index, pipeline_mode=pl.Buffered(1))


def _modulation_kernel(cond_ref, w_ref, b_ref, out_ref):
    a = jax.nn.silu(cond_ref[...]).astype(BF16)
    out_ref[...] = jnp.dot(a, w_ref[...].astype(BF16), preferred_element_type=F32) + b_ref[...]


def _modulation(cond, w_mod, b_mod):
    rows = cond.shape[0]
    n = w_mod.shape[1]
    tn = 1024
    return pl.pallas_call(
        _modulation_kernel,
        grid=(n // tn,),
        in_specs=[pl.BlockSpec((rows, D_MODEL), lambda j: (0, 0)),
                  pl.BlockSpec((D_MODEL, tn), lambda j: (0, j)),
                  pl.BlockSpec((1, tn), lambda j: (0, j))],
        out_specs=pl.BlockSpec((rows, tn), lambda j: (0, j)),
        out_shape=jax.ShapeDtypeStruct((rows, n), F32),
        name="modulation",
    )(cond, w_mod, b_mod)


def _zoh(lam_re, lam_im, log_dt):
    lam_re = jnp.minimum(lam_re, LAMBDA_RE_MAX)
    dt = jnp.exp(log_dt)
    mag = jnp.exp(lam_re * dt)
    ang = lam_im * dt
    lbar_re = mag * jnp.cos(ang)
    lbar_im = mag * jnp.sin(ang)
    nr = lbar_re - 1.0
    ni = lbar_im
    den = lam_re * lam_re + lam_im * lam_im
    f_re = (nr * lam_re + ni * lam_im) / den
    f_im = (ni * lam_re - nr * lam_im) / den
    return lbar_re, lbar_im, f_re, f_im


def _cmul(a, b):
    return a[0] * b[0] - a[1] * b[1], a[0] * b[1] + a[1] * b[0]


def _s5_weights_kernel(lam_ref, bt_ref, cre_ref, cim_ref, lamflat_ref, win_ref, wout_ref, wk_ref, lsq_ref):
    p_row = lax.broadcasted_iota(jnp.int32, (SSM_STATE, BLOCK_ST), 0)
    p_col = lax.broadcasted_iota(jnp.int32, (SSM_STATE, BLOCK_ST), 1) & (SSM_STATE - 1)
    spread = jnp.where(p_row == p_col, 1.0, 0.0).astype(BF16)
    g_row = lax.broadcasted_iota(jnp.int32, (LANES, BLOCK_ST), 0) // SSM_GROUP
    g_col = lax.broadcasted_iota(jnp.int32, (LANES, BLOCK_ST), 1) // SSM_STATE
    own_block = g_row == g_col

    def blocks(x):
        return jnp.where(own_block, jnp.dot(x.astype(BF16), spread, preferred_element_type=F32), 0.0)

    def blocks2(z):
        return blocks(z[0]).astype(BF16), blocks(z[1]).astype(BF16)

    def in_pair(src, c):
        dims = (((1,), (1,)), ((), ()))
        return (lax.dot_general(src[0], c[0], dims, preferred_element_type=F32)
                - lax.dot_general(src[1], c[1], dims, preferred_element_type=F32))

    zeros = jnp.zeros((LANES, LANES), F32)
    k_diag = zeros
    k_cross = {}
    for d in (FWD, BWD):
        lbar_re, lbar_im, f_re, f_im = _zoh(lam_ref[0, d], lam_ref[1, d], lam_ref[2, d])
        lbar = (lbar_re, lbar_im)
        bb = _cmul((f_re, f_im), (bt_ref[0, d], bt_ref[1, d]))
        lbb = _cmul(lbar, bb)
        c = (cre_ref[d], cim_ref[d])
        cl = _cmul(c, lbar)
        cl2 = _cmul(cl, lbar)
        bb_b, lbb_b, c_b = blocks2(bb), blocks2(lbb), blocks2(c)
        near, far = (0, 1) if d == FWD else (1, 0)
        for slot, w in ((near, lbb_b), (far, bb_b)):
            win_ref[d, slot * LANES:(slot + 1) * LANES, 0:BLOCK_ST] = w[0]
            win_ref[d, slot * LANES:(slot + 1) * LANES, BLOCK_ST:2 * BLOCK_ST] = w[1]
        for slot, w in ((near, cl), (far, cl2)):
            wout_ref[d, 0:BLOCK_ST, slot * LANES:(slot + 1) * LANES] = blocks(w[0]).T.astype(BF16)
            wout_ref[d, BLOCK_ST:2 * BLOCK_ST, slot * LANES:(slot + 1) * LANES] = blocks(-w[1]).T.astype(BF16)
        k_diag = k_diag + in_pair(bb_b, c_b)
        k_cross[d] = in_pair(lbb_b, c_b)
    wk_ref[0:LANES, 0:LANES] = k_diag.astype(BF16)
    wk_ref[LANES:2 * LANES, LANES:2 * LANES] = k_diag.astype(BF16)
    wk_ref[0:LANES, LANES:2 * LANES] = k_cross[FWD].astype(BF16)
    wk_ref[LANES:2 * LANES, 0:LANES] = k_cross[BWD].astype(BF16)

    @pl.when(pl.program_id(0) == 0)
    def _():
        lbar_re, lbar_im, _, _ = _zoh(lamflat_ref[0], lamflat_ref[1], lamflat_ref[2])
        sq = _cmul((lbar_re, lbar_im), (lbar_re, lbar_im))
        lsq_ref[0] = sq[0]
        lsq_ref[1] = sq[1]


def _s5_weights(lam_re, lam_im, log_dt, b_re, b_im, c_re, c_im):
    n = SSM_GROUPS * SSM_GROUP
    lam3 = jnp.stack([lam_re, lam_im, jnp.broadcast_to(log_dt[:, :, None], lam_re.shape)])
    lam_rows = jnp.broadcast_to(lam3[:, :, :, None, :], (3, 2, SSM_GROUPS, SSM_GROUP, SSM_STATE)).reshape(3, 2, n, SSM_STATE)
    lam_flat = jnp.broadcast_to(lam3.reshape(3, 1, 2 * ALL_ST), (3, SEQS, 2 * ALL_ST))
    bt = jnp.transpose(jnp.stack([b_re, b_im]), (0, 1, 2, 4, 3)).reshape(2, 2, n, SSM_STATE)
    return pl.pallas_call(
        _s5_weights_kernel,
        grid=(SSM_BLOCKS,),
        in_specs=[pl.BlockSpec((3, 2, LANES, SSM_STATE), lambda m: (0, 0, m, 0)),
                  pl.BlockSpec((2, 2, LANES, SSM_STATE), lambda m: (0, 0, m, 0)),
                  pl.BlockSpec((2, LANES, SSM_STATE), lambda m: (0, m, 0)),
                  pl.BlockSpec((2, LANES, SSM_STATE), lambda m: (0, m, 0)),
                  pl.BlockSpec((3, SEQS, 2 * ALL_ST), lambda m: (0, 0, 0))],
        out_specs=[pl.BlockSpec((2, None, 2 * LANES, 2 * BLOCK_ST), lambda m: (0, m, 0, 0)),
                   pl.BlockSpec((2, None, 2 * BLOCK_ST, 2 * LANES), lambda m: (0, m, 0, 0)),
                   pl.BlockSpec((None, 2 * LANES, 2 * LANES), lambda m: (m, 0, 0)),
                   pl.BlockSpec((2, SEQS, 2 * ALL_ST), lambda m: (0, 0, 0))],
        out_shape=[jax.ShapeDtypeStruct((2, SSM_BLOCKS, 2 * LANES, 2 * BLOCK_ST), BF16),
                   jax.ShapeDtypeStruct((2, SSM_BLOCKS, 2 * BLOCK_ST, 2 * LANES), BF16),
                   jax.ShapeDtypeStruct((SSM_BLOCKS, 2 * LANES, 2 * LANES), BF16),
                   jax.ShapeDtypeStruct((2, SEQS, 2 * ALL_ST), F32)],
        name="s5_weights",
    )(lam_rows, bt, c_re.reshape(2, n, SSM_STATE), c_im.reshape(2, n, SSM_STATE), lam_flat)


def _rope_tables(length):
    pos = jnp.arange(length)
    row = (pos // GRID_W).astype(F32)
    col = (pos % GRID_W).astype(F32)
    freqs = ROPE_BASE ** (-jnp.arange(0, ROPE_AXIS_DIM, 2, dtype=F32) / ROPE_AXIS_DIM)
    ang_r = row[:, None] * freqs[None, :]
    ang_c = col[:, None] * freqs[None, :]
    cos = jnp.concatenate([jnp.cos(ang_r), jnp.cos(ang_r), jnp.cos(ang_c), jnp.cos(ang_c)], axis=-1)
    sin = jnp.concatenate([-jnp.sin(ang_r), jnp.sin(ang_r), -jnp.sin(ang_c), jnp.sin(ang_c)], axis=-1)
    return jnp.tile(cos, (1, 2)), jnp.tile(sin, (1, 2))


def _seq_rows(b):
    return slice(b * TL, (b + 1) * TL)


def _mod_row(b, per_sequence):
    return b if per_sequence else SEQS


def _norm_mod(h_ref, x_ref, mod_ref, nw_ref, b, per_sequence):
    xb = x_ref[b]
    ms = jnp.mean(xb * xb, axis=-1, keepdims=True)
    m = _mod_row(b, per_sequence)
    shift = mod_ref[m:m + 1, 0:D_MODEL]
    gain = nw_ref[...] * (1.0 + mod_ref[m:m + 1, D_MODEL:2 * D_MODEL])
    h_ref[_seq_rows(b), :] = ((xb * lax.rsqrt(ms + EPS)) * gain + shift).astype(BF16)


def _rope(x, cos, sin):
    lane = lax.broadcasted_iota(jnp.int32, x.shape, 1)
    first = (lane & 31) < 16
    partner = jnp.where(first, pltpu.roll(x, LANES - 16, 1), pltpu.roll(x, 16, 1))
    x3 = x.reshape(SEQS, TL, LANES)
    p3 = partner.reshape(SEQS, TL, LANES)
    return (x3 * cos[None] + p3 * sin[None]).reshape(ROWS, LANES)


def _store_column_blocks(dst_ref, val, first_seq):
    for j in range(val.shape[0] // TL):
        b = first_seq + j
        for cb in range(SSM_BLOCKS):
            dst_ref[cb, b * PITCH:b * PITCH + TL, :] = val[_seq_rows(j), cb * LANES:(cb + 1) * LANES]


def _token_rows(l):
    return l // 2, slice((l % 2) * SEQS, (l % 2 + 1) * SEQS)


def _to_token_major(dst_ref, src_ref):
    for l in range(TL):
        j, rows = _token_rows(l)
        for cb in range(SSM_BLOCKS):
            dst_ref[j, rows, cb * LANES:(cb + 1) * LANES] = src_ref[cb, pl.ds(l, SEQS, stride=PITCH), :]


def _to_sequence_major(dst_ref, src_ref):
    for l in range(TL):
        j, rows = _token_rows(l)
        for cb in range(SSM_BLOCKS):
            dst_ref[cb, pl.ds(l, SEQS, stride=PITCH), :] = src_ref[j, rows, cb * LANES:(cb + 1) * LANES]


def _scan_chunk(s_ref, lsq_ref, carry_ref, direction, blk, reverse):
    cols = slice(blk * BLOCK_ST, (blk + 1) * BLOCK_ST)
    lanes = slice(direction * ALL_ST + blk * BLOCK_ST, direction * ALL_ST + (blk + 1) * BLOCK_ST)
    lr = lsq_ref[0, :, lanes]
    li = lsq_ref[1, :, lanes]
    sr = carry_ref[0, :, cols]
    si = carry_ref[1, :, cols]
    for i in range(CHUNK_PAIRS):
        j = (CHUNK_PAIRS - 1 - i) if reverse else i
        rows = slice(j * SEQS, (j + 1) * SEQS)
        inc_r = s_ref[rows, 0:BLOCK_ST]
        inc_i = s_ref[rows, BLOCK_ST:2 * BLOCK_ST]
        s_ref[rows, 0:BLOCK_ST] = sr
        s_ref[rows, BLOCK_ST:2 * BLOCK_ST] = si
        sr, si = lr * sr - li * si + inc_r, lr * si + li * sr + inc_i
    carry_ref[0, :, cols] = sr
    carry_ref[1, :, cols] = si


def _s5_direction(lhs_ref, s_ref, carry_ref, win_ref, wout_ref, lsq_ref, first_ref, first_s, emit, direction,
                  wk_ref=None, between=None):
    reverse = direction == BWD
    order = [(N_CHUNKS - 1 - ci) if reverse else ci for ci in range(N_CHUNKS)]
    units = [(blk, c) for c in order for blk in range(SSM_BLOCKS)]

    def lhs(blk, c):
        return lhs_ref[c * CHUNK_ROWS:(c + 1) * CHUNK_ROWS, blk * 2 * LANES:(blk + 1) * 2 * LANES]

    def project_in(i):
        blk, c = units[i]
        s_ref[i % 2] = jnp.dot(lhs(blk, c), win_ref[blk], preferred_element_type=F32)

    project_in(0)
    for i, (blk, c) in enumerate(units):
        if i + 1 < len(units):
            project_in(i + 1)
        buf = s_ref.at[i % 2]
        _scan_chunk(buf, lsq_ref, carry_ref, direction, blk, reverse)
        y = jnp.dot(buf[...].astype(BF16), wout_ref[blk], preferred_element_type=F32)
        if wk_ref is not None:
            y = y + jnp.dot(lhs(blk, c), wk_ref[blk], preferred_element_type=F32)
        emit(c * CHUNK_PAIRS, CHUNK_PAIRS, blk, y)
        if between is not None:
            between(i)

    if first_ref is not None:
        first_rows = slice(PAIR_ROWS - 2 * SEQS, PAIR_ROWS) if reverse else slice(0, 2 * SEQS)
        slot = 1 if reverse else 0
        zeros = jnp.zeros((2 * SEQS, LANES), BF16)
        for blk in range(SSM_BLOCKS):
            c0 = blk * 2 * LANES + slot * LANES
            u_first = lhs_ref[first_rows, c0:c0 + LANES]
            pair = jnp.concatenate([u_first, zeros] if reverse else [zeros, u_first], axis=1)
            inc = jnp.dot(pair, win_ref[blk], preferred_element_type=F32)
            inc = inc[SEQS:2 * SEQS] if reverse else inc[0:SEQS]
            first_s[0, :, blk * BLOCK_ST:(blk + 1) * BLOCK_ST] = inc[:, 0:BLOCK_ST]
            first_s[1, :, blk * BLOCK_ST:(blk + 1) * BLOCK_ST] = inc[:, BLOCK_ST:2 * BLOCK_ST]

        @pl.when(pl.program_id(1) == 0)
        def _():
            first_ref[...] = first_s[...]


def _s5_param_specs(direction):
    return [_const_spec((None, SSM_BLOCKS, 2 * LANES, 2 * BLOCK_ST), (direction, 0, 0, 0)),
            _const_spec((None, SSM_BLOCKS, 2 * BLOCK_ST, 2 * LANES), (direction, 0, 0, 0)),
            _const_spec((2, SEQS, 2 * ALL_ST))]


def _h0_specs(direction):
    return [_const_spec((SEQS, ALL_ST), (0, direction))] * 2


def _pass_a_kernel(*refs, rotary):
    if rotary:
        (x_ref, mod_ref, nw_ref, w_ref, win_ref, wout_ref, lsq_ref, wk_ref, d_ref, h0re_ref, h0im_ref, cos_ref, sin_ref,
         kcat_ref, vt_ref, lhs_ref, yb_ref, zs_ref, za_ref, ql_ref, qr_ref, h_s, u_sm, u_tm, s_s, carry) = refs
        fin_ref = first_s = None
    else:
        (x_ref, mod_ref, nw_ref, w_ref, win_ref, wout_ref, lsq_ref, wk_ref, d_ref,
         k_ref, v_ref, kcat_ref, vt_ref, lhs_ref, yb_ref, zs_ref, za_ref, ql_ref, qr_ref, fin_ref,
         h_s, u_sm, u_tm, s_s, carry, first_s) = refs

    @pl.when(pl.program_id(1) == 0)
    def _():
        if rotary:
            carry[0] = h0re_ref[...]
            carry[1] = h0im_ref[...]
        else:
            carry[...] = jnp.zeros_like(carry)

    half_seqs = SEQS // 2
    for hf in range(2):
        for b in range(hf * half_seqs, (hf + 1) * half_seqs):
            _norm_mod(h_s, x_ref, mod_ref, nw_ref, b, per_sequence=rotary)
        rows = slice(hf * half_seqs * TL, (hf + 1) * half_seqs * TL)
        u_half = jnp.dot(h_s[rows, :], w_ref[:, W_IN_U:W_IN_U + SSM_WIDTH], preferred_element_type=F32)
        _store_column_blocks(u_sm, u_half, first_seq=hf * half_seqs)
    _to_token_major(u_tm, u_sm)
    for blk in range(SSM_BLOCKS):
        for slot in range(2):
            piece = u_tm[:, slot * SEQS:(slot + 1) * SEQS, blk * LANES:(blk + 1) * LANES]
            c0 = blk * 2 * LANES + slot * LANES
            lhs_ref[:, c0:c0 + LANES] = piece.reshape(PAIR_ROWS, LANES).astype(BF16)

    def project(c0, width):
        return jnp.dot(h_s[...], w_ref[:, c0:c0 + width], preferred_element_type=F32)

    def project_kv():
        kv = project(W_IN_K, 2 * KV_WIDTH)
        k = kv[:, 0:KV_WIDTH]
        v = kv[:, KV_WIDTH:2 * KV_WIDTH]
        if rotary:
            k = _rope(k, cos_ref[...], sin_ref[...])
        else:
            k_ref[...] = k.reshape(SEQS, TL, KV_WIDTH)
            v_ref[...] = v.reshape(SEQS, TL, KV_WIDTH)
        kcat = jnp.concatenate([k, pltpu.roll(k, HEAD_DIM, 1)], axis=1).astype(BF16)
        kcat_ref[...] = kcat.reshape(SEQS, TL, 2 * KV_WIDTH)
        for b in range(SEQS):
            vt_ref[b] = v[_seq_rows(b), :].T.astype(BF16)

    def project_q():
        qall = project(W_IN_Q, ATTN_WIDTH)
        even_head = lax.broadcasted_iota(jnp.int32, (ROWS, LANES), 1) < HEAD_DIM
        for cg in range(ATTN_WIDTH // LANES):
            cols = slice(cg * LANES, (cg + 1) * LANES)
            q = qall[:, cols]
            if rotary:
                q = _rope(q, cos_ref[...], sin_ref[...])
            q = q * (LOG2E * HEAD_DIM ** -0.5)
            ql_ref[:, cols] = jnp.where(even_head, q, 0.0).astype(BF16)
            qr_ref[:, cols] = jnp.where(even_head, 0.0, q).astype(BF16)

    def project_gate(c0, out_ref):
        out_ref[...] = jax.nn.silu(project(c0, out_ref.shape[1])).astype(BF16)

    side_work = [project_kv, project_q, lambda: project_gate(W_IN_ZS, zs_ref), lambda: project_gate(W_IN_ZA, za_ref)]

    def between(i):
        if i < len(side_work):
            side_work[i]()

    def emit(pair0, npairs, blk, y):
        lanes = slice(blk * LANES, (blk + 1) * LANES)
        for slot in range(2):
            rows = slice(slot * SEQS, (slot + 1) * SEQS)
            u = u_tm[pair0:pair0 + npairs, rows, lanes]
            ys = y[:, slot * LANES:(slot + 1) * LANES].reshape(npairs, SEQS, LANES)
            yb_ref[pair0:pair0 + npairs, rows, lanes] = ys + u * d_ref[:, lanes]

    _s5_direction(lhs_ref, s_s, carry, win_ref, wout_ref, lsq_ref, fin_ref, first_s, emit, BWD,
                  wk_ref=wk_ref, between=between)


def _pass_a(x, mod, norm_w, w_in, win, wout, lsq, wk, d, h0, rope):
    batch, length, _ = x.shape
    ng, nt = batch // SEQS, length // TL
    rotary = rope is not None
    rev = lambda t: nt - 1 - t
    in_specs = [
        pl.BlockSpec((SEQS, TL, D_MODEL), lambda g, t: (g, rev(t), 0)),
        _const_spec(mod.shape),
        _const_spec((1, D_MODEL)),
        _const_spec((D_MODEL, D_IN)),
    ] + _s5_param_specs(BWD) + [
        _const_spec((SSM_BLOCKS, 2 * LANES, 2 * LANES)),
        _const_spec((1, SSM_WIDTH)),
    ]
    args = [x, mod, norm_w, w_in, win, wout, lsq, wk, d]
    if rotary:
        in_specs += _h0_specs(BWD) + [pl.BlockSpec((TL, LANES), lambda g, t: (rev(t), 0))] * 2
        args += list(h0) + list(rope)
    assert ATTN_WIDTH == SSM_WIDTH
    tile_spec = pl.BlockSpec((None, None, ROWS, SSM_WIDTH), lambda g, t: (g, rev(t), 0, 0))
    tile_bf16 = jax.ShapeDtypeStruct((ng, nt, ROWS, SSM_WIDTH), BF16)
    out_specs = [
        pl.BlockSpec((SEQS, TL, 2 * KV_WIDTH), lambda g, t: (g, rev(t), 0)),
        pl.BlockSpec((SEQS, KV_WIDTH, TL), lambda g, t: (g, 0, rev(t))),
        pl.BlockSpec((None, None, PAIR_ROWS, 2 * SSM_WIDTH), lambda g, t: (g, rev(t), 0, 0)),
        pl.BlockSpec((None, None, PAIRS, 2 * SEQS, SSM_WIDTH), lambda g, t: (g, rev(t), 0, 0, 0)),
    ] + [tile_spec] * 4
    out_shape = [
        jax.ShapeDtypeStruct((batch, length, 2 * KV_WIDTH), BF16),
        jax.ShapeDtypeStruct((batch, KV_WIDTH, length), BF16),
        jax.ShapeDtypeStruct((ng, nt, PAIR_ROWS, 2 * SSM_WIDTH), BF16),
        jax.ShapeDtypeStruct((ng, nt, PAIRS, 2 * SEQS, SSM_WIDTH), F32),
        tile_bf16, tile_bf16,
        tile_bf16, tile_bf16,
    ]
    scratch = [
        pltpu.VMEM((ROWS, D_MODEL), BF16),
        pltpu.VMEM((SSM_BLOCKS, SEQS * PITCH, LANES), F32),
        pltpu.VMEM((PAIRS, 2 * SEQS, SSM_WIDTH), F32),
        pltpu.VMEM((2, CHUNK_ROWS, 2 * BLOCK_ST), F32),
        pltpu.VMEM((2, SEQS, ALL_ST), F32),
    ]
    if not rotary:
        cache_spec = pl.BlockSpec((SEQS, TL, KV_WIDTH), lambda g, t: (g, rev(t), 0))
        cache_shape = jax.ShapeDtypeStruct((batch, length, KV_WIDTH), F32)
        out_specs = [cache_spec, cache_spec] + out_specs + [pl.BlockSpec((None, 2, SEQS, ALL_ST), lambda g, t: (g, 0, 0, 0))]
        out_shape = [cache_shape, cache_shape] + out_shape + [jax.ShapeDtypeStruct((ng, 2, SEQS, ALL_ST), F32)]
        scratch += [pltpu.VMEM((2, SEQS, ALL_ST), F32)]
    return pl.pallas_call(
        functools.partial(_pass_a_kernel, rotary=rotary),
        grid=(ng, nt),
        in_specs=in_specs, out_specs=out_specs, out_shape=out_shape, scratch_shapes=scratch,
        compiler_params=pltpu.CompilerParams(
            dimension_semantics=("arbitrary", "arbitrary"), vmem_limit_bytes=VMEM_LIMIT),
        name="pass_a_latent" if rotary else "pass_a_context",
    )(*args)


def _attention(b, q_refs, za_ref, mix_s, kcat_refs, vt_refs, sink_ref, masks):
    rows = _seq_rows(b)
    kcat = jnp.concatenate([r[b] for r in kcat_refs], axis=0)
    vt = jnp.concatenate([r[b] for r in vt_refs], axis=1)
    segment = lax.broadcasted_iota(jnp.int32, (1, 4 * TL), 1) // TL
    kv_heads = range(N_KV_HEADS)

    def side(g, grp):
        return g ^ grp

    def head(g, i, grp):
        return 4 * g + 2 * i + side(g, grp)

    def scores(grp):
        q = jnp.concatenate([q_refs[side(g, grp)][rows, blk * LANES:(blk + 1) * LANES]
                             for g in kv_heads for blk in (2 * g, 2 * g + 1)], axis=0)
        return lax.dot_general(kcat[:, grp * KV_WIDTH:(grp + 1) * KV_WIDTH], q, (((1,), (1,)), ((), ())),
                               preferred_element_type=F32)

    def softmax(grp, s):
        if masks is None:
            pieces = [s]
        else:
            pieces = [jnp.where(masks[0], s[0:TL], NEG_INF), s[TL:2 * TL],
                      jnp.where(masks[1], s[2 * TL:3 * TL], NEG_INF), s[3 * TL:]]
        sinks = [sink_ref[head(g, i, grp)] * LOG2E for g in kv_heads for i in range(2)]
        sink = jnp.where(segment == 0, sinks[0], jnp.where(segment == 1, sinks[1],
                                                           jnp.where(segment == 2, sinks[2], sinks[3])))
        m = sink
        for piece in pieces:
            m = jnp.maximum(m, jnp.max(piece, axis=0, keepdims=True))
        ps = [jnp.exp2(piece - m).astype(BF16) for piece in pieces]
        return (ps[0] if len(ps) == 1 else jnp.concatenate(ps, axis=0)), jnp.exp2(sink - m)

    ones_rows = jnp.ones((BF16_SUBLANES, vt.shape[1]), BF16)

    all_scores = [scores(grp) for grp in range(2)]
    probs, sink_terms = zip(*[softmax(grp, s) for grp, s in enumerate(all_scores)])
    for g in kv_heads:
        cols = slice(g * 2 * TL, (g + 1) * 2 * TL)
        values = jnp.concatenate([vt[g * HEAD_DIM:(g + 1) * HEAD_DIM, :], ones_rows], axis=0)
        p = jnp.concatenate([probs[0][:, cols], probs[1][:, cols]], axis=1)
        sink_term = jnp.concatenate([sink_terms[0][:, cols], sink_terms[1][:, cols]], axis=1)
        ot = jnp.dot(values, p, preferred_element_type=F32)
        out_t = ot[0:HEAD_DIM] / (ot[HEAD_DIM:HEAD_DIM + 1] + sink_term)
        for i in range(2):
            left = out_t[:, (2 * g + i) * TL:(2 * g + i + 1) * TL]
            right = out_t[:, (2 * (g ^ 1) + i) * TL:(2 * (g ^ 1) + i + 1) * TL]
            blk_cols = slice((2 * g + i) * LANES, (2 * g + i + 1) * LANES)
            pair_t = jnp.concatenate([left, right], axis=0)
            gated = pair_t.T * za_ref[rows, blk_cols].astype(F32)
            mix_s[rows, ATTN_WIDTH + blk_cols.start:ATTN_WIDTH + blk_cols.stop] = gated.astype(BF16)


def _pass_b_kernel(*refs, windowed, n_tiles):
    if windowed:
        (sink_ref, x_ref, mod_ref, win_ref, wout_ref, lsq_ref, wg_ref, bg_ref,
         lhs_ref, yb_ref, zs_ref, za_ref, ql_ref, qr_ref, kx_ref, vx_ref, wo_ref, fw_ref, h0re_ref, h0im_ref,
         kp_ref, kc_ref, kn_ref, vp_ref, vc_ref, vn_ref,
         y_ref, y_sm, y_tm, s_s, carry, mix_s) = refs
        fin_ref = first_s = None
    else:
        (sink_ref, x_ref, mod_ref, win_ref, wout_ref, lsq_ref, wg_ref, bg_ref,
         lhs_ref, yb_ref, zs_ref, za_ref, ql_ref, qr_ref, kx_ref, vx_ref, wo_ref, fw_ref,
         y_ref, fin_ref, y_sm, y_tm, s_s, carry, mix_s, first_s) = refs
    t = pl.program_id(1)

    @pl.when(t == 0)
    def _():
        if windowed:
            carry[0] = h0re_ref[...]
            carry[1] = h0im_ref[...]
        else:
            carry[...] = jnp.zeros_like(carry)

    if windowed:
        kj = lax.broadcasted_iota(jnp.int32, (TL, 4 * TL), 0)
        qi = lax.broadcasted_iota(jnp.int32, (TL, 4 * TL), 1) & (TL - 1)
        masks = ((kj >= qi) & (t > 0), (kj <= qi) & (t < n_tiles - 1))
        key_refs = (kp_ref, kc_ref, kn_ref, kx_ref)
        val_refs = (vp_ref, vc_ref, vn_ref, vx_ref)
    else:
        masks = None
        key_refs = (kx_ref,)
        val_refs = (vx_ref,)

    seqs_per_unit = SEQS // N_S5_UNITS

    def between(i):
        for b in range(i * seqs_per_unit, (i + 1) * seqs_per_unit):
            _attention(b, (ql_ref, qr_ref), za_ref, mix_s, key_refs, val_refs, sink_ref, masks)

    def emit(pair0, npairs, blk, y):
        lanes = slice(blk * LANES, (blk + 1) * LANES)
        for slot in range(2):
            rows = slice(slot * SEQS, (slot + 1) * SEQS)
            ys = y[:, slot * LANES:(slot + 1) * LANES].reshape(npairs, SEQS, LANES)
            y_tm[pair0:pair0 + npairs, rows, lanes] = ys + yb_ref[pair0:pair0 + npairs, rows, lanes]

    _s5_direction(lhs_ref, s_s, carry, win_ref, wout_ref, lsq_ref, fin_ref, first_s, emit, FWD, between=between)

    glu_pairs = 16
    for rc in range(PAIRS // glu_pairs):
        pr = slice(rc * glu_pairs, (rc + 1) * glu_pairs)
        y = jax.nn.gelu(y_tm[pr].reshape(glu_pairs * 2 * SEQS, SSM_WIDTH))
        gl = jnp.dot(y.astype(BF16), wg_ref[...], preferred_element_type=F32) + bg_ref[...]
        y_tm[pr] = (y * jax.nn.sigmoid(gl)).reshape(glu_pairs, 2 * SEQS, SSM_WIDTH)
    _to_sequence_major(y_sm, y_tm)
    for b in range(SEQS):
        for cb in range(SSM_BLOCKS):
            cols = slice(cb * LANES, (cb + 1) * LANES)
            gated = y_sm[cb, b * PITCH:b * PITCH + TL, :] * zs_ref[_seq_rows(b), cols].astype(F32)
            mix_s[_seq_rows(b), cols] = gated.astype(BF16)

    half_seqs = SEQS // 2
    for hf in range(2):
        rows = slice(hf * half_seqs * TL, (hf + 1) * half_seqs * TL)
        proj = jnp.dot(mix_s[rows, :], wo_ref[...], preferred_element_type=F32)
        y_ref[hf * half_seqs:(hf + 1) * half_seqs] = proj.reshape(half_seqs, TL, D_MODEL)
    for b in range(SEQS):
        m = _mod_row(b, per_sequence=windowed)
        gate = mod_ref[m:m + 1, 2 * D_MODEL:3 * D_MODEL]
        r = x_ref[b] + gate * y_ref[b]
        ms = jnp.mean(r * r, axis=-1, keepdims=True)
        y_ref[b] = (r * lax.rsqrt(ms + EPS)) * fw_ref[...]


def _pass_b(x, mod, win, wout, lsq, w_glu, b_glu, from_a, kx, vx, w_out, fnorm_w, sink, h0, k_loc, v_loc):
    batch, length, _ = x.shape
    ng, nt = batch // SEQS, length // TL
    windowed = k_loc is not None
    tile_spec = pl.BlockSpec((None, None, ROWS, SSM_WIDTH), lambda g, t: (g, t, 0, 0))
    in_specs = [
        pl.BlockSpec(memory_space=pltpu.SMEM),
        pl.BlockSpec((SEQS, TL, D_MODEL), lambda g, t: (g, t, 0)),
        _const_spec(mod.shape),
    ] + _s5_param_specs(FWD) + [
        _const_spec((SSM_WIDTH, SSM_WIDTH)),
        _const_spec((1, SSM_WIDTH)),
        pl.BlockSpec((None, None, PAIR_ROWS, 2 * SSM_WIDTH), lambda g, t: (g, t, 0, 0)),
        pl.BlockSpec((None, None, PAIRS, 2 * SEQS, SSM_WIDTH), lambda g, t: (g, t, 0, 0, 0)),
    ] + [tile_spec] * 4 + [
        pl.BlockSpec((SEQS, PAST_LEN, 2 * KV_WIDTH), lambda g, t: (g, 0, 0)),
        pl.BlockSpec((SEQS, KV_WIDTH, PAST_LEN), lambda g, t: (g, 0, 0)),
        _const_spec((D_MODEL, D_MODEL)),
        _const_spec((1, D_MODEL)),
    ]
    args = [sink, x, mod, win, wout, lsq, w_glu, b_glu, *from_a, kx, vx, w_out, fnorm_w]
    if windowed:
        in_specs += _h0_specs(FWD)
        args += list(h0)
        band_t = (lambda t: jnp.maximum(t - 1, 0), lambda t: t, lambda t: jnp.minimum(t + 1, nt - 1))
        in_specs += [pl.BlockSpec((SEQS, TL, 2 * KV_WIDTH), lambda g, t, f=f: (g, f(t), 0)) for f in band_t]
        in_specs += [pl.BlockSpec((SEQS, KV_WIDTH, TL), lambda g, t, f=f: (g, 0, f(t))) for f in band_t]
        args += [k_loc] * 3 + [v_loc] * 3
    out_specs = [pl.BlockSpec((SEQS, TL, D_MODEL), lambda g, t: (g, t, 0))]
    out_shape = [jax.ShapeDtypeStruct((batch, length, D_MODEL), F32)]
    scratch = [
        pltpu.VMEM((SSM_BLOCKS, SEQS * PITCH, LANES), F32),
        pltpu.VMEM((PAIRS, 2 * SEQS, SSM_WIDTH), F32),
        pltpu.VMEM((2, CHUNK_ROWS, 2 * BLOCK_ST), F32),
        pltpu.VMEM((2, SEQS, ALL_ST), F32),
        pltpu.VMEM((ROWS, D_MODEL), BF16),
    ]
    if not windowed:
        out_specs += [pl.BlockSpec((None, 2, SEQS, ALL_ST), lambda g, t: (g, 0, 0, 0))]
        out_shape += [jax.ShapeDtypeStruct((ng, 2, SEQS, ALL_ST), F32)]
        scratch += [pltpu.VMEM((2, SEQS, ALL_ST), F32)]
    return pl.pallas_call(
        functools.partial(_pass_b_kernel, windowed=windowed, n_tiles=nt),
        grid=(ng, nt),
        in_specs=in_specs, out_specs=out_specs, out_shape=out_shape, scratch_shapes=scratch,
        compiler_params=pltpu.CompilerParams(
            dimension_semantics=("arbitrary", "arbitrary"), vmem_limit_bytes=VMEM_LIMIT),
        name="pass_b_latent" if windowed else "pass_b_context",
    )(*args)


def kernel(x_prompt, x_sample, c, cache_k, cache_v, state_ssm_re, state_ssm_im, c_ctx, norm_w, w_mod, b_mod, w_in, ssm_lambda_re, ssm_lambda_im, ssm_log_dt, ssm_b_re, ssm_b_im, ssm_c_re, ssm_c_im, ssm_d, w_glu, b_glu, attn_sink, w_out, final_norm_w):
    assert norm_w.shape[0] == 1, "single trunk layer"
    batch, seq, _ = x_prompt.shape
    dec_batch, dec_seq, _ = x_sample.shape
    assert dec_batch == SEQS and batch % SEQS == 0 and seq % TL == 0 and dec_seq % TL == 0
    assert seq == PAST_LEN and cache_k.shape[2] == PAST_LEN

    w_in0 = w_in[0].astype(BF16)
    nw = norm_w[0][None, :]
    fw = final_norm_w[None, :]
    d = ssm_d[0][None, :]
    wg = w_glu[0].astype(BF16)
    bg = b_glu[0][None, :]
    wo = w_out[0].astype(BF16)
    sink = attn_sink[0]

    cond = jnp.concatenate([c, c_ctx[None, :], jnp.zeros((16 - SEQS - 1, D_MODEL), F32)], axis=0)
    mod = _modulation(cond, w_mod[0], b_mod[0][None, :])

    win, wout, wk, lsq = _s5_weights(ssm_lambda_re[0], ssm_lambda_im[0], ssm_log_dt[0], ssm_b_re[0], ssm_b_im[0],
                                     ssm_c_re[0], ssm_c_im[0])
    h0 = (state_ssm_re.reshape(SEQS, 2 * ALL_ST), state_ssm_im.reshape(SEQS, 2 * ALL_ST))
    rope = _rope_tables(dec_seq)

    k_ctx, v_ctx, kcat_ctx, vt_ctx, *tiles_ctx, fin_b = _pass_a(
        x_prompt, mod, nw, w_in0, win, wout, lsq, wk, d, None, None)
    y_prompt, fin_f = _pass_b(x_prompt, mod, win, wout, lsq, wg, bg, tiles_ctx, kcat_ctx, vt_ctx, wo, fw, sink,
                              None, None, None)

    kcat_lat, vt_lat, *tiles_lat = _pass_a(x_sample, mod, nw, w_in0, win, wout, lsq, wk, d, h0, rope)
    kx = cache_k[:, 0].reshape(dec_batch, PAST_LEN, KV_WIDTH)
    kx = jnp.concatenate([kx, jnp.roll(kx, HEAD_DIM, axis=-1)], axis=-1).astype(BF16)
    vx = jnp.swapaxes(cache_v[:, 0].reshape(dec_batch, PAST_LEN, KV_WIDTH), 1, 2).astype(BF16)
    (y_sample,) = _pass_b(x_sample, mod, win, wout, lsq, wg, bg, tiles_lat, kx, vx, wo, fw, sink,
                          h0, kcat_lat, vt_lat)

    new_cache_k = k_ctx.reshape(batch, 1, seq, N_KV_HEADS, HEAD_DIM)
    new_cache_v = v_ctx.reshape(batch, 1, seq, N_KV_HEADS, HEAD_DIM)

    def states(fin, part):
        return fin[:, part].reshape(batch, SSM_GROUPS, SSM_STATE)

    new_re = jnp.stack([states(fin_f, 0), states(fin_b, 0)], axis=1)[:, None]
    new_im = jnp.stack([states(fin_f, 1), states(fin_b, 1)], axis=1)[:, None]
    return (y_prompt, y_sample, new_cache_k, new_cache_v, new_re, new_im)
```

```python
import functools
import math

import jax
import jax.numpy as jnp
from jax import lax
from jax.experimental import pallas as pl
from jax.experimental.pallas import tpu as pltpu

F32 = jnp.float32
BF16 = jnp.bfloat16

D_MODEL = 1024
SSM_WIDTH = 512
ATTN_WIDTH = 512
SSM_GROUP = 16
SSM_GROUPS = 32
SSM_STATE = 64
HEAD_DIM = 64
N_HEADS = 8
N_KV_HEADS = 2
KV_REP = 4
KV_WIDTH = 128
GRID_W = 64
ROPE_AXIS_DIM = 32
ROPE_BASE = 10000.0
EPS = 1e-6
LAMBDA_RE_MAX = -1e-4
NEG_INF = -1e30
LOG2E = math.log2(math.e)
PAST_LEN = 256

LANES = 128
BF16_SUBLANES = 16
SEQS = 8
TL = 128
ROWS = SEQS * TL
PITCH = TL + 8
SSM_BLOCKS = SSM_WIDTH // LANES
ALL_ST = SSM_GROUPS * SSM_STATE
BT = 4
KTILE = 2 * LANES
PIECE = KTILE // BT
UNIT_GROUPS = PIECE // SSM_GROUP
UNIT_ST = UNIT_GROUPS * SSM_STATE
N_UNITS = SSM_WIDTH // PIECE
QUADS = TL // BT
QROWS = QUADS * SEQS
S5_BUFS = 4
VMEM_LIMIT = 58 * 1024 * 1024

BWD, FWD = 1, 0
W_IN_U = 0
W_IN_ZS = SSM_WIDTH
W_IN_Q = 2 * SSM_WIDTH
W_IN_K = 2 * SSM_WIDTH + ATTN_WIDTH
W_IN_ZA = W_IN_K + 2 * KV_WIDTH
D_IN = W_IN_ZA + ATTN_WIDTH


def _const_spec(shape, index=None):
    index = (0,) * len(shape) if index is None else index
    return pl.BlockSpec(shape, lambda g, t: index, pipeline_mode=pl.Buffered(1))


def _modulation_kernel(cond_ref, w_ref, b_ref, out_ref):
    a = jax.nn.silu(cond_ref[...]).astype(BF16)
    out_ref[...] = jnp.dot(a, w_ref[...].astype(BF16), preferred_element_type=F32) + b_ref[...]


def _modulation(cond, w_mod, b_mod):
    rows = cond.shape[0]
    n = w_mod.shape[1]
    tn = 1024
    return pl.pallas_call(
        _modulation_kernel,
        grid=(n // tn,),
        in_specs=[pl.BlockSpec((rows, D_MODEL), lambda j: (0, 0)),
                  pl.BlockSpec((D_MODEL, tn), lambda j: (0, j)),
                  pl.BlockSpec((1, tn), lambda j: (0, j))],
        out_specs=pl.BlockSpec((rows, tn), lambda j: (0, j)),
        out_shape=jax.ShapeDtypeStruct((rows, n), F32),
        name="modulation",
    )(cond, w_mod, b_mod)


def _zoh(lam_re, lam_im, log_dt):
    lam_re = jnp.minimum(lam_re, LAMBDA_RE_MAX)
    dt = jnp.exp(log_dt)
    mag = jnp.exp(lam_re * dt)
    ang = lam_im * dt
    lbar_re = mag * jnp.cos(ang)
    lbar_im = mag * jnp.sin(ang)
    nr = lbar_re - 1.0
    ni = lbar_im
    den = lam_re * lam_re + lam_im * lam_im
    f_re = (nr * lam_re + ni * lam_im) / den
    f_im = (ni * lam_re - nr * lam_im) / den
    return lbar_re, lbar_im, f_re, f_im


def _cmul(a, b):
    return a[0] * b[0] - a[1] * b[1], a[0] * b[1] + a[1] * b[0]


def _s5_weights_kernel(lam_ref, bt_ref, cre_ref, cim_ref, lamflat_ref, win_ref, wout_ref, wk_ref, lpow_ref):
    p_row = lax.broadcasted_iota(jnp.int32, (SSM_STATE, UNIT_ST), 0)
    p_col = lax.broadcasted_iota(jnp.int32, (SSM_STATE, UNIT_ST), 1) & (SSM_STATE - 1)
    spread = jnp.where(p_row == p_col, 1.0, 0.0).astype(BF16)
    g_row = lax.broadcasted_iota(jnp.int32, (PIECE, UNIT_ST), 0) // SSM_GROUP
    g_col = lax.broadcasted_iota(jnp.int32, (PIECE, UNIT_ST), 1) // SSM_STATE
    own_block = g_row == g_col

    def blocks(x):
        return jnp.where(own_block, jnp.dot(x.astype(BF16), spread, preferred_element_type=F32), 0.0)

    def blocks2(z):
        return blocks(z[0]).astype(BF16), blocks(z[1]).astype(BF16)

    def in_block(src, c):
        dims = (((1,), (1,)), ((), ()))
        return (lax.dot_general(src[0], c[0], dims, preferred_element_type=F32)
                - lax.dot_general(src[1], c[1], dims, preferred_element_type=F32))

    def powers(z, lbar, n):
        out = [z]
        for _ in range(n):
            out.append(_cmul(out[-1], lbar))
        return out

    k_lag = {}
    for d in (FWD, BWD):
        lbar_re, lbar_im, f_re, f_im = _zoh(lam_ref[0, d], lam_ref[1, d], lam_ref[2, d])
        lbar = (lbar_re, lbar_im)
        bb_pow = powers(_cmul((f_re, f_im), (bt_ref[0, d], bt_ref[1, d])), lbar, BT - 1)
        c_pow = powers((cre_ref[d], cim_ref[d]), lbar, BT)
        c_b = blocks2(c_pow[0])
        to_end = [(BT - 1 - s) if d == FWD else s for s in range(BT)]
        for s in range(BT):
            w = blocks2(bb_pow[to_end[s]])
            win_ref[d, s * PIECE:(s + 1) * PIECE, 0:UNIT_ST] = w[0]
            win_ref[d, s * PIECE:(s + 1) * PIECE, UNIT_ST:2 * UNIT_ST] = w[1]
        out_re = jnp.concatenate([blocks(c_pow[BT - to_end[s]][0]) for s in range(BT)], axis=0)
        out_im = jnp.concatenate([blocks(-c_pow[BT - to_end[s]][1]) for s in range(BT)], axis=0)
        wout_ref[d, 0:UNIT_ST, :] = out_re.T.astype(BF16)
        wout_ref[d, UNIT_ST:2 * UNIT_ST, :] = out_im.T.astype(BF16)
        k_lag[d] = [in_block(blocks2(bb_pow[m]), c_b) for m in range(BT)]
    for src in range(BT):
        for out in range(BT):
            if out == src:
                k = k_lag[FWD][0] + k_lag[BWD][0]
            elif out > src:
                k = k_lag[FWD][out - src]
            else:
                k = k_lag[BWD][src - out]
            wk_ref[src * PIECE:(src + 1) * PIECE, out * PIECE:(out + 1) * PIECE] = k.astype(BF16)

    @pl.when(pl.program_id(0) == 0)
    def _():
        lbar_re, lbar_im, _, _ = _zoh(lamflat_ref[0], lamflat_ref[1], lamflat_ref[2])
        p = (lbar_re, lbar_im)
        for _ in range(BT.bit_length() - 1):
            p = _cmul(p, p)
        lpow_ref[0] = p[0]
        lpow_ref[1] = p[1]


def _s5_weights(lam_re, lam_im, log_dt, b_re, b_im, c_re, c_im):
    assert BT & (BT - 1) == 0
    n = SSM_GROUPS * SSM_GROUP
    lam3 = jnp.stack([lam_re, lam_im, jnp.broadcast_to(log_dt[:, :, None], lam_re.shape)])
    lam_rows = jnp.broadcast_to(lam3[:, :, :, None, :], (3, 2, SSM_GROUPS, SSM_GROUP, SSM_STATE)).reshape(3, 2, n, SSM_STATE)
    lam_flat = jnp.broadcast_to(lam3.reshape(3, 1, 2 * ALL_ST), (3, SEQS, 2 * ALL_ST))
    bt = jnp.transpose(jnp.stack([b_re, b_im]), (0, 1, 2, 4, 3)).reshape(2, 2, n, SSM_STATE)
    return pl.pallas_call(
        _s5_weights_kernel,
        grid=(N_UNITS,),
        in_specs=[pl.BlockSpec((3, 2, PIECE, SSM_STATE), lambda m: (0, 0, m, 0)),
                  pl.BlockSpec((2, 2, PIECE, SSM_STATE), lambda m: (0, 0, m, 0)),
                  pl.BlockSpec((2, PIECE, SSM_STATE), lambda m: (0, m, 0)),
                  pl.BlockSpec((2, PIECE, SSM_STATE), lambda m: (0, m, 0)),
                  pl.BlockSpec((3, SEQS, 2 * ALL_ST), lambda m: (0, 0, 0))],
        out_specs=[pl.BlockSpec((2, None, KTILE, 2 * UNIT_ST), lambda m: (0, m, 0, 0)),
                   pl.BlockSpec((2, None, 2 * UNIT_ST, KTILE), lambda m: (0, m, 0, 0)),
                   pl.BlockSpec((None, KTILE, KTILE), lambda m: (m, 0, 0)),
                   pl.BlockSpec((2, SEQS, 2 * ALL_ST), lambda m: (0, 0, 0))],
        out_shape=[jax.ShapeDtypeStruct((2, N_UNITS, KTILE, 2 * UNIT_ST), BF16),
                   jax.ShapeDtypeStruct((2, N_UNITS, 2 * UNIT_ST, KTILE), BF16),
                   jax.ShapeDtypeStruct((N_UNITS, KTILE, KTILE), BF16),
                   jax.ShapeDtypeStruct((2, SEQS, 2 * ALL_ST), F32)],
        name="s5_weights",
    )(lam_rows, bt, c_re.reshape(2, n, SSM_STATE), c_im.reshape(2, n, SSM_STATE), lam_flat)


def _rope_tables(length):
    pos = jnp.arange(length)
    row = (pos // GRID_W).astype(F32)
    col = (pos % GRID_W).astype(F32)
    freqs = ROPE_BASE ** (-jnp.arange(0, ROPE_AXIS_DIM, 2, dtype=F32) / ROPE_AXIS_DIM)
    ang_r = row[:, None] * freqs[None, :]
    ang_c = col[:, None] * freqs[None, :]
    cos = jnp.concatenate([jnp.cos(ang_r), jnp.cos(ang_r), jnp.cos(ang_c), jnp.cos(ang_c)], axis=-1)
    sin = jnp.concatenate([-jnp.sin(ang_r), jnp.sin(ang_r), -jnp.sin(ang_c), jnp.sin(ang_c)], axis=-1)
    return jnp.tile(cos, (1, 2)), jnp.tile(sin, (1, 2))


def _seq_rows(b):
    return slice(b * TL, (b + 1) * TL)


def _mod_row(b, per_sequence):
    return b if per_sequence else SEQS


def _norm_mod(h_ref, x_ref, mod_ref, nw_ref, b, per_sequence):
    xb = x_ref[b]
    ms = jnp.mean(xb * xb, axis=-1, keepdims=True)
    m = _mod_row(b, per_sequence)
    shift = mod_ref[m:m + 1, 0:D_MODEL]
    gain = nw_ref[...] * (1.0 + mod_ref[m:m + 1, D_MODEL:2 * D_MODEL])
    h_ref[_seq_rows(b), :] = ((xb * lax.rsqrt(ms + EPS)) * gain + shift).astype(BF16)


def _rope(x, cos, sin):
    lane = lax.broadcasted_iota(jnp.int32, x.shape, 1)
    first = (lane & 31) < 16
    partner = jnp.where(first, pltpu.roll(x, LANES - 16, 1), pltpu.roll(x, 16, 1))
    x3 = x.reshape(SEQS, TL, LANES)
    p3 = partner.reshape(SEQS, TL, LANES)
    return (x3 * cos[None] + p3 * sin[None]).reshape(ROWS, LANES)


def _store_column_blocks(dst_ref, val, first_seq):
    for j in range(val.shape[0] // TL):
        b = first_seq + j
        for cb in range(SSM_BLOCKS):
            dst_ref[cb, b * PITCH:b * PITCH + TL, :] = val[_seq_rows(j), cb * LANES:(cb + 1) * LANES]


def _token_rows(l):
    return l // BT, slice((l % BT) * SEQS, (l % BT + 1) * SEQS)


def _slot_rows(slot):
    return slice(slot * SEQS, (slot + 1) * SEQS)


def _to_token_major(dst_ref, src_ref):
    for l in range(TL):
        j, rows = _token_rows(l)
        for cb in range(SSM_BLOCKS):
            dst_ref[j, rows, cb * LANES:(cb + 1) * LANES] = src_ref[cb, pl.ds(l, SEQS, stride=PITCH), :]


def _to_sequence_major(dst_ref, src_ref):
    for l in range(TL):
        j, rows = _token_rows(l)
        for cb in range(SSM_BLOCKS):
            dst_ref[cb, pl.ds(l, SEQS, stride=PITCH), :] = src_ref[j, rows, cb * LANES:(cb + 1) * LANES]


def _low_half():
    return lax.broadcasted_iota(jnp.int32, (QROWS, LANES), 1) < PIECE


def _split_halves(a, b):
    low = _low_half()
    return (jnp.where(low, a, pltpu.roll(b, PIECE, 1)), jnp.where(low, pltpu.roll(a, PIECE, 1), b))


def _merge_halves(left, right):
    low = _low_half()
    return (jnp.where(low, left, pltpu.roll(right, PIECE, 1)), jnp.where(low, pltpu.roll(left, PIECE, 1), right))


def _pack_scan_operand(lhs_ref, u_tm):
    assert PIECE * 2 == LANES and BT % 2 == 0
    for blk in range(SSM_BLOCKS):
        lanes = slice(blk * LANES, (blk + 1) * LANES)
        for sp in range(BT // 2):
            a = u_tm[:, _slot_rows(2 * sp), lanes].reshape(QROWS, LANES)
            b = u_tm[:, _slot_rows(2 * sp + 1), lanes].reshape(QROWS, LANES)
            for side, val in enumerate(_split_halves(a, b)):
                c0 = (2 * blk + side) * KTILE + sp * LANES
                lhs_ref[:, c0:c0 + LANES] = val.astype(BF16)


def _unpack_outputs(y_low, y_high):
    out = []
    for sp in range(BT // 2):
        cols = slice(sp * LANES, (sp + 1) * LANES)
        for val in _merge_halves(y_low[:, cols], y_high[:, cols]):
            out.append(val.reshape(QUADS, SEQS, LANES))
    return out


def _scan_unit(s_ref, lpow_ref, carry_ref, direction, unit, reverse):
    cols = slice(unit * UNIT_ST, (unit + 1) * UNIT_ST)
    lanes = slice(direction * ALL_ST + unit * UNIT_ST, direction * ALL_ST + (unit + 1) * UNIT_ST)
    lr = lpow_ref[0, :, lanes]
    li = lpow_ref[1, :, lanes]
    sr = carry_ref[0, :, cols]
    si = carry_ref[1, :, cols]
    for i in range(QUADS):
        j = (QUADS - 1 - i) if reverse else i
        rows = slice(j * SEQS, (j + 1) * SEQS)
        inc_r = s_ref[rows, 0:UNIT_ST]
        inc_i = s_ref[rows, UNIT_ST:2 * UNIT_ST]
        s_ref[rows, 0:UNIT_ST] = sr
        s_ref[rows, UNIT_ST:2 * UNIT_ST] = si
        sr, si = lr * sr - li * si + inc_r, lr * si + li * sr + inc_i
    carry_ref[0, :, cols] = sr
    carry_ref[1, :, cols] = si


def _s5_direction(lhs_ref, s_ref, carry_ref, win_ref, wout_ref, lpow_ref, first_ref, first_s, emit, direction,
                  wk_ref=None, between=None):
    reverse = direction == BWD

    def lhs(unit):
        return lhs_ref[:, unit * KTILE:(unit + 1) * KTILE]

    def project_in(unit):
        s_ref[unit % S5_BUFS] = jnp.dot(lhs(unit), win_ref[unit], preferred_element_type=F32)

    for unit in range(min(S5_BUFS - 1, N_UNITS)):
        project_in(unit)
    y_low = None
    for unit in range(N_UNITS):
        if unit + S5_BUFS - 1 < N_UNITS:
            project_in(unit + S5_BUFS - 1)
        buf = s_ref.at[unit % S5_BUFS]
        _scan_unit(buf, lpow_ref, carry_ref, direction, unit, reverse)
        y = jnp.dot(buf[...].astype(BF16), wout_ref[unit], preferred_element_type=F32)
        if wk_ref is not None:
            y = y + jnp.dot(lhs(unit), wk_ref[unit], preferred_element_type=F32)
        if unit % 2 == 0:
            y_low = y
        else:
            emit(unit // 2, _unpack_outputs(y_low, y))
        if between is not None:
            between(unit)

    if first_ref is not None:
        first_rows = slice(QROWS - 2 * SEQS, QROWS) if reverse else slice(0, 2 * SEQS)
        low = lax.broadcasted_iota(jnp.int32, (2 * SEQS, LANES), 1) < PIECE
        zeros = jnp.zeros((2 * SEQS, LANES), F32)
        for unit in range(N_UNITS):
            if reverse:
                tail = lhs_ref[first_rows, (unit + 1) * KTILE - LANES:(unit + 1) * KTILE].astype(F32)
                block = [jnp.where(low, pltpu.roll(tail, PIECE, 1), 0.0), zeros]
            else:
                head = lhs_ref[first_rows, unit * KTILE:unit * KTILE + LANES].astype(F32)
                block = [zeros, jnp.where(low, 0.0, pltpu.roll(head, PIECE, 1))]
            inc = jnp.dot(jnp.concatenate(block, axis=1).astype(BF16), win_ref[unit], preferred_element_type=F32)
            inc = inc[SEQS:2 * SEQS] if reverse else inc[0:SEQS]
            first_s[0, :, unit * UNIT_ST:(unit + 1) * UNIT_ST] = inc[:, 0:UNIT_ST]
            first_s[1, :, unit * UNIT_ST:(unit + 1) * UNIT_ST] = inc[:, UNIT_ST:2 * UNIT_ST]

        @pl.when(pl.program_id(1) == 0)
        def _():
            first_ref[...] = first_s[...]


def _s5_param_specs(direction):
    return [_const_spec((None, N_UNITS, KTILE, 2 * UNIT_ST), (direction, 0, 0, 0)),
            _const_spec((None, N_UNITS, 2 * UNIT_ST, KTILE), (direction, 0, 0, 0)),
            _const_spec((2, SEQS, 2 * ALL_ST))]


def _h0_specs(direction):
    return [_const_spec((SEQS, ALL_ST), (0, direction))] * 2


def _pass_a_kernel(*refs, rotary):
    if rotary:
        (x_ref, mod_ref, nw_ref, w_ref, win_ref, wout_ref, lsq_ref, wk_ref, d_ref, h0re_ref, h0im_ref, cos_ref, sin_ref,
         kcat_ref, vt_ref, lhs_ref, yb_ref, zs_ref, za_ref, ql_ref, qr_ref, h_s, u_sm, u_tm, s_s, carry) = refs
        fin_ref = first_s = None
    else:
        (x_ref, mod_ref, nw_ref, w_ref, win_ref, wout_ref, lsq_ref, wk_ref, d_ref,
         k_ref, v_ref, kcat_ref, vt_ref, lhs_ref, yb_ref, zs_ref, za_ref, ql_ref, qr_ref, fin_ref,
         h_s, u_sm, u_tm, s_s, carry, first_s) = refs

    @pl.when(pl.program_id(1) == 0)
    def _():
        if rotary:
            carry[0] = h0re_ref[...]
            carry[1] = h0im_ref[...]
        else:
            carry[...] = jnp.zeros_like(carry)

    half_seqs = SEQS // 2
    for hf in range(2):
        for b in range(hf * half_seqs, (hf + 1) * half_seqs):
            _norm_mod(h_s, x_ref, mod_ref, nw_ref, b, per_sequence=rotary)
        rows = slice(hf * half_seqs * TL, (hf + 1) * half_seqs * TL)
        u_half = jnp.dot(h_s[rows, :], w_ref[:, W_IN_U:W_IN_U + SSM_WIDTH], preferred_element_type=F32)
        _store_column_blocks(u_sm, u_half, first_seq=hf * half_seqs)
    _to_token_major(u_tm, u_sm)
    _pack_scan_operand(lhs_ref, u_tm)

    def project(c0, width):
        return jnp.dot(h_s[...], w_ref[:, c0:c0 + width], preferred_element_type=F32)

    def project_kv():
        kv = project(W_IN_K, 2 * KV_WIDTH)
        k = kv[:, 0:KV_WIDTH]
        v = kv[:, KV_WIDTH:2 * KV_WIDTH]
        if rotary:
            k = _rope(k, cos_ref[...], sin_ref[...])
        else:
            k_ref[...] = k.reshape(SEQS, TL, KV_WIDTH)
            v_ref[...] = v.reshape(SEQS, TL, KV_WIDTH)
        kcat = jnp.concatenate([k, pltpu.roll(k, HEAD_DIM, 1)], axis=1).astype(BF16)
        kcat_ref[...] = kcat.reshape(SEQS, TL, 2 * KV_WIDTH)
        for b in range(SEQS):
            vt_ref[b] = v[_seq_rows(b), :].T.astype(BF16)

    def project_q():
        qall = project(W_IN_Q, ATTN_WIDTH)
        even_head = lax.broadcasted_iota(jnp.int32, (ROWS, LANES), 1) < HEAD_DIM
        for cg in range(ATTN_WIDTH // LANES):
            cols = slice(cg * LANES, (cg + 1) * LANES)
            q = qall[:, cols]
            if rotary:
                q = _rope(q, cos_ref[...], sin_ref[...])
            q = q * (LOG2E * HEAD_DIM ** -0.5)
            ql_ref[:, cols] = jnp.where(even_head, q, 0.0).astype(BF16)
            qr_ref[:, cols] = jnp.where(even_head, 0.0, q).astype(BF16)

    def project_gate(c0, out_ref):
        out_ref[...] = jax.nn.silu(project(c0, out_ref.shape[1])).astype(BF16)

    side_work = [project_kv, project_q, lambda: project_gate(W_IN_ZS, zs_ref), lambda: project_gate(W_IN_ZA, za_ref)]

    def between(i):
        if i < len(side_work):
            side_work[i]()

    def emit(blk, slots):
        lanes = slice(blk * LANES, (blk + 1) * LANES)
        for slot, ys in enumerate(slots):
            rows = _slot_rows(slot)
            yb_ref[:, rows, lanes] = ys + u_tm[:, rows, lanes] * d_ref[:, lanes]

    _s5_direction(lhs_ref, s_s, carry, win_ref, wout_ref, lsq_ref, fin_ref, first_s, emit, BWD,
                  wk_ref=wk_ref, between=between)


def _pass_a(x, mod, norm_w, w_in, win, wout, lsq, wk, d, h0, rope):
    batch, length, _ = x.shape
    ng, nt = batch // SEQS, length // TL
    rotary = rope is not None
    rev = lambda t: nt - 1 - t
    in_specs = [
        pl.BlockSpec((SEQS, TL, D_MODEL), lambda g, t: (g, rev(t), 0)),
        _const_spec(mod.shape),
        _const_spec((1, D_MODEL)),
        _const_spec((D_MODEL, D_IN)),
    ] + _s5_param_specs(BWD) + [
        _const_spec((N_UNITS, KTILE, KTILE)),
        _const_spec((1, SSM_WIDTH)),
    ]
    args = [x, mod, norm_w, w_in, win, wout, lsq, wk, d]
    if rotary:
        in_specs += _h0_specs(BWD) + [pl.BlockSpec((TL, LANES), lambda g, t: (rev(t), 0))] * 2
        args += list(h0) + list(rope)
    assert ATTN_WIDTH == SSM_WIDTH
    tile_spec = pl.BlockSpec((None, None, ROWS, SSM_WIDTH), lambda g, t: (g, rev(t), 0, 0))
    tile_bf16 = jax.ShapeDtypeStruct((ng, nt, ROWS, SSM_WIDTH), BF16)
    out_specs = [
        pl.BlockSpec((SEQS, TL, 2 * KV_WIDTH), lambda g, t: (g, rev(t), 0)),
        pl.BlockSpec((SEQS, KV_WIDTH, TL), lambda g, t: (g, 0, rev(t))),
        pl.BlockSpec((None, None, QROWS, N_UNITS * KTILE), lambda g, t: (g, rev(t), 0, 0)),
        pl.BlockSpec((None, None, QUADS, BT * SEQS, SSM_WIDTH), lambda g, t: (g, rev(t), 0, 0, 0)),
    ] + [tile_spec] * 4
    out_shape = [
        jax.ShapeDtypeStruct((batch, length, 2 * KV_WIDTH), BF16),
        jax.ShapeDtypeStruct((batch, KV_WIDTH, length), BF16),
        jax.ShapeDtypeStruct((ng, nt, QROWS, N_UNITS * KTILE), BF16),
        jax.ShapeDtypeStruct((ng, nt, QUADS, BT * SEQS, SSM_WIDTH), F32),
        tile_bf16, tile_bf16,
        tile_bf16, tile_bf16,
    ]
    scratch = [
        pltpu.VMEM((ROWS, D_MODEL), BF16),
        pltpu.VMEM((SSM_BLOCKS, SEQS * PITCH, LANES), F32),
        pltpu.VMEM((QUADS, BT * SEQS, SSM_WIDTH), F32),
        pltpu.VMEM((S5_BUFS, QROWS, 2 * UNIT_ST), F32),
        pltpu.VMEM((2, SEQS, ALL_ST), F32),
    ]
    if not rotary:
        cache_spec = pl.BlockSpec((SEQS, TL, KV_WIDTH), lambda g, t: (g, rev(t), 0))
        cache_shape = jax.ShapeDtypeStruct((batch, length, KV_WIDTH), F32)
        out_specs = [cache_spec, cache_spec] + out_specs + [pl.BlockSpec((None, 2, SEQS, ALL_ST), lambda g, t: (g, 0, 0, 0))]
        out_shape = [cache_shape, cache_shape] + out_shape + [jax.ShapeDtypeStruct((ng, 2, SEQS, ALL_ST), F32)]
        scratch += [pltpu.VMEM((2, SEQS, ALL_ST), F32)]
    return pl.pallas_call(
        functools.partial(_pass_a_kernel, rotary=rotary),
        grid=(ng, nt),
        in_specs=in_specs, out_specs=out_specs, out_shape=out_shape, scratch_shapes=scratch,
        compiler_params=pltpu.CompilerParams(
            dimension_semantics=("arbitrary", "arbitrary"), vmem_limit_bytes=VMEM_LIMIT),
        name="pass_a_latent" if rotary else "pass_a_context",
    )(*args)


def _attention(b, q_refs, za_ref, mix_s, kcat_refs, vt_refs, sink_ref, masks):
    rows = _seq_rows(b)
    kcat = jnp.concatenate([r[b] for r in kcat_refs], axis=0)
    vt = jnp.concatenate([r[b] for r in vt_refs], axis=1)
    segment = lax.broadcasted_iota(jnp.int32, (1, 4 * TL), 1) // TL
    kv_heads = range(N_KV_HEADS)

    def side(g, grp):
        return g ^ grp

    def head(g, i, grp):
        return 4 * g + 2 * i + side(g, grp)

    def scores(grp):
        q = jnp.concatenate([q_refs[side(g, grp)][rows, blk * LANES:(blk + 1) * LANES]
                             for g in kv_heads for blk in (2 * g, 2 * g + 1)], axis=0)
        return lax.dot_general(kcat[:, grp * KV_WIDTH:(grp + 1) * KV_WIDTH], q, (((1,), (1,)), ((), ())),
                               preferred_element_type=F32)

    def softmax(grp, s):
        if masks is None:
            pieces = [s]
        else:
            pieces = [jnp.where(masks[0], s[0:TL], NEG_INF), s[TL:2 * TL],
                      jnp.where(masks[1], s[2 * TL:3 * TL], NEG_INF), s[3 * TL:]]
        sinks = [sink_ref[head(g, i, grp)] * LOG2E for g in kv_heads for i in range(2)]
        sink = jnp.where(segment == 0, sinks[0], jnp.where(segment == 1, sinks[1],
                                                           jnp.where(segment == 2, sinks[2], sinks[3])))
        m = sink
        for piece in pieces:
            m = jnp.maximum(m, jnp.max(piece, axis=0, keepdims=True))
        ps = [jnp.exp2(piece - m).astype(BF16) for piece in pieces]
        return (ps[0] if len(ps) == 1 else jnp.concatenate(ps, axis=0)), jnp.exp2(sink - m)

    ones_rows = jnp.ones((BF16_SUBLANES, vt.shape[1]), BF16)

    all_scores = [scores(grp) for grp in range(2)]
    probs, sink_terms = zip(*[softmax(grp, s) for grp, s in enumerate(all_scores)])
    for g in kv_heads:
        cols = slice(g * 2 * TL, (g + 1) * 2 * TL)
        values = jnp.concatenate([vt[g * HEAD_DIM:(g + 1) * HEAD_DIM, :], ones_rows], axis=0)
        p = jnp.concatenate([probs[0][:, cols], probs[1][:, cols]], axis=1)
        sink_term = jnp.concatenate([sink_terms[0][:, cols], sink_terms[1][:, cols]], axis=1)
        ot = jnp.dot(values, p, preferred_element_type=F32)
        out_t = ot[0:HEAD_DIM] / (ot[HEAD_DIM:HEAD_DIM + 1] + sink_term)
        for i in range(2):
            left = out_t[:, (2 * g + i) * TL:(2 * g + i + 1) * TL]
            right = out_t[:, (2 * (g ^ 1) + i) * TL:(2 * (g ^ 1) + i + 1) * TL]
            blk_cols = slice((2 * g + i) * LANES, (2 * g + i + 1) * LANES)
            pair_t = jnp.concatenate([left, right], axis=0)
            gated = pair_t.T * za_ref[rows, blk_cols].astype(F32)
            mix_s[rows, ATTN_WIDTH + blk_cols.start:ATTN_WIDTH + blk_cols.stop] = gated.astype(BF16)


def _pass_b_kernel(*refs, windowed, n_tiles):
    if windowed:
        (sink_ref, x_ref, mod_ref, win_ref, wout_ref, lsq_ref, wg_ref, bg_ref,
         lhs_ref, yb_ref, zs_ref, za_ref, ql_ref, qr_ref, kx_ref, vx_ref, wo_ref, fw_ref, h0re_ref, h0im_ref,
         kp_ref, kc_ref, kn_ref, vp_ref, vc_ref, vn_ref,
         y_ref, y_sm, y_tm, s_s, carry, mix_s) = refs
        fin_ref = first_s = None
    else:
        (sink_ref, x_ref, mod_ref, win_ref, wout_ref, lsq_ref, wg_ref, bg_ref,
         lhs_ref, yb_ref, zs_ref, za_ref, ql_ref, qr_ref, kx_ref, vx_ref, wo_ref, fw_ref,
         y_ref, fin_ref, y_sm, y_tm, s_s, carry, mix_s, first_s) = refs
    t = pl.program_id(1)

    @pl.when(t == 0)
    def _():
        if windowed:
            carry[0] = h0re_ref[...]
            carry[1] = h0im_ref[...]
        else:
            carry[...] = jnp.zeros_like(carry)

    if windowed:
        kj = lax.broadcasted_iota(jnp.int32, (TL, 4 * TL), 0)
        qi = lax.broadcasted_iota(jnp.int32, (TL, 4 * TL), 1) & (TL - 1)
        masks = ((kj >= qi) & (t > 0), (kj <= qi) & (t < n_tiles - 1))
        key_refs = (kp_ref, kc_ref, kn_ref, kx_ref)
        val_refs = (vp_ref, vc_ref, vn_ref, vx_ref)
    else:
        masks = None
        key_refs = (kx_ref,)
        val_refs = (vx_ref,)

    seqs_per_unit = SEQS // N_UNITS

    def between(i):
        for b in range(i * seqs_per_unit, (i + 1) * seqs_per_unit):
            _attention(b, (ql_ref, qr_ref), za_ref, mix_s, key_refs, val_refs, sink_ref, masks)

    def emit(blk, slots):
        lanes = slice(blk * LANES, (blk + 1) * LANES)
        for slot, ys in enumerate(slots):
            rows = _slot_rows(slot)
            y_tm[:, rows, lanes] = ys + yb_ref[:, rows, lanes]

    _s5_direction(lhs_ref, s_s, carry, win_ref, wout_ref, lsq_ref, fin_ref, first_s, emit, FWD, between=between)

    glu_blocks = QUADS // 4
    for rc in range(QUADS // glu_blocks):
        pr = slice(rc * glu_blocks, (rc + 1) * glu_blocks)
        y = jax.nn.gelu(y_tm[pr].reshape(glu_blocks * BT * SEQS, SSM_WIDTH))
        gl = jnp.dot(y.astype(BF16), wg_ref[...], preferred_element_type=F32) + bg_ref[...]
        y_tm[pr] = (y * jax.nn.sigmoid(gl)).reshape(glu_blocks, BT * SEQS, SSM_WIDTH)
    _to_sequence_major(y_sm, y_tm)
    for b in range(SEQS):
        for cb in range(SSM_BLOCKS):
            cols = slice(cb * LANES, (cb + 1) * LANES)
            gated = y_sm[cb, b * PITCH:b * PITCH + TL, :] * zs_ref[_seq_rows(b), cols].astype(F32)
            mix_s[_seq_rows(b), cols] = gated.astype(BF16)

    half_seqs = SEQS // 2
    for hf in range(2):
        rows = slice(hf * half_seqs * TL, (hf + 1) * half_seqs * TL)
        proj = jnp.dot(mix_s[rows, :], wo_ref[...], preferred_element_type=F32)
        y_ref[hf * half_seqs:(hf + 1) * half_seqs] = proj.reshape(half_seqs, TL, D_MODEL)
    for b in range(SEQS):
        m = _mod_row(b, per_sequence=windowed)
        gate = mod_ref[m:m + 1, 2 * D_MODEL:3 * D_MODEL]
        r = x_ref[b] + gate * y_ref[b]
        ms = jnp.mean(r * r, axis=-1, keepdims=True)
        y_ref[b] = (r * lax.rsqrt(ms + EPS)) * fw_ref[...]


def _pass_b(x, mod, win, wout, lsq, w_glu, b_glu, from_a, kx, vx, w_out, fnorm_w, sink, h0, k_loc, v_loc):
    batch, length, _ = x.shape
    ng, nt = batch // SEQS, length // TL
    windowed = k_loc is not None
    tile_spec = pl.BlockSpec((None, None, ROWS, SSM_WIDTH), lambda g, t: (g, t, 0, 0))
    in_specs = [
        pl.BlockSpec(memory_space=pltpu.SMEM),
        pl.BlockSpec((SEQS, TL, D_MODEL), lambda g, t: (g, t, 0)),
        _const_spec(mod.shape),
    ] + _s5_param_specs(FWD) + [
        _const_spec((SSM_WIDTH, SSM_WIDTH)),
        _const_spec((1, SSM_WIDTH)),
        pl.BlockSpec((None, None, QROWS, N_UNITS * KTILE), lambda g, t: (g, t, 0, 0)),
        pl.BlockSpec((None, None, QUADS, BT * SEQS, SSM_WIDTH), lambda g, t: (g, t, 0, 0, 0)),
    ] + [tile_spec] * 4 + [
        pl.BlockSpec((SEQS, PAST_LEN, 2 * KV_WIDTH), lambda g, t: (g, 0, 0)),
        pl.BlockSpec((SEQS, KV_WIDTH, PAST_LEN), lambda g, t: (g, 0, 0)),
        _const_spec((D_MODEL, D_MODEL)),
        _const_spec((1, D_MODEL)),
    ]
    args = [sink, x, mod, win, wout, lsq, w_glu, b_glu, *from_a, kx, vx, w_out, fnorm_w]
    if windowed:
        in_specs += _h0_specs(FWD)
        args += list(h0)
        band_t = (lambda t: jnp.maximum(t - 1, 0), lambda t: t, lambda t: jnp.minimum(t + 1, nt - 1))
        in_specs += [pl.BlockSpec((SEQS, TL, 2 * KV_WIDTH), lambda g, t, f=f: (g, f(t), 0)) for f in band_t]
        in_specs += [pl.BlockSpec((SEQS, KV_WIDTH, TL), lambda g, t, f=f: (g, 0, f(t))) for f in band_t]
        args += [k_loc] * 3 + [v_loc] * 3
    out_specs = [pl.BlockSpec((SEQS, TL, D_MODEL), lambda g, t: (g, t, 0))]
    out_shape = [jax.ShapeDtypeStruct((batch, length, D_MODEL), F32)]
    scratch = [
        pltpu.VMEM((SSM_BLOCKS, SEQS * PITCH, LANES), F32),
        pltpu.VMEM((QUADS, BT * SEQS, SSM_WIDTH), F32),
        pltpu.VMEM((S5_BUFS, QROWS, 2 * UNIT_ST), F32),
        pltpu.VMEM((2, SEQS, ALL_ST), F32),
        pltpu.VMEM((ROWS, D_MODEL), BF16),
    ]
    if not windowed:
        out_specs += [pl.BlockSpec((None, 2, SEQS, ALL_ST), lambda g, t: (g, 0, 0, 0))]
        out_shape += [jax.ShapeDtypeStruct((ng, 2, SEQS, ALL_ST), F32)]
        scratch += [pltpu.VMEM((2, SEQS, ALL_ST), F32)]
    return pl.pallas_call(
        functools.partial(_pass_b_kernel, windowed=windowed, n_tiles=nt),
        grid=(ng, nt),
        in_specs=in_specs, out_specs=out_specs, out_shape=out_shape, scratch_shapes=scratch,
        compiler_params=pltpu.CompilerParams(
            dimension_semantics=("arbitrary", "arbitrary"), vmem_limit_bytes=VMEM_LIMIT),
        name="pass_b_latent" if windowed else "pass_b_context",
    )(*args)


def kernel(x_prompt, x_sample, c, cache_k, cache_v, state_ssm_re, state_ssm_im, c_ctx, norm_w, w_mod, b_mod, w_in, ssm_lambda_re, ssm_lambda_im, ssm_log_dt, ssm_b_re, ssm_b_im, ssm_c_re, ssm_c_im, ssm_d, w_glu, b_glu, attn_sink, w_out, final_norm_w):
    assert norm_w.shape[0] == 1, "single trunk layer"
    batch, seq, _ = x_prompt.shape
    dec_batch, dec_seq, _ = x_sample.shape
    assert dec_batch == SEQS and batch % SEQS == 0 and seq % TL == 0 and dec_seq % TL == 0
    assert seq == PAST_LEN and cache_k.shape[2] == PAST_LEN

    w_in0 = w_in[0].astype(BF16)
    nw = norm_w[0][None, :]
    fw = final_norm_w[None, :]
    d = ssm_d[0][None, :]
    wg = w_glu[0].astype(BF16)
    bg = b_glu[0][None, :]
    wo = w_out[0].astype(BF16)
    sink = attn_sink[0]

    cond = jnp.concatenate([c, c_ctx[None, :], jnp.zeros((16 - SEQS - 1, D_MODEL), F32)], axis=0)
    mod = _modulation(cond, w_mod[0], b_mod[0][None, :])

    win, wout, wk, lsq = _s5_weights(ssm_lambda_re[0], ssm_lambda_im[0], ssm_log_dt[0], ssm_b_re[0], ssm_b_im[0],
                                     ssm_c_re[0], ssm_c_im[0])
    h0 = (state_ssm_re.reshape(SEQS, 2 * ALL_ST), state_ssm_im.reshape(SEQS, 2 * ALL_ST))
    rope = _rope_tables(dec_seq)

    k_ctx, v_ctx, kcat_ctx, vt_ctx, *tiles_ctx, fin_b = _pass_a(
        x_prompt, mod, nw, w_in0, win, wout, lsq, wk, d, None, None)
    y_prompt, fin_f = _pass_b(x_prompt, mod, win, wout, lsq, wg, bg, tiles_ctx, kcat_ctx, vt_ctx, wo, fw, sink,
                              None, None, None)

    kcat_lat, vt_lat, *tiles_lat = _pass_a(x_sample, mod, nw, w_in0, win, wout, lsq, wk, d, h0, rope)
    kx = cache_k[:, 0].reshape(dec_batch, PAST_LEN, KV_WIDTH)
    kx = jnp.concatenate([kx, jnp.roll(kx, HEAD_DIM, axis=-1)], axis=-1).astype(BF16)
    vx = jnp.swapaxes(cache_v[:, 0].reshape(dec_batch, PAST_LEN, KV_WIDTH), 1, 2).astype(BF16)
    (y_sample,) = _pass_b(x_sample, mod, win, wout, lsq, wg, bg, tiles_lat, kx, vx, wo, fw, sink,
                          h0, kcat_lat, vt_lat)

    new_cache_k = k_ctx.reshape(batch, 1, seq, N_KV_HEADS, HEAD_DIM)
    new_cache_v = v_ctx.reshape(batch, 1, seq, N_KV_HEADS, HEAD_DIM)

    def states(fin, part):
        return fin[:, part].reshape(batch, SSM_GROUPS, SSM_STATE)

    new_re = jnp.stack([states(fin_f, 0), states(fin_b, 0)], axis=1)[:, None]
    new_im = jnp.stack([states(fin_f, 1), states(fin_b, 1)], axis=1)[:, None]
    return (y_prompt, y_sample, new_cache_k, new_cache_v, new_re, new_im)
```

```python
import functools
import math

import jax
import jax.numpy as jnp
from jax import lax
from jax.experimental import pallas as pl
from jax.experimental.pallas import tpu as pltpu

F32 = jnp.float32
BF16 = jnp.bfloat16

D_MODEL = 1024
SSM_WIDTH = 512
ATTN_WIDTH = 512
SSM_GROUP = 16
SSM_GROUPS = 32
SSM_STATE = 64
HEAD_DIM = 64
N_HEADS = 8
N_KV_HEADS = 2
KV_REP = 4
KV_WIDTH = 128
GRID_W = 64
ROPE_AXIS_DIM = 32
ROPE_BASE = 10000.0
EPS = 1e-6
LAMBDA_RE_MAX = -1e-4
NEG_INF = -1e30
LOG2E = math.log2(math.e)
PAST_LEN = 256

LANES = 128
BF16_SUBLANES = 16
SEQS = 8
TL = 128
ROWS = SEQS * TL
PITCH = TL + 8
SSM_BLOCKS = SSM_WIDTH // LANES
ALL_ST = SSM_GROUPS * SSM_STATE
BT = 4
KTILE = 2 * LANES
PIECE = KTILE // BT
UNIT_GROUPS = PIECE // SSM_GROUP
UNIT_ST = UNIT_GROUPS * SSM_STATE
N_UNITS = SSM_WIDTH // PIECE
QUADS = TL // BT
QROWS = QUADS * SEQS
S5_BUFS_A = 8
S5_BUFS_B = 2
VMEM_LIMIT = 58 * 1024 * 1024

BWD, FWD = 1, 0
W_IN_U = 0
W_IN_ZS = SSM_WIDTH
W_IN_Q = 2 * SSM_WIDTH
W_IN_K = 2 * SSM_WIDTH + ATTN_WIDTH
W_IN_ZA = W_IN_K + 2 * KV_WIDTH
D_IN = W_IN_ZA + ATTN_WIDTH


def _const_spec(shape, index=None):
    index = (0,) * len(shape) if index is None else index
    return pl.BlockSpec(shape, lambda g, t: index, pipeline_mode=pl.Buffered(1))


def _modulation_kernel(cond_ref, w_ref, b_ref, out_ref):
    a = jax.nn.silu(cond_ref[...]).astype(BF16)
    out_ref[...] = jnp.dot(a, w_ref[...].astype(BF16), preferred_element_type=F32) + b_ref[...]


def _modulation(cond, w_mod, b_mod):
    rows = cond.shape[0]
    n = w_mod.shape[1]
    tn = 1024
    return pl.pallas_call(
        _modulation_kernel,
        grid=(n // tn,),
        in_specs=[pl.BlockSpec((rows, D_MODEL), lambda j: (0, 0)),
                  pl.BlockSpec((D_MODEL, tn), lambda j: (0, j)),
                  pl.BlockSpec((1, tn), lambda j: (0, j))],
        out_specs=pl.BlockSpec((rows, tn), lambda j: (0, j)),
        out_shape=jax.ShapeDtypeStruct((rows, n), F32),
        name="modulation",
    )(cond, w_mod, b_mod)


def _zoh(lam_re, lam_im, log_dt):
    lam_re = jnp.minimum(lam_re, LAMBDA_RE_MAX)
    dt = jnp.exp(log_dt)
    mag = jnp.exp(lam_re * dt)
    ang = lam_im * dt
    lbar_re = mag * jnp.cos(ang)
    lbar_im = mag * jnp.sin(ang)
    nr = lbar_re - 1.0
    ni = lbar_im
    den = lam_re * lam_re + lam_im * lam_im
    f_re = (nr * lam_re + ni * lam_im) / den
    f_im = (ni * lam_re - nr * lam_im) / den
    return lbar_re, lbar_im, f_re, f_im


def _cmul(a, b):
    return a[0] * b[0] - a[1] * b[1], a[0] * b[1] + a[1] * b[0]


def _s5_weights_kernel(lam_ref, bt_ref, cre_ref, cim_ref, lamflat_ref, win_ref, wout_ref, wk_ref, lpow_ref):
    p_row = lax.broadcasted_iota(jnp.int32, (SSM_STATE, UNIT_ST), 0)
    p_col = lax.broadcasted_iota(jnp.int32, (SSM_STATE, UNIT_ST), 1) & (SSM_STATE - 1)
    spread = jnp.where(p_row == p_col, 1.0, 0.0).astype(BF16)
    g_row = lax.broadcasted_iota(jnp.int32, (PIECE, UNIT_ST), 0) // SSM_GROUP
    g_col = lax.broadcasted_iota(jnp.int32, (PIECE, UNIT_ST), 1) // SSM_STATE
    own_block = g_row == g_col

    def blocks(x):
        return jnp.where(own_block, jnp.dot(x.astype(BF16), spread, preferred_element_type=F32), 0.0)

    def blocks2(z):
        return blocks(z[0]).astype(BF16), blocks(z[1]).astype(BF16)

    def in_block(src, c):
        dims = (((1,), (1,)), ((), ()))
        return (lax.dot_general(src[0], c[0], dims, preferred_element_type=F32)
                - lax.dot_general(src[1], c[1], dims, preferred_element_type=F32))

    def powers(z, lbar, n):
        out = [z]
        for _ in range(n):
            out.append(_cmul(out[-1], lbar))
        return out

    k_lag = {}
    for d in (FWD, BWD):
        lbar_re, lbar_im, f_re, f_im = _zoh(lam_ref[0, d], lam_ref[1, d], lam_ref[2, d])
        lbar = (lbar_re, lbar_im)
        bb_pow = powers(_cmul((f_re, f_im), (bt_ref[0, d], bt_ref[1, d])), lbar, BT - 1)
        c_pow = powers((cre_ref[d], cim_ref[d]), lbar, BT)
        c_b = blocks2(c_pow[0])
        to_end = [(BT - 1 - s) if d == FWD else s for s in range(BT)]
        for s in range(BT):
            w = blocks2(bb_pow[to_end[s]])
            win_ref[d, s * PIECE:(s + 1) * PIECE, 0:UNIT_ST] = w[0]
            win_ref[d, s * PIECE:(s + 1) * PIECE, UNIT_ST:2 * UNIT_ST] = w[1]
        out_re = jnp.concatenate([blocks(c_pow[BT - to_end[s]][0]) for s in range(BT)], axis=0)
        out_im = jnp.concatenate([blocks(-c_pow[BT - to_end[s]][1]) for s in range(BT)], axis=0)
        wout_ref[d, 0:UNIT_ST, :] = out_re.T.astype(BF16)
        wout_ref[d, UNIT_ST:2 * UNIT_ST, :] = out_im.T.astype(BF16)
        k_lag[d] = [in_block(blocks2(bb_pow[m]), c_b) for m in range(BT)]
    for src in range(BT):
        for out in range(BT):
            if out == src:
                k = k_lag[FWD][0] + k_lag[BWD][0]
            elif out > src:
                k = k_lag[FWD][out - src]
            else:
                k = k_lag[BWD][src - out]
            wk_ref[src * PIECE:(src + 1) * PIECE, out * PIECE:(out + 1) * PIECE] = k.astype(BF16)

    @pl.when(pl.program_id(0) == 0)
    def _():
        lbar_re, lbar_im, _, _ = _zoh(lamflat_ref[0], lamflat_ref[1], lamflat_ref[2])
        p = (lbar_re, lbar_im)
        for _ in range(BT.bit_length() - 1):
            p = _cmul(p, p)
        lpow_ref[0] = p[0]
        lpow_ref[1] = p[1]


def _s5_weights(lam_re, lam_im, log_dt, b_re, b_im, c_re, c_im):
    assert BT & (BT - 1) == 0
    n = SSM_GROUPS * SSM_GROUP
    lam3 = jnp.stack([lam_re, lam_im, jnp.broadcast_to(log_dt[:, :, None], lam_re.shape)])
    lam_rows = jnp.broadcast_to(lam3[:, :, :, None, :], (3, 2, SSM_GROUPS, SSM_GROUP, SSM_STATE)).reshape(3, 2, n, SSM_STATE)
    lam_flat = jnp.broadcast_to(lam3.reshape(3, 1, 2 * ALL_ST), (3, SEQS, 2 * ALL_ST))
    bt = jnp.transpose(jnp.stack([b_re, b_im]), (0, 1, 2, 4, 3)).reshape(2, 2, n, SSM_STATE)
    return pl.pallas_call(
        _s5_weights_kernel,
        grid=(N_UNITS,),
        in_specs=[pl.BlockSpec((3, 2, PIECE, SSM_STATE), lambda m: (0, 0, m, 0)),
                  pl.BlockSpec((2, 2, PIECE, SSM_STATE), lambda m: (0, 0, m, 0)),
                  pl.BlockSpec((2, PIECE, SSM_STATE), lambda m: (0, m, 0)),
                  pl.BlockSpec((2, PIECE, SSM_STATE), lambda m: (0, m, 0)),
                  pl.BlockSpec((3, SEQS, 2 * ALL_ST), lambda m: (0, 0, 0))],
        out_specs=[pl.BlockSpec((2, None, KTILE, 2 * UNIT_ST), lambda m: (0, m, 0, 0)),
                   pl.BlockSpec((2, None, 2 * UNIT_ST, KTILE), lambda m: (0, m, 0, 0)),
                   pl.BlockSpec((None, KTILE, KTILE), lambda m: (m, 0, 0)),
                   pl.BlockSpec((2, SEQS, 2 * ALL_ST), lambda m: (0, 0, 0))],
        out_shape=[jax.ShapeDtypeStruct((2, N_UNITS, KTILE, 2 * UNIT_ST), BF16),
                   jax.ShapeDtypeStruct((2, N_UNITS, 2 * UNIT_ST, KTILE), BF16),
                   jax.ShapeDtypeStruct((N_UNITS, KTILE, KTILE), BF16),
                   jax.ShapeDtypeStruct((2, SEQS, 2 * ALL_ST), F32)],
        name="s5_weights",
    )(lam_rows, bt, c_re.reshape(2, n, SSM_STATE), c_im.reshape(2, n, SSM_STATE), lam_flat)


def _rope_tables(length):
    pos = jnp.arange(length)
    row = (pos // GRID_W).astype(F32)
    col = (pos % GRID_W).astype(F32)
    freqs = ROPE_BASE ** (-jnp.arange(0, ROPE_AXIS_DIM, 2, dtype=F32) / ROPE_AXIS_DIM)
    ang_r = row[:, None] * freqs[None, :]
    ang_c = col[:, None] * freqs[None, :]
    cos = jnp.concatenate([jnp.cos(ang_r), jnp.cos(ang_r), jnp.cos(ang_c), jnp.cos(ang_c)], axis=-1)
    sin = jnp.concatenate([-jnp.sin(ang_r), jnp.sin(ang_r), -jnp.sin(ang_c), jnp.sin(ang_c)], axis=-1)
    return jnp.tile(cos, (1, 2)), jnp.tile(sin, (1, 2))


def _seq_rows(b):
    return slice(b * TL, (b + 1) * TL)


def _mod_row(b, per_sequence):
    return b if per_sequence else SEQS


def _norm_mod(h_ref, x_ref, mod_ref, nw_ref, b, per_sequence):
    xb = x_ref[b]
    ms = jnp.mean(xb * xb, axis=-1, keepdims=True)
    m = _mod_row(b, per_sequence)
    shift = mod_ref[m:m + 1, 0:D_MODEL]
    gain = nw_ref[...] * (1.0 + mod_ref[m:m + 1, D_MODEL:2 * D_MODEL])
    h_ref[_seq_rows(b), :] = ((xb * lax.rsqrt(ms + EPS)) * gain + shift).astype(BF16)


def _rope(x, cos, sin):
    lane = lax.broadcasted_iota(jnp.int32, x.shape, 1)
    first = (lane & 31) < 16
    partner = jnp.where(first, pltpu.roll(x, LANES - 16, 1), pltpu.roll(x, 16, 1))
    x3 = x.reshape(SEQS, TL, LANES)
    p3 = partner.reshape(SEQS, TL, LANES)
    return (x3 * cos[None] + p3 * sin[None]).reshape(ROWS, LANES)


def _store_column_blocks(dst_ref, val, first_seq):
    for j in range(val.shape[0] // TL):
        b = first_seq + j
        for cb in range(SSM_BLOCKS):
            dst_ref[cb, b * PITCH:b * PITCH + TL, :] = val[_seq_rows(j), cb * LANES:(cb + 1) * LANES]


def _token_rows(l):
    return l // BT, slice((l % BT) * SEQS, (l % BT + 1) * SEQS)


def _slot_rows(slot):
    return slice(slot * SEQS, (slot + 1) * SEQS)


def _to_token_major(dst_ref, src_ref):
    for l in range(TL):
        j, rows = _token_rows(l)
        for cb in range(SSM_BLOCKS):
            dst_ref[j, rows, cb * LANES:(cb + 1) * LANES] = src_ref[cb, pl.ds(l, SEQS, stride=PITCH), :]


def _to_sequence_major(dst_ref, src_ref):
    for l in range(TL):
        j, rows = _token_rows(l)
        for cb in range(SSM_BLOCKS):
            dst_ref[cb, pl.ds(l, SEQS, stride=PITCH), :] = src_ref[j, rows, cb * LANES:(cb + 1) * LANES]


def _low_half():
    return lax.broadcasted_iota(jnp.int32, (QROWS, LANES), 1) < PIECE


def _split_halves(a, b):
    low = _low_half()
    return (jnp.where(low, a, pltpu.roll(b, PIECE, 1)), jnp.where(low, pltpu.roll(a, PIECE, 1), b))


def _merge_halves(left, right):
    low = _low_half()
    return (jnp.where(low, left, pltpu.roll(right, PIECE, 1)), jnp.where(low, pltpu.roll(left, PIECE, 1), right))


def _pack_scan_operand(lhs_ref, u_tm):
    assert PIECE * 2 == LANES and BT % 2 == 0
    for blk in range(SSM_BLOCKS):
        lanes = slice(blk * LANES, (blk + 1) * LANES)
        for sp in range(BT // 2):
            a = u_tm[:, _slot_rows(2 * sp), lanes].reshape(QROWS, LANES)
            b = u_tm[:, _slot_rows(2 * sp + 1), lanes].reshape(QROWS, LANES)
            for side, val in enumerate(_split_halves(a, b)):
                c0 = (2 * blk + side) * KTILE + sp * LANES
                lhs_ref[:, c0:c0 + LANES] = val.astype(BF16)


def _unpack_outputs(y_low, y_high):
    out = []
    for sp in range(BT // 2):
        cols = slice(sp * LANES, (sp + 1) * LANES)
        for val in _merge_halves(y_low[:, cols], y_high[:, cols]):
            out.append(val.reshape(QUADS, SEQS, LANES))
    return out


def _scan_unit(s_ref, lpow_ref, carry_ref, direction, unit, reverse):
    cols = slice(unit * UNIT_ST, (unit + 1) * UNIT_ST)
    lanes = slice(direction * ALL_ST + unit * UNIT_ST, direction * ALL_ST + (unit + 1) * UNIT_ST)
    lr = lpow_ref[0, :, lanes]
    li = lpow_ref[1, :, lanes]
    sr = carry_ref[0, :, cols]
    si = carry_ref[1, :, cols]
    for i in range(QUADS):
        j = (QUADS - 1 - i) if reverse else i
        rows = slice(j * SEQS, (j + 1) * SEQS)
        inc_r = s_ref[rows, 0:UNIT_ST]
        inc_i = s_ref[rows, UNIT_ST:2 * UNIT_ST]
        s_ref[rows, 0:UNIT_ST] = sr
        s_ref[rows, UNIT_ST:2 * UNIT_ST] = si
        sr, si = lr * sr - li * si + inc_r, lr * si + li * sr + inc_i
    carry_ref[0, :, cols] = sr
    carry_ref[1, :, cols] = si


def _s5_direction(lhs_ref, s_ref, carry_ref, win_ref, wout_ref, lpow_ref, first_ref, first_s, emit, direction,
                  wk_ref=None, between=None):
    reverse = direction == BWD

    def lhs(unit):
        return lhs_ref[:, unit * KTILE:(unit + 1) * KTILE]

    bufs = s_ref.shape[0]
    ahead = min(bufs, N_UNITS) if bufs >= N_UNITS else bufs - 1

    def project_in(unit):
        s_ref[unit % bufs] = jnp.dot(lhs(unit), win_ref[unit], preferred_element_type=F32)

    for unit in range(min(ahead, N_UNITS)):
        project_in(unit)
    y_low = None
    for unit in range(N_UNITS):
        if ahead <= unit + ahead < N_UNITS:
            project_in(unit + ahead)
        buf = s_ref.at[unit % bufs]
        _scan_unit(buf, lpow_ref, carry_ref, direction, unit, reverse)
        y = jnp.dot(buf[...].astype(BF16), wout_ref[unit], preferred_element_type=F32)
        if wk_ref is not None:
            y = y + jnp.dot(lhs(unit), wk_ref[unit], preferred_element_type=F32)
        if unit % 2 == 0:
            y_low = y
        else:
            emit(unit // 2, _unpack_outputs(y_low, y))
        if between is not None:
            between(unit)

    if first_ref is not None:
        first_rows = slice(QROWS - 2 * SEQS, QROWS) if reverse else slice(0, 2 * SEQS)
        low = lax.broadcasted_iota(jnp.int32, (2 * SEQS, LANES), 1) < PIECE
        zeros = jnp.zeros((2 * SEQS, LANES), F32)
        for unit in range(N_UNITS):
            if reverse:
                tail = lhs_ref[first_rows, (unit + 1) * KTILE - LANES:(unit + 1) * KTILE].astype(F32)
                block = [jnp.where(low, pltpu.roll(tail, PIECE, 1), 0.0), zeros]
            else:
                head = lhs_ref[first_rows, unit * KTILE:unit * KTILE + LANES].astype(F32)
                block = [zeros, jnp.where(low, 0.0, pltpu.roll(head, PIECE, 1))]
            inc = jnp.dot(jnp.concatenate(block, axis=1).astype(BF16), win_ref[unit], preferred_element_type=F32)
            inc = inc[SEQS:2 * SEQS] if reverse else inc[0:SEQS]
            first_s[0, :, unit * UNIT_ST:(unit + 1) * UNIT_ST] = inc[:, 0:UNIT_ST]
            first_s[1, :, unit * UNIT_ST:(unit + 1) * UNIT_ST] = inc[:, UNIT_ST:2 * UNIT_ST]

        @pl.when(pl.program_id(1) == 0)
        def _():
            first_ref[...] = first_s[...]


def _s5_param_specs(direction):
    return [_const_spec((None, N_UNITS, KTILE, 2 * UNIT_ST), (direction, 0, 0, 0)),
            _const_spec((None, N_UNITS, 2 * UNIT_ST, KTILE), (direction, 0, 0, 0)),
            _const_spec((2, SEQS, 2 * ALL_ST))]


def _h0_specs(direction):
    return [_const_spec((SEQS, ALL_ST), (0, direction))] * 2


def _pass_a_kernel(*refs, rotary):
    if rotary:
        (x_ref, mod_ref, nw_ref, w_ref, win_ref, wout_ref, lsq_ref, wk_ref, d_ref, h0re_ref, h0im_ref, cos_ref, sin_ref,
         kcat_ref, vt_ref, lhs_ref, yb_ref, zs_ref, za_ref, ql_ref, qr_ref, h_s, u_sm, u_tm, s_s, carry) = refs
        fin_ref = first_s = None
    else:
        (x_ref, mod_ref, nw_ref, w_ref, win_ref, wout_ref, lsq_ref, wk_ref, d_ref,
         k_ref, v_ref, kcat_ref, vt_ref, lhs_ref, yb_ref, zs_ref, za_ref, ql_ref, qr_ref, fin_ref,
         h_s, u_sm, u_tm, s_s, carry, first_s) = refs

    @pl.when(pl.program_id(1) == 0)
    def _():
        if rotary:
            carry[0] = h0re_ref[...]
            carry[1] = h0im_ref[...]
        else:
            carry[...] = jnp.zeros_like(carry)

    half_seqs = SEQS // 2
    for hf in range(2):
        for b in range(hf * half_seqs, (hf + 1) * half_seqs):
            _norm_mod(h_s, x_ref, mod_ref, nw_ref, b, per_sequence=rotary)
        rows = slice(hf * half_seqs * TL, (hf + 1) * half_seqs * TL)
        u_half = jnp.dot(h_s[rows, :], w_ref[:, W_IN_U:W_IN_U + SSM_WIDTH], preferred_element_type=F32)
        _store_column_blocks(u_sm, u_half, first_seq=hf * half_seqs)
    _to_token_major(u_tm, u_sm)
    _pack_scan_operand(lhs_ref, u_tm)

    def project(c0, width):
        return jnp.dot(h_s[...], w_ref[:, c0:c0 + width], preferred_element_type=F32)

    def project_kv():
        kv = project(W_IN_K, 2 * KV_WIDTH)
        k = kv[:, 0:KV_WIDTH]
        v = kv[:, KV_WIDTH:2 * KV_WIDTH]
        if rotary:
            k = _rope(k, cos_ref[...], sin_ref[...])
        else:
            k_ref[...] = k.reshape(SEQS, TL, KV_WIDTH)
            v_ref[...] = v.reshape(SEQS, TL, KV_WIDTH)
        kcat = jnp.concatenate([k, pltpu.roll(k, HEAD_DIM, 1)], axis=1).astype(BF16)
        kcat_ref[...] = kcat.reshape(SEQS, TL, 2 * KV_WIDTH)
        for b in range(SEQS):
            vt_ref[b] = v[_seq_rows(b), :].T.astype(BF16)

    def project_q():
        qall = project(W_IN_Q, ATTN_WIDTH)
        even_head = lax.broadcasted_iota(jnp.int32, (ROWS, LANES), 1) < HEAD_DIM
        for cg in range(ATTN_WIDTH // LANES):
            cols = slice(cg * LANES, (cg + 1) * LANES)
            q = qall[:, cols]
            if rotary:
                q = _rope(q, cos_ref[...], sin_ref[...])
            q = q * (LOG2E * HEAD_DIM ** -0.5)
            ql_ref[:, cols] = jnp.where(even_head, q, 0.0).astype(BF16)
            qr_ref[:, cols] = jnp.where(even_head, 0.0, q).astype(BF16)

    def project_gate(c0, out_ref):
        out_ref[...] = jax.nn.silu(project(c0, out_ref.shape[1])).astype(BF16)

    side_work = [project_kv, project_q, lambda: project_gate(W_IN_ZS, zs_ref), lambda: project_gate(W_IN_ZA, za_ref)]

    def between(i):
        if i < len(side_work):
            side_work[i]()

    def emit(blk, slots):
        lanes = slice(blk * LANES, (blk + 1) * LANES)
        for slot, ys in enumerate(slots):
            rows = _slot_rows(slot)
            yb_ref[:, rows, lanes] = ys + u_tm[:, rows, lanes] * d_ref[:, lanes]

    _s5_direction(lhs_ref, s_s, carry, win_ref, wout_ref, lsq_ref, fin_ref, first_s, emit, BWD,
                  wk_ref=wk_ref, between=between)


def _pass_a(x, mod, norm_w, w_in, win, wout, lsq, wk, d, h0, rope):
    batch, length, _ = x.shape
    ng, nt = batch // SEQS, length // TL
    rotary = rope is not None
    rev = lambda t: nt - 1 - t
    in_specs = [
        pl.BlockSpec((SEQS, TL, D_MODEL), lambda g, t: (g, rev(t), 0)),
        _const_spec(mod.shape),
        _const_spec((1, D_MODEL)),
        _const_spec((D_MODEL, D_IN)),
    ] + _s5_param_specs(BWD) + [
        _const_spec((N_UNITS, KTILE, KTILE)),
        _const_spec((1, SSM_WIDTH)),
    ]
    args = [x, mod, norm_w, w_in, win, wout, lsq, wk, d]
    if rotary:
        in_specs += _h0_specs(BWD) + [pl.BlockSpec((TL, LANES), lambda g, t: (rev(t), 0))] * 2
        args += list(h0) + list(rope)
    assert ATTN_WIDTH == SSM_WIDTH
    tile_spec = pl.BlockSpec((None, None, ROWS, SSM_WIDTH), lambda g, t: (g, rev(t), 0, 0))
    tile_bf16 = jax.ShapeDtypeStruct((ng, nt, ROWS, SSM_WIDTH), BF16)
    out_specs = [
        pl.BlockSpec((SEQS, TL, 2 * KV_WIDTH), lambda g, t: (g, rev(t), 0)),
        pl.BlockSpec((SEQS, KV_WIDTH, TL), lambda g, t: (g, 0, rev(t))),
        pl.BlockSpec((None, None, QROWS, N_UNITS * KTILE), lambda g, t: (g, rev(t), 0, 0)),
        pl.BlockSpec((None, None, QUADS, BT * SEQS, SSM_WIDTH), lambda g, t: (g, rev(t), 0, 0, 0)),
    ] + [tile_spec] * 4
    out_shape = [
        jax.ShapeDtypeStruct((batch, length, 2 * KV_WIDTH), BF16),
        jax.ShapeDtypeStruct((batch, KV_WIDTH, length), BF16),
        jax.ShapeDtypeStruct((ng, nt, QROWS, N_UNITS * KTILE), BF16),
        jax.ShapeDtypeStruct((ng, nt, QUADS, BT * SEQS, SSM_WIDTH), F32),
        tile_bf16, tile_bf16,
        tile_bf16, tile_bf16,
    ]
    scratch = [
        pltpu.VMEM((ROWS, D_MODEL), BF16),
        pltpu.VMEM((SSM_BLOCKS, SEQS * PITCH, LANES), F32),
        pltpu.VMEM((QUADS, BT * SEQS, SSM_WIDTH), F32),
        pltpu.VMEM((S5_BUFS_A, QROWS, 2 * UNIT_ST), F32),
        pltpu.VMEM((2, SEQS, ALL_ST), F32),
    ]
    if not rotary:
        cache_spec = pl.BlockSpec((SEQS, TL, KV_WIDTH), lambda g, t: (g, rev(t), 0))
        cache_shape = jax.ShapeDtypeStruct((batch, length, KV_WIDTH), F32)
        out_specs = [cache_spec, cache_spec] + out_specs + [pl.BlockSpec((None, 2, SEQS, ALL_ST), lambda g, t: (g, 0, 0, 0))]
        out_shape = [cache_shape, cache_shape] + out_shape + [jax.ShapeDtypeStruct((ng, 2, SEQS, ALL_ST), F32)]
        scratch += [pltpu.VMEM((2, SEQS, ALL_ST), F32)]
    return pl.pallas_call(
        functools.partial(_pass_a_kernel, rotary=rotary),
        grid=(ng, nt),
        in_specs=in_specs, out_specs=out_specs, out_shape=out_shape, scratch_shapes=scratch,
        compiler_params=pltpu.CompilerParams(
            dimension_semantics=("arbitrary", "arbitrary"), vmem_limit_bytes=VMEM_LIMIT),
        name="pass_a_latent" if rotary else "pass_a_context",
    )(*args)


def _attention(b, q_refs, za_ref, mix_s, kcat_refs, vt_refs, sink_ref, masks):
    rows = _seq_rows(b)
    kcat = jnp.concatenate([r[b] for r in kcat_refs], axis=0)
    vt = jnp.concatenate([r[b] for r in vt_refs], axis=1)
    segment = lax.broadcasted_iota(jnp.int32, (1, 4 * TL), 1) // TL
    kv_heads = range(N_KV_HEADS)

    def side(g, grp):
        return g ^ grp

    def head(g, i, grp):
        return 4 * g + 2 * i + side(g, grp)

    def scores(grp):
        q = jnp.concatenate([q_refs[side(g, grp)][rows, blk * LANES:(blk + 1) * LANES]
                             for g in kv_heads for blk in (2 * g, 2 * g + 1)], axis=0)
        return lax.dot_general(kcat[:, grp * KV_WIDTH:(grp + 1) * KV_WIDTH], q, (((1,), (1,)), ((), ())),
                               preferred_element_type=F32)

    def softmax(grp, s):
        if masks is None:
            pieces = [s]
        else:
            pieces = [jnp.where(masks[0], s[0:TL], NEG_INF), s[TL:2 * TL],
                      jnp.where(masks[1], s[2 * TL:3 * TL], NEG_INF), s[3 * TL:]]
        sinks = [sink_ref[head(g, i, grp)] * LOG2E for g in kv_heads for i in range(2)]
        sink = jnp.where(segment == 0, sinks[0], jnp.where(segment == 1, sinks[1],
                                                           jnp.where(segment == 2, sinks[2], sinks[3])))
        m = sink
        for piece in pieces:
            m = jnp.maximum(m, jnp.max(piece, axis=0, keepdims=True))
        ps = [jnp.exp2(piece - m).astype(BF16) for piece in pieces]
        return (ps[0] if len(ps) == 1 else jnp.concatenate(ps, axis=0)), jnp.exp2(sink - m)

    ones_rows = jnp.ones((BF16_SUBLANES, vt.shape[1]), BF16)

    all_scores = [scores(grp) for grp in range(2)]
    probs, sink_terms = zip(*[softmax(grp, s) for grp, s in enumerate(all_scores)])
    for g in kv_heads:
        cols = slice(g * 2 * TL, (g + 1) * 2 * TL)
        values = jnp.concatenate([vt[g * HEAD_DIM:(g + 1) * HEAD_DIM, :], ones_rows], axis=0)
        p = jnp.concatenate([probs[0][:, cols], probs[1][:, cols]], axis=1)
        sink_term = jnp.concatenate([sink_terms[0][:, cols], sink_terms[1][:, cols]], axis=1)
        ot = jnp.dot(values, p, preferred_element_type=F32)
        out_t = ot[0:HEAD_DIM] / (ot[HEAD_DIM:HEAD_DIM + 1] + sink_term)
        for i in range(2):
            left = out_t[:, (2 * g + i) * TL:(2 * g + i + 1) * TL]
            right = out_t[:, (2 * (g ^ 1) + i) * TL:(2 * (g ^ 1) + i + 1) * TL]
            blk_cols = slice((2 * g + i) * LANES, (2 * g + i + 1) * LANES)
            pair_t = jnp.concatenate([left, right], axis=0)
            gated = pair_t.T * za_ref[rows, blk_cols].astype(F32)
            mix_s[rows, ATTN_WIDTH + blk_cols.start:ATTN_WIDTH + blk_cols.stop] = gated.astype(BF16)


def _pass_b_kernel(*refs, windowed, n_tiles):
    if windowed:
        (sink_ref, x_ref, mod_ref, win_ref, wout_ref, lsq_ref, wg_ref, bg_ref,
         lhs_ref, yb_ref, zs_ref, za_ref, ql_ref, qr_ref, kx_ref, vx_ref, wo_ref, fw_ref, h0re_ref, h0im_ref,
         kp_ref, kc_ref, kn_ref, vp_ref, vc_ref, vn_ref,
         y_ref, y_sm, y_tm, s_s, carry, mix_s) = refs
        fin_ref = first_s = None
    else:
        (sink_ref, x_ref, mod_ref, win_ref, wout_ref, lsq_ref, wg_ref, bg_ref,
         lhs_ref, yb_ref, zs_ref, za_ref, ql_ref, qr_ref, kx_ref, vx_ref, wo_ref, fw_ref,
         y_ref, fin_ref, y_sm, y_tm, s_s, carry, mix_s, first_s) = refs
    t = pl.program_id(1)

    @pl.when(t == 0)
    def _():
        if windowed:
            carry[0] = h0re_ref[...]
            carry[1] = h0im_ref[...]
        else:
            carry[...] = jnp.zeros_like(carry)

    if windowed:
        kj = lax.broadcasted_iota(jnp.int32, (TL, 4 * TL), 0)
        qi = lax.broadcasted_iota(jnp.int32, (TL, 4 * TL), 1) & (TL - 1)
        masks = ((kj >= qi) & (t > 0), (kj <= qi) & (t < n_tiles - 1))
        key_refs = (kp_ref, kc_ref, kn_ref, kx_ref)
        val_refs = (vp_ref, vc_ref, vn_ref, vx_ref)
    else:
        masks = None
        key_refs = (kx_ref,)
        val_refs = (vx_ref,)

    seqs_per_unit = SEQS // N_UNITS

    def between(i):
        for b in range(i * seqs_per_unit, (i + 1) * seqs_per_unit):
            _attention(b, (ql_ref, qr_ref), za_ref, mix_s, key_refs, val_refs, sink_ref, masks)

    def emit(blk, slots):
        lanes = slice(blk * LANES, (blk + 1) * LANES)
        for slot, ys in enumerate(slots):
            rows = _slot_rows(slot)
            y_tm[:, rows, lanes] = ys + yb_ref[:, rows, lanes]

    _s5_direction(lhs_ref, s_s, carry, win_ref, wout_ref, lsq_ref, fin_ref, first_s, emit, FWD, between=between)

    glu_blocks = QUADS // 4
    for rc in range(QUADS // glu_blocks):
        pr = slice(rc * glu_blocks, (rc + 1) * glu_blocks)
        y = jax.nn.gelu(y_tm[pr].reshape(glu_blocks * BT * SEQS, SSM_WIDTH))
        gl = jnp.dot(y.astype(BF16), wg_ref[...], preferred_element_type=F32) + bg_ref[...]
        y_tm[pr] = (y * jax.nn.sigmoid(gl)).reshape(glu_blocks, BT * SEQS, SSM_WIDTH)
    _to_sequence_major(y_sm, y_tm)
    for b in range(SEQS):
        for cb in range(SSM_BLOCKS):
            cols = slice(cb * LANES, (cb + 1) * LANES)
            gated = y_sm[cb, b * PITCH:b * PITCH + TL, :] * zs_ref[_seq_rows(b), cols].astype(F32)
            mix_s[_seq_rows(b), cols] = gated.astype(BF16)

    half_seqs = SEQS // 2
    for hf in range(2):
        rows = slice(hf * half_seqs * TL, (hf + 1) * half_seqs * TL)
        proj = jnp.dot(mix_s[rows, :], wo_ref[...], preferred_element_type=F32)
        y_ref[hf * half_seqs:(hf + 1) * half_seqs] = proj.reshape(half_seqs, TL, D_MODEL)
    for b in range(SEQS):
        m = _mod_row(b, per_sequence=windowed)
        gate = mod_ref[m:m + 1, 2 * D_MODEL:3 * D_MODEL]
        r = x_ref[b] + gate * y_ref[b]
        ms = jnp.mean(r * r, axis=-1, keepdims=True)
        y_ref[b] = (r * lax.rsqrt(ms + EPS)) * fw_ref[...]


def _pass_b(x, mod, win, wout, lsq, w_glu, b_glu, from_a, kx, vx, w_out, fnorm_w, sink, h0, k_loc, v_loc):
    batch, length, _ = x.shape
    ng, nt = batch // SEQS, length // TL
    windowed = k_loc is not None
    tile_spec = pl.BlockSpec((None, None, ROWS, SSM_WIDTH), lambda g, t: (g, t, 0, 0))
    in_specs = [
        pl.BlockSpec(memory_space=pltpu.SMEM),
        pl.BlockSpec((SEQS, TL, D_MODEL), lambda g, t: (g, t, 0)),
        _const_spec(mod.shape),
    ] + _s5_param_specs(FWD) + [
        _const_spec((SSM_WIDTH, SSM_WIDTH)),
        _const_spec((1, SSM_WIDTH)),
        pl.BlockSpec((None, None, QROWS, N_UNITS * KTILE), lambda g, t: (g, t, 0, 0)),
        pl.BlockSpec((None, None, QUADS, BT * SEQS, SSM_WIDTH), lambda g, t: (g, t, 0, 0, 0)),
    ] + [tile_spec] * 4 + [
        pl.BlockSpec((SEQS, PAST_LEN, 2 * KV_WIDTH), lambda g, t: (g, 0, 0)),
        pl.BlockSpec((SEQS, KV_WIDTH, PAST_LEN), lambda g, t: (g, 0, 0)),
        _const_spec((D_MODEL, D_MODEL)),
        _const_spec((1, D_MODEL)),
    ]
    args = [sink, x, mod, win, wout, lsq, w_glu, b_glu, *from_a, kx, vx, w_out, fnorm_w]
    if windowed:
        in_specs += _h0_specs(FWD)
        args += list(h0)
        band_t = (lambda t: jnp.maximum(t - 1, 0), lambda t: t, lambda t: jnp.minimum(t + 1, nt - 1))
        in_specs += [pl.BlockSpec((SEQS, TL, 2 * KV_WIDTH), lambda g, t, f=f: (g, f(t), 0)) for f in band_t]
        in_specs += [pl.BlockSpec((SEQS, KV_WIDTH, TL), lambda g, t, f=f: (g, 0, f(t))) for f in band_t]
        args += [k_loc] * 3 + [v_loc] * 3
    out_specs = [pl.BlockSpec((SEQS, TL, D_MODEL), lambda g, t: (g, t, 0))]
    out_shape = [jax.ShapeDtypeStruct((batch, length, D_MODEL), F32)]
    scratch = [
        pltpu.VMEM((SSM_BLOCKS, SEQS * PITCH, LANES), F32),
        pltpu.VMEM((QUADS, BT * SEQS, SSM_WIDTH), F32),
        pltpu.VMEM((S5_BUFS_B, QROWS, 2 * UNIT_ST), F32),
        pltpu.VMEM((2, SEQS, ALL_ST), F32),
        pltpu.VMEM((ROWS, D_MODEL), BF16),
    ]
    if not windowed:
        out_specs += [pl.BlockSpec((None, 2, SEQS, ALL_ST), lambda g, t: (g, 0, 0, 0))]
        out_shape += [jax.ShapeDtypeStruct((ng, 2, SEQS, ALL_ST), F32)]
        scratch += [pltpu.VMEM((2, SEQS, ALL_ST), F32)]
    return pl.pallas_call(
        functools.partial(_pass_b_kernel, windowed=windowed, n_tiles=nt),
        grid=(ng, nt),
        in_specs=in_specs, out_specs=out_specs, out_shape=out_shape, scratch_shapes=scratch,
        compiler_params=pltpu.CompilerParams(
            dimension_semantics=("arbitrary", "arbitrary"), vmem_limit_bytes=VMEM_LIMIT),
        name="pass_b_latent" if windowed else "pass_b_context",
    )(*args)


def kernel(x_prompt, x_sample, c, cache_k, cache_v, state_ssm_re, state_ssm_im, c_ctx, norm_w, w_mod, b_mod, w_in, ssm_lambda_re, ssm_lambda_im, ssm_log_dt, ssm_b_re, ssm_b_im, ssm_c_re, ssm_c_im, ssm_d, w_glu, b_glu, attn_sink, w_out, final_norm_w):
    assert norm_w.shape[0] == 1, "single trunk layer"
    batch, seq, _ = x_prompt.shape
    dec_batch, dec_seq, _ = x_sample.shape
    assert dec_batch == SEQS and batch % SEQS == 0 and seq % TL == 0 and dec_seq % TL == 0
    assert seq == PAST_LEN and cache_k.shape[2] == PAST_LEN

    w_in0 = w_in[0].astype(BF16)
    nw = norm_w[0][None, :]
    fw = final_norm_w[None, :]
    d = ssm_d[0][None, :]
    wg = w_glu[0].astype(BF16)
    bg = b_glu[0][None, :]
    wo = w_out[0].astype(BF16)
    sink = attn_sink[0]

    cond = jnp.concatenate([c, c_ctx[None, :], jnp.zeros((16 - SEQS - 1, D_MODEL), F32)], axis=0)
    mod = _modulation(cond, w_mod[0], b_mod[0][None, :])

    win, wout, wk, lsq = _s5_weights(ssm_lambda_re[0], ssm_lambda_im[0], ssm_log_dt[0], ssm_b_re[0], ssm_b_im[0],
                                     ssm_c_re[0], ssm_c_im[0])
    h0 = (state_ssm_re.reshape(SEQS, 2 * ALL_ST), state_ssm_im.reshape(SEQS, 2 * ALL_ST))
    rope = _rope_tables(dec_seq)

    k_ctx, v_ctx, kcat_ctx, vt_ctx, *tiles_ctx, fin_b = _pass_a(
        x_prompt, mod, nw, w_in0, win, wout, lsq, wk, d, None, None)
    y_prompt, fin_f = _pass_b(x_prompt, mod, win, wout, lsq, wg, bg, tiles_ctx, kcat_ctx, vt_ctx, wo, fw, sink,
                              None, None, None)

    kcat_lat, vt_lat, *tiles_lat = _pass_a(x_sample, mod, nw, w_in0, win, wout, lsq, wk, d, h0, rope)
    kx = cache_k[:, 0].reshape(dec_batch, PAST_LEN, KV_WIDTH)
    kx = jnp.concatenate([kx, jnp.roll(kx, HEAD_DIM, axis=-1)], axis=-1).astype(BF16)
    vx = jnp.swapaxes(cache_v[:, 0].reshape(dec_batch, PAST_LEN, KV_WIDTH), 1, 2).astype(BF16)
    (y_sample,) = _pass_b(x_sample, mod, win, wout, lsq, wg, bg, tiles_lat, kx, vx, wo, fw, sink,
                          h0, kcat_lat, vt_lat)

    new_cache_k = k_ctx.reshape(batch, 1, seq, N_KV_HEADS, HEAD_DIM)
    new_cache_v = v_ctx.reshape(batch, 1, seq, N_KV_HEADS, HEAD_DIM)

    def states(fin, part):
        return fin[:, part].reshape(batch, SSM_GROUPS, SSM_STATE)

    new_re = jnp.stack([states(fin_f, 0), states(fin_b, 0)], axis=1)[:, None]
    new_im = jnp.stack([states(fin_f, 1), states(fin_b, 1)], axis=1)[:, None]
    return (y_prompt, y_sample, new_cache_k, new_cache_v, new_re, new_im)
```

```python
import functools
import math

import jax
import jax.numpy as jnp
from jax import lax
from jax.experimental import pallas as pl
from jax.experimental.pallas import tpu as pltpu

F32 = jnp.float32
BF16 = jnp.bfloat16

D_MODEL = 1024
SSM_WIDTH = 512
ATTN_WIDTH = 512
SSM_GROUP = 16
SSM_GROUPS = 32
SSM_STATE = 64
HEAD_DIM = 64
N_KV_HEADS = 2
KV_WIDTH = 128
GRID_W = 64
ROPE_AXIS_DIM = 32
ROPE_BASE = 10000.0
EPS = 1e-6
LAMBDA_RE_MAX = -1e-4
NEG_INF = -1e30
LOG2E = math.log2(math.e)
PAST_LEN = 256

LANES = 128
BF16_SUBLANES = 16
SEQS = 8
TL = 128
ROWS = SEQS * TL
PITCH = TL + 8
SSM_BLOCKS = SSM_WIDTH // LANES
ALL_ST = SSM_GROUPS * SSM_STATE
BT = 4
KTILE = 2 * LANES
PIECE = KTILE // BT
UNIT_GROUPS = PIECE // SSM_GROUP
UNIT_ST = UNIT_GROUPS * SSM_STATE
N_UNITS = SSM_WIDTH // PIECE
QUADS = TL // BT
QROWS = QUADS * SEQS
S5_BUFS_A = 8
S5_BUFS_B = 2
V7X_VMEM_BYTES = 64 * 1024 * 1024
VMEM_LIMIT = V7X_VMEM_BYTES - 6 * 1024 * 1024

BWD, FWD = 1, 0
W_IN_U = 0
W_IN_ZS = SSM_WIDTH
W_IN_Q = 2 * SSM_WIDTH
W_IN_K = 2 * SSM_WIDTH + ATTN_WIDTH
W_IN_ZA = W_IN_K + 2 * KV_WIDTH
D_IN = W_IN_ZA + ATTN_WIDTH


def _const_spec(shape, index=None):
    index = (0,) * len(shape) if index is None else index
    return pl.BlockSpec(shape, lambda g, t: index, pipeline_mode=pl.Buffered(1))


def _modulation_kernel(cond_ref, w_ref, b_ref, out_ref):
    a = jax.nn.silu(cond_ref[...]).astype(BF16)
    out_ref[...] = jnp.dot(a, w_ref[...].astype(BF16), preferred_element_type=F32) + b_ref[...]


def _modulation(cond, w_mod, b_mod):
    rows = cond.shape[0]
    n = w_mod.shape[1]
    tn = 1024
    return pl.pallas_call(
        _modulation_kernel,
        grid=(n // tn,),
        in_specs=[pl.BlockSpec((rows, D_MODEL), lambda j: (0, 0)),
                  pl.BlockSpec((D_MODEL, tn), lambda j: (0, j)),
                  pl.BlockSpec((1, tn), lambda j: (0, j))],
        out_specs=pl.BlockSpec((rows, tn), lambda j: (0, j)),
        out_shape=jax.ShapeDtypeStruct((rows, n), F32),
        name="modulation",
    )(cond, w_mod, b_mod)


def _zoh(lam_re, lam_im, log_dt):
    lam_re = jnp.minimum(lam_re, LAMBDA_RE_MAX)
    dt = jnp.exp(log_dt)
    mag = jnp.exp(lam_re * dt)
    ang = lam_im * dt
    lbar_re = mag * jnp.cos(ang)
    lbar_im = mag * jnp.sin(ang)
    nr = lbar_re - 1.0
    ni = lbar_im
    den = lam_re * lam_re + lam_im * lam_im
    f_re = (nr * lam_re + ni * lam_im) / den
    f_im = (ni * lam_re - nr * lam_im) / den
    return lbar_re, lbar_im, f_re, f_im


def _cmul(a, b):
    return a[0] * b[0] - a[1] * b[1], a[0] * b[1] + a[1] * b[0]


def _s5_weights_kernel(lam_ref, bt_ref, cre_ref, cim_ref, lamflat_ref, win_ref, wout_ref, wk_ref, lpow_ref):
    p_row = lax.broadcasted_iota(jnp.int32, (SSM_STATE, UNIT_ST), 0)
    p_col = lax.broadcasted_iota(jnp.int32, (SSM_STATE, UNIT_ST), 1) & (SSM_STATE - 1)
    spread = jnp.where(p_row == p_col, 1.0, 0.0).astype(BF16)
    g_row = lax.broadcasted_iota(jnp.int32, (PIECE, UNIT_ST), 0) // SSM_GROUP
    g_col = lax.broadcasted_iota(jnp.int32, (PIECE, UNIT_ST), 1) // SSM_STATE
    own_block = g_row == g_col

    def blocks(x):
        return jnp.where(own_block, jnp.dot(x.astype(BF16), spread, preferred_element_type=F32), 0.0)

    def blocks2(z):
        return blocks(z[0]).astype(BF16), blocks(z[1]).astype(BF16)

    def in_block(src, c):
        dims = (((1,), (1,)), ((), ()))
        return (lax.dot_general(src[0], c[0], dims, preferred_element_type=F32)
                - lax.dot_general(src[1], c[1], dims, preferred_element_type=F32))

    def powers(z, lbar, n):
        out = [z]
        for _ in range(n):
            out.append(_cmul(out[-1], lbar))
        return out

    k_lag = {}
    for d in (FWD, BWD):
        lbar_re, lbar_im, f_re, f_im = _zoh(lam_ref[0, d], lam_ref[1, d], lam_ref[2, d])
        lbar = (lbar_re, lbar_im)
        bb_pow = powers(_cmul((f_re, f_im), (bt_ref[0, d], bt_ref[1, d])), lbar, BT - 1)
        c_pow = powers((cre_ref[d], cim_ref[d]), lbar, BT)
        c_b = blocks2(c_pow[0])
        to_end = [(BT - 1 - s) if d == FWD else s for s in range(BT)]
        for s in range(BT):
            w = blocks2(bb_pow[to_end[s]])
            win_ref[d, s * PIECE:(s + 1) * PIECE, 0:UNIT_ST] = w[0]
            win_ref[d, s * PIECE:(s + 1) * PIECE, UNIT_ST:2 * UNIT_ST] = w[1]
        out_re = jnp.concatenate([blocks(c_pow[BT - to_end[s]][0]) for s in range(BT)], axis=0)
        out_im = jnp.concatenate([blocks(-c_pow[BT - to_end[s]][1]) for s in range(BT)], axis=0)
        wout_ref[d, 0:UNIT_ST, :] = out_re.T.astype(BF16)
        wout_ref[d, UNIT_ST:2 * UNIT_ST, :] = out_im.T.astype(BF16)
        k_lag[d] = [in_block(blocks2(bb_pow[m]), c_b) for m in range(BT)]
    for src in range(BT):
        for out in range(BT):
            if out == src:
                k = k_lag[FWD][0] + k_lag[BWD][0]
            elif out > src:
                k = k_lag[FWD][out - src]
            else:
                k = k_lag[BWD][src - out]
            wk_ref[src * PIECE:(src + 1) * PIECE, out * PIECE:(out + 1) * PIECE] = k.astype(BF16)

    @pl.when(pl.program_id(0) == 0)
    def _():
        lbar_re, lbar_im, _, _ = _zoh(lamflat_ref[0], lamflat_ref[1], lamflat_ref[2])
        p = (lbar_re, lbar_im)
        for _ in range(BT.bit_length() - 1):
            p = _cmul(p, p)
        lpow_ref[0] = p[0]
        lpow_ref[1] = p[1]


def _s5_weights(lam_re, lam_im, log_dt, b_re, b_im, c_re, c_im):
    assert BT & (BT - 1) == 0
    n = SSM_GROUPS * SSM_GROUP
    lam3 = jnp.stack([lam_re, lam_im, jnp.broadcast_to(log_dt[:, :, None], lam_re.shape)])
    lam_rows = jnp.broadcast_to(lam3[:, :, :, None, :], (3, 2, SSM_GROUPS, SSM_GROUP, SSM_STATE)).reshape(3, 2, n, SSM_STATE)
    lam_flat = jnp.broadcast_to(lam3.reshape(3, 1, 2 * ALL_ST), (3, SEQS, 2 * ALL_ST))
    bt = jnp.transpose(jnp.stack([b_re, b_im]), (0, 1, 2, 4, 3)).reshape(2, 2, n, SSM_STATE)
    return pl.pallas_call(
        _s5_weights_kernel,
        grid=(N_UNITS,),
        in_specs=[pl.BlockSpec((3, 2, PIECE, SSM_STATE), lambda m: (0, 0, m, 0)),
                  pl.BlockSpec((2, 2, PIECE, SSM_STATE), lambda m: (0, 0, m, 0)),
                  pl.BlockSpec((2, PIECE, SSM_STATE), lambda m: (0, m, 0)),
                  pl.BlockSpec((2, PIECE, SSM_STATE), lambda m: (0, m, 0)),
                  pl.BlockSpec((3, SEQS, 2 * ALL_ST), lambda m: (0, 0, 0))],
        out_specs=[pl.BlockSpec((2, None, KTILE, 2 * UNIT_ST), lambda m: (0, m, 0, 0)),
                   pl.BlockSpec((2, None, 2 * UNIT_ST, KTILE), lambda m: (0, m, 0, 0)),
                   pl.BlockSpec((None, KTILE, KTILE), lambda m: (m, 0, 0)),
                   pl.BlockSpec((2, SEQS, 2 * ALL_ST), lambda m: (0, 0, 0))],
        out_shape=[jax.ShapeDtypeStruct((2, N_UNITS, KTILE, 2 * UNIT_ST), BF16),
                   jax.ShapeDtypeStruct((2, N_UNITS, 2 * UNIT_ST, KTILE), BF16),
                   jax.ShapeDtypeStruct((N_UNITS, KTILE, KTILE), BF16),
                   jax.ShapeDtypeStruct((2, SEQS, 2 * ALL_ST), F32)],
        name="s5_weights",
    )(lam_rows, bt, c_re.reshape(2, n, SSM_STATE), c_im.reshape(2, n, SSM_STATE), lam_flat)


def _rope_tables(length):
    pos = jnp.arange(length)
    row = (pos // GRID_W).astype(F32)
    col = (pos % GRID_W).astype(F32)
    freqs = ROPE_BASE ** (-jnp.arange(0, ROPE_AXIS_DIM, 2, dtype=F32) / ROPE_AXIS_DIM)
    ang_r = row[:, None] * freqs[None, :]
    ang_c = col[:, None] * freqs[None, :]
    cos = jnp.concatenate([jnp.cos(ang_r), jnp.cos(ang_r), jnp.cos(ang_c), jnp.cos(ang_c)], axis=-1)
    sin = jnp.concatenate([-jnp.sin(ang_r), jnp.sin(ang_r), -jnp.sin(ang_c), jnp.sin(ang_c)], axis=-1)
    return jnp.tile(cos, (1, 2)), jnp.tile(sin, (1, 2))


def _seq_rows(b):
    return slice(b * TL, (b + 1) * TL)


def _mod_row(b, per_sequence):
    return b if per_sequence else SEQS


def _norm_mod(h_ref, x_ref, mod_ref, nw_ref, b, per_sequence):
    xb = x_ref[b]
    ms = jnp.mean(xb * xb, axis=-1, keepdims=True)
    m = _mod_row(b, per_sequence)
    shift = mod_ref[m:m + 1, 0:D_MODEL]
    gain = nw_ref[...] * (1.0 + mod_ref[m:m + 1, D_MODEL:2 * D_MODEL])
    h_ref[_seq_rows(b), :] = ((xb * lax.rsqrt(ms + EPS)) * gain + shift).astype(BF16)


def _rope(x, cos, sin):
    lane = lax.broadcasted_iota(jnp.int32, x.shape, 1)
    first = (lane & 31) < 16
    partner = jnp.where(first, pltpu.roll(x, LANES - 16, 1), pltpu.roll(x, 16, 1))
    x3 = x.reshape(SEQS, TL, LANES)
    p3 = partner.reshape(SEQS, TL, LANES)
    return (x3 * cos[None] + p3 * sin[None]).reshape(ROWS, LANES)


def _store_column_blocks(dst_ref, val, first_seq):
    for j in range(val.shape[0] // TL):
        b = first_seq + j
        for cb in range(SSM_BLOCKS):
            dst_ref[cb, b * PITCH:b * PITCH + TL, :] = val[_seq_rows(j), cb * LANES:(cb + 1) * LANES]


def _token_rows(l):
    return l // BT, slice((l % BT) * SEQS, (l % BT + 1) * SEQS)


def _slot_rows(slot):
    return slice(slot * SEQS, (slot + 1) * SEQS)


def _to_token_major(dst_ref, src_ref):
    for l in range(TL):
        j, rows = _token_rows(l)
        for cb in range(SSM_BLOCKS):
            dst_ref[j, rows, cb * LANES:(cb + 1) * LANES] = src_ref[cb, pl.ds(l, SEQS, stride=PITCH), :]


def _to_sequence_major(dst_ref, src_ref):
    for l in range(TL):
        j, rows = _token_rows(l)
        for cb in range(SSM_BLOCKS):
            dst_ref[cb, pl.ds(l, SEQS, stride=PITCH), :] = src_ref[j, rows, cb * LANES:(cb + 1) * LANES]


def _low_half():
    return lax.broadcasted_iota(jnp.int32, (QROWS, LANES), 1) < PIECE


def _split_halves(a, b):
    low = _low_half()
    return (jnp.where(low, a, pltpu.roll(b, PIECE, 1)), jnp.where(low, pltpu.roll(a, PIECE, 1), b))


def _merge_halves(left, right):
    low = _low_half()
    return (jnp.where(low, left, pltpu.roll(right, PIECE, 1)), jnp.where(low, pltpu.roll(left, PIECE, 1), right))


def _pack_scan_operand(lhs_ref, u_tm):
    assert PIECE * 2 == LANES and BT % 2 == 0
    for blk in range(SSM_BLOCKS):
        lanes = slice(blk * LANES, (blk + 1) * LANES)
        for sp in range(BT // 2):
            a = u_tm[:, _slot_rows(2 * sp), lanes].reshape(QROWS, LANES)
            b = u_tm[:, _slot_rows(2 * sp + 1), lanes].reshape(QROWS, LANES)
            for side, val in enumerate(_split_halves(a, b)):
                c0 = (2 * blk + side) * KTILE + sp * LANES
                lhs_ref[:, c0:c0 + LANES] = val.astype(BF16)


def _unpack_outputs(y_low, y_high):
    out = []
    for sp in range(BT // 2):
        cols = slice(sp * LANES, (sp + 1) * LANES)
        for val in _merge_halves(y_low[:, cols], y_high[:, cols]):
            out.append(val.reshape(QUADS, SEQS, LANES))
    return out


def _scan_unit(s_ref, lpow_ref, carry_ref, direction, unit, reverse):
    cols = slice(unit * UNIT_ST, (unit + 1) * UNIT_ST)
    lanes = slice(direction * ALL_ST + unit * UNIT_ST, direction * ALL_ST + (unit + 1) * UNIT_ST)
    lr = lpow_ref[0, :, lanes]
    li = lpow_ref[1, :, lanes]
    sr = carry_ref[0, :, cols]
    si = carry_ref[1, :, cols]
    for i in range(QUADS):
        j = (QUADS - 1 - i) if reverse else i
        rows = slice(j * SEQS, (j + 1) * SEQS)
        inc_r = s_ref[rows, 0:UNIT_ST]
        inc_i = s_ref[rows, UNIT_ST:2 * UNIT_ST]
        s_ref[rows, 0:UNIT_ST] = sr
        s_ref[rows, UNIT_ST:2 * UNIT_ST] = si
        sr, si = lr * sr - li * si + inc_r, lr * si + li * sr + inc_i
    carry_ref[0, :, cols] = sr
    carry_ref[1, :, cols] = si


def _s5_direction(lhs_ref, s_ref, carry_ref, win_ref, wout_ref, lpow_ref, first_ref, first_s, emit, direction,
                  wk_ref=None, between=None):
    reverse = direction == BWD

    def lhs(unit):
        return lhs_ref[:, unit * KTILE:(unit + 1) * KTILE]

    bufs = s_ref.shape[0]
    ahead = min(bufs, N_UNITS) if bufs >= N_UNITS else bufs - 1

    def project_in(unit):
        s_ref[unit % bufs] = jnp.dot(lhs(unit), win_ref[unit], preferred_element_type=F32)

    for unit in range(min(ahead, N_UNITS)):
        project_in(unit)
    y_low = None
    for unit in range(N_UNITS):
        if ahead <= unit + ahead < N_UNITS:
            project_in(unit + ahead)
        buf = s_ref.at[unit % bufs]
        _scan_unit(buf, lpow_ref, carry_ref, direction, unit, reverse)
        y = jnp.dot(buf[...].astype(BF16), wout_ref[unit], preferred_element_type=F32)
        if wk_ref is not None:
            y = y + jnp.dot(lhs(unit), wk_ref[unit], preferred_element_type=F32)
        if unit % 2 == 0:
            y_low = y
        else:
            emit(unit // 2, _unpack_outputs(y_low, y))
        if between is not None:
            between(unit)

    if first_ref is not None:
        first_rows = slice(QROWS - 2 * SEQS, QROWS) if reverse else slice(0, 2 * SEQS)
        low = lax.broadcasted_iota(jnp.int32, (2 * SEQS, LANES), 1) < PIECE
        zeros = jnp.zeros((2 * SEQS, LANES), F32)
        for unit in range(N_UNITS):
            if reverse:
                tail = lhs_ref[first_rows, (unit + 1) * KTILE - LANES:(unit + 1) * KTILE].astype(F32)
                block = [jnp.where(low, pltpu.roll(tail, PIECE, 1), 0.0), zeros]
            else:
                head = lhs_ref[first_rows, unit * KTILE:unit * KTILE + LANES].astype(F32)
                block = [zeros, jnp.where(low, 0.0, pltpu.roll(head, PIECE, 1))]
            inc = jnp.dot(jnp.concatenate(block, axis=1).astype(BF16), win_ref[unit], preferred_element_type=F32)
            inc = inc[SEQS:2 * SEQS] if reverse else inc[0:SEQS]
            first_s[0, :, unit * UNIT_ST:(unit + 1) * UNIT_ST] = inc[:, 0:UNIT_ST]
            first_s[1, :, unit * UNIT_ST:(unit + 1) * UNIT_ST] = inc[:, UNIT_ST:2 * UNIT_ST]

        @pl.when(pl.program_id(1) == 0)
        def _():
            first_ref[...] = first_s[...]


def _s5_param_specs(direction):
    return [_const_spec((None, N_UNITS, KTILE, 2 * UNIT_ST), (direction, 0, 0, 0)),
            _const_spec((None, N_UNITS, 2 * UNIT_ST, KTILE), (direction, 0, 0, 0)),
            _const_spec((2, SEQS, 2 * ALL_ST))]


def _h0_specs(direction):
    return [_const_spec((SEQS, ALL_ST), (0, direction))] * 2


def _pass_a_kernel(*refs, rotary):
    if rotary:
        (x_ref, mod_ref, nw_ref, w_ref, win_ref, wout_ref, lpow_ref, wk_ref, d_ref, h0re_ref, h0im_ref, cos_ref, sin_ref,
         kcat_ref, vt_ref, lhs_ref, yb_ref, zs_ref, za_ref, ql_ref, qr_ref, h_s, u_sm, u_tm, s_s, carry) = refs
        fin_ref = first_s = None
    else:
        (x_ref, mod_ref, nw_ref, w_ref, win_ref, wout_ref, lpow_ref, wk_ref, d_ref,
         k_ref, v_ref, kcat_ref, vt_ref, lhs_ref, yb_ref, zs_ref, za_ref, ql_ref, qr_ref, fin_ref,
         h_s, u_sm, u_tm, s_s, carry, first_s) = refs

    @pl.when(pl.program_id(1) == 0)
    def _():
        if rotary:
            carry[0] = h0re_ref[...]
            carry[1] = h0im_ref[...]
        else:
            carry[...] = jnp.zeros_like(carry)

    half_seqs = SEQS // 2
    for hf in range(2):
        for b in range(hf * half_seqs, (hf + 1) * half_seqs):
            _norm_mod(h_s, x_ref, mod_ref, nw_ref, b, per_sequence=rotary)
        rows = slice(hf * half_seqs * TL, (hf + 1) * half_seqs * TL)
        u_half = jnp.dot(h_s[rows, :], w_ref[:, W_IN_U:W_IN_U + SSM_WIDTH], preferred_element_type=F32)
        _store_column_blocks(u_sm, u_half, first_seq=hf * half_seqs)
    _to_token_major(u_tm, u_sm)
    _pack_scan_operand(lhs_ref, u_tm)

    def project(c0, width):
        return jnp.dot(h_s[...], w_ref[:, c0:c0 + width], preferred_element_type=F32)

    def project_kv():
        kv = project(W_IN_K, 2 * KV_WIDTH)
        k = kv[:, 0:KV_WIDTH]
        v = kv[:, KV_WIDTH:2 * KV_WIDTH]
        if rotary:
            k = _rope(k, cos_ref[...], sin_ref[...])
        else:
            k_ref[...] = k.reshape(SEQS, TL, KV_WIDTH)
            v_ref[...] = v.reshape(SEQS, TL, KV_WIDTH)
        kcat = jnp.concatenate([k, pltpu.roll(k, HEAD_DIM, 1)], axis=1).astype(BF16)
        kcat_ref[...] = kcat.reshape(SEQS, TL, 2 * KV_WIDTH)
        for b in range(SEQS):
            vt_ref[b] = v[_seq_rows(b), :].T.astype(BF16)

    def project_q():
        qall = project(W_IN_Q, ATTN_WIDTH)
        even_head = lax.broadcasted_iota(jnp.int32, (ROWS, LANES), 1) < HEAD_DIM
        for cg in range(ATTN_WIDTH // LANES):
            cols = slice(cg * LANES, (cg + 1) * LANES)
            q = qall[:, cols]
            if rotary:
                q = _rope(q, cos_ref[...], sin_ref[...])
            q = q * (LOG2E * HEAD_DIM ** -0.5)
            ql_ref[:, cols] = jnp.where(even_head, q, 0.0).astype(BF16)
            qr_ref[:, cols] = jnp.where(even_head, 0.0, q).astype(BF16)

    def project_gate(c0, out_ref):
        out_ref[...] = jax.nn.silu(project(c0, out_ref.shape[1])).astype(BF16)

    side_work = [project_kv, project_q, lambda: project_gate(W_IN_ZS, zs_ref), lambda: project_gate(W_IN_ZA, za_ref)]

    def between(i):
        if i < len(side_work):
            side_work[i]()

    def emit(blk, slots):
        lanes = slice(blk * LANES, (blk + 1) * LANES)
        for slot, ys in enumerate(slots):
            rows = _slot_rows(slot)
            yb_ref[:, rows, lanes] = ys + u_tm[:, rows, lanes] * d_ref[:, lanes]

    _s5_direction(lhs_ref, s_s, carry, win_ref, wout_ref, lpow_ref, fin_ref, first_s, emit, BWD,
                  wk_ref=wk_ref, between=between)


def _pass_a(x, mod, norm_w, w_in, win, wout, lpow, wk, d, h0, rope):
    batch, length, _ = x.shape
    ng, nt = batch // SEQS, length // TL
    rotary = rope is not None
    rev = lambda t: nt - 1 - t
    in_specs = [
        pl.BlockSpec((SEQS, TL, D_MODEL), lambda g, t: (g, rev(t), 0)),
        _const_spec(mod.shape),
        _const_spec((1, D_MODEL)),
        _const_spec((D_MODEL, D_IN)),
    ] + _s5_param_specs(BWD) + [
        _const_spec((N_UNITS, KTILE, KTILE)),
        _const_spec((1, SSM_WIDTH)),
    ]
    args = [x, mod, norm_w, w_in, win, wout, lpow, wk, d]
    if rotary:
        in_specs += _h0_specs(BWD) + [pl.BlockSpec((TL, LANES), lambda g, t: (rev(t), 0))] * 2
        args += list(h0) + list(rope)
    assert ATTN_WIDTH == SSM_WIDTH
    tile_spec = pl.BlockSpec((None, None, ROWS, SSM_WIDTH), lambda g, t: (g, rev(t), 0, 0))
    tile_bf16 = jax.ShapeDtypeStruct((ng, nt, ROWS, SSM_WIDTH), BF16)
    out_specs = [
        pl.BlockSpec((SEQS, TL, 2 * KV_WIDTH), lambda g, t: (g, rev(t), 0)),
        pl.BlockSpec((SEQS, KV_WIDTH, TL), lambda g, t: (g, 0, rev(t))),
        pl.BlockSpec((None, None, QROWS, N_UNITS * KTILE), lambda g, t: (g, rev(t), 0, 0)),
        pl.BlockSpec((None, None, QUADS, BT * SEQS, SSM_WIDTH), lambda g, t: (g, rev(t), 0, 0, 0)),
    ] + [tile_spec] * 4
    out_shape = [
        jax.ShapeDtypeStruct((batch, length, 2 * KV_WIDTH), BF16),
        jax.ShapeDtypeStruct((batch, KV_WIDTH, length), BF16),
        jax.ShapeDtypeStruct((ng, nt, QROWS, N_UNITS * KTILE), BF16),
        jax.ShapeDtypeStruct((ng, nt, QUADS, BT * SEQS, SSM_WIDTH), F32),
        tile_bf16, tile_bf16,
        tile_bf16, tile_bf16,
    ]
    scratch = [
        pltpu.VMEM((ROWS, D_MODEL), BF16),
        pltpu.VMEM((SSM_BLOCKS, SEQS * PITCH, LANES), F32),
        pltpu.VMEM((QUADS, BT * SEQS, SSM_WIDTH), F32),
        pltpu.VMEM((S5_BUFS_A, QROWS, 2 * UNIT_ST), F32),
        pltpu.VMEM((2, SEQS, ALL_ST), F32),
    ]
    if not rotary:
        cache_spec = pl.BlockSpec((SEQS, TL, KV_WIDTH), lambda g, t: (g, rev(t), 0))
        cache_shape = jax.ShapeDtypeStruct((batch, length, KV_WIDTH), F32)
        out_specs = [cache_spec, cache_spec] + out_specs + [pl.BlockSpec((None, 2, SEQS, ALL_ST), lambda g, t: (g, 0, 0, 0))]
        out_shape = [cache_shape, cache_shape] + out_shape + [jax.ShapeDtypeStruct((ng, 2, SEQS, ALL_ST), F32)]
        scratch += [pltpu.VMEM((2, SEQS, ALL_ST), F32)]
    return pl.pallas_call(
        functools.partial(_pass_a_kernel, rotary=rotary),
        grid=(ng, nt),
        in_specs=in_specs, out_specs=out_specs, out_shape=out_shape, scratch_shapes=scratch,
        compiler_params=pltpu.CompilerParams(
            dimension_semantics=("arbitrary", "arbitrary"), vmem_limit_bytes=VMEM_LIMIT),
        name="pass_a_latent" if rotary else "pass_a_context",
    )(*args)


def _attention(b, q_refs, za_ref, mix_s, kcat_refs, vt_refs, sink_ref, masks):
    rows = _seq_rows(b)
    kcat = jnp.concatenate([r[b] for r in kcat_refs], axis=0)
    vt = jnp.concatenate([r[b] for r in vt_refs], axis=1)
    segment = lax.broadcasted_iota(jnp.int32, (1, 4 * TL), 1) // TL
    kv_heads = range(N_KV_HEADS)

    def side(g, grp):
        return g ^ grp

    def head(g, i, grp):
        return 4 * g + 2 * i + side(g, grp)

    def scores(grp):
        q = jnp.concatenate([q_refs[side(g, grp)][rows, blk * LANES:(blk + 1) * LANES]
                             for g in kv_heads for blk in (2 * g, 2 * g + 1)], axis=0)
        return lax.dot_general(kcat[:, grp * KV_WIDTH:(grp + 1) * KV_WIDTH], q, (((1,), (1,)), ((), ())),
                               preferred_element_type=F32)

    def softmax(grp, s):
        if masks is None:
            pieces = [s]
        else:
            pieces = [jnp.where(masks[0], s[0:TL], NEG_INF), s[TL:2 * TL],
                      jnp.where(masks[1], s[2 * TL:3 * TL], NEG_INF), s[3 * TL:]]
        sinks = [sink_ref[head(g, i, grp)] * LOG2E for g in kv_heads for i in range(2)]
        sink = jnp.where(segment == 0, sinks[0], jnp.where(segment == 1, sinks[1],
                                                           jnp.where(segment == 2, sinks[2], sinks[3])))
        m = sink
        for piece in pieces:
            m = jnp.maximum(m, jnp.max(piece, axis=0, keepdims=True))
        ps = [jnp.exp2(piece - m).astype(BF16) for piece in pieces]
        return (ps[0] if len(ps) == 1 else jnp.concatenate(ps, axis=0)), jnp.exp2(sink - m)

    ones_rows = jnp.ones((BF16_SUBLANES, vt.shape[1]), BF16)

    all_scores = [scores(grp) for grp in range(2)]
    probs, sink_terms = zip(*[softmax(grp, s) for grp, s in enumerate(all_scores)])
    for g in kv_heads:
        cols = slice(g * 2 * TL, (g + 1) * 2 * TL)
        values = jnp.concatenate([vt[g * HEAD_DIM:(g + 1) * HEAD_DIM, :], ones_rows], axis=0)
        p = jnp.concatenate([probs[0][:, cols], probs[1][:, cols]], axis=1)
        sink_term = jnp.concatenate([sink_terms[0][:, cols], sink_terms[1][:, cols]], axis=1)
        ot = jnp.dot(values, p, preferred_element_type=F32)
        out_t = ot[0:HEAD_DIM] * (1.0 / (ot[HEAD_DIM:HEAD_DIM + 1] + sink_term))
        for i in range(2):
            left = out_t[:, (2 * g + i) * TL:(2 * g + i + 1) * TL]
            right = out_t[:, (2 * (g ^ 1) + i) * TL:(2 * (g ^ 1) + i + 1) * TL]
            blk_cols = slice((2 * g + i) * LANES, (2 * g + i + 1) * LANES)
            pair_t = jnp.concatenate([left, right], axis=0)
            gated = pair_t.T * za_ref[rows, blk_cols].astype(F32)
            mix_s[rows, ATTN_WIDTH + blk_cols.start:ATTN_WIDTH + blk_cols.stop] = gated.astype(BF16)


def _pass_b_kernel(*refs, windowed, n_tiles):
    if windowed:
        (sink_ref, x_ref, mod_ref, win_ref, wout_ref, lpow_ref, wg_ref, bg_ref,
         lhs_ref, yb_ref, zs_ref, za_ref, ql_ref, qr_ref, kx_ref, vx_ref, wo_ref, fw_ref, h0re_ref, h0im_ref,
         kp_ref, kc_ref, kn_ref, vp_ref, vc_ref, vn_ref,
         y_ref, y_sm, y_tm, s_s, carry, mix_s) = refs
        fin_ref = first_s = None
    else:
        (sink_ref, x_ref, mod_ref, win_ref, wout_ref, lpow_ref, wg_ref, bg_ref,
         lhs_ref, yb_ref, zs_ref, za_ref, ql_ref, qr_ref, kx_ref, vx_ref, wo_ref, fw_ref,
         y_ref, fin_ref, y_sm, y_tm, s_s, carry, mix_s, first_s) = refs
    t = pl.program_id(1)

    @pl.when(t == 0)
    def _():
        if windowed:
            carry[0] = h0re_ref[...]
            carry[1] = h0im_ref[...]
        else:
            carry[...] = jnp.zeros_like(carry)

    if windowed:
        kj = lax.broadcasted_iota(jnp.int32, (TL, 4 * TL), 0)
        qi = lax.broadcasted_iota(jnp.int32, (TL, 4 * TL), 1) & (TL - 1)
        masks = ((kj >= qi) & (t > 0), (kj <= qi) & (t < n_tiles - 1))
        key_refs = (kp_ref, kc_ref, kn_ref, kx_ref)
        val_refs = (vp_ref, vc_ref, vn_ref, vx_ref)
    else:
        masks = None
        key_refs = (kx_ref,)
        val_refs = (vx_ref,)

    def attention(b):
        _attention(b, (ql_ref, qr_ref), za_ref, mix_s, key_refs, val_refs, sink_ref, masks)

    def emit(blk, slots):
        lanes = slice(blk * LANES, (blk + 1) * LANES)
        for slot, ys in enumerate(slots):
            rows = _slot_rows(slot)
            y_tm[:, rows, lanes] = ys + yb_ref[:, rows, lanes]

    glu_blocks = QUADS // 4

    def glu(rc):
        pr = slice(rc * glu_blocks, (rc + 1) * glu_blocks)
        y = jax.nn.gelu(y_tm[pr].reshape(glu_blocks * BT * SEQS, SSM_WIDTH))
        gl = jnp.dot(y.astype(BF16), wg_ref[...], preferred_element_type=F32) + bg_ref[...]
        y_tm[pr] = (y * jax.nn.sigmoid(gl)).reshape(glu_blocks, BT * SEQS, SSM_WIDTH)

    def gate_s5(seqs):
        for b in seqs:
            for cb in range(SSM_BLOCKS):
                cols = slice(cb * LANES, (cb + 1) * LANES)
                gated = y_sm[cb, b * PITCH:b * PITCH + TL, :] * zs_ref[_seq_rows(b), cols].astype(F32)
                mix_s[_seq_rows(b), cols] = gated.astype(BF16)

    half_seqs = SEQS // 2

    def project_out(hf):
        rows = slice(hf * half_seqs * TL, (hf + 1) * half_seqs * TL)
        proj = jnp.dot(mix_s[rows, :], wo_ref[...], preferred_element_type=F32)
        y_ref[hf * half_seqs:(hf + 1) * half_seqs] = proj.reshape(half_seqs, TL, D_MODEL)

    def finish(seqs):
        for b in seqs:
            m = _mod_row(b, per_sequence=windowed)
            gate = mod_ref[m:m + 1, 2 * D_MODEL:3 * D_MODEL]
            r = x_ref[b] + gate * y_ref[b]
            ms = jnp.mean(r * r, axis=-1, keepdims=True)
            y_ref[b] = (r * lax.rsqrt(ms + EPS)) * fw_ref[...]

    first_half, second_half = range(0, half_seqs), range(half_seqs, SEQS)

    _s5_direction(lhs_ref, s_s, carry, win_ref, wout_ref, lpow_ref, fin_ref, first_s, emit, FWD)
    assert SEQS == 8
    follow_up = [
        lambda: glu(0), lambda: glu(1), lambda: glu(2),
        lambda: (glu(3), _to_sequence_major(y_sm, y_tm)),
        lambda: (gate_s5(first_half), project_out(0)),
        lambda: finish(first_half),
        lambda: gate_s5(second_half),
        lambda: (project_out(1), finish(second_half)),
    ]
    for b in range(SEQS):
        attention(b)
        follow_up[b]()


def _pass_b(x, mod, win, wout, lpow, w_glu, b_glu, from_a, kx, vx, w_out, fnorm_w, sink, h0, k_loc, v_loc):
    batch, length, _ = x.shape
    ng, nt = batch // SEQS, length // TL
    windowed = k_loc is not None
    tile_spec = pl.BlockSpec((None, None, ROWS, SSM_WIDTH), lambda g, t: (g, t, 0, 0))
    in_specs = [
        pl.BlockSpec(memory_space=pltpu.SMEM),
        pl.BlockSpec((SEQS, TL, D_MODEL), lambda g, t: (g, t, 0)),
        _const_spec(mod.shape),
    ] + _s5_param_specs(FWD) + [
        _const_spec((SSM_WIDTH, SSM_WIDTH)),
        _const_spec((1, SSM_WIDTH)),
        pl.BlockSpec((None, None, QROWS, N_UNITS * KTILE), lambda g, t: (g, t, 0, 0)),
        pl.BlockSpec((None, None, QUADS, BT * SEQS, SSM_WIDTH), lambda g, t: (g, t, 0, 0, 0)),
    ] + [tile_spec] * 4 + [
        pl.BlockSpec((SEQS, PAST_LEN, 2 * KV_WIDTH), lambda g, t: (g, 0, 0)),
        pl.BlockSpec((SEQS, KV_WIDTH, PAST_LEN), lambda g, t: (g, 0, 0)),
        _const_spec((D_MODEL, D_MODEL)),
        _const_spec((1, D_MODEL)),
    ]
    args = [sink, x, mod, win, wout, lpow, w_glu, b_glu, *from_a, kx, vx, w_out, fnorm_w]
    if windowed:
        in_specs += _h0_specs(FWD)
        args += list(h0)
        band_t = (lambda t: jnp.maximum(t - 1, 0), lambda t: t, lambda t: jnp.minimum(t + 1, nt - 1))
        in_specs += [pl.BlockSpec((SEQS, TL, 2 * KV_WIDTH), lambda g, t, f=f: (g, f(t), 0)) for f in band_t]
        in_specs += [pl.BlockSpec((SEQS, KV_WIDTH, TL), lambda g, t, f=f: (g, 0, f(t))) for f in band_t]
        args += [k_loc] * 3 + [v_loc] * 3
    out_specs = [pl.BlockSpec((SEQS, TL, D_MODEL), lambda g, t: (g, t, 0))]
    out_shape = [jax.ShapeDtypeStruct((batch, length, D_MODEL), F32)]
    scratch = [
        pltpu.VMEM((SSM_BLOCKS, SEQS * PITCH, LANES), F32),
        pltpu.VMEM((QUADS, BT * SEQS, SSM_WIDTH), F32),
        pltpu.VMEM((S5_BUFS_B, QROWS, 2 * UNIT_ST), F32),
        pltpu.VMEM((2, SEQS, ALL_ST), F32),
        pltpu.VMEM((ROWS, D_MODEL), BF16),
    ]
    if not windowed:
        out_specs += [pl.BlockSpec((None, 2, SEQS, ALL_ST), lambda g, t: (g, 0, 0, 0))]
        out_shape += [jax.ShapeDtypeStruct((ng, 2, SEQS, ALL_ST), F32)]
        scratch += [pltpu.VMEM((2, SEQS, ALL_ST), F32)]
    return pl.pallas_call(
        functools.partial(_pass_b_kernel, windowed=windowed, n_tiles=nt),
        grid=(ng, nt),
        in_specs=in_specs, out_specs=out_specs, out_shape=out_shape, scratch_shapes=scratch,
        compiler_params=pltpu.CompilerParams(
            dimension_semantics=("arbitrary", "arbitrary"), vmem_limit_bytes=VMEM_LIMIT),
        name="pass_b_latent" if windowed else "pass_b_context",
    )(*args)


def kernel(x_prompt, x_sample, c, cache_k, cache_v, state_ssm_re, state_ssm_im, c_ctx, norm_w, w_mod, b_mod, w_in, ssm_lambda_re, ssm_lambda_im, ssm_log_dt, ssm_b_re, ssm_b_im, ssm_c_re, ssm_c_im, ssm_d, w_glu, b_glu, attn_sink, w_out, final_norm_w):
    assert norm_w.shape[0] == 1, "single trunk layer"
    batch, seq, _ = x_prompt.shape
    dec_batch, dec_seq, _ = x_sample.shape
    assert dec_batch == SEQS and batch % SEQS == 0 and seq % TL == 0 and dec_seq % TL == 0
    assert seq == PAST_LEN and cache_k.shape[2] == PAST_LEN

    w_in0 = w_in[0].astype(BF16)
    nw = norm_w[0][None, :]
    fw = final_norm_w[None, :]
    d = ssm_d[0][None, :]
    wg = w_glu[0].astype(BF16)
    bg = b_glu[0][None, :]
    wo = w_out[0].astype(BF16)
    sink = attn_sink[0]

    cond = jnp.concatenate([c, c_ctx[None, :], jnp.zeros((16 - SEQS - 1, D_MODEL), F32)], axis=0)
    mod = _modulation(cond, w_mod[0], b_mod[0][None, :])

    win, wout, wk, lpow = _s5_weights(ssm_lambda_re[0], ssm_lambda_im[0], ssm_log_dt[0], ssm_b_re[0], ssm_b_im[0],
                                     ssm_c_re[0], ssm_c_im[0])
    h0 = (state_ssm_re.reshape(SEQS, 2 * ALL_ST), state_ssm_im.reshape(SEQS, 2 * ALL_ST))
    rope = _rope_tables(dec_seq)

    k_ctx, v_ctx, kcat_ctx, vt_ctx, *tiles_ctx, fin_b = _pass_a(
        x_prompt, mod, nw, w_in0, win, wout, lpow, wk, d, None, None)
    y_prompt, fin_f = _pass_b(x_prompt, mod, win, wout, lpow, wg, bg, tiles_ctx, kcat_ctx, vt_ctx, wo, fw, sink,
                              None, None, None)

    kcat_lat, vt_lat, *tiles_lat = _pass_a(x_sample, mod, nw, w_in0, win, wout, lpow, wk, d, h0, rope)
    kx = cache_k[:, 0].reshape(dec_batch, PAST_LEN, KV_WIDTH)
    kx = jnp.concatenate([kx, jnp.roll(kx, HEAD_DIM, axis=-1)], axis=-1).astype(BF16)
    vx = jnp.swapaxes(cache_v[:, 0].reshape(dec_batch, PAST_LEN, KV_WIDTH), 1, 2).astype(BF16)
    (y_sample,) = _pass_b(x_sample, mod, win, wout, lpow, wg, bg, tiles_lat, kx, vx, wo, fw, sink,
                          h0, kcat_lat, vt_lat)

    new_cache_k = k_ctx.reshape(batch, 1, seq, N_KV_HEADS, HEAD_DIM)
    new_cache_v = v_ctx.reshape(batch, 1, seq, N_KV_HEADS, HEAD_DIM)

    def states(fin, part):
        return fin[:, part].reshape(batch, SSM_GROUPS, SSM_STATE)

    new_re = jnp.stack([states(fin_f, 0), states(fin_b, 0)], axis=1)[:, None]
    new_im = jnp.stack([states(fin_f, 1), states(fin_b, 1)], axis=1)[:, None]
    return (y_prompt, y_sample, new_cache_k, new_cache_v, new_re, new_im)
```

```python
import functools
import math

import jax
import jax.numpy as jnp
from jax import lax
from jax.experimental import pallas as pl
from jax.experimental.pallas import tpu as pltpu

F32 = jnp.float32
BF16 = jnp.bfloat16

D_MODEL = 1024
SSM_WIDTH = 512
ATTN_WIDTH = 512
SSM_GROUP = 16
SSM_GROUPS = 32
SSM_STATE = 64
HEAD_DIM = 64
N_KV_HEADS = 2
KV_WIDTH = 128
GRID_W = 64
ROPE_AXIS_DIM = 32
ROPE_BASE = 10000.0
EPS = 1e-6
LAMBDA_RE_MAX = -1e-4
NEG_INF = -1e30
LOG2E = math.log2(math.e)
PAST_LEN = 256

LANES = 128
BF16_SUBLANES = 16
SEQS = 8
TL = 128
ROWS = SEQS * TL
PITCH = TL + 8
SSM_BLOCKS = SSM_WIDTH // LANES
ALL_ST = SSM_GROUPS * SSM_STATE
BT = 4
KTILE = 2 * LANES
PIECE = KTILE // BT
PIECES = LANES // PIECE
UNIT_GROUPS = PIECE // SSM_GROUP
UNIT_ST = UNIT_GROUPS * SSM_STATE
N_UNITS = SSM_WIDTH // PIECE
QUADS = TL // BT
QROWS = QUADS * SEQS
S5_BUFS_A = 8
S5_BUFS_B = 2
V7X_VMEM_BYTES = 64 * 1024 * 1024
VMEM_LIMIT = V7X_VMEM_BYTES - 6 * 1024 * 1024

BWD, FWD = 1, 0
W_IN_U = 0
W_IN_ZS = SSM_WIDTH
W_IN_Q = 2 * SSM_WIDTH
W_IN_K = 2 * SSM_WIDTH + ATTN_WIDTH
W_IN_ZA = W_IN_K + 2 * KV_WIDTH
D_IN = W_IN_ZA + ATTN_WIDTH


def _const_spec(shape, index=None):
    index = (0,) * len(shape) if index is None else index
    return pl.BlockSpec(shape, lambda g, t: index, pipeline_mode=pl.Buffered(1))


def _modulation_kernel(cond_ref, w_ref, b_ref, out_ref):
    a = jax.nn.silu(cond_ref[...]).astype(BF16)
    out_ref[...] = jnp.dot(a, w_ref[...].astype(BF16), preferred_element_type=F32) + b_ref[...]


def _modulation(cond, w_mod, b_mod):
    rows = cond.shape[0]
    n = w_mod.shape[1]
    tn = 1024
    return pl.pallas_call(
        _modulation_kernel,
        grid=(n // tn,),
        in_specs=[pl.BlockSpec((rows, D_MODEL), lambda j: (0, 0)),
                  pl.BlockSpec((D_MODEL, tn), lambda j: (0, j)),
                  pl.BlockSpec((1, tn), lambda j: (0, j))],
        out_specs=pl.BlockSpec((rows, tn), lambda j: (0, j)),
        out_shape=jax.ShapeDtypeStruct((rows, n), F32),
        name="modulation",
    )(cond, w_mod, b_mod)


def _zoh(lam_re, lam_im, log_dt):
    lam_re = jnp.minimum(lam_re, LAMBDA_RE_MAX)
    dt = jnp.exp(log_dt)
    mag = jnp.exp(lam_re * dt)
    ang = lam_im * dt
    lbar_re = mag * jnp.cos(ang)
    lbar_im = mag * jnp.sin(ang)
    nr = lbar_re - 1.0
    ni = lbar_im
    den = lam_re * lam_re + lam_im * lam_im
    f_re = (nr * lam_re + ni * lam_im) / den
    f_im = (ni * lam_re - nr * lam_im) / den
    return lbar_re, lbar_im, f_re, f_im


def _cmul(a, b):
    return a[0] * b[0] - a[1] * b[1], a[0] * b[1] + a[1] * b[0]


def _s5_weights_kernel(lam_ref, bt_ref, cre_ref, cim_ref, lamflat_ref, win_ref, wout_ref, wk_ref, lpow_ref):
    p_row = lax.broadcasted_iota(jnp.int32, (SSM_STATE, UNIT_ST), 0)
    p_col = lax.broadcasted_iota(jnp.int32, (SSM_STATE, UNIT_ST), 1) & (SSM_STATE - 1)
    spread = jnp.where(p_row == p_col, 1.0, 0.0).astype(BF16)
    g_row = lax.broadcasted_iota(jnp.int32, (PIECE, UNIT_ST), 0) // SSM_GROUP
    g_col = lax.broadcasted_iota(jnp.int32, (PIECE, UNIT_ST), 1) // SSM_STATE
    own_block = g_row == g_col

    def blocks(x):
        return jnp.where(own_block, jnp.dot(x.astype(BF16), spread, preferred_element_type=F32), 0.0)

    def blocks2(z):
        return blocks(z[0]).astype(BF16), blocks(z[1]).astype(BF16)

    def in_block(src, c):
        dims = (((1,), (1,)), ((), ()))
        return (lax.dot_general(src[0], c[0], dims, preferred_element_type=F32)
                - lax.dot_general(src[1], c[1], dims, preferred_element_type=F32))

    def powers(z, lbar, n):
        out = [z]
        for _ in range(n):
            out.append(_cmul(out[-1], lbar))
        return out

    k_lag = {}
    for d in (FWD, BWD):
        lbar_re, lbar_im, f_re, f_im = _zoh(lam_ref[0, d], lam_ref[1, d], lam_ref[2, d])
        lbar = (lbar_re, lbar_im)
        bb_pow = powers(_cmul((f_re, f_im), (bt_ref[0, d], bt_ref[1, d])), lbar, BT - 1)
        c_pow = powers((cre_ref[d], cim_ref[d]), lbar, BT)
        c_b = blocks2(c_pow[0])
        bb_b = [blocks2(z) for z in bb_pow]
        to_end = [(BT - 1 - s) if d == FWD else s for s in range(BT)]
        for s in range(BT):
            w = bb_b[to_end[s]]
            win_ref[d, s * PIECE:(s + 1) * PIECE, 0:UNIT_ST] = w[0]
            win_ref[d, s * PIECE:(s + 1) * PIECE, UNIT_ST:2 * UNIT_ST] = w[1]
        out_re = jnp.concatenate([blocks(c_pow[BT - to_end[s]][0]) for s in range(BT)], axis=0)
        out_im = jnp.concatenate([blocks(-c_pow[BT - to_end[s]][1]) for s in range(BT)], axis=0)
        wout_ref[d, 0:UNIT_ST, :] = out_re.T.astype(BF16)
        wout_ref[d, UNIT_ST:2 * UNIT_ST, :] = out_im.T.astype(BF16)
        k_lag[d] = [in_block(bb_b[m], c_b) for m in range(BT)]
    for src in range(BT):
        for out in range(BT):
            if out == src:
                k = k_lag[FWD][0] + k_lag[BWD][0]
            elif out > src:
                k = k_lag[FWD][out - src]
            else:
                k = k_lag[BWD][src - out]
            wk_ref[src * PIECE:(src + 1) * PIECE, out * PIECE:(out + 1) * PIECE] = k.astype(BF16)

    @pl.when(pl.program_id(0) == 0)
    def _():
        lbar_re, lbar_im, _, _ = _zoh(lamflat_ref[0], lamflat_ref[1], lamflat_ref[2])
        p = (lbar_re, lbar_im)
        for _ in range(BT.bit_length() - 1):
            p = _cmul(p, p)
        lpow_ref[0] = p[0]
        lpow_ref[1] = p[1]


def _s5_weights(lam_re, lam_im, log_dt, b_re, b_im, c_re, c_im):
    assert BT & (BT - 1) == 0
    n = SSM_GROUPS * SSM_GROUP
    lam3 = jnp.stack([lam_re, lam_im, jnp.broadcast_to(log_dt[:, :, None], lam_re.shape)])
    lam_rows = jnp.broadcast_to(lam3[:, :, :, None, :], (3, 2, SSM_GROUPS, SSM_GROUP, SSM_STATE)).reshape(3, 2, n, SSM_STATE)
    lam_flat = jnp.broadcast_to(lam3.reshape(3, 1, 2 * ALL_ST), (3, SEQS, 2 * ALL_ST))
    bt = jnp.transpose(jnp.stack([b_re, b_im]), (0, 1, 2, 4, 3)).reshape(2, 2, n, SSM_STATE)
    return pl.pallas_call(
        _s5_weights_kernel,
        grid=(N_UNITS,),
        in_specs=[pl.BlockSpec((3, 2, PIECE, SSM_STATE), lambda m: (0, 0, m, 0)),
                  pl.BlockSpec((2, 2, PIECE, SSM_STATE), lambda m: (0, 0, m, 0)),
                  pl.BlockSpec((2, PIECE, SSM_STATE), lambda m: (0, m, 0)),
                  pl.BlockSpec((2, PIECE, SSM_STATE), lambda m: (0, m, 0)),
                  pl.BlockSpec((3, SEQS, 2 * ALL_ST), lambda m: (0, 0, 0))],
        out_specs=[pl.BlockSpec((2, None, KTILE, 2 * UNIT_ST), lambda m: (0, m, 0, 0)),
                   pl.BlockSpec((2, None, 2 * UNIT_ST, KTILE), lambda m: (0, m, 0, 0)),
                   pl.BlockSpec((None, KTILE, KTILE), lambda m: (m, 0, 0)),
                   pl.BlockSpec((2, SEQS, 2 * ALL_ST), lambda m: (0, 0, 0))],
        out_shape=[jax.ShapeDtypeStruct((2, N_UNITS, KTILE, 2 * UNIT_ST), BF16),
                   jax.ShapeDtypeStruct((2, N_UNITS, 2 * UNIT_ST, KTILE), BF16),
                   jax.ShapeDtypeStruct((N_UNITS, KTILE, KTILE), BF16),
                   jax.ShapeDtypeStruct((2, SEQS, 2 * ALL_ST), F32)],
        name="s5_weights",
    )(lam_rows, bt, c_re.reshape(2, n, SSM_STATE), c_im.reshape(2, n, SSM_STATE), lam_flat)


def _rope_tables(length):
    pos = jnp.arange(length)
    row = (pos // GRID_W).astype(F32)
    col = (pos % GRID_W).astype(F32)
    freqs = ROPE_BASE ** (-jnp.arange(0, ROPE_AXIS_DIM, 2, dtype=F32) / ROPE_AXIS_DIM)
    ang_r = row[:, None] * freqs[None, :]
    ang_c = col[:, None] * freqs[None, :]
    cos = jnp.concatenate([jnp.cos(ang_r), jnp.cos(ang_r), jnp.cos(ang_c), jnp.cos(ang_c)], axis=-1)
    sin = jnp.concatenate([-jnp.sin(ang_r), jnp.sin(ang_r), -jnp.sin(ang_c), jnp.sin(ang_c)], axis=-1)
    return jnp.tile(cos, (1, 2)), jnp.tile(sin, (1, 2))


def _seq_rows(b):
    return slice(b * TL, (b + 1) * TL)


def _mod_row(b, per_sequence):
    return b if per_sequence else SEQS


def _norm_mod(h_ref, x_ref, mod_ref, nw_ref, b, per_sequence):
    xb = x_ref[b]
    ms = jnp.mean(xb * xb, axis=-1, keepdims=True)
    m = _mod_row(b, per_sequence)
    shift = mod_ref[m:m + 1, 0:D_MODEL]
    gain = nw_ref[...] * (1.0 + mod_ref[m:m + 1, D_MODEL:2 * D_MODEL])
    h_ref[_seq_rows(b), :] = ((xb * lax.rsqrt(ms + EPS)) * gain + shift).astype(BF16)


def _rope(x, cos, sin):
    lane = lax.broadcasted_iota(jnp.int32, x.shape, 1)
    first = (lane & 31) < 16
    partner = jnp.where(first, pltpu.roll(x, LANES - 16, 1), pltpu.roll(x, 16, 1))
    x3 = x.reshape(SEQS, TL, LANES)
    p3 = partner.reshape(SEQS, TL, LANES)
    return (x3 * cos[None] + p3 * sin[None]).reshape(ROWS, LANES)


def _store_column_blocks(dst_ref, val, first_seq):
    for j in range(val.shape[0] // TL):
        b = first_seq + j
        for cb in range(SSM_BLOCKS):
            dst_ref[cb, b * PITCH:b * PITCH + TL, :] = val[_seq_rows(j), cb * LANES:(cb + 1) * LANES]


def _token_rows(l):
    return l // BT, slice((l % BT) * SEQS, (l % BT + 1) * SEQS)


def _slot_rows(slot):
    return slice(slot * SEQS, (slot + 1) * SEQS)


def _to_token_major(dst_ref, src_ref):
    for l in range(TL):
        j, rows = _token_rows(l)
        for cb in range(SSM_BLOCKS):
            dst_ref[j, rows, cb * LANES:(cb + 1) * LANES] = src_ref[cb, pl.ds(l, SEQS, stride=PITCH), :]


def _to_sequence_major(dst_ref, src_ref):
    for l in range(TL):
        j, rows = _token_rows(l)
        for cb in range(SSM_BLOCKS):
            dst_ref[cb, pl.ds(l, SEQS, stride=PITCH), :] = src_ref[j, rows, cb * LANES:(cb + 1) * LANES]


def _transpose_pieces(vs):
    vs = list(vs)
    assert len(vs) == PIECES
    lane = lax.broadcasted_iota(jnp.int32, vs[0].shape, 1)
    span = LANES // 2
    while span >= PIECE:
        stride = span // PIECE
        even = (lane // span) % 2 == 0
        out = list(vs)
        for i in range(PIECES):
            if (i // stride) % 2 == 0:
                a, b = vs[i], vs[i + stride]
                out[i] = jnp.where(even, a, pltpu.roll(b, span, 1))
                out[i + stride] = jnp.where(even, pltpu.roll(a, LANES - span, 1), b)
        vs = out
        span //= 2
    return vs


def _pack_scan_operand(lhs_ref, u_tm):
    for blk in range(SSM_BLOCKS):
        lanes = slice(blk * LANES, (blk + 1) * LANES)
        for v in range(BT // PIECES):
            slots = [u_tm[:, _slot_rows(v * PIECES + j), lanes].reshape(QROWS, LANES) for j in range(PIECES)]
            for k, val in enumerate(_transpose_pieces(slots)):
                c0 = (blk * PIECES + k) * KTILE + v * LANES
                lhs_ref[:, c0:c0 + LANES] = val.astype(BF16)


def _unpack_outputs(ys):
    out = []
    for v in range(BT // PIECES):
        cols = slice(v * LANES, (v + 1) * LANES)
        for val in _transpose_pieces([y[:, cols] for y in ys]):
            out.append(val.reshape(QUADS, SEQS, LANES))
    return out


def _scan_unit(s_ref, lpow_ref, carry_ref, direction, unit, reverse):
    cols = slice(unit * UNIT_ST, (unit + 1) * UNIT_ST)
    lanes = slice(direction * ALL_ST + unit * UNIT_ST, direction * ALL_ST + (unit + 1) * UNIT_ST)
    lr = lpow_ref[0, :, lanes]
    li = lpow_ref[1, :, lanes]
    sr = carry_ref[0, :, cols]
    si = carry_ref[1, :, cols]
    for i in range(QUADS):
        j = (QUADS - 1 - i) if reverse else i
        rows = slice(j * SEQS, (j + 1) * SEQS)
        inc_r = s_ref[rows, 0:UNIT_ST]
        inc_i = s_ref[rows, UNIT_ST:2 * UNIT_ST]
        s_ref[rows, 0:UNIT_ST] = sr
        s_ref[rows, UNIT_ST:2 * UNIT_ST] = si
        sr, si = lr * sr - li * si + inc_r, lr * si + li * sr + inc_i
    carry_ref[0, :, cols] = sr
    carry_ref[1, :, cols] = si


def _s5_direction(lhs_ref, s_ref, carry_ref, win_ref, wout_ref, lpow_ref, first_ref, first_s, emit, direction,
                  wk_ref=None, between=None):
    reverse = direction == BWD

    def lhs(unit):
        return lhs_ref[:, unit * KTILE:(unit + 1) * KTILE]

    bufs = s_ref.shape[0]
    ahead = min(bufs, N_UNITS) if bufs >= N_UNITS else bufs - 1

    def project_in(unit):
        s_ref[unit % bufs] = jnp.dot(lhs(unit), win_ref[unit], preferred_element_type=F32)

    for unit in range(min(ahead, N_UNITS)):
        project_in(unit)
    pending = []
    for unit in range(N_UNITS):
        if ahead <= unit + ahead < N_UNITS:
            project_in(unit + ahead)
        buf = s_ref.at[unit % bufs]
        _scan_unit(buf, lpow_ref, carry_ref, direction, unit, reverse)
        y = jnp.dot(buf[...].astype(BF16), wout_ref[unit], preferred_element_type=F32)
        if wk_ref is not None:
            y = y + jnp.dot(lhs(unit), wk_ref[unit], preferred_element_type=F32)
        pending.append(y)
        if len(pending) == PIECES:
            emit(unit // PIECES, _unpack_outputs(pending))
            pending = []
        if between is not None:
            between(unit)

    if first_ref is not None:
        first_rows = slice(QROWS - 2 * SEQS, QROWS) if reverse else slice(0, 2 * SEQS)
        lane = lax.broadcasted_iota(jnp.int32, (2 * SEQS, LANES), 1)
        zeros = jnp.zeros((2 * SEQS, LANES), F32)
        other = [zeros] * (KTILE // LANES - 1)
        for unit in range(N_UNITS):
            if reverse:
                tail = lhs_ref[first_rows, (unit + 1) * KTILE - LANES:(unit + 1) * KTILE].astype(F32)
                block = [jnp.where(lane < PIECE, pltpu.roll(tail, PIECE, 1), 0.0)] + other
            else:
                head = lhs_ref[first_rows, unit * KTILE:unit * KTILE + LANES].astype(F32)
                block = other + [jnp.where(lane >= LANES - PIECE, pltpu.roll(head, LANES - PIECE, 1), 0.0)]
            inc = jnp.dot(jnp.concatenate(block, axis=1).astype(BF16), win_ref[unit], preferred_element_type=F32)
            inc = inc[SEQS:2 * SEQS] if reverse else inc[0:SEQS]
            first_s[0, :, unit * UNIT_ST:(unit + 1) * UNIT_ST] = inc[:, 0:UNIT_ST]
            first_s[1, :, unit * UNIT_ST:(unit + 1) * UNIT_ST] = inc[:, UNIT_ST:2 * UNIT_ST]

        @pl.when(pl.program_id(1) == 0)
        def _():
            first_ref[...] = first_s[...]


def _s5_param_specs(direction):
    return [_const_spec((None, N_UNITS, KTILE, 2 * UNIT_ST), (direction, 0, 0, 0)),
            _const_spec((None, N_UNITS, 2 * UNIT_ST, KTILE), (direction, 0, 0, 0)),
            _const_spec((2, SEQS, 2 * ALL_ST))]


def _h0_specs(direction):
    return [_const_spec((SEQS, ALL_ST), (0, direction))] * 2


def _pass_a_kernel(*refs, rotary):
    if rotary:
        (x_ref, mod_ref, nw_ref, w_ref, win_ref, wout_ref, lpow_ref, wk_ref, d_ref, h0re_ref, h0im_ref, cos_ref, sin_ref,
         kcat_ref, vt_ref, lhs_ref, yb_ref, zs_ref, za_ref, ql_ref, qr_ref, h_s, u_sm, u_tm, s_s, carry) = refs
        fin_ref = first_s = None
    else:
        (x_ref, mod_ref, nw_ref, w_ref, win_ref, wout_ref, lpow_ref, wk_ref, d_ref,
         k_ref, v_ref, kcat_ref, vt_ref, lhs_ref, yb_ref, zs_ref, za_ref, ql_ref, qr_ref, fin_ref,
         h_s, u_sm, u_tm, s_s, carry, first_s) = refs

    @pl.when(pl.program_id(1) == 0)
    def _():
        if rotary:
            carry[0] = h0re_ref[...]
            carry[1] = h0im_ref[...]
        else:
            carry[...] = jnp.zeros_like(carry)

    half_seqs = SEQS // 2
    for hf in range(2):
        for b in range(hf * half_seqs, (hf + 1) * half_seqs):
            _norm_mod(h_s, x_ref, mod_ref, nw_ref, b, per_sequence=rotary)
        rows = slice(hf * half_seqs * TL, (hf + 1) * half_seqs * TL)
        u_half = jnp.dot(h_s[rows, :], w_ref[:, W_IN_U:W_IN_U + SSM_WIDTH], preferred_element_type=F32)
        _store_column_blocks(u_sm, u_half, first_seq=hf * half_seqs)
    _to_token_major(u_tm, u_sm)
    _pack_scan_operand(lhs_ref, u_tm)

    def project(c0, width):
        return jnp.dot(h_s[...], w_ref[:, c0:c0 + width], preferred_element_type=F32)

    def project_kv():
        kv = project(W_IN_K, 2 * KV_WIDTH)
        k = kv[:, 0:KV_WIDTH]
        v = kv[:, KV_WIDTH:2 * KV_WIDTH]
        if rotary:
            k = _rope(k, cos_ref[...], sin_ref[...])
        else:
            k_ref[...] = k.reshape(SEQS, TL, KV_WIDTH)
            v_ref[...] = v.reshape(SEQS, TL, KV_WIDTH)
        kcat = jnp.concatenate([k, pltpu.roll(k, HEAD_DIM, 1)], axis=1).astype(BF16)
        kcat_ref[...] = kcat.reshape(SEQS, TL, 2 * KV_WIDTH)
        for b in range(SEQS):
            vt_ref[b] = v[_seq_rows(b), :].T.astype(BF16)

    def project_q():
        qall = project(W_IN_Q, ATTN_WIDTH)
        even_head = lax.broadcasted_iota(jnp.int32, (ROWS, LANES), 1) < HEAD_DIM
        for cg in range(ATTN_WIDTH // LANES):
            cols = slice(cg * LANES, (cg + 1) * LANES)
            q = qall[:, cols]
            if rotary:
                q = _rope(q, cos_ref[...], sin_ref[...])
            q = q * (LOG2E * HEAD_DIM ** -0.5)
            ql_ref[:, cols] = jnp.where(even_head, q, 0.0).astype(BF16)
            qr_ref[:, cols] = jnp.where(even_head, 0.0, q).astype(BF16)

    def project_gate(c0, out_ref):
        out_ref[...] = jax.nn.silu(project(c0, out_ref.shape[1])).astype(BF16)

    side_work = [project_kv, project_q, lambda: project_gate(W_IN_ZS, zs_ref), lambda: project_gate(W_IN_ZA, za_ref)]

    def between(i):
        if i < len(side_work):
            side_work[i]()

    def emit(blk, slots):
        lanes = slice(blk * LANES, (blk + 1) * LANES)
        for slot, ys in enumerate(slots):
            rows = _slot_rows(slot)
            yb_ref[:, rows, lanes] = ys + u_tm[:, rows, lanes] * d_ref[:, lanes]

    _s5_direction(lhs_ref, s_s, carry, win_ref, wout_ref, lpow_ref, fin_ref, first_s, emit, BWD,
                  wk_ref=wk_ref, between=between)


def _pass_a(x, mod, norm_w, w_in, win, wout, lpow, wk, d, h0, rope):
    batch, length, _ = x.shape
    ng, nt = batch // SEQS, length // TL
    rotary = rope is not None
    rev = lambda t: nt - 1 - t
    in_specs = [
        pl.BlockSpec((SEQS, TL, D_MODEL), lambda g, t: (g, rev(t), 0)),
        _const_spec(mod.shape),
        _const_spec((1, D_MODEL)),
        _const_spec((D_MODEL, D_IN)),
    ] + _s5_param_specs(BWD) + [
        _const_spec((N_UNITS, KTILE, KTILE)),
        _const_spec((1, SSM_WIDTH)),
    ]
    args = [x, mod, norm_w, w_in, win, wout, lpow, wk, d]
    if rotary:
        in_specs += _h0_specs(BWD) + [pl.BlockSpec((TL, LANES), lambda g, t: (rev(t), 0))] * 2
        args += list(h0) + list(rope)
    assert ATTN_WIDTH == SSM_WIDTH
    tile_spec = pl.BlockSpec((None, None, ROWS, SSM_WIDTH), lambda g, t: (g, rev(t), 0, 0))
    tile_bf16 = jax.ShapeDtypeStruct((ng, nt, ROWS, SSM_WIDTH), BF16)
    out_specs = [
        pl.BlockSpec((SEQS, TL, 2 * KV_WIDTH), lambda g, t: (g, rev(t), 0)),
        pl.BlockSpec((SEQS, KV_WIDTH, TL), lambda g, t: (g, 0, rev(t))),
        pl.BlockSpec((None, None, QROWS, N_UNITS * KTILE), lambda g, t: (g, rev(t), 0, 0)),
        pl.BlockSpec((None, None, QUADS, BT * SEQS, SSM_WIDTH), lambda g, t: (g, rev(t), 0, 0, 0)),
    ] + [tile_spec] * 4
    out_shape = [
        jax.ShapeDtypeStruct((batch, length, 2 * KV_WIDTH), BF16),
        jax.ShapeDtypeStruct((batch, KV_WIDTH, length), BF16),
        jax.ShapeDtypeStruct((ng, nt, QROWS, N_UNITS * KTILE), BF16),
        jax.ShapeDtypeStruct((ng, nt, QUADS, BT * SEQS, SSM_WIDTH), F32),
        tile_bf16, tile_bf16,
        tile_bf16, tile_bf16,
    ]
    scratch = [
        pltpu.VMEM((ROWS, D_MODEL), BF16),
        pltpu.VMEM((SSM_BLOCKS, SEQS * PITCH, LANES), F32),
        pltpu.VMEM((QUADS, BT * SEQS, SSM_WIDTH), F32),
        pltpu.VMEM((S5_BUFS_A, QROWS, 2 * UNIT_ST), F32),
        pltpu.VMEM((2, SEQS, ALL_ST), F32),
    ]
    if not rotary:
        cache_spec = pl.BlockSpec((SEQS, TL, KV_WIDTH), lambda g, t: (g, rev(t), 0))
        cache_shape = jax.ShapeDtypeStruct((batch, length, KV_WIDTH), F32)
        out_specs = [cache_spec, cache_spec] + out_specs + [pl.BlockSpec((None, 2, SEQS, ALL_ST), lambda g, t: (g, 0, 0, 0))]
        out_shape = [cache_shape, cache_shape] + out_shape + [jax.ShapeDtypeStruct((ng, 2, SEQS, ALL_ST), F32)]
        scratch += [pltpu.VMEM((2, SEQS, ALL_ST), F32)]
    return pl.pallas_call(
        functools.partial(_pass_a_kernel, rotary=rotary),
        grid=(ng, nt),
        in_specs=in_specs, out_specs=out_specs, out_shape=out_shape, scratch_shapes=scratch,
        compiler_params=pltpu.CompilerParams(
            dimension_semantics=("arbitrary", "arbitrary"), vmem_limit_bytes=VMEM_LIMIT),
        name="pass_a_latent" if rotary else "pass_a_context",
    )(*args)


def _attention(b, q_refs, za_ref, mix_s, kcat_refs, vt_refs, sink_ref, masks):
    rows = _seq_rows(b)
    kcat = jnp.concatenate([r[b] for r in kcat_refs], axis=0)
    vt = jnp.concatenate([r[b] for r in vt_refs], axis=1)
    segment = lax.broadcasted_iota(jnp.int32, (1, 4 * TL), 1) // TL
    kv_heads = range(N_KV_HEADS)

    def side(g, grp):
        return g ^ grp

    def head(g, i, grp):
        return 4 * g + 2 * i + side(g, grp)

    def scores(grp):
        q = jnp.concatenate([q_refs[side(g, grp)][rows, blk * LANES:(blk + 1) * LANES]
                             for g in kv_heads for blk in (2 * g, 2 * g + 1)], axis=0)
        return lax.dot_general(kcat[:, grp * KV_WIDTH:(grp + 1) * KV_WIDTH], q, (((1,), (1,)), ((), ())),
                               preferred_element_type=F32)

    def softmax(grp, s):
        if masks is None:
            pieces = [s]
        else:
            pieces = [jnp.where(masks[0], s[0:TL], NEG_INF), s[TL:2 * TL],
                      jnp.where(masks[1], s[2 * TL:3 * TL], NEG_INF), s[3 * TL:]]
        sinks = [sink_ref[head(g, i, grp)] * LOG2E for g in kv_heads for i in range(2)]
        sink = jnp.where(segment == 0, sinks[0], jnp.where(segment == 1, sinks[1],
                                                           jnp.where(segment == 2, sinks[2], sinks[3])))
        m = sink
        for piece in pieces:
            m = jnp.maximum(m, jnp.max(piece, axis=0, keepdims=True))
        ps = [jnp.exp2(piece - m).astype(BF16) for piece in pieces]
        return (ps[0] if len(ps) == 1 else jnp.concatenate(ps, axis=0)), jnp.exp2(sink - m)

    ones_rows = jnp.ones((BF16_SUBLANES, vt.shape[1]), BF16)

    all_scores = [scores(grp) for grp in range(2)]
    probs, sink_terms = zip(*[softmax(grp, s) for grp, s in enumerate(all_scores)])
    for g in kv_heads:
        cols = slice(g * 2 * TL, (g + 1) * 2 * TL)
        values = jnp.concatenate([vt[g * HEAD_DIM:(g + 1) * HEAD_DIM, :], ones_rows], axis=0)
        p = jnp.concatenate([probs[0][:, cols], probs[1][:, cols]], axis=1)
        sink_term = jnp.concatenate([sink_terms[0][:, cols], sink_terms[1][:, cols]], axis=1)
        ot = jnp.dot(values, p, preferred_element_type=F32)
        out_t = ot[0:HEAD_DIM] * (1.0 / (ot[HEAD_DIM:HEAD_DIM + 1] + sink_term))
        for i in range(2):
            left = out_t[:, (2 * g + i) * TL:(2 * g + i + 1) * TL]
            right = out_t[:, (2 * (g ^ 1) + i) * TL:(2 * (g ^ 1) + i + 1) * TL]
            blk_cols = slice((2 * g + i) * LANES, (2 * g + i + 1) * LANES)
            pair_t = jnp.concatenate([left, right], axis=0)
            gated = pair_t.T * za_ref[rows, blk_cols].astype(F32)
            mix_s[rows, ATTN_WIDTH + blk_cols.start:ATTN_WIDTH + blk_cols.stop] = gated.astype(BF16)


def _pass_b_kernel(*refs, windowed, n_tiles):
    if windowed:
        (sink_ref, x_ref, mod_ref, win_ref, wout_ref, lpow_ref, wg_ref, bg_ref,
         lhs_ref, yb_ref, zs_ref, za_ref, ql_ref, qr_ref, kx_ref, vx_ref, wo_ref, fw_ref, h0re_ref, h0im_ref,
         kp_ref, kc_ref, kn_ref, vp_ref, vc_ref, vn_ref,
         y_ref, y_sm, y_tm, s_s, carry, mix_s) = refs
        fin_ref = first_s = None
    else:
        (sink_ref, x_ref, mod_ref, win_ref, wout_ref, lpow_ref, wg_ref, bg_ref,
         lhs_ref, yb_ref, zs_ref, za_ref, ql_ref, qr_ref, kx_ref, vx_ref, wo_ref, fw_ref,
         y_ref, fin_ref, y_sm, y_tm, s_s, carry, mix_s, first_s) = refs
    t = pl.program_id(1)

    @pl.when(t == 0)
    def _():
        if windowed:
            carry[0] = h0re_ref[...]
            carry[1] = h0im_ref[...]
        else:
            carry[...] = jnp.zeros_like(carry)

    if windowed:
        kj = lax.broadcasted_iota(jnp.int32, (TL, 4 * TL), 0)
        qi = lax.broadcasted_iota(jnp.int32, (TL, 4 * TL), 1) & (TL - 1)
        masks = ((kj >= qi) & (t > 0), (kj <= qi) & (t < n_tiles - 1))
        key_refs = (kp_ref, kc_ref, kn_ref, kx_ref)
        val_refs = (vp_ref, vc_ref, vn_ref, vx_ref)
    else:
        masks = None
        key_refs = (kx_ref,)
        val_refs = (vx_ref,)

    def attention(b):
        _attention(b, (ql_ref, qr_ref), za_ref, mix_s, key_refs, val_refs, sink_ref, masks)

    def emit(blk, slots):
        lanes = slice(blk * LANES, (blk + 1) * LANES)
        for slot, ys in enumerate(slots):
            rows = _slot_rows(slot)
            y_tm[:, rows, lanes] = ys + yb_ref[:, rows, lanes]

    glu_blocks = QUADS // 4

    def glu(rc):
        pr = slice(rc * glu_blocks, (rc + 1) * glu_blocks)
        y = jax.nn.gelu(y_tm[pr].reshape(glu_blocks * BT * SEQS, SSM_WIDTH))
        gl = jnp.dot(y.astype(BF16), wg_ref[...], preferred_element_type=F32) + bg_ref[...]
        y_tm[pr] = (y * jax.nn.sigmoid(gl)).reshape(glu_blocks, BT * SEQS, SSM_WIDTH)

    def gate_s5(seqs):
        for b in seqs:
            for cb in range(SSM_BLOCKS):
                cols = slice(cb * LANES, (cb + 1) * LANES)
                gated = y_sm[cb, b * PITCH:b * PITCH + TL, :] * zs_ref[_seq_rows(b), cols].astype(F32)
                mix_s[_seq_rows(b), cols] = gated.astype(BF16)

    half_seqs = SEQS // 2

    def project_out(hf):
        rows = slice(hf * half_seqs * TL, (hf + 1) * half_seqs * TL)
        proj = jnp.dot(mix_s[rows, :], wo_ref[...], preferred_element_type=F32)
        y_ref[hf * half_seqs:(hf + 1) * half_seqs] = proj.reshape(half_seqs, TL, D_MODEL)

    def finish(seqs):
        for b in seqs:
            m = _mod_row(b, per_sequence=windowed)
            gate = mod_ref[m:m + 1, 2 * D_MODEL:3 * D_MODEL]
            r = x_ref[b] + gate * y_ref[b]
            ms = jnp.mean(r * r, axis=-1, keepdims=True)
            y_ref[b] = (r * lax.rsqrt(ms + EPS)) * fw_ref[...]

    first_half, second_half = range(0, half_seqs), range(half_seqs, SEQS)

    _s5_direction(lhs_ref, s_s, carry, win_ref, wout_ref, lpow_ref, fin_ref, first_s, emit, FWD)
    assert SEQS == 8
    follow_up = [
        lambda: glu(0), lambda: glu(1), lambda: glu(2),
        lambda: (glu(3), _to_sequence_major(y_sm, y_tm)),
        lambda: (gate_s5(first_half), project_out(0)),
        lambda: finish(first_half),
        lambda: gate_s5(second_half),
        lambda: (project_out(1), finish(second_half)),
    ]
    for b in range(SEQS):
        attention(b)
        follow_up[b]()


def _pass_b(x, mod, win, wout, lpow, w_glu, b_glu, from_a, kx, vx, w_out, fnorm_w, sink, h0, k_loc, v_loc):
    batch, length, _ = x.shape
    ng, nt = batch // SEQS, length // TL
    windowed = k_loc is not None
    tile_spec = pl.BlockSpec((None, None, ROWS, SSM_WIDTH), lambda g, t: (g, t, 0, 0))
    in_specs = [
        pl.BlockSpec(memory_space=pltpu.SMEM),
        pl.BlockSpec((SEQS, TL, D_MODEL), lambda g, t: (g, t, 0)),
        _const_spec(mod.shape),
    ] + _s5_param_specs(FWD) + [
        _const_spec((SSM_WIDTH, SSM_WIDTH)),
        _const_spec((1, SSM_WIDTH)),
        pl.BlockSpec((None, None, QROWS, N_UNITS * KTILE), lambda g, t: (g, t, 0, 0)),
        pl.BlockSpec((None, None, QUADS, BT * SEQS, SSM_WIDTH), lambda g, t: (g, t, 0, 0, 0)),
    ] + [tile_spec] * 4 + [
        pl.BlockSpec((SEQS, PAST_LEN, 2 * KV_WIDTH), lambda g, t: (g, 0, 0)),
        pl.BlockSpec((SEQS, KV_WIDTH, PAST_LEN), lambda g, t: (g, 0, 0)),
        _const_spec((D_MODEL, D_MODEL)),
        _const_spec((1, D_MODEL)),
    ]
    args = [sink, x, mod, win, wout, lpow, w_glu, b_glu, *from_a, kx, vx, w_out, fnorm_w]
    if windowed:
        in_specs += _h0_specs(FWD)
        args += list(h0)
        band_t = (lambda t: jnp.maximum(t - 1, 0), lambda t: t, lambda t: jnp.minimum(t + 1, nt - 1))
        in_specs += [pl.BlockSpec((SEQS, TL, 2 * KV_WIDTH), lambda g, t, f=f: (g, f(t), 0)) for f in band_t]
        in_specs += [pl.BlockSpec((SEQS, KV_WIDTH, TL), lambda g, t, f=f: (g, 0, f(t))) for f in band_t]
        args += [k_loc] * 3 + [v_loc] * 3
    out_specs = [pl.BlockSpec((SEQS, TL, D_MODEL), lambda g, t: (g, t, 0))]
    out_shape = [jax.ShapeDtypeStruct((batch, length, D_MODEL), F32)]
    scratch = [
        pltpu.VMEM((SSM_BLOCKS, SEQS * PITCH, LANES), F32),
        pltpu.VMEM((QUADS, BT * SEQS, SSM_WIDTH), F32),
        pltpu.VMEM((S5_BUFS_B, QROWS, 2 * UNIT_ST), F32),
        pltpu.VMEM((2, SEQS, ALL_ST), F32),
        pltpu.VMEM((ROWS, D_MODEL), BF16),
    ]
    if not windowed:
        out_specs += [pl.BlockSpec((None, 2, SEQS, ALL_ST), lambda g, t: (g, 0, 0, 0))]
        out_shape += [jax.ShapeDtypeStruct((ng, 2, SEQS, ALL_ST), F32)]
        scratch += [pltpu.VMEM((2, SEQS, ALL_ST), F32)]
    return pl.pallas_call(
        functools.partial(_pass_b_kernel, windowed=windowed, n_tiles=nt),
        grid=(ng, nt),
        in_specs=in_specs, out_specs=out_specs, out_shape=out_shape, scratch_shapes=scratch,
        compiler_params=pltpu.CompilerParams(
            dimension_semantics=("arbitrary", "arbitrary"), vmem_limit_bytes=VMEM_LIMIT),
        name="pass_b_latent" if windowed else "pass_b_context",
    )(*args)


def kernel(x_prompt, x_sample, c, cache_k, cache_v, state_ssm_re, state_ssm_im, c_ctx, norm_w, w_mod, b_mod, w_in, ssm_lambda_re, ssm_lambda_im, ssm_log_dt, ssm_b_re, ssm_b_im, ssm_c_re, ssm_c_im, ssm_d, w_glu, b_glu, attn_sink, w_out, final_norm_w):
    assert norm_w.shape[0] == 1, "single trunk layer"
    batch, seq, _ = x_prompt.shape
    dec_batch, dec_seq, _ = x_sample.shape
    assert dec_batch == SEQS and batch % SEQS == 0 and seq % TL == 0 and dec_seq % TL == 0
    assert seq == PAST_LEN and cache_k.shape[2] == PAST_LEN

    w_in0 = w_in[0].astype(BF16)
    nw = norm_w[0][None, :]
    fw = final_norm_w[None, :]
    d = ssm_d[0][None, :]
    wg = w_glu[0].astype(BF16)
    bg = b_glu[0][None, :]
    wo = w_out[0].astype(BF16)
    sink = attn_sink[0]

    cond = jnp.concatenate([c, c_ctx[None, :], jnp.zeros((16 - SEQS - 1, D_MODEL), F32)], axis=0)
    mod = _modulation(cond, w_mod[0], b_mod[0][None, :])

    win, wout, wk, lpow = _s5_weights(ssm_lambda_re[0], ssm_lambda_im[0], ssm_log_dt[0], ssm_b_re[0], ssm_b_im[0],
                                     ssm_c_re[0], ssm_c_im[0])
    h0 = (state_ssm_re.reshape(SEQS, 2 * ALL_ST), state_ssm_im.reshape(SEQS, 2 * ALL_ST))
    rope = _rope_tables(dec_seq)

    k_ctx, v_ctx, kcat_ctx, vt_ctx, *tiles_ctx, fin_b = _pass_a(
        x_prompt, mod, nw, w_in0, win, wout, lpow, wk, d, None, None)
    y_prompt, fin_f = _pass_b(x_prompt, mod, win, wout, lpow, wg, bg, tiles_ctx, kcat_ctx, vt_ctx, wo, fw, sink,
                              None, None, None)

    kcat_lat, vt_lat, *tiles_lat = _pass_a(x_sample, mod, nw, w_in0, win, wout, lpow, wk, d, h0, rope)
    kx = cache_k[:, 0].reshape(dec_batch, PAST_LEN, KV_WIDTH)
    kx = jnp.concatenate([kx, jnp.roll(kx, HEAD_DIM, axis=-1)], axis=-1).astype(BF16)
    vx = jnp.swapaxes(cache_v[:, 0].reshape(dec_batch, PAST_LEN, KV_WIDTH), 1, 2).astype(BF16)
    (y_sample,) = _pass_b(x_sample, mod, win, wout, lpow, wg, bg, tiles_lat, kx, vx, wo, fw, sink,
                          h0, kcat_lat, vt_lat)

    new_cache_k = k_ctx.reshape(batch, 1, seq, N_KV_HEADS, HEAD_DIM)
    new_cache_v = v_ctx.reshape(batch, 1, seq, N_KV_HEADS, HEAD_DIM)

    def states(fin, part):
        return fin[:, part].reshape(batch, SSM_GROUPS, SSM_STATE)

    new_re = jnp.stack([states(fin_f, 0), states(fin_b, 0)], axis=1)[:, None]
    new_im = jnp.stack([states(fin_f, 1), states(fin_b, 1)], axis=1)[:, None]
    return (y_prompt, y_sample, new_cache_k, new_cache_v, new_re, new_im)
```

```python
import functools
import math

import jax
import jax.numpy as jnp
from jax import lax
from jax.experimental import pallas as pl
from jax.experimental.pallas import tpu as pltpu

F32 = jnp.float32
BF16 = jnp.bfloat16

D_MODEL = 1024
SSM_WIDTH = 512
ATTN_WIDTH = 512
SSM_GROUP = 16
SSM_GROUPS = 32
SSM_STATE = 64
HEAD_DIM = 64
N_KV_HEADS = 2
KV_WIDTH = 128
GRID_W = 64
ROPE_AXIS_DIM = 32
ROPE_BASE = 10000.0
EPS = 1e-6
LAMBDA_RE_MAX = -1e-4
NEG_INF = -1e30
LOG2E = math.log2(math.e)
PAST_LEN = 256

LANES = 128
BF16_SUBLANES = 16
SEQS = 8
TL = 128
ROWS = SEQS * TL
PITCH = TL + 8
SSM_BLOCKS = SSM_WIDTH // LANES
ALL_ST = SSM_GROUPS * SSM_STATE
BT = 8
KTILE = 2 * LANES
PIECE = KTILE // BT
PIECES = LANES // PIECE
UNIT_GROUPS = PIECE // SSM_GROUP
UNIT_ST = UNIT_GROUPS * SSM_STATE
N_UNITS = SSM_WIDTH // PIECE
W_UNITS_PER_STEP = LANES // PIECE
QUADS = TL // BT
QROWS = QUADS * SEQS
S5_BUFS_A = 8
S5_BUFS_B = 2
V7X_VMEM_BYTES = 64 * 1024 * 1024
VMEM_LIMIT = V7X_VMEM_BYTES - 6 * 1024 * 1024

BWD, FWD = 1, 0
W_IN_U = 0
W_IN_ZS = SSM_WIDTH
W_IN_Q = 2 * SSM_WIDTH
W_IN_K = 2 * SSM_WIDTH + ATTN_WIDTH
W_IN_ZA = W_IN_K + 2 * KV_WIDTH
D_IN = W_IN_ZA + ATTN_WIDTH


def _const_spec(shape, index=None):
    index = (0,) * len(shape) if index is None else index
    return pl.BlockSpec(shape, lambda g, t: index, pipeline_mode=pl.Buffered(1))


def _modulation_kernel(cond_ref, w_ref, b_ref, out_ref):
    a = jax.nn.silu(cond_ref[...]).astype(BF16)
    out_ref[...] = jnp.dot(a, w_ref[...].astype(BF16), preferred_element_type=F32) + b_ref[...]


def _modulation(cond, w_mod, b_mod):
    rows = cond.shape[0]
    n = w_mod.shape[1]
    tn = 1024
    return pl.pallas_call(
        _modulation_kernel,
        grid=(n // tn,),
        in_specs=[pl.BlockSpec((rows, D_MODEL), lambda j: (0, 0)),
                  pl.BlockSpec((D_MODEL, tn), lambda j: (0, j)),
                  pl.BlockSpec((1, tn), lambda j: (0, j))],
        out_specs=pl.BlockSpec((rows, tn), lambda j: (0, j)),
        out_shape=jax.ShapeDtypeStruct((rows, n), F32),
        name="modulation",
    )(cond, w_mod, b_mod)


def _zoh(lam_re, lam_im, log_dt):
    lam_re = jnp.minimum(lam_re, LAMBDA_RE_MAX)
    dt = jnp.exp(log_dt)
    mag = jnp.exp(lam_re * dt)
    ang = lam_im * dt
    lbar_re = mag * jnp.cos(ang)
    lbar_im = mag * jnp.sin(ang)
    nr = lbar_re - 1.0
    ni = lbar_im
    den = lam_re * lam_re + lam_im * lam_im
    f_re = (nr * lam_re + ni * lam_im) / den
    f_im = (ni * lam_re - nr * lam_im) / den
    return lbar_re, lbar_im, f_re, f_im


def _cmul(a, b):
    return a[0] * b[0] - a[1] * b[1], a[0] * b[1] + a[1] * b[0]


def _s5_weights_kernel(lam_ref, bt_ref, cre_ref, cim_ref, lamflat_ref, win_ref, wout_ref, wk_ref, lpow_ref):
    p_row = lax.broadcasted_iota(jnp.int32, (SSM_STATE, UNIT_ST), 0)
    p_col = lax.broadcasted_iota(jnp.int32, (SSM_STATE, UNIT_ST), 1) & (SSM_STATE - 1)
    spread = jnp.where(p_row == p_col, 1.0, 0.0).astype(BF16)
    g_row = lax.broadcasted_iota(jnp.int32, (PIECE, UNIT_ST), 0) // SSM_GROUP
    g_col = lax.broadcasted_iota(jnp.int32, (PIECE, UNIT_ST), 1) // SSM_STATE
    own_block = g_row == g_col

    def blocks(x):
        return jnp.where(own_block, jnp.dot(x.astype(BF16), spread, preferred_element_type=F32), 0.0)

    def blocks2(z):
        return blocks(z[0]).astype(BF16), blocks(z[1]).astype(BF16)

    def in_block(src, c):
        dims = (((1,), (1,)), ((), ()))
        return (lax.dot_general(src[0], c[0], dims, preferred_element_type=F32)
                - lax.dot_general(src[1], c[1], dims, preferred_element_type=F32))

    def powers(z, lbar, n):
        out = [z]
        for _ in range(n):
            out.append(_cmul(out[-1], lbar))
        return out

    pows = {}
    for d in (FWD, BWD):
        lbar_re, lbar_im, f_re, f_im = _zoh(lam_ref[0, d], lam_ref[1, d], lam_ref[2, d])
        lbar = (lbar_re, lbar_im)
        pows[d] = (powers(_cmul((f_re, f_im), (bt_ref[0, d], bt_ref[1, d])), lbar, BT - 1),
                   powers((cre_ref[d], cim_ref[d]), lbar, BT))
    for k in range(W_UNITS_PER_STEP):
        rows = slice(k * PIECE, (k + 1) * PIECE)
        unit_rows = lambda z: (z[0][rows], z[1][rows])
        k_lag = {}
        for d in (FWD, BWD):
            bb_pow = [unit_rows(z) for z in pows[d][0]]
            c_pow = [unit_rows(z) for z in pows[d][1]]
            c_b = blocks2(c_pow[0])
            bb_b = [blocks2(z) for z in bb_pow]
            to_end = [(BT - 1 - s) if d == FWD else s for s in range(BT)]
            for s in range(BT):
                w = bb_b[to_end[s]]
                win_ref[d, k, s * PIECE:(s + 1) * PIECE, 0:UNIT_ST] = w[0]
                win_ref[d, k, s * PIECE:(s + 1) * PIECE, UNIT_ST:2 * UNIT_ST] = w[1]
            out_re = jnp.concatenate([blocks(c_pow[BT - to_end[s]][0]) for s in range(BT)], axis=0)
            out_im = jnp.concatenate([blocks(-c_pow[BT - to_end[s]][1]) for s in range(BT)], axis=0)
            wout_ref[d, k, 0:UNIT_ST, :] = out_re.T.astype(BF16)
            wout_ref[d, k, UNIT_ST:2 * UNIT_ST, :] = out_im.T.astype(BF16)
            k_lag[d] = [in_block(bb_b[m], c_b) for m in range(BT)]
        for src in range(BT):
            for out in range(BT):
                if out == src:
                    kk = k_lag[FWD][0] + k_lag[BWD][0]
                elif out > src:
                    kk = k_lag[FWD][out - src]
                else:
                    kk = k_lag[BWD][src - out]
                wk_ref[k, src * PIECE:(src + 1) * PIECE, out * PIECE:(out + 1) * PIECE] = kk.astype(BF16)

    @pl.when(pl.program_id(0) == 0)
    def _():
        lbar_re, lbar_im, _, _ = _zoh(lamflat_ref[0], lamflat_ref[1], lamflat_ref[2])
        p = (lbar_re, lbar_im)
        for _ in range(BT.bit_length() - 1):
            p = _cmul(p, p)
        lpow_ref[0] = p[0]
        lpow_ref[1] = p[1]


def _s5_weights(lam_re, lam_im, log_dt, b_re, b_im, c_re, c_im):
    assert BT & (BT - 1) == 0
    n = SSM_GROUPS * SSM_GROUP
    lam3 = jnp.stack([lam_re, lam_im, jnp.broadcast_to(log_dt[:, :, None], lam_re.shape)])
    lam_rows = jnp.broadcast_to(lam3[:, :, :, None, :], (3, 2, SSM_GROUPS, SSM_GROUP, SSM_STATE)).reshape(3, 2, n, SSM_STATE)
    lam_flat = jnp.broadcast_to(lam3.reshape(3, 1, 2 * ALL_ST), (3, SEQS, 2 * ALL_ST))
    bt = jnp.transpose(jnp.stack([b_re, b_im]), (0, 1, 2, 4, 3)).reshape(2, 2, n, SSM_STATE)
    return pl.pallas_call(
        _s5_weights_kernel,
        grid=(N_UNITS // W_UNITS_PER_STEP,),
        in_specs=[pl.BlockSpec((3, 2, W_UNITS_PER_STEP * PIECE, SSM_STATE), lambda m: (0, 0, m, 0)),
                  pl.BlockSpec((2, 2, W_UNITS_PER_STEP * PIECE, SSM_STATE), lambda m: (0, 0, m, 0)),
                  pl.BlockSpec((2, W_UNITS_PER_STEP * PIECE, SSM_STATE), lambda m: (0, m, 0)),
                  pl.BlockSpec((2, W_UNITS_PER_STEP * PIECE, SSM_STATE), lambda m: (0, m, 0)),
                  pl.BlockSpec((3, SEQS, 2 * ALL_ST), lambda m: (0, 0, 0))],
        out_specs=[pl.BlockSpec((2, W_UNITS_PER_STEP, KTILE, 2 * UNIT_ST), lambda m: (0, m, 0, 0)),
                   pl.BlockSpec((2, W_UNITS_PER_STEP, 2 * UNIT_ST, KTILE), lambda m: (0, m, 0, 0)),
                   pl.BlockSpec((W_UNITS_PER_STEP, KTILE, KTILE), lambda m: (m, 0, 0)),
                   pl.BlockSpec((2, SEQS, 2 * ALL_ST), lambda m: (0, 0, 0))],
        out_shape=[jax.ShapeDtypeStruct((2, N_UNITS, KTILE, 2 * UNIT_ST), BF16),
                   jax.ShapeDtypeStruct((2, N_UNITS, 2 * UNIT_ST, KTILE), BF16),
                   jax.ShapeDtypeStruct((N_UNITS, KTILE, KTILE), BF16),
                   jax.ShapeDtypeStruct((2, SEQS, 2 * ALL_ST), F32)],
        name="s5_weights",
    )(lam_rows, bt, c_re.reshape(2, n, SSM_STATE), c_im.reshape(2, n, SSM_STATE), lam_flat)


def _rope_tables(length):
    pos = jnp.arange(length)
    row = (pos // GRID_W).astype(F32)
    col = (pos % GRID_W).astype(F32)
    freqs = ROPE_BASE ** (-jnp.arange(0, ROPE_AXIS_DIM, 2, dtype=F32) / ROPE_AXIS_DIM)
    ang_r = row[:, None] * freqs[None, :]
    ang_c = col[:, None] * freqs[None, :]
    cos = jnp.concatenate([jnp.cos(ang_r), jnp.cos(ang_r), jnp.cos(ang_c), jnp.cos(ang_c)], axis=-1)
    sin = jnp.concatenate([-jnp.sin(ang_r), jnp.sin(ang_r), -jnp.sin(ang_c), jnp.sin(ang_c)], axis=-1)
    return jnp.tile(cos, (1, 2)), jnp.tile(sin, (1, 2))


def _seq_rows(b):
    return slice(b * TL, (b + 1) * TL)


def _mod_row(b, per_sequence):
    return b if per_sequence else SEQS


def _norm_mod(h_ref, x_ref, mod_ref, nw_ref, b, per_sequence):
    xb = x_ref[b]
    ms = jnp.mean(xb * xb, axis=-1, keepdims=True)
    m = _mod_row(b, per_sequence)
    shift = mod_ref[m:m + 1, 0:D_MODEL]
    gain = nw_ref[...] * (1.0 + mod_ref[m:m + 1, D_MODEL:2 * D_MODEL])
    h_ref[_seq_rows(b), :] = ((xb * lax.rsqrt(ms + EPS)) * gain + shift).astype(BF16)


def _rope(x, cos, sin):
    lane = lax.broadcasted_iota(jnp.int32, x.shape, 1)
    first = (lane & 31) < 16
    partner = jnp.where(first, pltpu.roll(x, LANES - 16, 1), pltpu.roll(x, 16, 1))
    x3 = x.reshape(SEQS, TL, LANES)
    p3 = partner.reshape(SEQS, TL, LANES)
    return (x3 * cos[None] + p3 * sin[None]).reshape(ROWS, LANES)


def _store_column_blocks(dst_ref, val, first_seq):
    for j in range(val.shape[0] // TL):
        b = first_seq + j
        for cb in range(SSM_BLOCKS):
            dst_ref[cb, b * PITCH:b * PITCH + TL, :] = val[_seq_rows(j), cb * LANES:(cb + 1) * LANES]


def _token_rows(l):
    return l // BT, slice((l % BT) * SEQS, (l % BT + 1) * SEQS)


def _slot_rows(slot):
    return slice(slot * SEQS, (slot + 1) * SEQS)


def _to_token_major(dst_ref, src_ref):
    for l in range(TL):
        j, rows = _token_rows(l)
        for cb in range(SSM_BLOCKS):
            dst_ref[j, rows, cb * LANES:(cb + 1) * LANES] = src_ref[cb, pl.ds(l, SEQS, stride=PITCH), :]


def _to_sequence_major(dst_ref, src_ref):
    for l in range(TL):
        j, rows = _token_rows(l)
        for cb in range(SSM_BLOCKS):
            dst_ref[cb, pl.ds(l, SEQS, stride=PITCH), :] = src_ref[j, rows, cb * LANES:(cb + 1) * LANES]


def _transpose_pieces(vs):
    vs = list(vs)
    assert len(vs) == PIECES
    lane = lax.broadcasted_iota(jnp.int32, vs[0].shape, 1)
    span = LANES // 2
    while span >= PIECE:
        stride = span // PIECE
        even = (lane // span) % 2 == 0
        out = list(vs)
        for i in range(PIECES):
            if (i // stride) % 2 == 0:
                a, b = vs[i], vs[i + stride]
                out[i] = jnp.where(even, a, pltpu.roll(b, span, 1))
                out[i + stride] = jnp.where(even, pltpu.roll(a, LANES - span, 1), b)
        vs = out
        span //= 2
    return vs


def _pack_scan_operand(lhs_ref, u_tm):
    for blk in range(SSM_BLOCKS):
        lanes = slice(blk * LANES, (blk + 1) * LANES)
        for v in range(BT // PIECES):
            slots = [u_tm[:, _slot_rows(v * PIECES + j), lanes].reshape(QROWS, LANES) for j in range(PIECES)]
            for k, val in enumerate(_transpose_pieces(slots)):
                c0 = (blk * PIECES + k) * KTILE + v * LANES
                lhs_ref[:, c0:c0 + LANES] = val.astype(BF16)


def _unpack_outputs(ys):
    out = []
    for v in range(BT // PIECES):
        cols = slice(v * LANES, (v + 1) * LANES)
        for val in _transpose_pieces([y[:, cols] for y in ys]):
            out.append(val.reshape(QUADS, SEQS, LANES))
    return out


def _scan_unit(s_ref, lpow_ref, carry_ref, direction, unit, reverse):
    cols = slice(unit * UNIT_ST, (unit + 1) * UNIT_ST)
    lanes = slice(direction * ALL_ST + unit * UNIT_ST, direction * ALL_ST + (unit + 1) * UNIT_ST)
    lr = lpow_ref[0, :, lanes]
    li = lpow_ref[1, :, lanes]
    sr = carry_ref[0, :, cols]
    si = carry_ref[1, :, cols]
    for i in range(QUADS):
        j = (QUADS - 1 - i) if reverse else i
        rows = slice(j * SEQS, (j + 1) * SEQS)
        inc_r = s_ref[rows, 0:UNIT_ST]
        inc_i = s_ref[rows, UNIT_ST:2 * UNIT_ST]
        s_ref[rows, 0:UNIT_ST] = sr
        s_ref[rows, UNIT_ST:2 * UNIT_ST] = si
        sr, si = lr * sr - li * si + inc_r, lr * si + li * sr + inc_i
    carry_ref[0, :, cols] = sr
    carry_ref[1, :, cols] = si


def _s5_direction(lhs_ref, s_ref, carry_ref, win_ref, wout_ref, lpow_ref, first_ref, first_s, emit, direction,
                  wk_ref=None, between=None):
    reverse = direction == BWD

    def lhs(unit):
        return lhs_ref[:, unit * KTILE:(unit + 1) * KTILE]

    bufs = s_ref.shape[0]
    ahead = min(bufs, N_UNITS) if bufs >= N_UNITS else bufs - 1

    def project_in(unit):
        s_ref[unit % bufs] = jnp.dot(lhs(unit), win_ref[unit], preferred_element_type=F32)

    for unit in range(min(ahead, N_UNITS)):
        project_in(unit)
    pending = []
    for unit in range(N_UNITS):
        if ahead <= unit + ahead < N_UNITS:
            project_in(unit + ahead)
        buf = s_ref.at[unit % bufs]
        _scan_unit(buf, lpow_ref, carry_ref, direction, unit, reverse)
        y = jnp.dot(buf[...].astype(BF16), wout_ref[unit], preferred_element_type=F32)
        if wk_ref is not None:
            y = y + jnp.dot(lhs(unit), wk_ref[unit], preferred_element_type=F32)
        pending.append(y)
        if len(pending) == PIECES:
            emit(unit // PIECES, _unpack_outputs(pending))
            pending = []
        if between is not None:
            between(unit)

    if first_ref is not None:
        first_rows = slice(QROWS - 2 * SEQS, QROWS) if reverse else slice(0, 2 * SEQS)
        lane = lax.broadcasted_iota(jnp.int32, (2 * SEQS, LANES), 1)
        zeros = jnp.zeros((2 * SEQS, LANES), F32)
        other = [zeros] * (KTILE // LANES - 1)
        for unit in range(N_UNITS):
            if reverse:
                tail = lhs_ref[first_rows, (unit + 1) * KTILE - LANES:(unit + 1) * KTILE].astype(F32)
                block = [jnp.where(lane < PIECE, pltpu.roll(tail, PIECE, 1), 0.0)] + other
            else:
                head = lhs_ref[first_rows, unit * KTILE:unit * KTILE + LANES].astype(F32)
                block = other + [jnp.where(lane >= LANES - PIECE, pltpu.roll(head, LANES - PIECE, 1), 0.0)]
            inc = jnp.dot(jnp.concatenate(block, axis=1).astype(BF16), win_ref[unit], preferred_element_type=F32)
            inc = inc[SEQS:2 * SEQS] if reverse else inc[0:SEQS]
            first_s[0, :, unit * UNIT_ST:(unit + 1) * UNIT_ST] = inc[:, 0:UNIT_ST]
            first_s[1, :, unit * UNIT_ST:(unit + 1) * UNIT_ST] = inc[:, UNIT_ST:2 * UNIT_ST]

        @pl.when(pl.program_id(1) == 0)
        def _():
            first_ref[...] = first_s[...]


def _s5_param_specs(direction):
    return [_const_spec((None, N_UNITS, KTILE, 2 * UNIT_ST), (direction, 0, 0, 0)),
            _const_spec((None, N_UNITS, 2 * UNIT_ST, KTILE), (direction, 0, 0, 0)),
            _const_spec((2, SEQS, 2 * ALL_ST))]


def _h0_specs(direction):
    return [_const_spec((SEQS, ALL_ST), (0, direction))] * 2


def _pass_a_kernel(*refs, rotary):
    if rotary:
        (x_ref, mod_ref, nw_ref, w_ref, win_ref, wout_ref, lpow_ref, wk_ref, d_ref, h0re_ref, h0im_ref, cos_ref, sin_ref,
         kcat_ref, vt_ref, lhs_ref, yb_ref, zs_ref, za_ref, ql_ref, qr_ref, h_s, u_sm, u_tm, s_s, carry) = refs
        fin_ref = first_s = None
    else:
        (x_ref, mod_ref, nw_ref, w_ref, win_ref, wout_ref, lpow_ref, wk_ref, d_ref,
         k_ref, v_ref, kcat_ref, vt_ref, lhs_ref, yb_ref, zs_ref, za_ref, ql_ref, qr_ref, fin_ref,
         h_s, u_sm, u_tm, s_s, carry, first_s) = refs

    @pl.when(pl.program_id(1) == 0)
    def _():
        if rotary:
            carry[0] = h0re_ref[...]
            carry[1] = h0im_ref[...]
        else:
            carry[...] = jnp.zeros_like(carry)

    half_seqs = SEQS // 2
    for hf in range(2):
        for b in range(hf * half_seqs, (hf + 1) * half_seqs):
            _norm_mod(h_s, x_ref, mod_ref, nw_ref, b, per_sequence=rotary)
        rows = slice(hf * half_seqs * TL, (hf + 1) * half_seqs * TL)
        u_half = jnp.dot(h_s[rows, :], w_ref[:, W_IN_U:W_IN_U + SSM_WIDTH], preferred_element_type=F32)
        _store_column_blocks(u_sm, u_half, first_seq=hf * half_seqs)
    _to_token_major(u_tm, u_sm)
    _pack_scan_operand(lhs_ref, u_tm)

    def project(c0, width):
        return jnp.dot(h_s[...], w_ref[:, c0:c0 + width], preferred_element_type=F32)

    def project_kv():
        kv = project(W_IN_K, 2 * KV_WIDTH)
        k = kv[:, 0:KV_WIDTH]
        v = kv[:, KV_WIDTH:2 * KV_WIDTH]
        if rotary:
            k = _rope(k, cos_ref[...], sin_ref[...])
        else:
            k_ref[...] = k.reshape(SEQS, TL, KV_WIDTH)
            v_ref[...] = v.reshape(SEQS, TL, KV_WIDTH)
        kcat = jnp.concatenate([k, pltpu.roll(k, HEAD_DIM, 1)], axis=1).astype(BF16)
        kcat_ref[...] = kcat.reshape(SEQS, TL, 2 * KV_WIDTH)
        for b in range(SEQS):
            vt_ref[b] = v[_seq_rows(b), :].T.astype(BF16)

    def project_q():
        qall = project(W_IN_Q, ATTN_WIDTH)
        even_head = lax.broadcasted_iota(jnp.int32, (ROWS, LANES), 1) < HEAD_DIM
        for cg in range(ATTN_WIDTH // LANES):
            cols = slice(cg * LANES, (cg + 1) * LANES)
            q = qall[:, cols]
            if rotary:
                q = _rope(q, cos_ref[...], sin_ref[...])
            q = q * (LOG2E * HEAD_DIM ** -0.5)
            ql_ref[:, cols] = jnp.where(even_head, q, 0.0).astype(BF16)
            qr_ref[:, cols] = jnp.where(even_head, 0.0, q).astype(BF16)

    def project_gate(c0, out_ref):
        out_ref[...] = jax.nn.silu(project(c0, out_ref.shape[1])).astype(BF16)

    side_work = [project_kv, project_q, lambda: project_gate(W_IN_ZS, zs_ref), lambda: project_gate(W_IN_ZA, za_ref)]

    def between(i):
        if i < len(side_work):
            side_work[i]()

    def emit(blk, slots):
        lanes = slice(blk * LANES, (blk + 1) * LANES)
        for slot, ys in enumerate(slots):
            rows = _slot_rows(slot)
            yb_ref[:, rows, lanes] = ys + u_tm[:, rows, lanes] * d_ref[:, lanes]

    _s5_direction(lhs_ref, s_s, carry, win_ref, wout_ref, lpow_ref, fin_ref, first_s, emit, BWD,
                  wk_ref=wk_ref, between=between)


def _pass_a(x, mod, norm_w, w_in, win, wout, lpow, wk, d, h0, rope):
    batch, length, _ = x.shape
    ng, nt = batch // SEQS, length // TL
    rotary = rope is not None
    rev = lambda t: nt - 1 - t
    in_specs = [
        pl.BlockSpec((SEQS, TL, D_MODEL), lambda g, t: (g, rev(t), 0)),
        _const_spec(mod.shape),
        _const_spec((1, D_MODEL)),
        _const_spec((D_MODEL, D_IN)),
    ] + _s5_param_specs(BWD) + [
        _const_spec((N_UNITS, KTILE, KTILE)),
        _const_spec((1, SSM_WIDTH)),
    ]
    args = [x, mod, norm_w, w_in, win, wout, lpow, wk, d]
    if rotary:
        in_specs += _h0_specs(BWD) + [pl.BlockSpec((TL, LANES), lambda g, t: (rev(t), 0))] * 2
        args += list(h0) + list(rope)
    assert ATTN_WIDTH == SSM_WIDTH
    tile_spec = pl.BlockSpec((None, None, ROWS, SSM_WIDTH), lambda g, t: (g, rev(t), 0, 0))
    tile_bf16 = jax.ShapeDtypeStruct((ng, nt, ROWS, SSM_WIDTH), BF16)
    out_specs = [
        pl.BlockSpec((SEQS, TL, 2 * KV_WIDTH), lambda g, t: (g, rev(t), 0)),
        pl.BlockSpec((SEQS, KV_WIDTH, TL), lambda g, t: (g, 0, rev(t))),
        pl.BlockSpec((None, None, QROWS, N_UNITS * KTILE), lambda g, t: (g, rev(t), 0, 0)),
        pl.BlockSpec((None, None, QUADS, BT * SEQS, SSM_WIDTH), lambda g, t: (g, rev(t), 0, 0, 0)),
    ] + [tile_spec] * 4
    out_shape = [
        jax.ShapeDtypeStruct((batch, length, 2 * KV_WIDTH), BF16),
        jax.ShapeDtypeStruct((batch, KV_WIDTH, length), BF16),
        jax.ShapeDtypeStruct((ng, nt, QROWS, N_UNITS * KTILE), BF16),
        jax.ShapeDtypeStruct((ng, nt, QUADS, BT * SEQS, SSM_WIDTH), F32),
        tile_bf16, tile_bf16,
        tile_bf16, tile_bf16,
    ]
    scratch = [
        pltpu.VMEM((ROWS, D_MODEL), BF16),
        pltpu.VMEM((SSM_BLOCKS, SEQS * PITCH, LANES), F32),
        pltpu.VMEM((QUADS, BT * SEQS, SSM_WIDTH), F32),
        pltpu.VMEM((S5_BUFS_A, QROWS, 2 * UNIT_ST), F32),
        pltpu.VMEM((2, SEQS, ALL_ST), F32),
    ]
    if not rotary:
        cache_spec = pl.BlockSpec((SEQS, TL, KV_WIDTH), lambda g, t: (g, rev(t), 0))
        cache_shape = jax.ShapeDtypeStruct((batch, length, KV_WIDTH), F32)
        out_specs = [cache_spec, cache_spec] + out_specs + [pl.BlockSpec((None, 2, SEQS, ALL_ST), lambda g, t: (g, 0, 0, 0))]
        out_shape = [cache_shape, cache_shape] + out_shape + [jax.ShapeDtypeStruct((ng, 2, SEQS, ALL_ST), F32)]
        scratch += [pltpu.VMEM((2, SEQS, ALL_ST), F32)]
    return pl.pallas_call(
        functools.partial(_pass_a_kernel, rotary=rotary),
        grid=(ng, nt),
        in_specs=in_specs, out_specs=out_specs, out_shape=out_shape, scratch_shapes=scratch,
        compiler_params=pltpu.CompilerParams(
            dimension_semantics=("arbitrary", "arbitrary"), vmem_limit_bytes=VMEM_LIMIT),
        name="pass_a_latent" if rotary else "pass_a_context",
    )(*args)


def _attention(b, q_refs, za_ref, mix_s, kcat_refs, vt_refs, sink_ref, masks):
    rows = _seq_rows(b)
    kcat = jnp.concatenate([r[b] for r in kcat_refs], axis=0)
    vt = jnp.concatenate([r[b] for r in vt_refs], axis=1)
    segment = lax.broadcasted_iota(jnp.int32, (1, 4 * TL), 1) // TL
    kv_heads = range(N_KV_HEADS)

    def side(g, grp):
        return g ^ grp

    def head(g, i, grp):
        return 4 * g + 2 * i + side(g, grp)

    def scores(grp):
        q = jnp.concatenate([q_refs[side(g, grp)][rows, blk * LANES:(blk + 1) * LANES]
                             for g in kv_heads for blk in (2 * g, 2 * g + 1)], axis=0)
        return lax.dot_general(kcat[:, grp * KV_WIDTH:(grp + 1) * KV_WIDTH], q, (((1,), (1,)), ((), ())),
                               preferred_element_type=F32)

    def softmax(grp, s):
        if masks is None:
            pieces = [s]
        else:
            pieces = [jnp.where(masks[0], s[0:TL], NEG_INF), s[TL:2 * TL],
                      jnp.where(masks[1], s[2 * TL:3 * TL], NEG_INF), s[3 * TL:]]
        sinks = [sink_ref[head(g, i, grp)] * LOG2E for g in kv_heads for i in range(2)]
        sink = jnp.where(segment == 0, sinks[0], jnp.where(segment == 1, sinks[1],
                                                           jnp.where(segment == 2, sinks[2], sinks[3])))
        m = sink
        for piece in pieces:
            m = jnp.maximum(m, jnp.max(piece, axis=0, keepdims=True))
        ps = [jnp.exp2(piece - m).astype(BF16) for piece in pieces]
        return (ps[0] if len(ps) == 1 else jnp.concatenate(ps, axis=0)), jnp.exp2(sink - m)

    ones_rows = jnp.ones((BF16_SUBLANES, vt.shape[1]), BF16)

    all_scores = [scores(grp) for grp in range(2)]
    probs, sink_terms = zip(*[softmax(grp, s) for grp, s in enumerate(all_scores)])
    for g in kv_heads:
        cols = slice(g * 2 * TL, (g + 1) * 2 * TL)
        values = jnp.concatenate([vt[g * HEAD_DIM:(g + 1) * HEAD_DIM, :], ones_rows], axis=0)
        p = jnp.concatenate([probs[0][:, cols], probs[1][:, cols]], axis=1)
        sink_term = jnp.concatenate([sink_terms[0][:, cols], sink_terms[1][:, cols]], axis=1)
        ot = jnp.dot(values, p, preferred_element_type=F32)
        out_t = ot[0:HEAD_DIM] * (1.0 / (ot[HEAD_DIM:HEAD_DIM + 1] + sink_term))
        for i in range(2):
            left = out_t[:, (2 * g + i) * TL:(2 * g + i + 1) * TL]
            right = out_t[:, (2 * (g ^ 1) + i) * TL:(2 * (g ^ 1) + i + 1) * TL]
            blk_cols = slice((2 * g + i) * LANES, (2 * g + i + 1) * LANES)
            pair_t = jnp.concatenate([left, right], axis=0)
            gated = pair_t.T * za_ref[rows, blk_cols].astype(F32)
            mix_s[rows, ATTN_WIDTH + blk_cols.start:ATTN_WIDTH + blk_cols.stop] = gated.astype(BF16)


def _pass_b_kernel(*refs, windowed, n_tiles):
    if windowed:
        (sink_ref, x_ref, mod_ref, win_ref, wout_ref, lpow_ref, wg_ref, bg_ref,
         lhs_ref, yb_ref, zs_ref, za_ref, ql_ref, qr_ref, kx_ref, vx_ref, wo_ref, fw_ref, h0re_ref, h0im_ref,
         kp_ref, kc_ref, kn_ref, vp_ref, vc_ref, vn_ref,
         y_ref, y_sm, y_tm, s_s, carry, mix_s) = refs
        fin_ref = first_s = None
    else:
        (sink_ref, x_ref, mod_ref, win_ref, wout_ref, lpow_ref, wg_ref, bg_ref,
         lhs_ref, yb_ref, zs_ref, za_ref, ql_ref, qr_ref, kx_ref, vx_ref, wo_ref, fw_ref,
         y_ref, fin_ref, y_sm, y_tm, s_s, carry, mix_s, first_s) = refs
    t = pl.program_id(1)

    @pl.when(t == 0)
    def _():
        if windowed:
            carry[0] = h0re_ref[...]
            carry[1] = h0im_ref[...]
        else:
            carry[...] = jnp.zeros_like(carry)

    if windowed:
        kj = lax.broadcasted_iota(jnp.int32, (TL, 4 * TL), 0)
        qi = lax.broadcasted_iota(jnp.int32, (TL, 4 * TL), 1) & (TL - 1)
        masks = ((kj >= qi) & (t > 0), (kj <= qi) & (t < n_tiles - 1))
        key_refs = (kp_ref, kc_ref, kn_ref, kx_ref)
        val_refs = (vp_ref, vc_ref, vn_ref, vx_ref)
    else:
        masks = None
        key_refs = (kx_ref,)
        val_refs = (vx_ref,)

    def attention(b):
        _attention(b, (ql_ref, qr_ref), za_ref, mix_s, key_refs, val_refs, sink_ref, masks)

    def emit(blk, slots):
        lanes = slice(blk * LANES, (blk + 1) * LANES)
        for slot, ys in enumerate(slots):
            rows = _slot_rows(slot)
            y_tm[:, rows, lanes] = ys + yb_ref[:, rows, lanes]

    glu_blocks = QUADS // 4

    def glu(rc):
        pr = slice(rc * glu_blocks, (rc + 1) * glu_blocks)
        y = jax.nn.gelu(y_tm[pr].reshape(glu_blocks * BT * SEQS, SSM_WIDTH))
        gl = jnp.dot(y.astype(BF16), wg_ref[...], preferred_element_type=F32) + bg_ref[...]
        y_tm[pr] = (y * jax.nn.sigmoid(gl)).reshape(glu_blocks, BT * SEQS, SSM_WIDTH)

    def gate_s5(seqs):
        for b in seqs:
            for cb in range(SSM_BLOCKS):
                cols = slice(cb * LANES, (cb + 1) * LANES)
                gated = y_sm[cb, b * PITCH:b * PITCH + TL, :] * zs_ref[_seq_rows(b), cols].astype(F32)
                mix_s[_seq_rows(b), cols] = gated.astype(BF16)

    half_seqs = SEQS // 2

    def project_out(hf):
        rows = slice(hf * half_seqs * TL, (hf + 1) * half_seqs * TL)
        proj = jnp.dot(mix_s[rows, :], wo_ref[...], preferred_element_type=F32)
        y_ref[hf * half_seqs:(hf + 1) * half_seqs] = proj.reshape(half_seqs, TL, D_MODEL)

    def finish(seqs):
        for b in seqs:
            m = _mod_row(b, per_sequence=windowed)
            gate = mod_ref[m:m + 1, 2 * D_MODEL:3 * D_MODEL]
            r = x_ref[b] + gate * y_ref[b]
            ms = jnp.mean(r * r, axis=-1, keepdims=True)
            y_ref[b] = (r * lax.rsqrt(ms + EPS)) * fw_ref[...]

    first_half, second_half = range(0, half_seqs), range(half_seqs, SEQS)

    _s5_direction(lhs_ref, s_s, carry, win_ref, wout_ref, lpow_ref, fin_ref, first_s, emit, FWD)
    assert SEQS == 8
    follow_up = [
        lambda: glu(0), lambda: glu(1), lambda: glu(2),
        lambda: (glu(3), _to_sequence_major(y_sm, y_tm)),
        lambda: (gate_s5(first_half), project_out(0)),
        lambda: finish(first_half),
        lambda: gate_s5(second_half),
        lambda: (project_out(1), finish(second_half)),
    ]
    for b in range(SEQS):
        attention(b)
        follow_up[b]()


def _pass_b(x, mod, win, wout, lpow, w_glu, b_glu, from_a, kx, vx, w_out, fnorm_w, sink, h0, k_loc, v_loc):
    batch, length, _ = x.shape
    ng, nt = batch // SEQS, length // TL
    windowed = k_loc is not None
    tile_spec = pl.BlockSpec((None, None, ROWS, SSM_WIDTH), lambda g, t: (g, t, 0, 0))
    in_specs = [
        pl.BlockSpec(memory_space=pltpu.SMEM),
        pl.BlockSpec((SEQS, TL, D_MODEL), lambda g, t: (g, t, 0)),
        _const_spec(mod.shape),
    ] + _s5_param_specs(FWD) + [
        _const_spec((SSM_WIDTH, SSM_WIDTH)),
        _const_spec((1, SSM_WIDTH)),
        pl.BlockSpec((None, None, QROWS, N_UNITS * KTILE), lambda g, t: (g, t, 0, 0)),
        pl.BlockSpec((None, None, QUADS, BT * SEQS, SSM_WIDTH), lambda g, t: (g, t, 0, 0, 0)),
    ] + [tile_spec] * 4 + [
        pl.BlockSpec((SEQS, PAST_LEN, 2 * KV_WIDTH), lambda g, t: (g, 0, 0)),
        pl.BlockSpec((SEQS, KV_WIDTH, PAST_LEN), lambda g, t: (g, 0, 0)),
        _const_spec((D_MODEL, D_MODEL)),
        _const_spec((1, D_MODEL)),
    ]
    args = [sink, x, mod, win, wout, lpow, w_glu, b_glu, *from_a, kx, vx, w_out, fnorm_w]
    if windowed:
        in_specs += _h0_specs(FWD)
        args += list(h0)
        band_t = (lambda t: jnp.maximum(t - 1, 0), lambda t: t, lambda t: jnp.minimum(t + 1, nt - 1))
        in_specs += [pl.BlockSpec((SEQS, TL, 2 * KV_WIDTH), lambda g, t, f=f: (g, f(t), 0)) for f in band_t]
        in_specs += [pl.BlockSpec((SEQS, KV_WIDTH, TL), lambda g, t, f=f: (g, 0, f(t))) for f in band_t]
        args += [k_loc] * 3 + [v_loc] * 3
    out_specs = [pl.BlockSpec((SEQS, TL, D_MODEL), lambda g, t: (g, t, 0))]
    out_shape = [jax.ShapeDtypeStruct((batch, length, D_MODEL), F32)]
    scratch = [
        pltpu.VMEM((SSM_BLOCKS, SEQS * PITCH, LANES), F32),
        pltpu.VMEM((QUADS, BT * SEQS, SSM_WIDTH), F32),
        pltpu.VMEM((S5_BUFS_B, QROWS, 2 * UNIT_ST), F32),
        pltpu.VMEM((2, SEQS, ALL_ST), F32),
        pltpu.VMEM((ROWS, D_MODEL), BF16),
    ]
    if not windowed:
        out_specs += [pl.BlockSpec((None, 2, SEQS, ALL_ST), lambda g, t: (g, 0, 0, 0))]
        out_shape += [jax.ShapeDtypeStruct((ng, 2, SEQS, ALL_ST), F32)]
        scratch += [pltpu.VMEM((2, SEQS, ALL_ST), F32)]
    return pl.pallas_call(
        functools.partial(_pass_b_kernel, windowed=windowed, n_tiles=nt),
        grid=(ng, nt),
        in_specs=in_specs, out_specs=out_specs, out_shape=out_shape, scratch_shapes=scratch,
        compiler_params=pltpu.CompilerParams(
            dimension_semantics=("arbitrary", "arbitrary"), vmem_limit_bytes=VMEM_LIMIT),
        name="pass_b_latent" if windowed else "pass_b_context",
    )(*args)


def kernel(x_prompt, x_sample, c, cache_k, cache_v, state_ssm_re, state_ssm_im, c_ctx, norm_w, w_mod, b_mod, w_in, ssm_lambda_re, ssm_lambda_im, ssm_log_dt, ssm_b_re, ssm_b_im, ssm_c_re, ssm_c_im, ssm_d, w_glu, b_glu, attn_sink, w_out, final_norm_w):
    assert norm_w.shape[0] == 1, "single trunk layer"
    batch, seq, _ = x_prompt.shape
    dec_batch, dec_seq, _ = x_sample.shape
    assert dec_batch == SEQS and batch % SEQS == 0 and seq % TL == 0 and dec_seq % TL == 0
    assert seq == PAST_LEN and cache_k.shape[2] == PAST_LEN

    w_in0 = w_in[0].astype(BF16)
    nw = norm_w[0][None, :]
    fw = final_norm_w[None, :]
    d = ssm_d[0][None, :]
    wg = w_glu[0].astype(BF16)
    bg = b_glu[0][None, :]
    wo = w_out[0].astype(BF16)
    sink = attn_sink[0]

    cond = jnp.concatenate([c, c_ctx[None, :], jnp.zeros((16 - SEQS - 1, D_MODEL), F32)], axis=0)
    mod = _modulation(cond, w_mod[0], b_mod[0][None, :])

    win, wout, wk, lpow = _s5_weights(ssm_lambda_re[0], ssm_lambda_im[0], ssm_log_dt[0], ssm_b_re[0], ssm_b_im[0],
                                     ssm_c_re[0], ssm_c_im[0])
    h0 = (state_ssm_re.reshape(SEQS, 2 * ALL_ST), state_ssm_im.reshape(SEQS, 2 * ALL_ST))
    rope = _rope_tables(dec_seq)

    k_ctx, v_ctx, kcat_ctx, vt_ctx, *tiles_ctx, fin_b = _pass_a(
        x_prompt, mod, nw, w_in0, win, wout, lpow, wk, d, None, None)
    y_prompt, fin_f = _pass_b(x_prompt, mod, win, wout, lpow, wg, bg, tiles_ctx, kcat_ctx, vt_ctx, wo, fw, sink,
                              None, None, None)

    kcat_lat, vt_lat, *tiles_lat = _pass_a(x_sample, mod, nw, w_in0, win, wout, lpow, wk, d, h0, rope)
    kx = cache_k[:, 0].reshape(dec_batch, PAST_LEN, KV_WIDTH)
    kx = jnp.concatenate([kx, jnp.roll(kx, HEAD_DIM, axis=-1)], axis=-1).astype(BF16)
    vx = jnp.swapaxes(cache_v[:, 0].reshape(dec_batch, PAST_LEN, KV_WIDTH), 1, 2).astype(BF16)
    (y_sample,) = _pass_b(x_sample, mod, win, wout, lpow, wg, bg, tiles_lat, kx, vx, wo, fw, sink,
                          h0, kcat_lat, vt_lat)

    new_cache_k = k_ctx.reshape(batch, 1, seq, N_KV_HEADS, HEAD_DIM)
    new_cache_v = v_ctx.reshape(batch, 1, seq, N_KV_HEADS, HEAD_DIM)

    def states(fin, part):
        return fin[:, part].reshape(batch, SSM_GROUPS, SSM_STATE)

    new_re = jnp.stack([states(fin_f, 0), states(fin_b, 0)], axis=1)[:, None]
    new_im = jnp.stack([states(fin_f, 1), states(fin_b, 1)], axis=1)[:, None]
    return (y_prompt, y_sample, new_cache_k, new_cache_v, new_re, new_im)
```

```python
import functools
import math

import jax
import jax.numpy as jnp
from jax import lax
from jax.experimental import pallas as pl
from jax.experimental.pallas import tpu as pltpu

F32 = jnp.float32
BF16 = jnp.bfloat16

D_MODEL = 1024
SSM_WIDTH = 512
ATTN_WIDTH = 512
SSM_GROUP = 16
SSM_GROUPS = 32
SSM_STATE = 64
HEAD_DIM = 64
N_KV_HEADS = 2
KV_WIDTH = 128
GRID_W = 64
ROPE_AXIS_DIM = 32
ROPE_BASE = 10000.0
EPS = 1e-6
LAMBDA_RE_MAX = -1e-4
NEG_INF = -1e30
LOG2E = math.log2(math.e)
PAST_LEN = 256

LANES = 128
BF16_SUBLANES = 16
SEQS = 8
TL = 128
ROWS = SEQS * TL
PITCH = TL + 8
SSM_BLOCKS = SSM_WIDTH // LANES
ALL_ST = SSM_GROUPS * SSM_STATE
BT = 8
KTILE = 2 * LANES
PIECE = KTILE // BT
PIECES = LANES // PIECE
UNIT_GROUPS = PIECE // SSM_GROUP
UNIT_ST = UNIT_GROUPS * SSM_STATE
N_UNITS = SSM_WIDTH // PIECE
W_UNITS_PER_STEP = LANES // PIECE
QUADS = TL // BT
QROWS = QUADS * SEQS
S5_BUFS_A = 8
S5_BUFS_B = 2
V7X_VMEM_BYTES = 64 * 1024 * 1024
VMEM_LIMIT = V7X_VMEM_BYTES - 6 * 1024 * 1024

BWD, FWD = 1, 0
W_IN_U = 0
W_IN_ZS = SSM_WIDTH
W_IN_Q = 2 * SSM_WIDTH
W_IN_K = 2 * SSM_WIDTH + ATTN_WIDTH
W_IN_ZA = W_IN_K + 2 * KV_WIDTH
D_IN = W_IN_ZA + ATTN_WIDTH


def _const_spec(shape, index=None):
    index = (0,) * len(shape) if index is None else index
    return pl.BlockSpec(shape, lambda g, t: index, pipeline_mode=pl.Buffered(1))


def _modulation_kernel(cond_ref, w_ref, b_ref, out_ref):
    a = jax.nn.silu(cond_ref[...]).astype(BF16)
    out_ref[...] = jnp.dot(a, w_ref[...].astype(BF16), preferred_element_type=F32) + b_ref[...]


def _modulation(cond, w_mod, b_mod):
    rows = cond.shape[0]
    n = w_mod.shape[1]
    tn = 1024
    return pl.pallas_call(
        _modulation_kernel,
        grid=(n // tn,),
        in_specs=[pl.BlockSpec((rows, D_MODEL), lambda j: (0, 0)),
                  pl.BlockSpec((D_MODEL, tn), lambda j: (0, j)),
                  pl.BlockSpec((1, tn), lambda j: (0, j))],
        out_specs=pl.BlockSpec((rows, tn), lambda j: (0, j)),
        out_shape=jax.ShapeDtypeStruct((rows, n), F32),
        name="modulation",
    )(cond, w_mod, b_mod)


def _zoh(lam_re, lam_im, log_dt):
    lam_re = jnp.minimum(lam_re, LAMBDA_RE_MAX)
    dt = jnp.exp(log_dt)
    mag = jnp.exp(lam_re * dt)
    ang = lam_im * dt
    lbar_re = mag * jnp.cos(ang)
    lbar_im = mag * jnp.sin(ang)
    nr = lbar_re - 1.0
    ni = lbar_im
    den = lam_re * lam_re + lam_im * lam_im
    f_re = (nr * lam_re + ni * lam_im) / den
    f_im = (ni * lam_re - nr * lam_im) / den
    return lbar_re, lbar_im, f_re, f_im


def _cmul(a, b):
    return a[0] * b[0] - a[1] * b[1], a[0] * b[1] + a[1] * b[0]


def _s5_weights_kernel(lam_ref, bt_ref, cre_ref, cim_ref, lamflat_ref, win_ref, wout_ref, wk_ref, lpow_ref):
    p_row = lax.broadcasted_iota(jnp.int32, (SSM_STATE, UNIT_ST), 0)
    p_col = lax.broadcasted_iota(jnp.int32, (SSM_STATE, UNIT_ST), 1) & (SSM_STATE - 1)
    spread = jnp.where(p_row == p_col, 1.0, 0.0).astype(BF16)
    g_row = lax.broadcasted_iota(jnp.int32, (PIECE, UNIT_ST), 0) // SSM_GROUP
    g_col = lax.broadcasted_iota(jnp.int32, (PIECE, UNIT_ST), 1) // SSM_STATE
    own_block = g_row == g_col

    def blocks(x):
        return jnp.where(own_block, jnp.dot(x.astype(BF16), spread, preferred_element_type=F32), 0.0)

    def blocks2(z):
        return blocks(z[0]).astype(BF16), blocks(z[1]).astype(BF16)

    def in_block(src, c):
        dims = (((1,), (1,)), ((), ()))
        return (lax.dot_general(src[0], c[0], dims, preferred_element_type=F32)
                - lax.dot_general(src[1], c[1], dims, preferred_element_type=F32))

    def powers(z, lbar, n):
        out = [z]
        for _ in range(n):
            out.append(_cmul(out[-1], lbar))
        return out

    pows = {}
    for d in (FWD, BWD):
        lbar_re, lbar_im, f_re, f_im = _zoh(lam_ref[0, d], lam_ref[1, d], lam_ref[2, d])
        lbar = (lbar_re, lbar_im)
        pows[d] = (powers(_cmul((f_re, f_im), (bt_ref[0, d], bt_ref[1, d])), lbar, BT - 1),
                   powers((cre_ref[d], cim_ref[d]), lbar, BT))
    for k in range(W_UNITS_PER_STEP):
        rows = slice(k * PIECE, (k + 1) * PIECE)
        unit_rows = lambda z: (z[0][rows], z[1][rows])
        k_lag = {}
        for d in (FWD, BWD):
            bb_pow = [unit_rows(z) for z in pows[d][0]]
            c_pow = [unit_rows(z) for z in pows[d][1]]
            c_b = blocks2(c_pow[0])
            bb_b = [blocks2(z) for z in bb_pow]
            to_end = [(BT - 1 - s) if d == FWD else s for s in range(BT)]
            for s in range(BT):
                w = bb_b[to_end[s]]
                win_ref[d, k, s * PIECE:(s + 1) * PIECE, 0:UNIT_ST] = w[0]
                win_ref[d, k, s * PIECE:(s + 1) * PIECE, UNIT_ST:2 * UNIT_ST] = w[1]
            out_re = jnp.concatenate([blocks(c_pow[BT - to_end[s]][0]) for s in range(BT)], axis=0)
            out_im = jnp.concatenate([blocks(-c_pow[BT - to_end[s]][1]) for s in range(BT)], axis=0)
            wout_ref[d, k, 0:UNIT_ST, :] = out_re.T.astype(BF16)
            wout_ref[d, k, UNIT_ST:2 * UNIT_ST, :] = out_im.T.astype(BF16)
            k_lag[d] = [in_block(bb_b[m], c_b) for m in range(BT)]
        for src in range(BT):
            for out in range(BT):
                if out == src:
                    kk = k_lag[FWD][0] + k_lag[BWD][0]
                elif out > src:
                    kk = k_lag[FWD][out - src]
                else:
                    kk = k_lag[BWD][src - out]
                wk_ref[k, src * PIECE:(src + 1) * PIECE, out * PIECE:(out + 1) * PIECE] = kk.astype(BF16)

    @pl.when(pl.program_id(0) == 0)
    def _():
        lbar_re, lbar_im, _, _ = _zoh(lamflat_ref[0], lamflat_ref[1], lamflat_ref[2])
        p = (lbar_re, lbar_im)
        for _ in range(BT.bit_length() - 1):
            p = _cmul(p, p)
        lpow_ref[0] = p[0]
        lpow_ref[1] = p[1]


def _s5_weights(lam_re, lam_im, log_dt, b_re, b_im, c_re, c_im):
    assert BT & (BT - 1) == 0
    n = SSM_GROUPS * SSM_GROUP
    lam3 = jnp.stack([lam_re, lam_im, jnp.broadcast_to(log_dt[:, :, None], lam_re.shape)])
    lam_rows = jnp.broadcast_to(lam3[:, :, :, None, :], (3, 2, SSM_GROUPS, SSM_GROUP, SSM_STATE)).reshape(3, 2, n, SSM_STATE)
    lam_flat = jnp.broadcast_to(lam3.reshape(3, 1, 2 * ALL_ST), (3, SEQS, 2 * ALL_ST))
    bt = jnp.transpose(jnp.stack([b_re, b_im]), (0, 1, 2, 4, 3)).reshape(2, 2, n, SSM_STATE)
    return pl.pallas_call(
        _s5_weights_kernel,
        grid=(N_UNITS // W_UNITS_PER_STEP,),
        in_specs=[pl.BlockSpec((3, 2, W_UNITS_PER_STEP * PIECE, SSM_STATE), lambda m: (0, 0, m, 0)),
                  pl.BlockSpec((2, 2, W_UNITS_PER_STEP * PIECE, SSM_STATE), lambda m: (0, 0, m, 0)),
                  pl.BlockSpec((2, W_UNITS_PER_STEP * PIECE, SSM_STATE), lambda m: (0, m, 0)),
                  pl.BlockSpec((2, W_UNITS_PER_STEP * PIECE, SSM_STATE), lambda m: (0, m, 0)),
                  pl.BlockSpec((3, SEQS, 2 * ALL_ST), lambda m: (0, 0, 0))],
        out_specs=[pl.BlockSpec((2, W_UNITS_PER_STEP, KTILE, 2 * UNIT_ST), lambda m: (0, m, 0, 0)),
                   pl.BlockSpec((2, W_UNITS_PER_STEP, 2 * UNIT_ST, KTILE), lambda m: (0, m, 0, 0)),
                   pl.BlockSpec((W_UNITS_PER_STEP, KTILE, KTILE), lambda m: (m, 0, 0)),
                   pl.BlockSpec((2, SEQS, 2 * ALL_ST), lambda m: (0, 0, 0))],
        out_shape=[jax.ShapeDtypeStruct((2, N_UNITS, KTILE, 2 * UNIT_ST), BF16),
                   jax.ShapeDtypeStruct((2, N_UNITS, 2 * UNIT_ST, KTILE), BF16),
                   jax.ShapeDtypeStruct((N_UNITS, KTILE, KTILE), BF16),
                   jax.ShapeDtypeStruct((2, SEQS, 2 * ALL_ST), F32)],
        name="s5_weights",
    )(lam_rows, bt, c_re.reshape(2, n, SSM_STATE), c_im.reshape(2, n, SSM_STATE), lam_flat)


def _rope_tables(length):
    pos = jnp.arange(length)
    row = (pos // GRID_W).astype(F32)
    col = (pos % GRID_W).astype(F32)
    freqs = ROPE_BASE ** (-jnp.arange(0, ROPE_AXIS_DIM, 2, dtype=F32) / ROPE_AXIS_DIM)
    ang_r = row[:, None] * freqs[None, :]
    ang_c = col[:, None] * freqs[None, :]
    cos = jnp.concatenate([jnp.cos(ang_r), jnp.cos(ang_r), jnp.cos(ang_c), jnp.cos(ang_c)], axis=-1)
    sin = jnp.concatenate([-jnp.sin(ang_r), jnp.sin(ang_r), -jnp.sin(ang_c), jnp.sin(ang_c)], axis=-1)
    return jnp.tile(cos, (1, 2)), jnp.tile(sin, (1, 2))


def _seq_rows(b):
    return slice(b * TL, (b + 1) * TL)


def _mod_row(b, per_sequence):
    return b if per_sequence else SEQS


def _norm_mod(h_ref, x_ref, mod_ref, nw_ref, b, per_sequence):
    xb = x_ref[b]
    ms = jnp.mean(xb * xb, axis=-1, keepdims=True)
    m = _mod_row(b, per_sequence)
    shift = mod_ref[m:m + 1, 0:D_MODEL]
    gain = nw_ref[...] * (1.0 + mod_ref[m:m + 1, D_MODEL:2 * D_MODEL])
    h_ref[_seq_rows(b), :] = ((xb * lax.rsqrt(ms + EPS)) * gain + shift).astype(BF16)


def _rope(x, cos, sin):
    lane = lax.broadcasted_iota(jnp.int32, x.shape, 1)
    first = (lane & 31) < 16
    partner = jnp.where(first, pltpu.roll(x, LANES - 16, 1), pltpu.roll(x, 16, 1))
    x3 = x.reshape(SEQS, TL, LANES)
    p3 = partner.reshape(SEQS, TL, LANES)
    return (x3 * cos[None] + p3 * sin[None]).reshape(ROWS, LANES)


def _store_column_blocks(dst_ref, val, first_seq):
    for j in range(val.shape[0] // TL):
        b = first_seq + j
        for cb in range(SSM_BLOCKS):
            dst_ref[cb, b * PITCH:b * PITCH + TL, :] = val[_seq_rows(j), cb * LANES:(cb + 1) * LANES]


def _token_rows(l):
    return l // BT, slice((l % BT) * SEQS, (l % BT + 1) * SEQS)


def _slot_rows(slot):
    return slice(slot * SEQS, (slot + 1) * SEQS)


def _to_token_major(dst_ref, src_ref):
    for l in range(TL):
        j, rows = _token_rows(l)
        for cb in range(SSM_BLOCKS):
            dst_ref[j, rows, cb * LANES:(cb + 1) * LANES] = src_ref[cb, pl.ds(l, SEQS, stride=PITCH), :]


def _to_sequence_major(dst_ref, src_ref):
    for l in range(TL):
        j, rows = _token_rows(l)
        for cb in range(SSM_BLOCKS):
            dst_ref[cb, pl.ds(l, SEQS, stride=PITCH), :] = src_ref[j, rows, cb * LANES:(cb + 1) * LANES]


def _transpose_pieces(vs):
    vs = list(vs)
    assert len(vs) == PIECES
    lane = lax.broadcasted_iota(jnp.int32, vs[0].shape, 1)
    span = LANES // 2
    while span >= PIECE:
        stride = span // PIECE
        even = (lane // span) % 2 == 0
        out = list(vs)
        for i in range(PIECES):
            if (i // stride) % 2 == 0:
                a, b = vs[i], vs[i + stride]
                out[i] = jnp.where(even, a, pltpu.roll(b, span, 1))
                out[i + stride] = jnp.where(even, pltpu.roll(a, LANES - span, 1), b)
        vs = out
        span //= 2
    return vs


def _pack_scan_operand(lhs_ref, u_tm):
    for blk in range(SSM_BLOCKS):
        lanes = slice(blk * LANES, (blk + 1) * LANES)
        for v in range(BT // PIECES):
            slots = [u_tm[:, _slot_rows(v * PIECES + j), lanes].reshape(QROWS, LANES) for j in range(PIECES)]
            for k, val in enumerate(_transpose_pieces(slots)):
                c0 = (blk * PIECES + k) * KTILE + v * LANES
                lhs_ref[:, c0:c0 + LANES] = val.astype(BF16)


def _unpack_outputs(ys):
    out = []
    for v in range(BT // PIECES):
        cols = slice(v * LANES, (v + 1) * LANES)
        for val in _transpose_pieces([y[:, cols] for y in ys]):
            out.append(val.reshape(QUADS, SEQS, LANES))
    return out


def _scan_unit(s_ref, lpow_ref, carry_ref, direction, unit, reverse):
    cols = slice(unit * UNIT_ST, (unit + 1) * UNIT_ST)
    lanes = slice(direction * ALL_ST + unit * UNIT_ST, direction * ALL_ST + (unit + 1) * UNIT_ST)
    lr = lpow_ref[0, :, lanes]
    li = lpow_ref[1, :, lanes]
    sr = carry_ref[0, :, cols]
    si = carry_ref[1, :, cols]
    for i in range(QUADS):
        j = (QUADS - 1 - i) if reverse else i
        rows = slice(j * SEQS, (j + 1) * SEQS)
        inc_r = s_ref[rows, 0:UNIT_ST]
        inc_i = s_ref[rows, UNIT_ST:2 * UNIT_ST]
        s_ref[rows, 0:UNIT_ST] = sr
        s_ref[rows, UNIT_ST:2 * UNIT_ST] = si
        sr, si = lr * sr - li * si + inc_r, lr * si + li * sr + inc_i
    carry_ref[0, :, cols] = sr
    carry_ref[1, :, cols] = si


def _s5_direction(lhs_ref, s_ref, carry_ref, win_ref, wout_ref, lpow_ref, first_ref, first_s, emit, direction,
                  wk_ref=None, between=None):
    reverse = direction == BWD

    def lhs(unit):
        return lhs_ref[:, unit * KTILE:(unit + 1) * KTILE]

    bufs = s_ref.shape[0]
    ahead = min(bufs, N_UNITS) if bufs >= N_UNITS else bufs - 1

    def project_in(unit):
        s_ref[unit % bufs] = jnp.dot(lhs(unit), win_ref[unit], preferred_element_type=F32)

    for unit in range(min(ahead, N_UNITS)):
        project_in(unit)
    pending = []
    for unit in range(N_UNITS):
        if ahead <= unit + ahead < N_UNITS:
            project_in(unit + ahead)
        buf = s_ref.at[unit % bufs]
        _scan_unit(buf, lpow_ref, carry_ref, direction, unit, reverse)
        y = jnp.dot(buf[...].astype(BF16), wout_ref[unit], preferred_element_type=F32)
        if wk_ref is not None:
            y = y + jnp.dot(lhs(unit), wk_ref[unit], preferred_element_type=F32)
        pending.append(y)
        if len(pending) == PIECES:
            emit(unit // PIECES, _unpack_outputs(pending))
            pending = []
        if between is not None:
            between(unit)

    if first_ref is not None:
        first_rows = slice(QROWS - 2 * SEQS, QROWS) if reverse else slice(0, 2 * SEQS)
        lane = lax.broadcasted_iota(jnp.int32, (2 * SEQS, LANES), 1)
        zeros = jnp.zeros((2 * SEQS, LANES), F32)
        other = [zeros] * (KTILE // LANES - 1)
        for unit in range(N_UNITS):
            if reverse:
                tail = lhs_ref[first_rows, (unit + 1) * KTILE - LANES:(unit + 1) * KTILE].astype(F32)
                block = [jnp.where(lane < PIECE, pltpu.roll(tail, PIECE, 1), 0.0)] + other
            else:
                head = lhs_ref[first_rows, unit * KTILE:unit * KTILE + LANES].astype(F32)
                block = other + [jnp.where(lane >= LANES - PIECE, pltpu.roll(head, LANES - PIECE, 1), 0.0)]
            inc = jnp.dot(jnp.concatenate(block, axis=1).astype(BF16), win_ref[unit], preferred_element_type=F32)
            inc = inc[SEQS:2 * SEQS] if reverse else inc[0:SEQS]
            first_s[0, :, unit * UNIT_ST:(unit + 1) * UNIT_ST] = inc[:, 0:UNIT_ST]
            first_s[1, :, unit * UNIT_ST:(unit + 1) * UNIT_ST] = inc[:, UNIT_ST:2 * UNIT_ST]

        @pl.when(pl.program_id(1) == 0)
        def _():
            first_ref[...] = first_s[...]


def _s5_param_specs(direction):
    return [_const_spec((None, N_UNITS, KTILE, 2 * UNIT_ST), (direction, 0, 0, 0)),
            _const_spec((None, N_UNITS, 2 * UNIT_ST, KTILE), (direction, 0, 0, 0)),
            _const_spec((2, SEQS, 2 * ALL_ST))]


def _h0_specs(direction):
    return [_const_spec((SEQS, ALL_ST), (0, direction))] * 2


def _pass_a_kernel(*refs, rotary):
    if rotary:
        (x_ref, mod_ref, nw_ref, w_ref, win_ref, wout_ref, lpow_ref, wk_ref, d_ref, h0re_ref, h0im_ref, cos_ref, sin_ref,
         kcat_ref, vt_ref, lhs_ref, yb_ref, zs_ref, za_ref, ql_ref, qr_ref, h_s, u_sm, u_tm, s_s, carry) = refs
        fin_ref = first_s = None
    else:
        (x_ref, mod_ref, nw_ref, w_ref, win_ref, wout_ref, lpow_ref, wk_ref, d_ref,
         k_ref, v_ref, kcat_ref, vt_ref, lhs_ref, yb_ref, zs_ref, za_ref, ql_ref, qr_ref, fin_ref,
         h_s, u_sm, u_tm, s_s, carry, first_s) = refs

    @pl.when(pl.program_id(1) == 0)
    def _():
        if rotary:
            carry[0] = h0re_ref[...]
            carry[1] = h0im_ref[...]
        else:
            carry[...] = jnp.zeros_like(carry)

    half_seqs = SEQS // 2
    for hf in range(2):
        for b in range(hf * half_seqs, (hf + 1) * half_seqs):
            _norm_mod(h_s, x_ref, mod_ref, nw_ref, b, per_sequence=rotary)
        rows = slice(hf * half_seqs * TL, (hf + 1) * half_seqs * TL)
        u_half = jnp.dot(h_s[rows, :], w_ref[:, W_IN_U:W_IN_U + SSM_WIDTH], preferred_element_type=F32)
        _store_column_blocks(u_sm, u_half, first_seq=hf * half_seqs)
    _to_token_major(u_tm, u_sm)
    _pack_scan_operand(lhs_ref, u_tm)

    def project(c0, width):
        return jnp.dot(h_s[...], w_ref[:, c0:c0 + width], preferred_element_type=F32)

    def project_kv():
        kv = project(W_IN_K, 2 * KV_WIDTH)
        k = kv[:, 0:KV_WIDTH]
        v = kv[:, KV_WIDTH:2 * KV_WIDTH]
        if rotary:
            k = _rope(k, cos_ref[...], sin_ref[...])
        else:
            k_ref[...] = k.reshape(SEQS, TL, KV_WIDTH)
            v_ref[...] = v.reshape(SEQS, TL, KV_WIDTH)
        kcat = jnp.concatenate([k, pltpu.roll(k, HEAD_DIM, 1)], axis=1).astype(BF16)
        kcat_ref[...] = kcat.reshape(SEQS, TL, 2 * KV_WIDTH)
        for b in range(SEQS):
            vt_ref[b] = v[_seq_rows(b), :].T.astype(BF16)

    def project_q():
        qall = project(W_IN_Q, ATTN_WIDTH)
        even_head = lax.broadcasted_iota(jnp.int32, (ROWS, LANES), 1) < HEAD_DIM
        for cg in range(ATTN_WIDTH // LANES):
            cols = slice(cg * LANES, (cg + 1) * LANES)
            q = qall[:, cols]
            if rotary:
                q = _rope(q, cos_ref[...], sin_ref[...])
            q = q * (LOG2E * HEAD_DIM ** -0.5)
            ql_ref[:, cols] = jnp.where(even_head, q, 0.0).astype(BF16)
            qr_ref[:, cols] = jnp.where(even_head, 0.0, q).astype(BF16)

    def project_gate(c0, out_ref):
        out_ref[...] = jax.nn.silu(project(c0, out_ref.shape[1])).astype(BF16)

    side_work = [project_kv, project_q, lambda: project_gate(W_IN_ZS, zs_ref), lambda: project_gate(W_IN_ZA, za_ref)]

    def between(i):
        if i < len(side_work):
            side_work[i]()

    def emit(blk, slots):
        lanes = slice(blk * LANES, (blk + 1) * LANES)
        for slot, ys in enumerate(slots):
            rows = _slot_rows(slot)
            yb_ref[:, rows, lanes] = ys + u_tm[:, rows, lanes] * d_ref[:, lanes]

    _s5_direction(lhs_ref, s_s, carry, win_ref, wout_ref, lpow_ref, fin_ref, first_s, emit, BWD,
                  wk_ref=wk_ref, between=between)


def _pass_a(x, mod, norm_w, w_in, win, wout, lpow, wk, d, h0, rope):
    batch, length, _ = x.shape
    ng, nt = batch // SEQS, length // TL
    rotary = rope is not None
    rev = lambda t: nt - 1 - t
    in_specs = [
        pl.BlockSpec((SEQS, TL, D_MODEL), lambda g, t: (g, rev(t), 0)),
        _const_spec(mod.shape),
        _const_spec((1, D_MODEL)),
        _const_spec((D_MODEL, D_IN)),
    ] + _s5_param_specs(BWD) + [
        _const_spec((N_UNITS, KTILE, KTILE)),
        _const_spec((1, SSM_WIDTH)),
    ]
    args = [x, mod, norm_w, w_in, win, wout, lpow, wk, d]
    if rotary:
        in_specs += _h0_specs(BWD) + [pl.BlockSpec((TL, LANES), lambda g, t: (rev(t), 0))] * 2
        args += list(h0) + list(rope)
    assert ATTN_WIDTH == SSM_WIDTH
    tile_spec = pl.BlockSpec((None, None, ROWS, SSM_WIDTH), lambda g, t: (g, rev(t), 0, 0))
    tile_bf16 = jax.ShapeDtypeStruct((ng, nt, ROWS, SSM_WIDTH), BF16)
    out_specs = [
        pl.BlockSpec((None, None, SEQS, TL, 2 * KV_WIDTH), lambda g, t: (g, rev(t), 0, 0, 0)),
        pl.BlockSpec((None, None, SEQS, KV_WIDTH, TL), lambda g, t: (g, rev(t), 0, 0, 0)),
        pl.BlockSpec((None, None, QROWS, N_UNITS * KTILE), lambda g, t: (g, rev(t), 0, 0)),
        pl.BlockSpec((None, None, QUADS, BT * SEQS, SSM_WIDTH), lambda g, t: (g, rev(t), 0, 0, 0)),
    ] + [tile_spec] * 4
    out_shape = [
        jax.ShapeDtypeStruct((ng, nt, SEQS, TL, 2 * KV_WIDTH), BF16),
        jax.ShapeDtypeStruct((ng, nt, SEQS, KV_WIDTH, TL), BF16),
        jax.ShapeDtypeStruct((ng, nt, QROWS, N_UNITS * KTILE), BF16),
        jax.ShapeDtypeStruct((ng, nt, QUADS, BT * SEQS, SSM_WIDTH), F32),
        tile_bf16, tile_bf16,
        tile_bf16, tile_bf16,
    ]
    scratch = [
        pltpu.VMEM((ROWS, D_MODEL), BF16),
        pltpu.VMEM((SSM_BLOCKS, SEQS * PITCH, LANES), F32),
        pltpu.VMEM((QUADS, BT * SEQS, SSM_WIDTH), F32),
        pltpu.VMEM((S5_BUFS_A, QROWS, 2 * UNIT_ST), F32),
        pltpu.VMEM((2, SEQS, ALL_ST), F32),
    ]
    if not rotary:
        cache_spec = pl.BlockSpec((SEQS, TL, KV_WIDTH), lambda g, t: (g, rev(t), 0))
        cache_shape = jax.ShapeDtypeStruct((batch, length, KV_WIDTH), F32)
        out_specs = [cache_spec, cache_spec] + out_specs + [pl.BlockSpec((None, 2, SEQS, ALL_ST), lambda g, t: (g, 0, 0, 0))]
        out_shape = [cache_shape, cache_shape] + out_shape + [jax.ShapeDtypeStruct((ng, 2, SEQS, ALL_ST), F32)]
        scratch += [pltpu.VMEM((2, SEQS, ALL_ST), F32)]
    return pl.pallas_call(
        functools.partial(_pass_a_kernel, rotary=rotary),
        grid=(ng, nt),
        in_specs=in_specs, out_specs=out_specs, out_shape=out_shape, scratch_shapes=scratch,
        compiler_params=pltpu.CompilerParams(
            dimension_semantics=("arbitrary", "arbitrary"), vmem_limit_bytes=VMEM_LIMIT),
        name="pass_a_latent" if rotary else "pass_a_context",
    )(*args)


def _attention(b, q_refs, za_ref, mix_s, kcat_refs, vt_refs, sink_ref, masks):
    rows = _seq_rows(b)
    kcat = jnp.concatenate([r[b] for r in kcat_refs], axis=0)
    vt = jnp.concatenate([r[b] for r in vt_refs], axis=1)
    segment = lax.broadcasted_iota(jnp.int32, (1, 4 * TL), 1) // TL
    kv_heads = range(N_KV_HEADS)

    def side(g, grp):
        return g ^ grp

    def head(g, i, grp):
        return 4 * g + 2 * i + side(g, grp)

    def scores(grp):
        q = jnp.concatenate([q_refs[side(g, grp)][rows, blk * LANES:(blk + 1) * LANES]
                             for g in kv_heads for blk in (2 * g, 2 * g + 1)], axis=0)
        return lax.dot_general(kcat[:, grp * KV_WIDTH:(grp + 1) * KV_WIDTH], q, (((1,), (1,)), ((), ())),
                               preferred_element_type=F32)

    def softmax(grp, s):
        if masks is None:
            pieces = [s]
        else:
            pieces = [jnp.where(masks[0], s[0:TL], NEG_INF), s[TL:2 * TL],
                      jnp.where(masks[1], s[2 * TL:3 * TL], NEG_INF), s[3 * TL:]]
        sinks = [sink_ref[head(g, i, grp)] * LOG2E for g in kv_heads for i in range(2)]
        sink = jnp.where(segment == 0, sinks[0], jnp.where(segment == 1, sinks[1],
                                                           jnp.where(segment == 2, sinks[2], sinks[3])))
        m = sink
        for piece in pieces:
            m = jnp.maximum(m, jnp.max(piece, axis=0, keepdims=True))
        ps = [jnp.exp2(piece - m).astype(BF16) for piece in pieces]
        return (ps[0] if len(ps) == 1 else jnp.concatenate(ps, axis=0)), jnp.exp2(sink - m)

    ones_rows = jnp.ones((BF16_SUBLANES, vt.shape[1]), BF16)

    all_scores = [scores(grp) for grp in range(2)]
    probs, sink_terms = zip(*[softmax(grp, s) for grp, s in enumerate(all_scores)])
    for g in kv_heads:
        cols = slice(g * 2 * TL, (g + 1) * 2 * TL)
        values = jnp.concatenate([vt[g * HEAD_DIM:(g + 1) * HEAD_DIM, :], ones_rows], axis=0)
        p = jnp.concatenate([probs[0][:, cols], probs[1][:, cols]], axis=1)
        sink_term = jnp.concatenate([sink_terms[0][:, cols], sink_terms[1][:, cols]], axis=1)
        ot = jnp.dot(values, p, preferred_element_type=F32)
        out_t = ot[0:HEAD_DIM] * (1.0 / (ot[HEAD_DIM:HEAD_DIM + 1] + sink_term))
        for i in range(2):
            left = out_t[:, (2 * g + i) * TL:(2 * g + i + 1) * TL]
            right = out_t[:, (2 * (g ^ 1) + i) * TL:(2 * (g ^ 1) + i + 1) * TL]
            blk_cols = slice((2 * g + i) * LANES, (2 * g + i + 1) * LANES)
            pair_t = jnp.concatenate([left, right], axis=0)
            gated = pair_t.T * za_ref[rows, blk_cols].astype(F32)
            mix_s[rows, ATTN_WIDTH + blk_cols.start:ATTN_WIDTH + blk_cols.stop] = gated.astype(BF16)


def _pass_b_kernel(*refs, windowed, n_tiles):
    if windowed:
        (sink_ref, x_ref, mod_ref, win_ref, wout_ref, lpow_ref, wg_ref, bg_ref,
         lhs_ref, yb_ref, zs_ref, za_ref, ql_ref, qr_ref, kx_ref, vx_ref, wo_ref, fw_ref, h0re_ref, h0im_ref,
         kp_ref, kc_ref, kn_ref, vp_ref, vc_ref, vn_ref,
         y_ref, y_sm, y_tm, s_s, carry, mix_s) = refs
        fin_ref = first_s = None
    else:
        (sink_ref, x_ref, mod_ref, win_ref, wout_ref, lpow_ref, wg_ref, bg_ref,
         lhs_ref, yb_ref, zs_ref, za_ref, ql_ref, qr_ref, *own_kv, wo_ref, fw_ref,
         y_ref, fin_ref, y_sm, y_tm, s_s, carry, mix_s, first_s) = refs
        assert len(own_kv) == 2 * n_tiles
    t = pl.program_id(1)

    @pl.when(t == 0)
    def _():
        if windowed:
            carry[0] = h0re_ref[...]
            carry[1] = h0im_ref[...]
        else:
            carry[...] = jnp.zeros_like(carry)

    if windowed:
        kj = lax.broadcasted_iota(jnp.int32, (TL, 4 * TL), 0)
        qi = lax.broadcasted_iota(jnp.int32, (TL, 4 * TL), 1) & (TL - 1)
        masks = ((kj >= qi) & (t > 0), (kj <= qi) & (t < n_tiles - 1))
        key_refs = (kp_ref, kc_ref, kn_ref, kx_ref)
        val_refs = (vp_ref, vc_ref, vn_ref, vx_ref)
    else:
        masks = None
        key_refs = tuple(own_kv[:n_tiles])
        val_refs = tuple(own_kv[n_tiles:])

    def attention(b):
        _attention(b, (ql_ref, qr_ref), za_ref, mix_s, key_refs, val_refs, sink_ref, masks)

    def emit(blk, slots):
        lanes = slice(blk * LANES, (blk + 1) * LANES)
        for slot, ys in enumerate(slots):
            rows = _slot_rows(slot)
            y_tm[:, rows, lanes] = ys + yb_ref[:, rows, lanes]

    glu_blocks = QUADS // 4

    def glu(rc):
        pr = slice(rc * glu_blocks, (rc + 1) * glu_blocks)
        y = jax.nn.gelu(y_tm[pr].reshape(glu_blocks * BT * SEQS, SSM_WIDTH))
        gl = jnp.dot(y.astype(BF16), wg_ref[...], preferred_element_type=F32) + bg_ref[...]
        y_tm[pr] = (y * jax.nn.sigmoid(gl)).reshape(glu_blocks, BT * SEQS, SSM_WIDTH)

    def gate_s5(seqs):
        for b in seqs:
            for cb in range(SSM_BLOCKS):
                cols = slice(cb * LANES, (cb + 1) * LANES)
                gated = y_sm[cb, b * PITCH:b * PITCH + TL, :] * zs_ref[_seq_rows(b), cols].astype(F32)
                mix_s[_seq_rows(b), cols] = gated.astype(BF16)

    half_seqs = SEQS // 2

    def project_out(hf):
        rows = slice(hf * half_seqs * TL, (hf + 1) * half_seqs * TL)
        proj = jnp.dot(mix_s[rows, :], wo_ref[...], preferred_element_type=F32)
        y_ref[hf * half_seqs:(hf + 1) * half_seqs] = proj.reshape(half_seqs, TL, D_MODEL)

    def finish(seqs):
        for b in seqs:
            m = _mod_row(b, per_sequence=windowed)
            gate = mod_ref[m:m + 1, 2 * D_MODEL:3 * D_MODEL]
            r = x_ref[b] + gate * y_ref[b]
            ms = jnp.mean(r * r, axis=-1, keepdims=True)
            y_ref[b] = (r * lax.rsqrt(ms + EPS)) * fw_ref[...]

    first_half, second_half = range(0, half_seqs), range(half_seqs, SEQS)

    _s5_direction(lhs_ref, s_s, carry, win_ref, wout_ref, lpow_ref, fin_ref, first_s, emit, FWD)
    assert SEQS == 8
    follow_up = [
        lambda: glu(0), lambda: glu(1), lambda: glu(2),
        lambda: (glu(3), _to_sequence_major(y_sm, y_tm)),
        lambda: (gate_s5(first_half), project_out(0)),
        lambda: finish(first_half),
        lambda: gate_s5(second_half),
        lambda: (project_out(1), finish(second_half)),
    ]
    for b in range(SEQS):
        attention(b)
        follow_up[b]()


def _pass_b(x, mod, win, wout, lpow, w_glu, b_glu, from_a, kx, vx, w_out, fnorm_w, sink, h0, k_loc, v_loc):
    batch, length, _ = x.shape
    ng, nt = batch // SEQS, length // TL
    windowed = k_loc is not None
    tile_spec = pl.BlockSpec((None, None, ROWS, SSM_WIDTH), lambda g, t: (g, t, 0, 0))

    def key_tile(f):
        return pl.BlockSpec((None, None, SEQS, TL, 2 * KV_WIDTH), lambda g, t: (g, f(t), 0, 0, 0))

    def value_tile(f):
        return pl.BlockSpec((None, None, SEQS, KV_WIDTH, TL), lambda g, t: (g, f(t), 0, 0, 0))

    if windowed:
        global_kv = [pl.BlockSpec((SEQS, PAST_LEN, 2 * KV_WIDTH), lambda g, t: (g, 0, 0)),
                     pl.BlockSpec((SEQS, KV_WIDTH, PAST_LEN), lambda g, t: (g, 0, 0))]
        global_args = [kx, vx]
    else:
        own = [lambda t, i=i: i for i in range(nt)]
        global_kv = [key_tile(f) for f in own] + [value_tile(f) for f in own]
        global_args = [kx] * nt + [vx] * nt
    in_specs = [
        pl.BlockSpec(memory_space=pltpu.SMEM),
        pl.BlockSpec((SEQS, TL, D_MODEL), lambda g, t: (g, t, 0)),
        _const_spec(mod.shape),
    ] + _s5_param_specs(FWD) + [
        _const_spec((SSM_WIDTH, SSM_WIDTH)),
        _const_spec((1, SSM_WIDTH)),
        pl.BlockSpec((None, None, QROWS, N_UNITS * KTILE), lambda g, t: (g, t, 0, 0)),
        pl.BlockSpec((None, None, QUADS, BT * SEQS, SSM_WIDTH), lambda g, t: (g, t, 0, 0, 0)),
    ] + [tile_spec] * 4 + global_kv + [
        _const_spec((D_MODEL, D_MODEL)),
        _const_spec((1, D_MODEL)),
    ]
    args = [sink, x, mod, win, wout, lpow, w_glu, b_glu, *from_a, *global_args, w_out, fnorm_w]
    if windowed:
        in_specs += _h0_specs(FWD)
        args += list(h0)
        band_t = (lambda t: jnp.maximum(t - 1, 0), lambda t: t, lambda t: jnp.minimum(t + 1, nt - 1))
        in_specs += [key_tile(f) for f in band_t] + [value_tile(f) for f in band_t]
        args += [k_loc] * 3 + [v_loc] * 3
    out_specs = [pl.BlockSpec((SEQS, TL, D_MODEL), lambda g, t: (g, t, 0))]
    out_shape = [jax.ShapeDtypeStruct((batch, length, D_MODEL), F32)]
    scratch = [
        pltpu.VMEM((SSM_BLOCKS, SEQS * PITCH, LANES), F32),
        pltpu.VMEM((QUADS, BT * SEQS, SSM_WIDTH), F32),
        pltpu.VMEM((S5_BUFS_B, QROWS, 2 * UNIT_ST), F32),
        pltpu.VMEM((2, SEQS, ALL_ST), F32),
        pltpu.VMEM((ROWS, D_MODEL), BF16),
    ]
    if not windowed:
        out_specs += [pl.BlockSpec((None, 2, SEQS, ALL_ST), lambda g, t: (g, 0, 0, 0))]
        out_shape += [jax.ShapeDtypeStruct((ng, 2, SEQS, ALL_ST), F32)]
        scratch += [pltpu.VMEM((2, SEQS, ALL_ST), F32)]
    return pl.pallas_call(
        functools.partial(_pass_b_kernel, windowed=windowed, n_tiles=nt),
        grid=(ng, nt),
        in_specs=in_specs, out_specs=out_specs, out_shape=out_shape, scratch_shapes=scratch,
        compiler_params=pltpu.CompilerParams(
            dimension_semantics=("arbitrary", "arbitrary"), vmem_limit_bytes=VMEM_LIMIT),
        name="pass_b_latent" if windowed else "pass_b_context",
    )(*args)


def kernel(x_prompt, x_sample, c, cache_k, cache_v, state_ssm_re, state_ssm_im, c_ctx, norm_w, w_mod, b_mod, w_in, ssm_lambda_re, ssm_lambda_im, ssm_log_dt, ssm_b_re, ssm_b_im, ssm_c_re, ssm_c_im, ssm_d, w_glu, b_glu, attn_sink, w_out, final_norm_w):
    assert norm_w.shape[0] == 1, "single trunk layer"
    batch, seq, _ = x_prompt.shape
    dec_batch, dec_seq, _ = x_sample.shape
    assert dec_batch == SEQS and batch % SEQS == 0 and seq % TL == 0 and dec_seq % TL == 0
    assert seq == PAST_LEN and cache_k.shape[2] == PAST_LEN

    w_in0 = w_in[0].astype(BF16)
    nw = norm_w[0][None, :]
    fw = final_norm_w[None, :]
    d = ssm_d[0][None, :]
    wg = w_glu[0].astype(BF16)
    bg = b_glu[0][None, :]
    wo = w_out[0].astype(BF16)
    sink = attn_sink[0]

    cond = jnp.concatenate([c, c_ctx[None, :], jnp.zeros((16 - SEQS - 1, D_MODEL), F32)], axis=0)
    mod = _modulation(cond, w_mod[0], b_mod[0][None, :])

    win, wout, wk, lpow = _s5_weights(ssm_lambda_re[0], ssm_lambda_im[0], ssm_log_dt[0], ssm_b_re[0], ssm_b_im[0],
                                     ssm_c_re[0], ssm_c_im[0])
    h0 = (state_ssm_re.reshape(SEQS, 2 * ALL_ST), state_ssm_im.reshape(SEQS, 2 * ALL_ST))
    rope = _rope_tables(dec_seq)

    k_ctx, v_ctx, kcat_ctx, vt_ctx, *tiles_ctx, fin_b = _pass_a(
        x_prompt, mod, nw, w_in0, win, wout, lpow, wk, d, None, None)
    y_prompt, fin_f = _pass_b(x_prompt, mod, win, wout, lpow, wg, bg, tiles_ctx, kcat_ctx, vt_ctx, wo, fw, sink,
                              None, None, None)

    kcat_lat, vt_lat, *tiles_lat = _pass_a(x_sample, mod, nw, w_in0, win, wout, lpow, wk, d, h0, rope)
    kx = cache_k[:, 0].reshape(dec_batch, PAST_LEN, KV_WIDTH)
    kx = jnp.concatenate([kx, jnp.roll(kx, HEAD_DIM, axis=-1)], axis=-1).astype(BF16)
    vx = jnp.swapaxes(cache_v[:, 0].reshape(dec_batch, PAST_LEN, KV_WIDTH), 1, 2).astype(BF16)
    (y_sample,) = _pass_b(x_sample, mod, win, wout, lpow, wg, bg, tiles_lat, kx, vx, wo, fw, sink,
                          h0, kcat_lat, vt_lat)

    new_cache_k = k_ctx.reshape(batch, 1, seq, N_KV_HEADS, HEAD_DIM)
    new_cache_v = v_ctx.reshape(batch, 1, seq, N_KV_HEADS, HEAD_DIM)

    def states(fin, part):
        return fin[:, part].reshape(batch, SSM_GROUPS, SSM_STATE)

    new_re = jnp.stack([states(fin_f, 0), states(fin_b, 0)], axis=1)[:, None]
    new_im = jnp.stack([states(fin_f, 1), states(fin_b, 1)], axis=1)[:, None]
    return (y_prompt, y_sample, new_cache_k, new_cache_v, new_re, new_im)
```

```python
import functools
import math

import jax
import jax.numpy as jnp
from jax import lax
from jax.experimental import pallas as pl
from jax.experimental.pallas import tpu as pltpu

F32 = jnp.float32
BF16 = jnp.bfloat16

D_MODEL = 1024
SSM_WIDTH = 512
ATTN_WIDTH = 512
SSM_GROUP = 16
SSM_GROUPS = 32
SSM_STATE = 64
HEAD_DIM = 64
N_KV_HEADS = 2
KV_WIDTH = 128
GRID_W = 64
ROPE_AXIS_DIM = 32
ROPE_BASE = 10000.0
EPS = 1e-6
LAMBDA_RE_MAX = -1e-4
NEG_INF = -1e30
LOG2E = math.log2(math.e)
PAST_LEN = 256

LANES = 128
BF16_SUBLANES = 16
SEQS = 8
TL = 128
ROWS = SEQS * TL
PITCH = TL + 8
SSM_BLOCKS = SSM_WIDTH // LANES
ALL_ST = SSM_GROUPS * SSM_STATE
BT = 8
KTILE = 2 * LANES
PIECE = KTILE // BT
PIECES = LANES // PIECE
UNIT_GROUPS = PIECE // SSM_GROUP
UNIT_ST = UNIT_GROUPS * SSM_STATE
N_UNITS = SSM_WIDTH // PIECE
W_UNITS_PER_STEP = LANES // PIECE
QUADS = TL // BT
QROWS = QUADS * SEQS
S5_BUFS_A = 8
S5_BUFS_B = 2
V7X_VMEM_BYTES = 64 * 1024 * 1024
VMEM_LIMIT = V7X_VMEM_BYTES - 6 * 1024 * 1024

BWD, FWD = 1, 0
W_IN_U = 0
W_IN_ZS = SSM_WIDTH
W_IN_Q = 2 * SSM_WIDTH
W_IN_K = 2 * SSM_WIDTH + ATTN_WIDTH
W_IN_ZA = W_IN_K + 2 * KV_WIDTH
D_IN = W_IN_ZA + ATTN_WIDTH


def _const_spec(shape, index=None):
    index = (0,) * len(shape) if index is None else index
    return pl.BlockSpec(shape, lambda g, t: index, pipeline_mode=pl.Buffered(1))


def _modulation_kernel(cond_ref, w_ref, b_ref, out_ref):
    a = jax.nn.silu(cond_ref[...]).astype(BF16)
    out_ref[...] = jnp.dot(a, w_ref[...].astype(BF16), preferred_element_type=F32) + b_ref[...]


def _modulation(cond, w_mod, b_mod):
    rows = cond.shape[0]
    n = w_mod.shape[1]
    return pl.pallas_call(
        _modulation_kernel,
        grid=(1,),
        in_specs=[pl.BlockSpec((rows, D_MODEL), lambda j: (0, 0)),
                  pl.BlockSpec((D_MODEL, n), lambda j: (0, 0), pipeline_mode=pl.Buffered(1)),
                  pl.BlockSpec((1, n), lambda j: (0, 0))],
        out_specs=pl.BlockSpec((rows, n), lambda j: (0, 0)),
        out_shape=jax.ShapeDtypeStruct((rows, n), F32),
        compiler_params=pltpu.CompilerParams(vmem_limit_bytes=VMEM_LIMIT // 2),
        name="modulation",
    )(cond, w_mod, b_mod)


def _zoh(lam_re, lam_im, log_dt):
    lam_re = jnp.minimum(lam_re, LAMBDA_RE_MAX)
    dt = jnp.exp(log_dt)
    mag = jnp.exp(lam_re * dt)
    ang = lam_im * dt
    lbar_re = mag * jnp.cos(ang)
    lbar_im = mag * jnp.sin(ang)
    nr = lbar_re - 1.0
    ni = lbar_im
    den = lam_re * lam_re + lam_im * lam_im
    f_re = (nr * lam_re + ni * lam_im) / den
    f_im = (ni * lam_re - nr * lam_im) / den
    return lbar_re, lbar_im, f_re, f_im


def _cmul(a, b):
    return a[0] * b[0] - a[1] * b[1], a[0] * b[1] + a[1] * b[0]


def _s5_weights_kernel(lam_ref, bt_ref, cre_ref, cim_ref, lamflat_ref, win_ref, wout_ref, wk_ref, lpow_ref):
    p_row = lax.broadcasted_iota(jnp.int32, (SSM_STATE, UNIT_ST), 0)
    p_col = lax.broadcasted_iota(jnp.int32, (SSM_STATE, UNIT_ST), 1) & (SSM_STATE - 1)
    spread = jnp.where(p_row == p_col, 1.0, 0.0).astype(BF16)
    g_row = lax.broadcasted_iota(jnp.int32, (PIECE, UNIT_ST), 0) // SSM_GROUP
    g_col = lax.broadcasted_iota(jnp.int32, (PIECE, UNIT_ST), 1) // SSM_STATE
    own_block = g_row == g_col

    def blocks(x):
        return jnp.where(own_block, jnp.dot(x.astype(BF16), spread, preferred_element_type=F32), 0.0)

    def blocks2(z):
        return blocks(z[0]).astype(BF16), blocks(z[1]).astype(BF16)

    def in_block(src, c):
        dims = (((1,), (1,)), ((), ()))
        return (lax.dot_general(src[0], c[0], dims, preferred_element_type=F32)
                - lax.dot_general(src[1], c[1], dims, preferred_element_type=F32))

    def powers(z, lbar, n):
        out = [z]
        for _ in range(n):
            out.append(_cmul(out[-1], lbar))
        return out

    pows = {}
    for d in (FWD, BWD):
        lbar_re, lbar_im, f_re, f_im = _zoh(lam_ref[0, d], lam_ref[1, d], lam_ref[2, d])
        lbar = (lbar_re, lbar_im)
        pows[d] = (powers(_cmul((f_re, f_im), (bt_ref[0, d], bt_ref[1, d])), lbar, BT - 1),
                   powers((cre_ref[d], cim_ref[d]), lbar, BT))
    for k in range(W_UNITS_PER_STEP):
        rows = slice(k * PIECE, (k + 1) * PIECE)
        unit_rows = lambda z: (z[0][rows], z[1][rows])
        k_lag = {}
        for d in (FWD, BWD):
            bb_pow = [unit_rows(z) for z in pows[d][0]]
            c_pow = [unit_rows(z) for z in pows[d][1]]
            c_b = blocks2(c_pow[0])
            bb_b = [blocks2(z) for z in bb_pow]
            to_end = [(BT - 1 - s) if d == FWD else s for s in range(BT)]
            for s in range(BT):
                w = bb_b[to_end[s]]
                win_ref[d, k, s * PIECE:(s + 1) * PIECE, 0:UNIT_ST] = w[0]
                win_ref[d, k, s * PIECE:(s + 1) * PIECE, UNIT_ST:2 * UNIT_ST] = w[1]
            out_re = jnp.concatenate([blocks(c_pow[BT - to_end[s]][0]) for s in range(BT)], axis=0)
            out_im = jnp.concatenate([blocks(-c_pow[BT - to_end[s]][1]) for s in range(BT)], axis=0)
            wout_ref[d, k, 0:UNIT_ST, :] = out_re.T.astype(BF16)
            wout_ref[d, k, UNIT_ST:2 * UNIT_ST, :] = out_im.T.astype(BF16)
            k_lag[d] = [in_block(bb_b[m], c_b) for m in range(BT)]
        for src in range(BT):
            for out in range(BT):
                if out == src:
                    kk = k_lag[FWD][0] + k_lag[BWD][0]
                elif out > src:
                    kk = k_lag[FWD][out - src]
                else:
                    kk = k_lag[BWD][src - out]
                wk_ref[k, src * PIECE:(src + 1) * PIECE, out * PIECE:(out + 1) * PIECE] = kk.astype(BF16)

    @pl.when(pl.program_id(0) == 0)
    def _():
        lbar_re, lbar_im, _, _ = _zoh(lamflat_ref[0], lamflat_ref[1], lamflat_ref[2])
        p = (lbar_re, lbar_im)
        for _ in range(BT.bit_length() - 1):
            p = _cmul(p, p)
        lpow_ref[0] = p[0]
        lpow_ref[1] = p[1]


def _s5_weights(lam_re, lam_im, log_dt, b_re, b_im, c_re, c_im):
    assert BT & (BT - 1) == 0
    n = SSM_GROUPS * SSM_GROUP
    lam3 = jnp.stack([lam_re, lam_im, jnp.broadcast_to(log_dt[:, :, None], lam_re.shape)])
    lam_rows = jnp.broadcast_to(lam3[:, :, :, None, :], (3, 2, SSM_GROUPS, SSM_GROUP, SSM_STATE)).reshape(3, 2, n, SSM_STATE)
    lam_flat = jnp.broadcast_to(lam3.reshape(3, 1, 2 * ALL_ST), (3, SEQS, 2 * ALL_ST))
    bt = jnp.transpose(jnp.stack([b_re, b_im]), (0, 1, 2, 4, 3)).reshape(2, 2, n, SSM_STATE)
    return pl.pallas_call(
        _s5_weights_kernel,
        grid=(N_UNITS // W_UNITS_PER_STEP,),
        in_specs=[pl.BlockSpec((3, 2, W_UNITS_PER_STEP * PIECE, SSM_STATE), lambda m: (0, 0, m, 0)),
                  pl.BlockSpec((2, 2, W_UNITS_PER_STEP * PIECE, SSM_STATE), lambda m: (0, 0, m, 0)),
                  pl.BlockSpec((2, W_UNITS_PER_STEP * PIECE, SSM_STATE), lambda m: (0, m, 0)),
                  pl.BlockSpec((2, W_UNITS_PER_STEP * PIECE, SSM_STATE), lambda m: (0, m, 0)),
                  pl.BlockSpec((3, SEQS, 2 * ALL_ST), lambda m: (0, 0, 0))],
        out_specs=[pl.BlockSpec((2, W_UNITS_PER_STEP, KTILE, 2 * UNIT_ST), lambda m: (0, m, 0, 0)),
                   pl.BlockSpec((2, W_UNITS_PER_STEP, 2 * UNIT_ST, KTILE), lambda m: (0, m, 0, 0)),
                   pl.BlockSpec((W_UNITS_PER_STEP, KTILE, KTILE), lambda m: (m, 0, 0)),
                   pl.BlockSpec((2, SEQS, 2 * ALL_ST), lambda m: (0, 0, 0))],
        out_shape=[jax.ShapeDtypeStruct((2, N_UNITS, KTILE, 2 * UNIT_ST), BF16),
                   jax.ShapeDtypeStruct((2, N_UNITS, 2 * UNIT_ST, KTILE), BF16),
                   jax.ShapeDtypeStruct((N_UNITS, KTILE, KTILE), BF16),
                   jax.ShapeDtypeStruct((2, SEQS, 2 * ALL_ST), F32)],
        name="s5_weights",
    )(lam_rows, bt, c_re.reshape(2, n, SSM_STATE), c_im.reshape(2, n, SSM_STATE), lam_flat)


def _rope_tables(length):
    pos = jnp.arange(length)
    row = (pos // GRID_W).astype(F32)
    col = (pos % GRID_W).astype(F32)
    freqs = ROPE_BASE ** (-jnp.arange(0, ROPE_AXIS_DIM, 2, dtype=F32) / ROPE_AXIS_DIM)
    ang_r = row[:, None] * freqs[None, :]
    ang_c = col[:, None] * freqs[None, :]
    cos = jnp.concatenate([jnp.cos(ang_r), jnp.cos(ang_r), jnp.cos(ang_c), jnp.cos(ang_c)], axis=-1)
    sin = jnp.concatenate([-jnp.sin(ang_r), jnp.sin(ang_r), -jnp.sin(ang_c), jnp.sin(ang_c)], axis=-1)
    return jnp.tile(cos, (1, 2)), jnp.tile(sin, (1, 2))


def _seq_rows(b):
    return slice(b * TL, (b + 1) * TL)


def _mod_row(b, per_sequence):
    return b if per_sequence else SEQS


def _norm_mod(h_ref, x_ref, mod_ref, nw_ref, b, per_sequence):
    xb = x_ref[b]
    ms = jnp.mean(xb * xb, axis=-1, keepdims=True)
    m = _mod_row(b, per_sequence)
    shift = mod_ref[m:m + 1, 0:D_MODEL]
    gain = nw_ref[...] * (1.0 + mod_ref[m:m + 1, D_MODEL:2 * D_MODEL])
    h_ref[_seq_rows(b), :] = ((xb * lax.rsqrt(ms + EPS)) * gain + shift).astype(BF16)


def _rope(x, cos, sin):
    lane = lax.broadcasted_iota(jnp.int32, x.shape, 1)
    first = (lane & 31) < 16
    partner = jnp.where(first, pltpu.roll(x, LANES - 16, 1), pltpu.roll(x, 16, 1))
    x3 = x.reshape(SEQS, TL, LANES)
    p3 = partner.reshape(SEQS, TL, LANES)
    return (x3 * cos[None] + p3 * sin[None]).reshape(ROWS, LANES)


def _store_column_blocks(dst_ref, val, first_seq):
    for j in range(val.shape[0] // TL):
        b = first_seq + j
        for cb in range(SSM_BLOCKS):
            dst_ref[cb, b * PITCH:b * PITCH + TL, :] = val[_seq_rows(j), cb * LANES:(cb + 1) * LANES]


def _token_rows(l):
    return l // BT, slice((l % BT) * SEQS, (l % BT + 1) * SEQS)


def _slot_rows(slot):
    return slice(slot * SEQS, (slot + 1) * SEQS)


def _to_token_major(dst_ref, src_ref):
    for l in range(TL):
        j, rows = _token_rows(l)
        for cb in range(SSM_BLOCKS):
            dst_ref[j, rows, cb * LANES:(cb + 1) * LANES] = src_ref[cb, pl.ds(l, SEQS, stride=PITCH), :]


def _to_sequence_major(dst_ref, src_ref):
    for l in range(TL):
        j, rows = _token_rows(l)
        for cb in range(SSM_BLOCKS):
            dst_ref[cb, pl.ds(l, SEQS, stride=PITCH), :] = src_ref[j, rows, cb * LANES:(cb + 1) * LANES]


def _transpose_pieces(vs):
    vs = list(vs)
    assert len(vs) == PIECES
    lane = lax.broadcasted_iota(jnp.int32, vs[0].shape, 1)
    span = LANES // 2
    while span >= PIECE:
        stride = span // PIECE
        even = (lane // span) % 2 == 0
        out = list(vs)
        for i in range(PIECES):
            if (i // stride) % 2 == 0:
                a, b = vs[i], vs[i + stride]
                out[i] = jnp.where(even, a, pltpu.roll(b, span, 1))
                out[i + stride] = jnp.where(even, pltpu.roll(a, LANES - span, 1), b)
        vs = out
        span //= 2
    return vs


def _pack_scan_operand(lhs_ref, u_tm):
    for blk in range(SSM_BLOCKS):
        lanes = slice(blk * LANES, (blk + 1) * LANES)
        for v in range(BT // PIECES):
            slots = [u_tm[:, _slot_rows(v * PIECES + j), lanes].reshape(QROWS, LANES) for j in range(PIECES)]
            for k, val in enumerate(_transpose_pieces(slots)):
                c0 = (blk * PIECES + k) * KTILE + v * LANES
                lhs_ref[:, c0:c0 + LANES] = val.astype(BF16)


def _unpack_outputs(ys):
    out = []
    for v in range(BT // PIECES):
        cols = slice(v * LANES, (v + 1) * LANES)
        for val in _transpose_pieces([y[:, cols] for y in ys]):
            out.append(val.reshape(QUADS, SEQS, LANES))
    return out


def _scan_unit(s_ref, lpow_ref, carry_ref, direction, unit, reverse):
    cols = slice(unit * UNIT_ST, (unit + 1) * UNIT_ST)
    lanes = slice(direction * ALL_ST + unit * UNIT_ST, direction * ALL_ST + (unit + 1) * UNIT_ST)
    lr = lpow_ref[0, :, lanes]
    li = lpow_ref[1, :, lanes]
    sr = carry_ref[0, :, cols]
    si = carry_ref[1, :, cols]
    for i in range(QUADS):
        j = (QUADS - 1 - i) if reverse else i
        rows = slice(j * SEQS, (j + 1) * SEQS)
        inc_r = s_ref[rows, 0:UNIT_ST]
        inc_i = s_ref[rows, UNIT_ST:2 * UNIT_ST]
        s_ref[rows, 0:UNIT_ST] = sr
        s_ref[rows, UNIT_ST:2 * UNIT_ST] = si
        sr, si = lr * sr - li * si + inc_r, lr * si + li * sr + inc_i
    carry_ref[0, :, cols] = sr
    carry_ref[1, :, cols] = si


def _s5_direction(lhs_ref, s_ref, carry_ref, win_ref, wout_ref, lpow_ref, first_ref, first_s, emit, direction,
                  wk_ref=None, between=None):
    reverse = direction == BWD

    def lhs(unit):
        return lhs_ref[:, unit * KTILE:(unit + 1) * KTILE]

    bufs = s_ref.shape[0]
    ahead = min(bufs, N_UNITS) if bufs >= N_UNITS else bufs - 1

    def project_in(unit):
        s_ref[unit % bufs] = jnp.dot(lhs(unit), win_ref[unit], preferred_element_type=F32)

    for unit in range(min(ahead, N_UNITS)):
        project_in(unit)
    pending = []
    for unit in range(N_UNITS):
        if ahead <= unit + ahead < N_UNITS:
            project_in(unit + ahead)
        buf = s_ref.at[unit % bufs]
        _scan_unit(buf, lpow_ref, carry_ref, direction, unit, reverse)
        y = jnp.dot(buf[...].astype(BF16), wout_ref[unit], preferred_element_type=F32)
        if wk_ref is not None:
            y = y + jnp.dot(lhs(unit), wk_ref[unit], preferred_element_type=F32)
        pending.append(y)
        if len(pending) == PIECES:
            emit(unit // PIECES, _unpack_outputs(pending))
            pending = []
        if between is not None:
            between(unit)

    if first_ref is not None:
        first_rows = slice(QROWS - 2 * SEQS, QROWS) if reverse else slice(0, 2 * SEQS)
        lane = lax.broadcasted_iota(jnp.int32, (2 * SEQS, LANES), 1)
        zeros = jnp.zeros((2 * SEQS, LANES), F32)
        other = [zeros] * (KTILE // LANES - 1)
        for unit in range(N_UNITS):
            if reverse:
                tail = lhs_ref[first_rows, (unit + 1) * KTILE - LANES:(unit + 1) * KTILE].astype(F32)
                block = [jnp.where(lane < PIECE, pltpu.roll(tail, PIECE, 1), 0.0)] + other
            else:
                head = lhs_ref[first_rows, unit * KTILE:unit * KTILE + LANES].astype(F32)
                block = other + [jnp.where(lane >= LANES - PIECE, pltpu.roll(head, LANES - PIECE, 1), 0.0)]
            inc = jnp.dot(jnp.concatenate(block, axis=1).astype(BF16), win_ref[unit], preferred_element_type=F32)
            inc = inc[SEQS:2 * SEQS] if reverse else inc[0:SEQS]
            first_s[0, :, unit * UNIT_ST:(unit + 1) * UNIT_ST] = inc[:, 0:UNIT_ST]
            first_s[1, :, unit * UNIT_ST:(unit + 1) * UNIT_ST] = inc[:, UNIT_ST:2 * UNIT_ST]

        @pl.when(pl.program_id(1) == 0)
        def _():
            first_ref[...] = first_s[...]


def _s5_param_specs(direction):
    return [_const_spec((None, N_UNITS, KTILE, 2 * UNIT_ST), (direction, 0, 0, 0)),
            _const_spec((None, N_UNITS, 2 * UNIT_ST, KTILE), (direction, 0, 0, 0)),
            _const_spec((2, SEQS, 2 * ALL_ST))]


def _h0_specs(direction):
    return [_const_spec((SEQS, ALL_ST), (0, direction))] * 2


def _pass_a_kernel(*refs, rotary):
    if rotary:
        (x_ref, mod_ref, nw_ref, w_ref, win_ref, wout_ref, lpow_ref, wk_ref, d_ref, h0re_ref, h0im_ref, cos_ref, sin_ref,
         kcat_ref, vt_ref, lhs_ref, yb_ref, zs_ref, za_ref, ql_ref, qr_ref, h_s, u_sm, u_tm, s_s, carry) = refs
        fin_ref = first_s = None
    else:
        (x_ref, mod_ref, nw_ref, w_ref, win_ref, wout_ref, lpow_ref, wk_ref, d_ref,
         k_ref, v_ref, kcat_ref, vt_ref, lhs_ref, yb_ref, zs_ref, za_ref, ql_ref, qr_ref, fin_ref,
         h_s, u_sm, u_tm, s_s, carry, first_s) = refs

    @pl.when(pl.program_id(1) == 0)
    def _():
        if rotary:
            carry[0] = h0re_ref[...]
            carry[1] = h0im_ref[...]
        else:
            carry[...] = jnp.zeros_like(carry)

    half_seqs = SEQS // 2
    for hf in range(2):
        for b in range(hf * half_seqs, (hf + 1) * half_seqs):
            _norm_mod(h_s, x_ref, mod_ref, nw_ref, b, per_sequence=rotary)
        rows = slice(hf * half_seqs * TL, (hf + 1) * half_seqs * TL)
        u_half = jnp.dot(h_s[rows, :], w_ref[:, W_IN_U:W_IN_U + SSM_WIDTH], preferred_element_type=F32)
        _store_column_blocks(u_sm, u_half, first_seq=hf * half_seqs)
    _to_token_major(u_tm, u_sm)
    _pack_scan_operand(lhs_ref, u_tm)

    def project(c0, width):
        return jnp.dot(h_s[...], w_ref[:, c0:c0 + width], preferred_element_type=F32)

    def project_kv():
        kv = project(W_IN_K, 2 * KV_WIDTH)
        k = kv[:, 0:KV_WIDTH]
        v = kv[:, KV_WIDTH:2 * KV_WIDTH]
        if rotary:
            k = _rope(k, cos_ref[...], sin_ref[...])
        else:
            k_ref[...] = k.reshape(SEQS, TL, KV_WIDTH)
            v_ref[...] = v.reshape(SEQS, TL, KV_WIDTH)
        kcat = jnp.concatenate([k, pltpu.roll(k, HEAD_DIM, 1)], axis=1).astype(BF16)
        kcat_ref[...] = kcat.reshape(SEQS, TL, 2 * KV_WIDTH)
        for b in range(SEQS):
            vt_ref[b] = v[_seq_rows(b), :].T.astype(BF16)

    def project_q():
        qall = project(W_IN_Q, ATTN_WIDTH)
        even_head = lax.broadcasted_iota(jnp.int32, (ROWS, LANES), 1) < HEAD_DIM
        for cg in range(ATTN_WIDTH // LANES):
            cols = slice(cg * LANES, (cg + 1) * LANES)
            q = qall[:, cols]
            if rotary:
                q = _rope(q, cos_ref[...], sin_ref[...])
            q = q * (LOG2E * HEAD_DIM ** -0.5)
            ql_ref[:, cols] = jnp.where(even_head, q, 0.0).astype(BF16)
            qr_ref[:, cols] = jnp.where(even_head, 0.0, q).astype(BF16)

    def project_gate(c0, out_ref):
        out_ref[...] = jax.nn.silu(project(c0, out_ref.shape[1])).astype(BF16)

    side_work = [project_kv, project_q, lambda: project_gate(W_IN_ZS, zs_ref), lambda: project_gate(W_IN_ZA, za_ref)]

    spacing = N_UNITS // len(side_work)

    def between(i):
        if i % spacing == 0:
            side_work[i // spacing]()

    def emit(blk, slots):
        lanes = slice(blk * LANES, (blk + 1) * LANES)
        for slot, ys in enumerate(slots):
            rows = _slot_rows(slot)
            yb_ref[:, rows, lanes] = ys + u_tm[:, rows, lanes] * d_ref[:, lanes]

    _s5_direction(lhs_ref, s_s, carry, win_ref, wout_ref, lpow_ref, fin_ref, first_s, emit, BWD,
                  wk_ref=wk_ref, between=between)


def _pass_a(x, mod, norm_w, w_in, win, wout, lpow, wk, d, h0, rope):
    batch, length, _ = x.shape
    ng, nt = batch // SEQS, length // TL
    rotary = rope is not None
    rev = lambda t: nt - 1 - t
    in_specs = [
        pl.BlockSpec((SEQS, TL, D_MODEL), lambda g, t: (g, rev(t), 0)),
        _const_spec(mod.shape),
        _const_spec((1, D_MODEL)),
        _const_spec((D_MODEL, D_IN)),
    ] + _s5_param_specs(BWD) + [
        _const_spec((N_UNITS, KTILE, KTILE)),
        _const_spec((1, SSM_WIDTH)),
    ]
    args = [x, mod, norm_w, w_in, win, wout, lpow, wk, d]
    if rotary:
        in_specs += _h0_specs(BWD) + [pl.BlockSpec((TL, LANES), lambda g, t: (rev(t), 0))] * 2
        args += list(h0) + list(rope)
    assert ATTN_WIDTH == SSM_WIDTH
    tile_spec = pl.BlockSpec((None, None, ROWS, SSM_WIDTH), lambda g, t: (g, rev(t), 0, 0))
    tile_bf16 = jax.ShapeDtypeStruct((ng, nt, ROWS, SSM_WIDTH), BF16)
    out_specs = [
        pl.BlockSpec((SEQS, TL, 2 * KV_WIDTH), lambda g, t: (g, rev(t), 0)),
        pl.BlockSpec((SEQS, KV_WIDTH, TL), lambda g, t: (g, 0, rev(t))),
        pl.BlockSpec((None, None, QROWS, N_UNITS * KTILE), lambda g, t: (g, rev(t), 0, 0)),
        pl.BlockSpec((None, None, QUADS, BT * SEQS, SSM_WIDTH), lambda g, t: (g, rev(t), 0, 0, 0)),
    ] + [tile_spec] * 4
    out_shape = [
        jax.ShapeDtypeStruct((batch, length, 2 * KV_WIDTH), BF16),
        jax.ShapeDtypeStruct((batch, KV_WIDTH, length), BF16),
        jax.ShapeDtypeStruct((ng, nt, QROWS, N_UNITS * KTILE), BF16),
        jax.ShapeDtypeStruct((ng, nt, QUADS, BT * SEQS, SSM_WIDTH), F32),
        tile_bf16, tile_bf16,
        tile_bf16, tile_bf16,
    ]
    scratch = [
        pltpu.VMEM((ROWS, D_MODEL), BF16),
        pltpu.VMEM((SSM_BLOCKS, SEQS * PITCH, LANES), F32),
        pltpu.VMEM((QUADS, BT * SEQS, SSM_WIDTH), F32),
        pltpu.VMEM((S5_BUFS_A, QROWS, 2 * UNIT_ST), F32),
        pltpu.VMEM((2, SEQS, ALL_ST), F32),
    ]
    if not rotary:
        cache_spec = pl.BlockSpec((SEQS, TL, KV_WIDTH), lambda g, t: (g, rev(t), 0))
        cache_shape = jax.ShapeDtypeStruct((batch, length, KV_WIDTH), F32)
        out_specs = [cache_spec, cache_spec] + out_specs + [pl.BlockSpec((None, 2, SEQS, ALL_ST), lambda g, t: (g, 0, 0, 0))]
        out_shape = [cache_shape, cache_shape] + out_shape + [jax.ShapeDtypeStruct((ng, 2, SEQS, ALL_ST), F32)]
        scratch += [pltpu.VMEM((2, SEQS, ALL_ST), F32)]
    return pl.pallas_call(
        functools.partial(_pass_a_kernel, rotary=rotary),
        grid=(ng, nt),
        in_specs=in_specs, out_specs=out_specs, out_shape=out_shape, scratch_shapes=scratch,
        compiler_params=pltpu.CompilerParams(
            dimension_semantics=("arbitrary", "arbitrary"), vmem_limit_bytes=VMEM_LIMIT),
        name="pass_a_latent" if rotary else "pass_a_context",
    )(*args)


def _attention(b, q_refs, za_ref, mix_s, kcat_refs, vt_refs, sink_ref, masks):
    rows = _seq_rows(b)
    kcat = jnp.concatenate([r[b] for r in kcat_refs], axis=0)
    vt = jnp.concatenate([r[b] for r in vt_refs], axis=1)
    segment = lax.broadcasted_iota(jnp.int32, (1, 4 * TL), 1) // TL
    kv_heads = range(N_KV_HEADS)

    def side(g, grp):
        return g ^ grp

    def head(g, i, grp):
        return 4 * g + 2 * i + side(g, grp)

    def scores(grp):
        q = jnp.concatenate([q_refs[side(g, grp)][rows, blk * LANES:(blk + 1) * LANES]
                             for g in kv_heads for blk in (2 * g, 2 * g + 1)], axis=0)
        return lax.dot_general(kcat[:, grp * KV_WIDTH:(grp + 1) * KV_WIDTH], q, (((1,), (1,)), ((), ())),
                               preferred_element_type=F32)

    def softmax(grp, s):
        if masks is None:
            pieces = [s]
        else:
            pieces = [jnp.where(masks[0], s[0:TL], NEG_INF), s[TL:2 * TL],
                      jnp.where(masks[1], s[2 * TL:3 * TL], NEG_INF), s[3 * TL:]]
        sinks = [sink_ref[head(g, i, grp)] * LOG2E for g in kv_heads for i in range(2)]
        sink = jnp.where(segment == 0, sinks[0], jnp.where(segment == 1, sinks[1],
                                                           jnp.where(segment == 2, sinks[2], sinks[3])))
        m = sink
        for piece in pieces:
            m = jnp.maximum(m, jnp.max(piece, axis=0, keepdims=True))
        ps = [jnp.exp2(piece - m).astype(BF16) for piece in pieces]
        return (ps[0] if len(ps) == 1 else jnp.concatenate(ps, axis=0)), jnp.exp2(sink - m)

    ones_rows = jnp.ones((BF16_SUBLANES, vt.shape[1]), BF16)

    all_scores = [scores(grp) for grp in range(2)]
    probs, sink_terms = zip(*[softmax(grp, s) for grp, s in enumerate(all_scores)])
    for g in kv_heads:
        cols = slice(g * 2 * TL, (g + 1) * 2 * TL)
        values = jnp.concatenate([vt[g * HEAD_DIM:(g + 1) * HEAD_DIM, :], ones_rows], axis=0)
        p = jnp.concatenate([probs[0][:, cols], probs[1][:, cols]], axis=1)
        sink_term = jnp.concatenate([sink_terms[0][:, cols], sink_terms[1][:, cols]], axis=1)
        ot = jnp.dot(values, p, preferred_element_type=F32)
        out_t = ot[0:HEAD_DIM] * (1.0 / (ot[HEAD_DIM:HEAD_DIM + 1] + sink_term))
        for i in range(2):
            left = out_t[:, (2 * g + i) * TL:(2 * g + i + 1) * TL]
            right = out_t[:, (2 * (g ^ 1) + i) * TL:(2 * (g ^ 1) + i + 1) * TL]
            blk_cols = slice((2 * g + i) * LANES, (2 * g + i + 1) * LANES)
            pair_t = jnp.concatenate([left, right], axis=0)
            gated = pair_t.T * za_ref[rows, blk_cols].astype(F32)
            mix_s[rows, ATTN_WIDTH + blk_cols.start:ATTN_WIDTH + blk_cols.stop] = gated.astype(BF16)


def _pass_b_kernel(*refs, windowed, n_tiles):
    if windowed:
        (sink_ref, x_ref, mod_ref, win_ref, wout_ref, lpow_ref, wg_ref, bg_ref,
         lhs_ref, yb_ref, zs_ref, za_ref, ql_ref, qr_ref, kx_ref, vx_ref, wo_ref, fw_ref, h0re_ref, h0im_ref,
         kp_ref, kc_ref, kn_ref, vp_ref, vc_ref, vn_ref,
         y_ref, y_sm, y_tm, s_s, carry, mix_s) = refs
        fin_ref = first_s = None
    else:
        (sink_ref, x_ref, mod_ref, win_ref, wout_ref, lpow_ref, wg_ref, bg_ref,
         lhs_ref, yb_ref, zs_ref, za_ref, ql_ref, qr_ref, kx_ref, vx_ref, wo_ref, fw_ref,
         y_ref, fin_ref, y_sm, y_tm, s_s, carry, mix_s, first_s) = refs
    t = pl.program_id(1)

    @pl.when(t == 0)
    def _():
        if windowed:
            carry[0] = h0re_ref[...]
            carry[1] = h0im_ref[...]
        else:
            carry[...] = jnp.zeros_like(carry)

    if windowed:
        kj = lax.broadcasted_iota(jnp.int32, (TL, 4 * TL), 0)
        qi = lax.broadcasted_iota(jnp.int32, (TL, 4 * TL), 1) & (TL - 1)
        masks = ((kj >= qi) & (t > 0), (kj <= qi) & (t < n_tiles - 1))
        key_refs = (kp_ref, kc_ref, kn_ref, kx_ref)
        val_refs = (vp_ref, vc_ref, vn_ref, vx_ref)
    else:
        masks = None
        key_refs = (kx_ref,)
        val_refs = (vx_ref,)

    def attention(b):
        _attention(b, (ql_ref, qr_ref), za_ref, mix_s, key_refs, val_refs, sink_ref, masks)

    def emit(blk, slots):
        lanes = slice(blk * LANES, (blk + 1) * LANES)
        for slot, ys in enumerate(slots):
            rows = _slot_rows(slot)
            y_tm[:, rows, lanes] = ys + yb_ref[:, rows, lanes]

    glu_blocks = QUADS // 4

    def glu(rc):
        pr = slice(rc * glu_blocks, (rc + 1) * glu_blocks)
        y = jax.nn.gelu(y_tm[pr].reshape(glu_blocks * BT * SEQS, SSM_WIDTH))
        gl = jnp.dot(y.astype(BF16), wg_ref[...], preferred_element_type=F32) + bg_ref[...]
        y_tm[pr] = (y * jax.nn.sigmoid(gl)).reshape(glu_blocks, BT * SEQS, SSM_WIDTH)

    def gate_s5(seqs):
        for b in seqs:
            for cb in range(SSM_BLOCKS):
                cols = slice(cb * LANES, (cb + 1) * LANES)
                gated = y_sm[cb, b * PITCH:b * PITCH + TL, :] * zs_ref[_seq_rows(b), cols].astype(F32)
                mix_s[_seq_rows(b), cols] = gated.astype(BF16)

    half_seqs = SEQS // 2

    def project_out(hf):
        rows = slice(hf * half_seqs * TL, (hf + 1) * half_seqs * TL)
        proj = jnp.dot(mix_s[rows, :], wo_ref[...], preferred_element_type=F32)
        y_ref[hf * half_seqs:(hf + 1) * half_seqs] = proj.reshape(half_seqs, TL, D_MODEL)

    def finish(seqs):
        for b in seqs:
            m = _mod_row(b, per_sequence=windowed)
            gate = mod_ref[m:m + 1, 2 * D_MODEL:3 * D_MODEL]
            r = x_ref[b] + gate * y_ref[b]
            ms = jnp.mean(r * r, axis=-1, keepdims=True)
            y_ref[b] = (r * lax.rsqrt(ms + EPS)) * fw_ref[...]

    first_half, second_half = range(0, half_seqs), range(half_seqs, SEQS)

    _s5_direction(lhs_ref, s_s, carry, win_ref, wout_ref, lpow_ref, fin_ref, first_s, emit, FWD)
    assert SEQS == 8
    follow_up = [
        lambda: glu(0), lambda: glu(1), lambda: glu(2),
        lambda: (glu(3), _to_sequence_major(y_sm, y_tm)),
        lambda: (gate_s5(first_half), project_out(0)),
        lambda: finish(first_half),
        lambda: gate_s5(second_half),
        lambda: (project_out(1), finish(second_half)),
    ]
    for b in range(SEQS):
        attention(b)
        follow_up[b]()


def _pass_b(x, mod, win, wout, lpow, w_glu, b_glu, from_a, kx, vx, w_out, fnorm_w, sink, h0, k_loc, v_loc):
    batch, length, _ = x.shape
    ng, nt = batch // SEQS, length // TL
    windowed = k_loc is not None
    tile_spec = pl.BlockSpec((None, None, ROWS, SSM_WIDTH), lambda g, t: (g, t, 0, 0))
    in_specs = [
        pl.BlockSpec(memory_space=pltpu.SMEM),
        pl.BlockSpec((SEQS, TL, D_MODEL), lambda g, t: (g, t, 0)),
        _const_spec(mod.shape),
    ] + _s5_param_specs(FWD) + [
        _const_spec((SSM_WIDTH, SSM_WIDTH)),
        _const_spec((1, SSM_WIDTH)),
        pl.BlockSpec((None, None, QROWS, N_UNITS * KTILE), lambda g, t: (g, t, 0, 0)),
        pl.BlockSpec((None, None, QUADS, BT * SEQS, SSM_WIDTH), lambda g, t: (g, t, 0, 0, 0)),
    ] + [tile_spec] * 4 + [
        pl.BlockSpec((SEQS, PAST_LEN, 2 * KV_WIDTH), lambda g, t: (g, 0, 0)),
        pl.BlockSpec((SEQS, KV_WIDTH, PAST_LEN), lambda g, t: (g, 0, 0)),
        _const_spec((D_MODEL, D_MODEL)),
        _const_spec((1, D_MODEL)),
    ]
    args = [sink, x, mod, win, wout, lpow, w_glu, b_glu, *from_a, kx, vx, w_out, fnorm_w]
    if windowed:
        in_specs += _h0_specs(FWD)
        args += list(h0)
        band_t = (lambda t: jnp.maximum(t - 1, 0), lambda t: t, lambda t: jnp.minimum(t + 1, nt - 1))
        in_specs += [pl.BlockSpec((SEQS, TL, 2 * KV_WIDTH), lambda g, t, f=f: (g, f(t), 0)) for f in band_t]
        in_specs += [pl.BlockSpec((SEQS, KV_WIDTH, TL), lambda g, t, f=f: (g, 0, f(t))) for f in band_t]
        args += [k_loc] * 3 + [v_loc] * 3
    out_specs = [pl.BlockSpec((SEQS, TL, D_MODEL), lambda g, t: (g, t, 0))]
    out_shape = [jax.ShapeDtypeStruct((batch, length, D_MODEL), F32)]
    scratch = [
        pltpu.VMEM((SSM_BLOCKS, SEQS * PITCH, LANES), F32),
        pltpu.VMEM((QUADS, BT * SEQS, SSM_WIDTH), F32),
        pltpu.VMEM((S5_BUFS_B, QROWS, 2 * UNIT_ST), F32),
        pltpu.VMEM((2, SEQS, ALL_ST), F32),
        pltpu.VMEM((ROWS, D_MODEL), BF16),
    ]
    if not windowed:
        out_specs += [pl.BlockSpec((None, 2, SEQS, ALL_ST), lambda g, t: (g, 0, 0, 0))]
        out_shape += [jax.ShapeDtypeStruct((ng, 2, SEQS, ALL_ST), F32)]
        scratch += [pltpu.VMEM((2, SEQS, ALL_ST), F32)]
    return pl.pallas_call(
        functools.partial(_pass_b_kernel, windowed=windowed, n_tiles=nt),
        grid=(ng, nt),
        in_specs=in_specs, out_specs=out_specs, out_shape=out_shape, scratch_shapes=scratch,
        compiler_params=pltpu.CompilerParams(
            dimension_semantics=("arbitrary", "arbitrary"), vmem_limit_bytes=VMEM_LIMIT),
        name="pass_b_latent" if windowed else "pass_b_context",
    )(*args)


def kernel(x_prompt, x_sample, c, cache_k, cache_v, state_ssm_re, state_ssm_im, c_ctx, norm_w, w_mod, b_mod, w_in, ssm_lambda_re, ssm_lambda_im, ssm_log_dt, ssm_b_re, ssm_b_im, ssm_c_re, ssm_c_im, ssm_d, w_glu, b_glu, attn_sink, w_out, final_norm_w):
    assert norm_w.shape[0] == 1, "single trunk layer"
    batch, seq, _ = x_prompt.shape
    dec_batch, dec_seq, _ = x_sample.shape
    assert dec_batch == SEQS and batch % SEQS == 0 and seq % TL == 0 and dec_seq % TL == 0
    assert seq == PAST_LEN and cache_k.shape[2] == PAST_LEN

    w_in0 = w_in[0].astype(BF16)
    nw = norm_w[0][None, :]
    fw = final_norm_w[None, :]
    d = ssm_d[0][None, :]
    wg = w_glu[0].astype(BF16)
    bg = b_glu[0][None, :]
    wo = w_out[0].astype(BF16)
    sink = attn_sink[0]

    cond = jnp.concatenate([c, c_ctx[None, :], jnp.zeros((16 - SEQS - 1, D_MODEL), F32)], axis=0)
    mod = _modulation(cond, w_mod[0], b_mod[0][None, :])

    win, wout, wk, lpow = _s5_weights(ssm_lambda_re[0], ssm_lambda_im[0], ssm_log_dt[0], ssm_b_re[0], ssm_b_im[0],
                                     ssm_c_re[0], ssm_c_im[0])
    h0 = (state_ssm_re.reshape(SEQS, 2 * ALL_ST), state_ssm_im.reshape(SEQS, 2 * ALL_ST))
    rope = _rope_tables(dec_seq)

    k_ctx, v_ctx, kcat_ctx, vt_ctx, *tiles_ctx, fin_b = _pass_a(
        x_prompt, mod, nw, w_in0, win, wout, lpow, wk, d, None, None)
    y_prompt, fin_f = _pass_b(x_prompt, mod, win, wout, lpow, wg, bg, tiles_ctx, kcat_ctx, vt_ctx, wo, fw, sink,
                              None, None, None)

    kcat_lat, vt_lat, *tiles_lat = _pass_a(x_sample, mod, nw, w_in0, win, wout, lpow, wk, d, h0, rope)
    kx = cache_k[:, 0].reshape(dec_batch, PAST_LEN, KV_WIDTH)
    kx = jnp.concatenate([kx, jnp.roll(kx, HEAD_DIM, axis=-1)], axis=-1).astype(BF16)
    vx = jnp.swapaxes(cache_v[:, 0].reshape(dec_batch, PAST_LEN, KV_WIDTH), 1, 2).astype(BF16)
    (y_sample,) = _pass_b(x_sample, mod, win, wout, lpow, wg, bg, tiles_lat, kx, vx, wo, fw, sink,
                          h0, kcat_lat, vt_lat)

    new_cache_k = k_ctx.reshape(batch, 1, seq, N_KV_HEADS, HEAD_DIM)
    new_cache_v = v_ctx.reshape(batch, 1, seq, N_KV_HEADS, HEAD_DIM)

    def states(fin, part):
        return fin[:, part].reshape(batch, SSM_GROUPS, SSM_STATE)

    new_re = jnp.stack([states(fin_f, 0), states(fin_b, 0)], axis=1)[:, None]
    new_im = jnp.stack([states(fin_f, 1), states(fin_b, 1)], axis=1)[:, None]
    return (y_prompt, y_sample, new_cache_k, new_cache_v, new_re, new_im)
```

```python
import functools
import math

import jax
import jax.numpy as jnp
from jax import lax
from jax.experimental import pallas as pl
from jax.experimental.pallas import tpu as pltpu

F32 = jnp.float32
BF16 = jnp.bfloat16

D_MODEL = 1024
SSM_WIDTH = 512
ATTN_WIDTH = 512
SSM_GROUP = 16
SSM_GROUPS = 32
SSM_STATE = 64
HEAD_DIM = 64
N_KV_HEADS = 2
KV_WIDTH = 128
GRID_W = 64
ROPE_AXIS_DIM = 32
ROPE_BASE = 10000.0
EPS = 1e-6
LAMBDA_RE_MAX = -1e-4
NEG_INF = -1e30
LOG2E = math.log2(math.e)
PAST_LEN = 256

LANES = 128
BF16_SUBLANES = 16
SEQS = 8
TL = 128
ROWS = SEQS * TL
PITCH = TL + 8
SSM_BLOCKS = SSM_WIDTH // LANES
ALL_ST = SSM_GROUPS * SSM_STATE
BT = 8
KTILE = 2 * LANES
PIECE = KTILE // BT
PIECES = LANES // PIECE
UNIT_GROUPS = PIECE // SSM_GROUP
UNIT_ST = UNIT_GROUPS * SSM_STATE
N_UNITS = SSM_WIDTH // PIECE
W_UNITS_PER_STEP = LANES // PIECE
QUADS = TL // BT
QROWS = QUADS * SEQS
S5_BUFS_A = 8
S5_BUFS_B = 2
V7X_VMEM_BYTES = 64 * 1024 * 1024
VMEM_LIMIT = V7X_VMEM_BYTES - 6 * 1024 * 1024

BWD, FWD = 1, 0
W_IN_U = 0
W_IN_ZS = SSM_WIDTH
W_IN_Q = 2 * SSM_WIDTH
W_IN_K = 2 * SSM_WIDTH + ATTN_WIDTH
W_IN_ZA = W_IN_K + 2 * KV_WIDTH
D_IN = W_IN_ZA + ATTN_WIDTH


def _const_spec(shape, index=None):
    index = (0,) * len(shape) if index is None else index
    return pl.BlockSpec(shape, lambda g, t: index, pipeline_mode=pl.Buffered(1))


def _modulation_kernel(cond_ref, w_ref, b_ref, out_ref):
    a = jax.nn.silu(cond_ref[...]).astype(BF16)
    out_ref[...] = jnp.dot(a, w_ref[...].astype(BF16), preferred_element_type=F32) + b_ref[...]


def _modulation(cond, w_mod, b_mod):
    rows = cond.shape[0]
    n = w_mod.shape[1]
    return pl.pallas_call(
        _modulation_kernel,
        grid=(1,),
        in_specs=[pl.BlockSpec((rows, D_MODEL), lambda j: (0, 0)),
                  pl.BlockSpec((D_MODEL, n), lambda j: (0, 0), pipeline_mode=pl.Buffered(1)),
                  pl.BlockSpec((1, n), lambda j: (0, 0))],
        out_specs=pl.BlockSpec((rows, n), lambda j: (0, 0)),
        out_shape=jax.ShapeDtypeStruct((rows, n), F32),
        compiler_params=pltpu.CompilerParams(vmem_limit_bytes=VMEM_LIMIT // 2),
        name="modulation",
    )(cond, w_mod, b_mod)


def _zoh(lam_re, lam_im, log_dt):
    lam_re = jnp.minimum(lam_re, LAMBDA_RE_MAX)
    dt = jnp.exp(log_dt)
    mag = jnp.exp(lam_re * dt)
    ang = lam_im * dt
    lbar_re = mag * jnp.cos(ang)
    lbar_im = mag * jnp.sin(ang)
    nr = lbar_re - 1.0
    ni = lbar_im
    den = lam_re * lam_re + lam_im * lam_im
    f_re = (nr * lam_re + ni * lam_im) / den
    f_im = (ni * lam_re - nr * lam_im) / den
    return lbar_re, lbar_im, f_re, f_im


def _cmul(a, b):
    return a[0] * b[0] - a[1] * b[1], a[0] * b[1] + a[1] * b[0]


def _s5_weights_kernel(lam_ref, bt_ref, cre_ref, cim_ref, lamflat_ref, win_ref, wout_ref, wk_ref, lpow_ref):
    p_row = lax.broadcasted_iota(jnp.int32, (SSM_STATE, UNIT_ST), 0)
    p_col = lax.broadcasted_iota(jnp.int32, (SSM_STATE, UNIT_ST), 1) & (SSM_STATE - 1)
    spread = jnp.where(p_row == p_col, 1.0, 0.0).astype(BF16)
    g_row = lax.broadcasted_iota(jnp.int32, (PIECE, UNIT_ST), 0) // SSM_GROUP
    g_col = lax.broadcasted_iota(jnp.int32, (PIECE, UNIT_ST), 1) // SSM_STATE
    own_block = g_row == g_col

    def blocks(x):
        return jnp.where(own_block, jnp.dot(x.astype(BF16), spread, preferred_element_type=F32), 0.0)

    def blocks2(z):
        return blocks(z[0]).astype(BF16), blocks(z[1]).astype(BF16)

    def in_block(src, c):
        dims = (((1,), (1,)), ((), ()))
        return (lax.dot_general(src[0], c[0], dims, preferred_element_type=F32)
                - lax.dot_general(src[1], c[1], dims, preferred_element_type=F32))

    def powers(z, lbar, n):
        out = [z]
        for _ in range(n):
            out.append(_cmul(out[-1], lbar))
        return out

    pows = {}
    for d in (FWD, BWD):
        lbar_re, lbar_im, f_re, f_im = _zoh(lam_ref[0, d], lam_ref[1, d], lam_ref[2, d])
        lbar = (lbar_re, lbar_im)
        pows[d] = (powers(_cmul((f_re, f_im), (bt_ref[0, d], bt_ref[1, d])), lbar, BT - 1),
                   powers((cre_ref[d], cim_ref[d]), lbar, BT))
    for k in range(W_UNITS_PER_STEP):
        rows = slice(k * PIECE, (k + 1) * PIECE)
        unit_rows = lambda z: (z[0][rows], z[1][rows])
        k_lag = {}
        for d in (FWD, BWD):
            bb_pow = [unit_rows(z) for z in pows[d][0]]
            c_pow = [unit_rows(z) for z in pows[d][1]]
            c_b = blocks2(c_pow[0])
            bb_b = [blocks2(z) for z in bb_pow]
            to_end = [(BT - 1 - s) if d == FWD else s for s in range(BT)]
            for s in range(BT):
                w = bb_b[to_end[s]]
                win_ref[d, k, s * PIECE:(s + 1) * PIECE, 0:UNIT_ST] = w[0]
                win_ref[d, k, s * PIECE:(s + 1) * PIECE, UNIT_ST:2 * UNIT_ST] = w[1]
            out_re = jnp.concatenate([blocks(c_pow[BT - to_end[s]][0]) for s in range(BT)], axis=0)
            out_im = jnp.concatenate([blocks(-c_pow[BT - to_end[s]][1]) for s in range(BT)], axis=0)
            wout_ref[d, k, 0:UNIT_ST, :] = out_re.T.astype(BF16)
            wout_ref[d, k, UNIT_ST:2 * UNIT_ST, :] = out_im.T.astype(BF16)
            k_lag[d] = [in_block(bb_b[m], c_b) for m in range(BT)]
        for src in range(BT):
            for out in range(BT):
                if out == src:
                    kk = k_lag[FWD][0] + k_lag[BWD][0]
                elif out > src:
                    kk = k_lag[FWD][out - src]
                else:
                    kk = k_lag[BWD][src - out]
                wk_ref[k, src * PIECE:(src + 1) * PIECE, out * PIECE:(out + 1) * PIECE] = kk.astype(BF16)

    @pl.when(pl.program_id(0) == 0)
    def _():
        lbar_re, lbar_im, _, _ = _zoh(lamflat_ref[0], lamflat_ref[1], lamflat_ref[2])
        p = (lbar_re, lbar_im)
        for _ in range(BT.bit_length() - 1):
            p = _cmul(p, p)
        lpow_ref[0] = p[0]
        lpow_ref[1] = p[1]


def _s5_weights(lam_re, lam_im, log_dt, b_re, b_im, c_re, c_im):
    assert BT & (BT - 1) == 0
    n = SSM_GROUPS * SSM_GROUP
    lam3 = jnp.stack([lam_re, lam_im, jnp.broadcast_to(log_dt[:, :, None], lam_re.shape)])
    lam_rows = jnp.broadcast_to(lam3[:, :, :, None, :], (3, 2, SSM_GROUPS, SSM_GROUP, SSM_STATE)).reshape(3, 2, n, SSM_STATE)
    lam_flat = jnp.broadcast_to(lam3.reshape(3, 1, 2 * ALL_ST), (3, SEQS, 2 * ALL_ST))
    bt = jnp.transpose(jnp.stack([b_re, b_im]), (0, 1, 2, 4, 3)).reshape(2, 2, n, SSM_STATE)
    return pl.pallas_call(
        _s5_weights_kernel,
        grid=(N_UNITS // W_UNITS_PER_STEP,),
        in_specs=[pl.BlockSpec((3, 2, W_UNITS_PER_STEP * PIECE, SSM_STATE), lambda m: (0, 0, m, 0)),
                  pl.BlockSpec((2, 2, W_UNITS_PER_STEP * PIECE, SSM_STATE), lambda m: (0, 0, m, 0)),
                  pl.BlockSpec((2, W_UNITS_PER_STEP * PIECE, SSM_STATE), lambda m: (0, m, 0)),
                  pl.BlockSpec((2, W_UNITS_PER_STEP * PIECE, SSM_STATE), lambda m: (0, m, 0)),
                  pl.BlockSpec((3, SEQS, 2 * ALL_ST), lambda m: (0, 0, 0))],
        out_specs=[pl.BlockSpec((2, W_UNITS_PER_STEP, KTILE, 2 * UNIT_ST), lambda m: (0, m, 0, 0)),
                   pl.BlockSpec((2, W_UNITS_PER_STEP, 2 * UNIT_ST, KTILE), lambda m: (0, m, 0, 0)),
                   pl.BlockSpec((W_UNITS_PER_STEP, KTILE, KTILE), lambda m: (m, 0, 0)),
                   pl.BlockSpec((2, SEQS, 2 * ALL_ST), lambda m: (0, 0, 0))],
        out_shape=[jax.ShapeDtypeStruct((2, N_UNITS, KTILE, 2 * UNIT_ST), BF16),
                   jax.ShapeDtypeStruct((2, N_UNITS, 2 * UNIT_ST, KTILE), BF16),
                   jax.ShapeDtypeStruct((N_UNITS, KTILE, KTILE), BF16),
                   jax.ShapeDtypeStruct((2, SEQS, 2 * ALL_ST), F32)],
        name="s5_weights",
    )(lam_rows, bt, c_re.reshape(2, n, SSM_STATE), c_im.reshape(2, n, SSM_STATE), lam_flat)


def _rope_tables(length):
    pos = jnp.arange(length)
    row = (pos // GRID_W).astype(F32)
    col = (pos % GRID_W).astype(F32)
    freqs = ROPE_BASE ** (-jnp.arange(0, ROPE_AXIS_DIM, 2, dtype=F32) / ROPE_AXIS_DIM)
    ang_r = row[:, None] * freqs[None, :]
    ang_c = col[:, None] * freqs[None, :]
    cos = jnp.concatenate([jnp.cos(ang_r), jnp.cos(ang_r), jnp.cos(ang_c), jnp.cos(ang_c)], axis=-1)
    sin = jnp.concatenate([-jnp.sin(ang_r), jnp.sin(ang_r), -jnp.sin(ang_c), jnp.sin(ang_c)], axis=-1)
    return jnp.tile(cos, (1, 2)), jnp.tile(sin, (1, 2))


def _seq_rows(b):
    return slice(b * TL, (b + 1) * TL)


def _mod_row(b, per_sequence):
    return b if per_sequence else SEQS


def _norm_mod(h_ref, x_ref, mod_ref, nw_ref, b, per_sequence):
    xb = x_ref[b]
    ms = jnp.mean(xb * xb, axis=-1, keepdims=True)
    m = _mod_row(b, per_sequence)
    shift = mod_ref[m:m + 1, 0:D_MODEL]
    gain = nw_ref[...] * (1.0 + mod_ref[m:m + 1, D_MODEL:2 * D_MODEL])
    h_ref[_seq_rows(b), :] = ((xb * lax.rsqrt(ms + EPS)) * gain + shift).astype(BF16)


def _rope(x, cos, sin):
    lane = lax.broadcasted_iota(jnp.int32, x.shape, 1)
    first = (lane & 31) < 16
    partner = jnp.where(first, pltpu.roll(x, LANES - 16, 1), pltpu.roll(x, 16, 1))
    x3 = x.reshape(SEQS, TL, LANES)
    p3 = partner.reshape(SEQS, TL, LANES)
    return (x3 * cos[None] + p3 * sin[None]).reshape(ROWS, LANES)


def _store_column_blocks(dst_ref, val, first_seq):
    for j in range(val.shape[0] // TL):
        b = first_seq + j
        for cb in range(SSM_BLOCKS):
            dst_ref[cb, b * PITCH:b * PITCH + TL, :] = val[_seq_rows(j), cb * LANES:(cb + 1) * LANES]


def _token_rows(l):
    return l // BT, slice((l % BT) * SEQS, (l % BT + 1) * SEQS)


def _slot_rows(slot):
    return slice(slot * SEQS, (slot + 1) * SEQS)


def _to_token_major(dst_ref, src_ref):
    for l in range(TL):
        j, rows = _token_rows(l)
        for cb in range(SSM_BLOCKS):
            dst_ref[j, rows, cb * LANES:(cb + 1) * LANES] = src_ref[cb, pl.ds(l, SEQS, stride=PITCH), :]


def _to_sequence_major(dst_ref, src_ref):
    for l in range(TL):
        j, rows = _token_rows(l)
        for cb in range(SSM_BLOCKS):
            dst_ref[cb, pl.ds(l, SEQS, stride=PITCH), :] = src_ref[j, rows, cb * LANES:(cb + 1) * LANES]


def _transpose_pieces(vs):
    vs = list(vs)
    assert len(vs) == PIECES
    lane = lax.broadcasted_iota(jnp.int32, vs[0].shape, 1)
    span = LANES // 2
    while span >= PIECE:
        stride = span // PIECE
        even = (lane // span) % 2 == 0
        out = list(vs)
        for i in range(PIECES):
            if (i // stride) % 2 == 0:
                a, b = vs[i], vs[i + stride]
                out[i] = jnp.where(even, a, pltpu.roll(b, span, 1))
                out[i + stride] = jnp.where(even, pltpu.roll(a, LANES - span, 1), b)
        vs = out
        span //= 2
    return vs


def _pack_scan_operand(lhs_ref, u_tm):
    for blk in range(SSM_BLOCKS):
        lanes = slice(blk * LANES, (blk + 1) * LANES)
        for v in range(BT // PIECES):
            slots = [u_tm[:, _slot_rows(v * PIECES + j), lanes].reshape(QROWS, LANES) for j in range(PIECES)]
            for k, val in enumerate(_transpose_pieces(slots)):
                c0 = (blk * PIECES + k) * KTILE + v * LANES
                lhs_ref[:, c0:c0 + LANES] = val.astype(BF16)


def _unpack_outputs(ys):
    out = []
    for v in range(BT // PIECES):
        cols = slice(v * LANES, (v + 1) * LANES)
        for val in _transpose_pieces([y[:, cols] for y in ys]):
            out.append(val.reshape(QUADS, SEQS, LANES))
    return out


def _scan_unit(s_ref, lpow_ref, carry_ref, direction, unit, reverse):
    cols = slice(unit * UNIT_ST, (unit + 1) * UNIT_ST)
    lanes = slice(direction * ALL_ST + unit * UNIT_ST, direction * ALL_ST + (unit + 1) * UNIT_ST)
    lr = lpow_ref[0, :, lanes]
    li = lpow_ref[1, :, lanes]
    sr = carry_ref[0, :, cols]
    si = carry_ref[1, :, cols]
    for i in range(QUADS):
        j = (QUADS - 1 - i) if reverse else i
        rows = slice(j * SEQS, (j + 1) * SEQS)
        inc_r = s_ref[rows, 0:UNIT_ST]
        inc_i = s_ref[rows, UNIT_ST:2 * UNIT_ST]
        s_ref[rows, 0:UNIT_ST] = sr
        s_ref[rows, UNIT_ST:2 * UNIT_ST] = si
        sr, si = lr * sr - li * si + inc_r, lr * si + li * sr + inc_i
    carry_ref[0, :, cols] = sr
    carry_ref[1, :, cols] = si


def _s5_direction(lhs_ref, s_ref, carry_ref, win_ref, wout_ref, lpow_ref, first_ref, first_s, emit, direction,
                  wk_ref=None, between=None):
    reverse = direction == BWD

    def lhs(unit):
        return lhs_ref[:, unit * KTILE:(unit + 1) * KTILE]

    bufs = s_ref.shape[0]
    ahead = min(bufs, N_UNITS) if bufs >= N_UNITS else bufs - 1

    def project_in(unit):
        s_ref[unit % bufs] = jnp.dot(lhs(unit), win_ref[unit], preferred_element_type=F32)

    for unit in range(min(ahead, N_UNITS)):
        project_in(unit)
    pending = []
    for unit in range(N_UNITS):
        if ahead <= unit + ahead < N_UNITS:
            project_in(unit + ahead)
        buf = s_ref.at[unit % bufs]
        _scan_unit(buf, lpow_ref, carry_ref, direction, unit, reverse)
        y = jnp.dot(buf[...].astype(BF16), wout_ref[unit], preferred_element_type=F32)
        if wk_ref is not None:
            y = y + jnp.dot(lhs(unit), wk_ref[unit], preferred_element_type=F32)
        pending.append(y)
        if len(pending) == PIECES:
            emit(unit // PIECES, _unpack_outputs(pending))
            pending = []
        if between is not None:
            between(unit)

    if first_ref is not None:
        first_rows = slice(QROWS - 2 * SEQS, QROWS) if reverse else slice(0, 2 * SEQS)
        lane = lax.broadcasted_iota(jnp.int32, (2 * SEQS, LANES), 1)
        zeros = jnp.zeros((2 * SEQS, LANES), F32)
        other = [zeros] * (KTILE // LANES - 1)
        for unit in range(N_UNITS):
            if reverse:
                tail = lhs_ref[first_rows, (unit + 1) * KTILE - LANES:(unit + 1) * KTILE].astype(F32)
                block = [jnp.where(lane < PIECE, pltpu.roll(tail, PIECE, 1), 0.0)] + other
            else:
                head = lhs_ref[first_rows, unit * KTILE:unit * KTILE + LANES].astype(F32)
                block = other + [jnp.where(lane >= LANES - PIECE, pltpu.roll(head, LANES - PIECE, 1), 0.0)]
            inc = jnp.dot(jnp.concatenate(block, axis=1).astype(BF16), win_ref[unit], preferred_element_type=F32)
            inc = inc[SEQS:2 * SEQS] if reverse else inc[0:SEQS]
            first_s[0, :, unit * UNIT_ST:(unit + 1) * UNIT_ST] = inc[:, 0:UNIT_ST]
            first_s[1, :, unit * UNIT_ST:(unit + 1) * UNIT_ST] = inc[:, UNIT_ST:2 * UNIT_ST]

        @pl.when(pl.program_id(1) == 0)
        def _():
            first_ref[...] = first_s[...]


def _s5_param_specs(direction):
    return [_const_spec((None, N_UNITS, KTILE, 2 * UNIT_ST), (direction, 0, 0, 0)),
            _const_spec((None, N_UNITS, 2 * UNIT_ST, KTILE), (direction, 0, 0, 0)),
            _const_spec((2, SEQS, 2 * ALL_ST))]


def _h0_specs(direction):
    return [_const_spec((SEQS, ALL_ST), (0, direction))] * 2


def _pass_a_kernel(*refs, rotary):
    if rotary:
        (x_ref, mod_ref, nw_ref, w_ref, win_ref, wout_ref, lpow_ref, wk_ref, d_ref, h0re_ref, h0im_ref, cos_ref, sin_ref,
         kcat_ref, vt_ref, lhs_ref, yb_ref, zs_ref, za_ref, ql_ref, qr_ref, h_s, u_sm, u_tm, s_s, carry) = refs
        fin_ref = first_s = None
    else:
        (x_ref, mod_ref, nw_ref, w_ref, win_ref, wout_ref, lpow_ref, wk_ref, d_ref,
         k_ref, v_ref, kcat_ref, vt_ref, lhs_ref, yb_ref, zs_ref, za_ref, ql_ref, qr_ref, fin_ref,
         h_s, u_sm, u_tm, s_s, carry, first_s) = refs

    @pl.when(pl.program_id(1) == 0)
    def _():
        if rotary:
            carry[0] = h0re_ref[...]
            carry[1] = h0im_ref[...]
        else:
            carry[...] = jnp.zeros_like(carry)

    half_seqs = SEQS // 2
    for hf in range(2):
        for b in range(hf * half_seqs, (hf + 1) * half_seqs):
            _norm_mod(h_s, x_ref, mod_ref, nw_ref, b, per_sequence=rotary)
        rows = slice(hf * half_seqs * TL, (hf + 1) * half_seqs * TL)
        u_half = jnp.dot(h_s[rows, :], w_ref[:, W_IN_U:W_IN_U + SSM_WIDTH], preferred_element_type=F32)
        _store_column_blocks(u_sm, u_half, first_seq=hf * half_seqs)
    _to_token_major(u_tm, u_sm)
    _pack_scan_operand(lhs_ref, u_tm)

    def project(c0, width):
        return jnp.dot(h_s[...], w_ref[:, c0:c0 + width], preferred_element_type=F32)

    def project_kv():
        kv = project(W_IN_K, 2 * KV_WIDTH)
        k = kv[:, 0:KV_WIDTH]
        v = kv[:, KV_WIDTH:2 * KV_WIDTH]
        if rotary:
            k = _rope(k, cos_ref[...], sin_ref[...])
        else:
            k_ref[...] = k.reshape(SEQS, TL, KV_WIDTH)
            v_ref[...] = v.reshape(SEQS, TL, KV_WIDTH)
        kcat = jnp.concatenate([k, pltpu.roll(k, HEAD_DIM, 1)], axis=1).astype(BF16)
        kcat_ref[...] = kcat.reshape(SEQS, TL, 2 * KV_WIDTH)
        for b in range(SEQS):
            vt_ref[b] = v[_seq_rows(b), :].T.astype(BF16)

    def project_q():
        qall = project(W_IN_Q, ATTN_WIDTH)
        even_head = lax.broadcasted_iota(jnp.int32, (ROWS, LANES), 1) < HEAD_DIM
        for cg in range(ATTN_WIDTH // LANES):
            cols = slice(cg * LANES, (cg + 1) * LANES)
            q = qall[:, cols]
            if rotary:
                q = _rope(q, cos_ref[...], sin_ref[...])
            q = q * (LOG2E * HEAD_DIM ** -0.5)
            ql_ref[:, cols] = jnp.where(even_head, q, 0.0).astype(BF16)
            qr_ref[:, cols] = jnp.where(even_head, 0.0, q).astype(BF16)

    def project_gate(c0, out_ref):
        out_ref[...] = jax.nn.silu(project(c0, out_ref.shape[1])).astype(BF16)

    side_work = [project_kv, project_q, lambda: project_gate(W_IN_ZS, zs_ref), lambda: project_gate(W_IN_ZA, za_ref)]

    spacing = N_UNITS // len(side_work)

    def between(i):
        if i % spacing == 0:
            side_work[i // spacing]()

    def emit(blk, slots):
        lanes = slice(blk * LANES, (blk + 1) * LANES)
        for slot, ys in enumerate(slots):
            rows = _slot_rows(slot)
            yb_ref[:, rows, lanes] = ys + u_tm[:, rows, lanes] * d_ref[:, lanes]

    _s5_direction(lhs_ref, s_s, carry, win_ref, wout_ref, lpow_ref, fin_ref, first_s, emit, BWD,
                  wk_ref=wk_ref, between=between)


def _pass_a(x, mod, norm_w, w_in, win, wout, lpow, wk, d, h0, rope):
    batch, length, _ = x.shape
    ng, nt = batch // SEQS, length // TL
    rotary = rope is not None
    rev = lambda t: nt - 1 - t
    in_specs = [
        pl.BlockSpec((SEQS, TL, D_MODEL), lambda g, t: (g, rev(t), 0)),
        _const_spec(mod.shape),
        _const_spec((1, D_MODEL)),
        _const_spec((D_MODEL, D_IN)),
    ] + _s5_param_specs(BWD) + [
        _const_spec((N_UNITS, KTILE, KTILE)),
        _const_spec((1, SSM_WIDTH)),
    ]
    args = [x, mod, norm_w, w_in, win, wout, lpow, wk, d]
    if rotary:
        in_specs += _h0_specs(BWD) + [pl.BlockSpec((TL, LANES), lambda g, t: (rev(t), 0))] * 2
        args += list(h0) + list(rope)
    assert ATTN_WIDTH == SSM_WIDTH
    tile_spec = pl.BlockSpec((None, None, ROWS, SSM_WIDTH), lambda g, t: (g, rev(t), 0, 0))
    tile_bf16 = jax.ShapeDtypeStruct((ng, nt, ROWS, SSM_WIDTH), BF16)
    out_specs = [
        pl.BlockSpec((SEQS, TL, 2 * KV_WIDTH), lambda g, t: (g, rev(t), 0)),
        pl.BlockSpec((SEQS, KV_WIDTH, TL), lambda g, t: (g, 0, rev(t))),
        pl.BlockSpec((None, None, QROWS, N_UNITS * KTILE), lambda g, t: (g, rev(t), 0, 0)),
        pl.BlockSpec((None, None, QUADS, BT * SEQS, SSM_WIDTH), lambda g, t: (g, rev(t), 0, 0, 0)),
    ] + [tile_spec] * 4
    out_shape = [
        jax.ShapeDtypeStruct((batch, length, 2 * KV_WIDTH), BF16),
        jax.ShapeDtypeStruct((batch, KV_WIDTH, length), BF16),
        jax.ShapeDtypeStruct((ng, nt, QROWS, N_UNITS * KTILE), BF16),
        jax.ShapeDtypeStruct((ng, nt, QUADS, BT * SEQS, SSM_WIDTH), F32),
        tile_bf16, tile_bf16,
        tile_bf16, tile_bf16,
    ]
    scratch = [
        pltpu.VMEM((ROWS, D_MODEL), BF16),
        pltpu.VMEM((SSM_BLOCKS, SEQS * PITCH, LANES), F32),
        pltpu.VMEM((QUADS, BT * SEQS, SSM_WIDTH), F32),
        pltpu.VMEM((S5_BUFS_A, QROWS, 2 * UNIT_ST), F32),
        pltpu.VMEM((2, SEQS, ALL_ST), F32),
    ]
    if not rotary:
        cache_spec = pl.BlockSpec((SEQS, TL, KV_WIDTH), lambda g, t: (g, rev(t), 0))
        cache_shape = jax.ShapeDtypeStruct((batch, length, KV_WIDTH), F32)
        out_specs = [cache_spec, cache_spec] + out_specs + [pl.BlockSpec((None, 2, SEQS, ALL_ST), lambda g, t: (g, 0, 0, 0))]
        out_shape = [cache_shape, cache_shape] + out_shape + [jax.ShapeDtypeStruct((ng, 2, SEQS, ALL_ST), F32)]
        scratch += [pltpu.VMEM((2, SEQS, ALL_ST), F32)]
    return pl.pallas_call(
        functools.partial(_pass_a_kernel, rotary=rotary),
        grid=(ng, nt),
        in_specs=in_specs, out_specs=out_specs, out_shape=out_shape, scratch_shapes=scratch,
        compiler_params=pltpu.CompilerParams(
            dimension_semantics=("arbitrary", "arbitrary"), vmem_limit_bytes=VMEM_LIMIT),
        name="pass_a_latent" if rotary else "pass_a_context",
    )(*args)


def _attention(b, q_refs, za_ref, mix_s, kcat_refs, vt_refs, sink_ref, masks):
    rows = _seq_rows(b)
    vt = jnp.concatenate([r[b] for r in vt_refs], axis=1)
    segment = lax.broadcasted_iota(jnp.int32, (1, 4 * TL), 1) // TL
    kv_heads = range(N_KV_HEADS)

    def side(g, grp):
        return g ^ grp

    def head(g, i, grp):
        return 4 * g + 2 * i + side(g, grp)

    def scores(grp):
        q = jnp.concatenate([q_refs[side(g, grp)][rows, blk * LANES:(blk + 1) * LANES]
                             for g in kv_heads for blk in (2 * g, 2 * g + 1)], axis=0)
        return [lax.dot_general(r[b][:, grp * KV_WIDTH:(grp + 1) * KV_WIDTH], q, (((1,), (1,)), ((), ())),
                                preferred_element_type=F32) for r in kcat_refs]

    def softmax(grp, pieces):
        if masks is not None:
            pieces = [jnp.where(masks[0], pieces[0], NEG_INF), pieces[1],
                      jnp.where(masks[1], pieces[2], NEG_INF), pieces[3]]
        sinks = [sink_ref[head(g, i, grp)] * LOG2E for g in kv_heads for i in range(2)]
        sink = jnp.where(segment == 0, sinks[0], jnp.where(segment == 1, sinks[1],
                                                           jnp.where(segment == 2, sinks[2], sinks[3])))
        m = sink
        for piece in pieces:
            m = jnp.maximum(m, jnp.max(piece, axis=0, keepdims=True))
        ps = [jnp.exp2(piece - m).astype(BF16) for piece in pieces]
        return (ps[0] if len(ps) == 1 else jnp.concatenate(ps, axis=0)), jnp.exp2(sink - m)

    ones_rows = jnp.ones((BF16_SUBLANES, vt.shape[1]), BF16)

    all_scores = [scores(grp) for grp in range(2)]
    probs, sink_terms = zip(*[softmax(grp, s) for grp, s in enumerate(all_scores)])
    for g in kv_heads:
        cols = slice(g * 2 * TL, (g + 1) * 2 * TL)
        values = jnp.concatenate([vt[g * HEAD_DIM:(g + 1) * HEAD_DIM, :], ones_rows], axis=0)
        p = jnp.concatenate([probs[0][:, cols], probs[1][:, cols]], axis=1)
        sink_term = jnp.concatenate([sink_terms[0][:, cols], sink_terms[1][:, cols]], axis=1)
        ot = jnp.dot(values, p, preferred_element_type=F32)
        out_t = ot[0:HEAD_DIM] * (1.0 / (ot[HEAD_DIM:HEAD_DIM + 1] + sink_term))
        for i in range(2):
            left = out_t[:, (2 * g + i) * TL:(2 * g + i + 1) * TL]
            right = out_t[:, (2 * (g ^ 1) + i) * TL:(2 * (g ^ 1) + i + 1) * TL]
            blk_cols = slice((2 * g + i) * LANES, (2 * g + i + 1) * LANES)
            pair_t = jnp.concatenate([left, right], axis=0)
            gated = pair_t.T * za_ref[rows, blk_cols].astype(F32)
            mix_s[rows, ATTN_WIDTH + blk_cols.start:ATTN_WIDTH + blk_cols.stop] = gated.astype(BF16)


def _pass_b_kernel(*refs, windowed, n_tiles):
    if windowed:
        (sink_ref, x_ref, mod_ref, win_ref, wout_ref, lpow_ref, wg_ref, bg_ref,
         lhs_ref, yb_ref, zs_ref, za_ref, ql_ref, qr_ref, kx_ref, vx_ref, wo_ref, fw_ref, h0re_ref, h0im_ref,
         kp_ref, kc_ref, kn_ref, vp_ref, vc_ref, vn_ref,
         y_ref, y_sm, y_tm, s_s, carry, mix_s) = refs
        fin_ref = first_s = None
    else:
        (sink_ref, x_ref, mod_ref, win_ref, wout_ref, lpow_ref, wg_ref, bg_ref,
         lhs_ref, yb_ref, zs_ref, za_ref, ql_ref, qr_ref, kx_ref, vx_ref, wo_ref, fw_ref,
         y_ref, fin_ref, y_sm, y_tm, s_s, carry, mix_s, first_s) = refs
    t = pl.program_id(1)

    @pl.when(t == 0)
    def _():
        if windowed:
            carry[0] = h0re_ref[...]
            carry[1] = h0im_ref[...]
        else:
            carry[...] = jnp.zeros_like(carry)

    if windowed:
        kj = lax.broadcasted_iota(jnp.int32, (TL, 4 * TL), 0)
        qi = lax.broadcasted_iota(jnp.int32, (TL, 4 * TL), 1) & (TL - 1)
        masks = ((kj >= qi) & (t > 0), (kj <= qi) & (t < n_tiles - 1))
        key_refs = (kp_ref, kc_ref, kn_ref, kx_ref)
        val_refs = (vp_ref, vc_ref, vn_ref, vx_ref)
    else:
        masks = None
        key_refs = (kx_ref,)
        val_refs = (vx_ref,)

    def attention(b):
        _attention(b, (ql_ref, qr_ref), za_ref, mix_s, key_refs, val_refs, sink_ref, masks)

    def emit(blk, slots):
        lanes = slice(blk * LANES, (blk + 1) * LANES)
        for slot, ys in enumerate(slots):
            rows = _slot_rows(slot)
            y_tm[:, rows, lanes] = ys + yb_ref[:, rows, lanes]

    glu_blocks = QUADS // 4

    def glu(rc):
        pr = slice(rc * glu_blocks, (rc + 1) * glu_blocks)
        y = jax.nn.gelu(y_tm[pr].reshape(glu_blocks * BT * SEQS, SSM_WIDTH))
        gl = jnp.dot(y.astype(BF16), wg_ref[...], preferred_element_type=F32) + bg_ref[...]
        y_tm[pr] = (y * jax.nn.sigmoid(gl)).reshape(glu_blocks, BT * SEQS, SSM_WIDTH)

    def gate_s5(seqs):
        for b in seqs:
            for cb in range(SSM_BLOCKS):
                cols = slice(cb * LANES, (cb + 1) * LANES)
                gated = y_sm[cb, b * PITCH:b * PITCH + TL, :] * zs_ref[_seq_rows(b), cols].astype(F32)
                mix_s[_seq_rows(b), cols] = gated.astype(BF16)

    half_seqs = SEQS // 2

    def project_out(hf):
        rows = slice(hf * half_seqs * TL, (hf + 1) * half_seqs * TL)
        proj = jnp.dot(mix_s[rows, :], wo_ref[...], preferred_element_type=F32)
        y_ref[hf * half_seqs:(hf + 1) * half_seqs] = proj.reshape(half_seqs, TL, D_MODEL)

    def finish(seqs):
        for b in seqs:
            m = _mod_row(b, per_sequence=windowed)
            gate = mod_ref[m:m + 1, 2 * D_MODEL:3 * D_MODEL]
            r = x_ref[b] + gate * y_ref[b]
            ms = jnp.mean(r * r, axis=-1, keepdims=True)
            y_ref[b] = (r * lax.rsqrt(ms + EPS)) * fw_ref[...]

    first_half, second_half = range(0, half_seqs), range(half_seqs, SEQS)

    _s5_direction(lhs_ref, s_s, carry, win_ref, wout_ref, lpow_ref, fin_ref, first_s, emit, FWD)
    assert SEQS == 8
    follow_up = [
        lambda: glu(0), lambda: glu(1), lambda: glu(2),
        lambda: (glu(3), _to_sequence_major(y_sm, y_tm)),
        lambda: (gate_s5(first_half), project_out(0)),
        lambda: finish(first_half),
        lambda: gate_s5(second_half),
        lambda: (project_out(1), finish(second_half)),
    ]
    for b in range(SEQS):
        attention(b)
        follow_up[b]()


def _pass_b(x, mod, win, wout, lpow, w_glu, b_glu, from_a, kx, vx, w_out, fnorm_w, sink, h0, k_loc, v_loc):
    batch, length, _ = x.shape
    ng, nt = batch // SEQS, length // TL
    windowed = k_loc is not None
    tile_spec = pl.BlockSpec((None, None, ROWS, SSM_WIDTH), lambda g, t: (g, t, 0, 0))
    in_specs = [
        pl.BlockSpec(memory_space=pltpu.SMEM),
        pl.BlockSpec((SEQS, TL, D_MODEL), lambda g, t: (g, t, 0)),
        _const_spec(mod.shape),
    ] + _s5_param_specs(FWD) + [
        _const_spec((SSM_WIDTH, SSM_WIDTH)),
        _const_spec((1, SSM_WIDTH)),
        pl.BlockSpec((None, None, QROWS, N_UNITS * KTILE), lambda g, t: (g, t, 0, 0)),
        pl.BlockSpec((None, None, QUADS, BT * SEQS, SSM_WIDTH), lambda g, t: (g, t, 0, 0, 0)),
    ] + [tile_spec] * 4 + [
        pl.BlockSpec((SEQS, PAST_LEN, 2 * KV_WIDTH), lambda g, t: (g, 0, 0)),
        pl.BlockSpec((SEQS, KV_WIDTH, PAST_LEN), lambda g, t: (g, 0, 0)),
        _const_spec((D_MODEL, D_MODEL)),
        _const_spec((1, D_MODEL)),
    ]
    args = [sink, x, mod, win, wout, lpow, w_glu, b_glu, *from_a, kx, vx, w_out, fnorm_w]
    if windowed:
        in_specs += _h0_specs(FWD)
        args += list(h0)
        band_t = (lambda t: jnp.maximum(t - 1, 0), lambda t: t, lambda t: jnp.minimum(t + 1, nt - 1))
        in_specs += [pl.BlockSpec((SEQS, TL, 2 * KV_WIDTH), lambda g, t, f=f: (g, f(t), 0)) for f in band_t]
        in_specs += [pl.BlockSpec((SEQS, KV_WIDTH, TL), lambda g, t, f=f: (g, 0, f(t))) for f in band_t]
        args += [k_loc] * 3 + [v_loc] * 3
    out_specs = [pl.BlockSpec((SEQS, TL, D_MODEL), lambda g, t: (g, t, 0))]
    out_shape = [jax.ShapeDtypeStruct((batch, length, D_MODEL), F32)]
    scratch = [
        pltpu.VMEM((SSM_BLOCKS, SEQS * PITCH, LANES), F32),
        pltpu.VMEM((QUADS, BT * SEQS, SSM_WIDTH), F32),
        pltpu.VMEM((S5_BUFS_B, QROWS, 2 * UNIT_ST), F32),
        pltpu.VMEM((2, SEQS, ALL_ST), F32),
        pltpu.VMEM((ROWS, D_MODEL), BF16),
    ]
    if not windowed:
        out_specs += [pl.BlockSpec((None, 2, SEQS, ALL_ST), lambda g, t: (g, 0, 0, 0))]
        out_shape += [jax.ShapeDtypeStruct((ng, 2, SEQS, ALL_ST), F32)]
        scratch += [pltpu.VMEM((2, SEQS, ALL_ST), F32)]
    return pl.pallas_call(
        functools.partial(_pass_b_kernel, windowed=windowed, n_tiles=nt),
        grid=(ng, nt),
        in_specs=in_specs, out_specs=out_specs, out_shape=out_shape, scratch_shapes=scratch,
        compiler_params=pltpu.CompilerParams(
            dimension_semantics=("arbitrary", "arbitrary"), vmem_limit_bytes=VMEM_LIMIT),
        name="pass_b_latent" if windowed else "pass_b_context",
    )(*args)


def kernel(x_prompt, x_sample, c, cache_k, cache_v, state_ssm_re, state_ssm_im, c_ctx, norm_w, w_mod, b_mod, w_in, ssm_lambda_re, ssm_lambda_im, ssm_log_dt, ssm_b_re, ssm_b_im, ssm_c_re, ssm_c_im, ssm_d, w_glu, b_glu, attn_sink, w_out, final_norm_w):
    assert norm_w.shape[0] == 1, "single trunk layer"
    batch, seq, _ = x_prompt.shape
    dec_batch, dec_seq, _ = x_sample.shape
    assert dec_batch == SEQS and batch % SEQS == 0 and seq % TL == 0 and dec_seq % TL == 0
    assert seq == PAST_LEN and cache_k.shape[2] == PAST_LEN

    w_in0 = w_in[0].astype(BF16)
    nw = norm_w[0][None, :]
    fw = final_norm_w[None, :]
    d = ssm_d[0][None, :]
    wg = w_glu[0].astype(BF16)
    bg = b_glu[0][None, :]
    wo = w_out[0].astype(BF16)
    sink = attn_sink[0]

    cond = jnp.concatenate([c, c_ctx[None, :], jnp.zeros((16 - SEQS - 1, D_MODEL), F32)], axis=0)
    mod = _modulation(cond, w_mod[0], b_mod[0][None, :])

    win, wout, wk, lpow = _s5_weights(ssm_lambda_re[0], ssm_lambda_im[0], ssm_log_dt[0], ssm_b_re[0], ssm_b_im[0],
                                     ssm_c_re[0], ssm_c_im[0])
    h0 = (state_ssm_re.reshape(SEQS, 2 * ALL_ST), state_ssm_im.reshape(SEQS, 2 * ALL_ST))
    rope = _rope_tables(dec_seq)

    k_ctx, v_ctx, kcat_ctx, vt_ctx, *tiles_ctx, fin_b = _pass_a(
        x_prompt, mod, nw, w_in0, win, wout, lpow, wk, d, None, None)
    y_prompt, fin_f = _pass_b(x_prompt, mod, win, wout, lpow, wg, bg, tiles_ctx, kcat_ctx, vt_ctx, wo, fw, sink,
                              None, None, None)

    kcat_lat, vt_lat, *tiles_lat = _pass_a(x_sample, mod, nw, w_in0, win, wout, lpow, wk, d, h0, rope)
    kx = cache_k[:, 0].reshape(dec_batch, PAST_LEN, KV_WIDTH)
    kx = jnp.concatenate([kx, jnp.roll(kx, HEAD_DIM, axis=-1)], axis=-1).astype(BF16)
    vx = jnp.swapaxes(cache_v[:, 0].reshape(dec_batch, PAST_LEN, KV_WIDTH), 1, 2).astype(BF16)
    (y_sample,) = _pass_b(x_sample, mod, win, wout, lpow, wg, bg, tiles_lat, kx, vx, wo, fw, sink,
                          h0, kcat_lat, vt_lat)

    new_cache_k = k_ctx.reshape(batch, 1, seq, N_KV_HEADS, HEAD_DIM)
    new_cache_v = v_ctx.reshape(batch, 1, seq, N_KV_HEADS, HEAD_DIM)

    def states(fin, part):
        return fin[:, part].reshape(batch, SSM_GROUPS, SSM_STATE)

    new_re = jnp.stack([states(fin_f, 0), states(fin_b, 0)], axis=1)[:, None]
    new_im = jnp.stack([states(fin_f, 1), states(fin_b, 1)], axis=1)[:, None]
    return (y_prompt, y_sample, new_cache_k, new_cache_v, new_re, new_im)
```
